```python
import math
import jax, jax.numpy as jnp
from jax import lax
import numpy as np

D_MODEL = 4096
BATCH = 2
SEQ = 8192
DEPTH = 1

CTX_LEN = 256
GRID_W = 64
MIX_W = D_MODEL
FOURIER_HEADS = 4
FOURIER_W = MIX_W // 2
FOURIER_HEAD_DIM = FOURIER_W // FOURIER_HEADS
S5_W = MIX_W - FOURIER_W
S5_GROUP = 16
S5_GROUPS = S5_W // S5_GROUP
S5_STATE = 64
FFN_HIDDEN = -(-8 * D_MODEL // (3 * 256)) * 256
N_MOD = 6
EPS = 1e-6
DT_MIN = 1e-3
DT_MAX = 1e-1

kernel_name = "fnet_s5_hybrid_prefix_dit_block"


def rms_norm(x, g):
    xf = x.astype(jnp.float32)
    y = xf * lax.rsqrt(jnp.mean(xf * xf, axis=-1, keepdims=True) + EPS)
    return (y * g.astype(jnp.float32)).astype(x.dtype)


def modulate(h, shift, scale):
    return h * (1 + scale) + shift


def fourier_mixer(u, w_f):
    bsz, length, _ = u.shape
    uh = u.reshape(bsz, length, FOURIER_HEADS, FOURIER_HEAD_DIM).astype(jnp.float32)
    f = jnp.fft.fftn(uh, axes=(1, 3), norm="ortho").real.astype(u.dtype)
    y = jnp.einsum('blhd,hde->blhe', f, w_f)
    return y.reshape(bsz, length, FOURIER_W)


def _ssm_combine(e1, e2):
    a1, b1 = e1
    a2, b2 = e2
    return a1 * a2, a2 * b1 + b2


def s5_scan(u, h0, lam_re, lam_im, log_dt, b_re, b_im, c_re, c_im):
    f32 = jnp.float32
    dt = jnp.exp(log_dt.astype(f32))[:, None]
    lam = lax.complex(jnp.minimum(lam_re.astype(f32), -1e-4), lam_im.astype(f32))
    lam_dt = lam * dt
    lam_bar = jnp.exp(lam_dt)
    b_bar = ((lam_bar - 1) / lam)[..., None] * lax.complex(b_re.astype(f32), b_im.astype(f32))
    c_mat = lax.complex(c_re.astype(f32), c_im.astype(f32))
    steps = jnp.arange(1, GRID_W + 1, dtype=f32)[:, None, None]
    carry_decay = jnp.exp(lam_dt[None] * steps)
    bsz, length, n_groups, group_w = u.shape
    rows = length // GRID_W
    u_rows = jnp.moveaxis(u.reshape(bsz, rows, GRID_W, n_groups, group_w), 1, 0)

    def row_step(h, u_r):
        bu = jnp.einsum('gph,bwgh->bwgp', b_bar, u_r.astype(jnp.complex64))
        a = jnp.broadcast_to(lam_bar, bu.shape)
        _, hs = lax.associative_scan(_ssm_combine, (a, bu), axis=1)
        hs = hs + carry_decay[None] * h[:, None]
        y = jnp.einsum('ghp,bwgp->bwgh', c_mat, hs).real
        return hs[:, -1], y

    h_last, ys = lax.scan(row_step, h0, u_rows)
    return jnp.moveaxis(ys, 0, 1).reshape(bsz, length, n_groups, group_w), h_last


def s5_bidir(u, h0_f, h0_b, lam_re, lam_im, log_dt, b_re, b_im, c_re, c_im, d_skip):
    bsz, length, _ = u.shape
    uf = u.astype(jnp.float32)
    ug = uf.reshape(bsz, length, S5_GROUPS, S5_GROUP)
    y_f, h_f = s5_scan(ug, h0_f, lam_re[0], lam_im[0], log_dt[0], b_re[0], b_im[0], c_re[0], c_im[0])
    y_b, h_b = s5_scan(jnp.flip(ug, 1), h0_b, lam_re[1], lam_im[1], log_dt[1], b_re[1], b_im[1], c_re[1], c_im[1])
    y = (y_f + jnp.flip(y_b, 1)).reshape(bsz, length, S5_W) + d_skip.astype(jnp.float32) * uf
    return y.astype(u.dtype), h_f, h_b


def s5_glu(y, w_a, b_a, w_b, b_b):
    g = jax.nn.gelu(y)
    return (g @ w_a + b_a) * jax.nn.sigmoid(g @ w_b + b_b)


def swiglu(h, w_gate, w_up, w_down):
    return (jax.nn.silu(h @ w_gate) * (h @ w_up)) @ w_down


def setup_inputs(seed: int = 0) -> dict:
    key = jax.random.key(seed)
    ks = jax.random.split(key, 32)
    f32 = jnp.float32
    nrm = lambda k, s, std: jax.random.normal(k, s, f32) * std
    G, P, H = S5_GROUPS, S5_STATE, S5_GROUP
    ada_std = 0.5 * D_MODEL ** -0.5
    lam_im = math.pi * jnp.arange(P, dtype=f32)[None, None, None, :] + nrm(ks[10], (DEPTH, 2, G, P), 0.01)
    return {
        "x": nrm(ks[0], (BATCH, SEQ, D_MODEL), 1.0),
        "c": nrm(ks[1], (BATCH, D_MODEL), 1.0),
        "ctx": nrm(ks[2], (BATCH, CTX_LEN, D_MODEL), 1.0),
        "c_ctx": nrm(ks[3], (D_MODEL,), 1.0),
        "ada_w": nrm(ks[4], (DEPTH, D_MODEL, N_MOD * D_MODEL), ada_std),
        "ada_b": nrm(ks[5], (DEPTH, N_MOD * D_MODEL), 0.01),
        "norm1_g": 1.0 + nrm(ks[6], (DEPTH, D_MODEL), 0.01),
        "norm2_g": 1.0 + nrm(ks[7], (DEPTH, D_MODEL), 0.01),
        "w_in": nrm(ks[8], (DEPTH, D_MODEL, MIX_W), D_MODEL ** -0.5),
        "w_out": nrm(ks[9], (DEPTH, MIX_W, D_MODEL), MIX_W ** -0.5),
        "fourier_w": nrm(ks[11], (DEPTH, FOURIER_HEADS, FOURIER_HEAD_DIM, FOURIER_HEAD_DIM), FOURIER_HEAD_DIM ** -0.5),
        "s5_lam_re": -0.5 + nrm(ks[12], (DEPTH, 2, G, P), 0.01),
        "s5_lam_im": lam_im,
        "s5_log_dt": jax.random.uniform(ks[13], (DEPTH, 2, G), f32, math.log(DT_MIN), math.log(DT_MAX)),
        "s5_b_re": nrm(ks[14], (DEPTH, 2, G, P, H), (2.0 * H) ** -0.5),
        "s5_b_im": nrm(ks[15], (DEPTH, 2, G, P, H), (2.0 * H) ** -0.5),
        "s5_c_re": nrm(ks[16], (DEPTH, 2, G, H, P), (2.0 * P) ** -0.5),
        "s5_c_im": nrm(ks[17], (DEPTH, 2, G, H, P), (2.0 * P) ** -0.5),
        "s5_d": nrm(ks[18], (DEPTH, S5_W), 1.0),
        "glu_w_a": nrm(ks[19], (DEPTH, S5_W, S5_W), S5_W ** -0.5),
        "glu_b_a": nrm(ks[20], (DEPTH, S5_W), 0.01),
        "glu_w_b": nrm(ks[21], (DEPTH, S5_W, S5_W), S5_W ** -0.5),
        "glu_b_b": nrm(ks[22], (DEPTH, S5_W), 0.01),
        "ffn_w_gate": nrm(ks[23], (DEPTH, D_MODEL, FFN_HIDDEN), D_MODEL ** -0.5),
        "ffn_w_up": nrm(ks[24], (DEPTH, D_MODEL, FFN_HIDDEN), D_MODEL ** -0.5),
        "ffn_w_down": nrm(ks[25], (DEPTH, FFN_HIDDEN, D_MODEL), FFN_HIDDEN ** -0.5),
        "final_g": 1.0 + nrm(ks[26], (D_MODEL,), 0.01),
    }


def reference(x, c, ctx, c_ctx, ada_w, ada_b, norm1_g, norm2_g, w_in, w_out, fourier_w,
              s5_lam_re, s5_lam_im, s5_log_dt, s5_b_re, s5_b_im, s5_c_re, s5_c_im, s5_d,
              glu_w_a, glu_b_a, glu_w_b, glu_b_b, ffn_w_gate, ffn_w_up, ffn_w_down, final_g):
    bsz = x.shape[0]
    h_zero = jnp.zeros((bsz, S5_GROUPS, S5_STATE), jnp.complex64)
    for layer in range(DEPTH):
        last = layer == DEPTH - 1
        s5_p = (s5_lam_re[layer], s5_lam_im[layer], s5_log_dt[layer], s5_b_re[layer], s5_b_im[layer],
                s5_c_re[layer], s5_c_im[layer], s5_d[layer])
        glu_p = (glu_w_a[layer], glu_b_a[layer], glu_w_b[layer], glu_b_b[layer])
        ffn_p = (ffn_w_gate[layer], ffn_w_up[layer], ffn_w_down[layer])
        mod = (jax.nn.silu(c) @ ada_w[layer] + ada_b[layer]).reshape(bsz, 1, N_MOD, D_MODEL)
        mod_c = (jax.nn.silu(c_ctx) @ ada_w[layer] + ada_b[layer]).reshape(1, 1, N_MOD, D_MODEL)
        sh1, sc1, g1, sh2, sc2, g2 = (mod[:, :, i] for i in range(N_MOD))
        csh1, csc1, cg1, csh2, csc2, cg2 = (mod_c[:, :, i] for i in range(N_MOD))

        hc = modulate(rms_norm(ctx, norm1_g[layer]), csh1, csc1)
        yc_s, hf_ctx, hb_ctx = s5_bidir(hc @ w_in[layer][:, FOURIER_W:], h_zero, h_zero, *s5_p)

        h = modulate(rms_norm(x, norm1_g[layer]), sh1, sc1)
        z = h @ w_in[layer]
        y_four = fourier_mixer(z[..., :FOURIER_W], fourier_w[layer])
        y_s, _, _ = s5_bidir(z[..., FOURIER_W:], hf_ctx, hb_ctx, *s5_p)
        y_s = s5_glu(y_s, *glu_p)
        x = x + g1 * (jnp.concatenate([y_four, y_s], axis=-1) @ w_out[layer])
        x = x + g2 * swiglu(modulate(rms_norm(x, norm2_g[layer]), sh2, sc2), *ffn_p)

        if not last:
            yc_four = fourier_mixer(hc @ w_in[layer][:, :FOURIER_W], fourier_w[layer])
            yc = jnp.concatenate([yc_four, s5_glu(yc_s, *glu_p)], axis=-1) @ w_out[layer]
            ctx = ctx + cg1 * yc
            ctx = ctx + cg2 * swiglu(modulate(rms_norm(ctx, norm2_g[layer]), csh2, csc2), *ffn_p)
    return rms_norm(x, final_g)
```

```python
import functools
import math

import numpy as np
import jax
import jax.numpy as jnp
from jax import lax
from jax.experimental import pallas as pl
from jax.experimental.pallas import tpu as pltpu

F32 = jnp.float32
BF16 = jnp.bfloat16
EPS = 1e-6
CHUNK = 64
N_MOD = 6
V7X_VMEM_LIMIT = 56 * 1024 * 1024
HI = lax.Precision.HIGHEST


def _cparams(*sem):
    return pltpu.CompilerParams(dimension_semantics=sem, vmem_limit_bytes=V7X_VMEM_LIMIT)


def _dot(a, b):
    return jnp.dot(a, b, preferred_element_type=F32)


def _ada_kernel(a_ref, w_ref, b_ref, o_ref):
    a = a_ref[...]
    s = a * jax.nn.sigmoid(a)
    s_hi = s.astype(BF16).astype(F32)
    row = lax.broadcasted_iota(jnp.int32, s.shape, 0)
    lhs = jnp.where(row < 8, s_hi, s - s_hi).astype(BF16)
    w = w_ref[...]
    w_hi = w.astype(BF16)
    w_lo = (w - w_hi.astype(F32)).astype(BF16)
    r = _dot(lhs, w_hi) + _dot(lhs, w_lo)
    o_ref[...] = r[0:8] + r[8:16] + b_ref[...]


def _ada(a16, w, b, tn=1024):
    d, n = w.shape
    tn = min(tn, n)
    return pl.pallas_call(
        _ada_kernel,
        grid=(n // tn,),
        in_specs=[pl.BlockSpec((16, d), lambda j: (0, 0)),
                  pl.BlockSpec((d, tn), lambda j: (0, j)),
                  pl.BlockSpec((1, tn), lambda j: (0, j))],
        out_specs=pl.BlockSpec((8, tn), lambda j: (0, j)),
        out_shape=jax.ShapeDtypeStruct((8, n), F32),
        compiler_params=_cparams("parallel"),
        name="ada_matvec",
    )(a16, w, b.reshape(1, n))


def _norm_mod_kernel(x_ref, g_ref, sh_ref, sc_ref, o_ref):
    x = x_ref[...]
    ms = jnp.mean(x * x, axis=-1, keepdims=True)
    y = x * lax.rsqrt(ms + EPS) * g_ref[...]
    o_ref[...] = (y * (1.0 + sc_ref[...]) + sh_ref[...]).astype(o_ref.dtype)


def _norm_mod(x, g, sh, sc, tm=256):
    bsz, length, d = x.shape
    tm = min(tm, length)
    bm = sh.shape[0]
    mod_map = (lambda b, i: (b, 0, 0)) if bm == bsz else (lambda b, i: (0, 0, 0))
    return pl.pallas_call(
        _norm_mod_kernel,
        grid=(bsz, length // tm),
        in_specs=[pl.BlockSpec((None, tm, d), lambda b, i: (b, i, 0)),
                  pl.BlockSpec((1, d), lambda b, i: (0, 0)),
                  pl.BlockSpec((None, 1, d), mod_map),
                  pl.BlockSpec((None, 1, d), mod_map)],
        out_specs=pl.BlockSpec((None, tm, d), lambda b, i: (b, i, 0)),
        out_shape=jax.ShapeDtypeStruct((bsz, length, d), BF16),
        compiler_params=_cparams("parallel", "parallel"),
        name="norm_mod",
    )(x, g.reshape(1, d), sh.reshape(bm, 1, d), sc.reshape(bm, 1, d))


def _fold_kernel(a_ref, b_ref, o_ref):
    o_ref[...] = jnp.dot(a_ref[...], b_ref[...], preferred_element_type=F32, precision=HI)


def _fold(a, b):
    hh, m, k = a.shape
    n = b.shape[-1]
    return pl.pallas_call(
        _fold_kernel,
        grid=(hh,),
        in_specs=[pl.BlockSpec((None, m, k), lambda h: (h, 0, 0)),
                  pl.BlockSpec((None, k, n), lambda h: (h, 0, 0))],
        out_specs=pl.BlockSpec((None, m, n), lambda h: (h, 0, 0)),
        out_shape=jax.ShapeDtypeStruct((hh, m, n), F32),
        compiler_params=_cparams("parallel"),
        name="weight_fold",
    )(a, b)


def _four_in_kernel(h_ref, w_ref, wcs_ref, pc_ref, ps_ref):
    hd = pc_ref.shape[-1]
    z = _dot(h_ref[...], w_ref[...]).astype(BF16)
    p = _dot(z, wcs_ref[...])
    pc_ref[...] = p[:, :hd].astype(BF16)
    ps_ref[...] = p[:, hd:].astype(BF16)


def _four_in(h, w_f, wcs, tm=1024):
    n, d = h.shape
    heads, hd, _ = wcs.shape
    tm = min(tm, n)
    fw = heads * hd
    out = jax.ShapeDtypeStruct((n, fw), BF16)
    return pl.pallas_call(
        _four_in_kernel,
        grid=(n // tm, heads),
        in_specs=[pl.BlockSpec((tm, d), lambda i, j: (i, 0)),
                  pl.BlockSpec((d, hd), lambda i, j: (0, j)),
                  pl.BlockSpec((None, hd, 2 * hd), lambda i, j: (j, 0, 0))],
        out_specs=[pl.BlockSpec((tm, hd), lambda i, j: (i, j)),
                   pl.BlockSpec((tm, hd), lambda i, j: (i, j))],
        out_shape=[out, out],
        compiler_params=_cparams("parallel", "arbitrary"),
        name="fourier_in_proj",
    )(h, w_f, wcs)


def _dft_tables(length, n2):
    n1 = length // n2
    k1 = np.arange(n1)[:, None, None]
    i1 = np.arange(n1)[None, :, None]
    i2 = np.arange(n2)[None, None, :]
    phase = (k1 * (n2 * i1 + i2)) % length
    phi = (2.0 * np.pi / length) * phase.astype(np.float64)
    c = np.cos(phi).transpose(2, 0, 1) / math.sqrt(n1)
    s = np.sin(phi).transpose(2, 0, 1) / math.sqrt(n1)
    g = np.concatenate([np.concatenate([c, -s], axis=2),
                        np.concatenate([-s, -c], axis=2)], axis=1)
    k2 = np.arange(n2)[:, None]
    j2 = np.arange(n2)[None, :]
    th = (2.0 * np.pi / n2) * ((k2 * j2) % n2).astype(np.float64)
    cs = np.concatenate([np.cos(th), np.sin(th)], axis=1) / math.sqrt(n2)
    return jnp.asarray(g, dtype=BF16), jnp.asarray(cs, dtype=BF16)


def _dft1_kernel(g_ref, pc_ref, ps_ref, o_ref, rhs_ref):
    n1 = pc_ref.shape[0]
    rhs_ref[0:n1, :] = pc_ref[...]
    rhs_ref[n1:2 * n1, :] = ps_ref[...]
    o_ref[...] = _dot(g_ref[...], rhs_ref[...]).astype(o_ref.dtype)


def _dft1(pc, ps, g, bsz, length):
    w = pc.shape[-1]
    n2, m, _ = g.shape
    n1 = m // 2
    pc3 = pc.reshape(bsz, n1, n2 * w)
    ps3 = ps.reshape(bsz, n1, n2 * w)
    return pl.pallas_call(
        _dft1_kernel,
        grid=(bsz, n2),
        in_specs=[pl.BlockSpec((None, m, m), lambda b, j: (j, 0, 0)),
                  pl.BlockSpec((None, n1, w), lambda b, j: (b, 0, j)),
                  pl.BlockSpec((None, n1, w), lambda b, j: (b, 0, j))],
        out_specs=pl.BlockSpec((None, None, m, w), lambda b, j: (b, j, 0, 0)),
        out_shape=jax.ShapeDtypeStruct((bsz, n2, m, w), BF16),
        scratch_shapes=[pltpu.VMEM((m, w), BF16)],
        compiler_params=_cparams("parallel", "parallel"),
        name="dft_stage1",
    )(g, pc3, ps3)


def _dft2_kernel(cs_ref, re_ref, im_ref, o_ref, rhs_ref):
    n2 = re_ref.shape[0]
    rhs_ref[0:n2, :] = re_ref[...]
    rhs_ref[n2:2 * n2, :] = im_ref[...]
    o_ref[...] = _dot(cs_ref[...], rhs_ref[...]).astype(o_ref.dtype)


def _dft2(t, cs, ct=8192):
    bsz, n2, m, w = t.shape
    n1 = m // 2
    half = n1 * w
    ct = min(ct, half)
    nblk = half // ct
    t3 = t.reshape(bsz, n2, m * w)
    y = pl.pallas_call(
        _dft2_kernel,
        grid=(bsz, nblk),
        in_specs=[pl.BlockSpec((n2, 2 * n2), lambda b, j: (0, 0)),
                  pl.BlockSpec((None, n2, ct), lambda b, j: (b, 0, j)),
                  pl.BlockSpec((None, n2, ct), lambda b, j: (b, 0, j + nblk))],
        out_specs=pl.BlockSpec((None, n2, ct), lambda b, j: (b, 0, j)),
        out_shape=jax.ShapeDtypeStruct((bsz, n2, half), BF16),
        scratch_shapes=[pltpu.VMEM((2 * n2, ct), BF16)],
        compiler_params=_cparams("parallel", "parallel"),
        name="dft_stage2",
    )(cs, t3, t3)
    return y.reshape(bsz, n2 * n1, w)


def _nt_kernel(w_ref, h_ref, o_ref):
    o_ref[...] = lax.dot_general(w_ref[...], h_ref[...], (((1,), (1,)), ((), ())),
                                 preferred_element_type=F32).astype(o_ref.dtype)


def _proj_t(w_t, h, tmc=1024, tn=1024):
    c, d = w_t.shape
    n = h.shape[0]
    tmc, tn = min(tmc, c), min(tn, n)
    return pl.pallas_call(
        _nt_kernel,
        grid=(n // tn, c // tmc),
        in_specs=[pl.BlockSpec((tmc, d), lambda i, j: (j, 0)),
                  pl.BlockSpec((tn, d), lambda i, j: (i, 0))],
        out_specs=pl.BlockSpec((tmc, tn), lambda i, j: (j, i)),
        out_shape=jax.ShapeDtypeStruct((c, n), BF16),
        compiler_params=_cparams("parallel", "arbitrary"),
        name="s5_in_proj_t",
    )(w_t, h)


def _s5_tables(lam_re, lam_im, log_dt, b_re, b_im, c_re, c_im):
    t = CHUNK
    _, g, p = lam_re.shape
    h = b_re.shape[-1]
    dt = jnp.exp(log_dt.astype(F32))[..., None]
    lam = lax.complex(jnp.minimum(lam_re.astype(F32), -1e-4), lam_im.astype(F32))
    lam_dt = lam * dt
    steps = jnp.arange(t + 1, dtype=F32)
    pw = jnp.exp(lam_dt[..., None] * steps)
    lam_bar = jnp.exp(lam_dt)
    b_bar = ((lam_bar - 1) / lam)[..., None] * lax.complex(b_re.astype(F32), b_im.astype(F32))
    c_mat = lax.complex(c_re.astype(F32), c_im.astype(F32))

    cb = c_mat[:, :, None, :, :] * jnp.swapaxes(pw[..., :t], -1, -2)[:, :, :, None, :]
    kern = jnp.einsum('dgtop,dgpi->dgtio', cb, b_bar, precision=HI).real
    kf, kb = kern[0], kern[1]
    kall = jnp.concatenate([jnp.flip(kb[:, 1:], axis=1), kf[:, :1] + kb[:, :1], kf[:, 1:]], axis=1)
    kall = jnp.transpose(kall, (0, 2, 3, 1)).astype(BF16)
    toep = jnp.stack([kall[..., t - 1 - s:2 * t - 1 - s] for s in range(t)], axis=2)
    toep = toep.reshape(g, h * t, h * t)

    pf = jnp.flip(pw[0, :, :, :t], axis=-1)
    pb = pw[1, :, :, :t]
    ef = jnp.einsum('gps,gph->ghsp', pf, b_bar[0])
    eb = jnp.einsum('gps,gph->ghsp', pb, b_bar[1])
    e = jnp.concatenate([ef.real, ef.imag, eb.real, eb.imag], axis=-1).reshape(g, h * t, 4 * p).astype(BF16)

    qf = pw[0, :, :, 1:]
    qb = jnp.flip(pw[1, :, :, 1:], axis=-1)
    df = jnp.einsum('gop,gpt->gpot', c_mat[0], qf)
    db = jnp.einsum('gop,gpt->gpot', c_mat[1], qb)
    dm = jnp.concatenate([df.real, -df.imag, db.real, -db.imag], axis=1).reshape(g, 4 * p, h * t).astype(BF16)

    at = pw[:, :, :, t]
    dec = jnp.stack([at[0].real, at[0].imag, at[1].real, at[1].imag]).reshape(4, g * p).astype(F32)
    return toep, e, dm, dec


GPS = 2


def _gather_chunks(z_ref, gi, h):
    return jnp.concatenate([z_ref[gi * h + k] for k in range(h)], axis=-1)


def _s5_state_kernel(z_ref, e_ref, fre_ref, fim_ref, bre_ref, bim_ref):
    h = z_ref.shape[0] // GPS
    p = e_ref.shape[-1] // 4
    outs = (fre_ref, fim_ref, bre_ref, bim_ref)
    for gi in range(GPS):
        a2 = _gather_chunks(z_ref, gi, h)
        s = _dot(a2, e_ref[gi])
        for k in range(4):
            outs[k][:, gi * p:(gi + 1) * p] = s[:, k * p:(k + 1) * p]


def _s5_states(z3, e):
    gh, nchunk, t = z3.shape
    g, ht, p4 = e.shape
    h, p = gh // g, p4 // 4
    out = jax.ShapeDtypeStruct((nchunk, g * p), F32)
    ospec = pl.BlockSpec((nchunk, GPS * p), lambda i: (0, i))
    return pl.pallas_call(
        _s5_state_kernel,
        grid=(g // GPS,),
        in_specs=[pl.BlockSpec((GPS * h, nchunk, t), lambda i: (i, 0, 0)),
                  pl.BlockSpec((GPS, ht, p4), lambda i: (i, 0, 0))],
        out_specs=[ospec] * 4,
        out_shape=[out] * 4,
        compiler_params=_cparams("parallel"),
        name="s5_chunk_states",
    )(z3, e)


def _s5_scan_kernel(dec_ref, cfre, cfim, cbre, cbim, sfre, sfim, sbre, sbim,
                    hfre, hfim, hbre, hbim):
    nctx = cfre.shape[0]
    nlat = sfre.shape[0]
    width = dec_ref.shape[-1]
    fr, fi = dec_ref[0:1, :], dec_ref[1:2, :]
    br, bi = dec_ref[2:3, :], dec_ref[3:4, :]
    zero = jnp.zeros((1, width), F32)

    def step(ar, ai, hr, hi, sr, si):
        return ar * hr - ai * hi + sr, ar * hi + ai * hr + si

    hr, hi = zero, zero
    for j in range(nctx):
        hr, hi = step(fr, fi, hr, hi, cfre[j:j + 1, :], cfim[j:j + 1, :])

    def fwd(j, carry):
        hr, hi = carry
        hfre[pl.ds(j, 1), :] = hr
        hfim[pl.ds(j, 1), :] = hi
        return step(fr, fi, hr, hi, sfre[pl.ds(j, 1), :], sfim[pl.ds(j, 1), :])

    lax.fori_loop(0, nlat, fwd, (hr, hi))

    hr, hi = zero, zero
    for j in range(nctx - 1, -1, -1):
        hr, hi = step(br, bi, hr, hi, cbre[j:j + 1, :], cbim[j:j + 1, :])

    def bwd(k, carry):
        hr, hi = carry
        j = nlat - 1 - k
        hbre[pl.ds(j, 1), :] = hr
        hbim[pl.ds(j, 1), :] = hi
        return step(br, bi, hr, hi, sbre[pl.ds(j, 1), :], sbim[pl.ds(j, 1), :])

    lax.fori_loop(0, nlat, bwd, (hr, hi))


def _s5_scan(dec, ctx_s, lat_s, bsz, tw=1024):
    gp = dec.shape[-1]
    nctx = ctx_s[0].shape[0] // bsz
    nlat = lat_s[0].shape[0] // bsz
    tw = min(tw, gp)
    cspec = pl.BlockSpec((None, nctx, tw), lambda b, i: (b, 0, i))
    lspec = pl.BlockSpec((None, nlat, tw), lambda b, i: (b, 0, i))
    out = jax.ShapeDtypeStruct((bsz, nlat, gp), F32)
    res = pl.pallas_call(
        _s5_scan_kernel,
        grid=(bsz, gp // tw),
        in_specs=[pl.BlockSpec((4, tw), lambda b, i: (0, i))] + [cspec] * 4 + [lspec] * 4,
        out_specs=[lspec] * 4,
        out_shape=[out] * 4,
        compiler_params=_cparams("parallel", "parallel"),
        name="s5_state_scan",
    )(dec, *[a.reshape(bsz, nctx, gp) for a in ctx_s], *[a.reshape(bsz, nlat, gp) for a in lat_s])
    return [r.reshape(bsz * nlat, gp) for r in res]


def _gelu_tanh(x):
    return 0.5 * x * (1.0 + jnp.tanh(math.sqrt(2.0 / math.pi) * (x + 0.044715 * (x * x * x))))


def _s5_out_kernel(z_ref, t_ref, d_ref, hfre, hfim, hbre, hbim, dsk_ref, o_ref):
    h = z_ref.shape[0] // GPS
    t = z_ref.shape[-1]
    p = d_ref.shape[1] // 4
    for gi in range(GPS):
        a2 = _gather_chunks(z_ref, gi, h)
        sl = slice(gi * p, (gi + 1) * p)
        hp = jnp.concatenate([hfre[:, sl], hfim[:, sl], hbre[:, sl], hbim[:, sl]], axis=-1).astype(BF16)
        y = _dot(a2, t_ref[gi]) + _dot(hp, d_ref[gi])
        y = y + a2.astype(F32) * dsk_ref[gi]
        gl = _gelu_tanh(y).astype(BF16)
        for k in range(h):
            o_ref[gi * h + k] = gl[:, k * t:(k + 1) * t]


def _s5_out(z3, toep, dm, hin, dsk):
    gh, nchunk, t = z3.shape
    g, ht, _ = toep.shape
    h = gh // g
    p4 = dm.shape[1]
    p = p4 // 4
    hspec = pl.BlockSpec((nchunk, GPS * p), lambda i: (0, i))
    return pl.pallas_call(
        _s5_out_kernel,
        grid=(g // GPS,),
        in_specs=[pl.BlockSpec((GPS * h, nchunk, t), lambda i: (i, 0, 0)),
                  pl.BlockSpec((GPS, ht, ht), lambda i: (i, 0, 0)),
                  pl.BlockSpec((GPS, p4, ht), lambda i: (i, 0, 0))] + [hspec] * 4 +
                 [pl.BlockSpec((GPS, 1, ht), lambda i: (i, 0, 0))],
        out_specs=pl.BlockSpec((GPS * h, nchunk, t), lambda i: (i, 0, 0)),
        out_shape=jax.ShapeDtypeStruct((gh, nchunk, t), BF16),
        compiler_params=_cparams("parallel"),
        name="s5_chunk_out",
    )(z3, toep, dm, *hin, dsk)


def _glu_kernel(gt_ref, wa_ref, wb_ref, ba_ref, bb_ref, o_ref):
    dn = (((0,), (0,)), ((), ()))
    gt = gt_ref[...]
    a = lax.dot_general(gt, wa_ref[...], dn, preferred_element_type=F32) + ba_ref[...]
    b = lax.dot_general(gt, wb_ref[...], dn, preferred_element_type=F32) + bb_ref[...]
    o_ref[...] = (a * jax.nn.sigmoid(b)).astype(o_ref.dtype)


def _glu(gt, wa, wb, ba, bb, tm=512, tn=1024):
    c, n = gt.shape
    co = wa.shape[-1]
    tm, tn = min(tm, n), min(tn, co)
    return pl.pallas_call(
        _glu_kernel,
        grid=(n // tm, co // tn),
        in_specs=[pl.BlockSpec((c, tm), lambda i, j: (0, i)),
                  pl.BlockSpec((c, tn), lambda i, j: (0, j)),
                  pl.BlockSpec((c, tn), lambda i, j: (0, j)),
                  pl.BlockSpec((1, tn), lambda i, j: (0, j)),
                  pl.BlockSpec((1, tn), lambda i, j: (0, j))],
        out_specs=pl.BlockSpec((tm, tn), lambda i, j: (i, j)),
        out_shape=jax.ShapeDtypeStruct((n, co), BF16),
        compiler_params=_cparams("parallel", "arbitrary"),
        name="s5_glu",
    )(gt, wa, wb, ba.reshape(1, co), bb.reshape(1, co))


def _out_proj_kernel(yf_ref, ys_ref, wf_ref, ws_ref, x_ref, g_ref, o_ref):
    acc = _dot(yf_ref[...], wf_ref[...]) + _dot(ys_ref[...], ws_ref[...])
    o_ref[...] = x_ref[...] + g_ref[...] * acc


def _out_proj(yf, ys, w_out, x2, gate, length, tm=1024, tn=1024):
    n, fw = yf.shape
    sw = ys.shape[-1]
    d = w_out.shape[-1]
    tm, tn = min(tm, length), min(tn, d)
    assert fw % sw == 0
    bsz = gate.shape[0]
    return pl.pallas_call(
        _out_proj_kernel,
        grid=(n // tm, d // tn),
        in_specs=[pl.BlockSpec((tm, fw), lambda i, j: (i, 0)),
                  pl.BlockSpec((tm, sw), lambda i, j: (i, 0)),
                  pl.BlockSpec((fw, tn), lambda i, j: (0, j)),
                  pl.BlockSpec((sw, tn), lambda i, j: (fw // sw, j)),
                  pl.BlockSpec((tm, tn), lambda i, j: (i, j)),
                  pl.BlockSpec((None, 1, tn), lambda i, j: ((i * tm) // length, 0, j))],
        out_specs=pl.BlockSpec((tm, tn), lambda i, j: (i, j)),
        out_shape=jax.ShapeDtypeStruct((n, d), F32),
        compiler_params=_cparams("parallel", "arbitrary"),
        name="out_proj_residual",
    )(yf, ys, w_out, w_out, x2, gate.reshape(bsz, 1, d))


def _ffn_kernel(h_ref, wg_ref, wu_ref, wd_ref, o_ref):
    j = pl.program_id(1)
    hh = h_ref[...]
    g = _dot(hh, wg_ref[...])
    u = _dot(hh, wu_ref[...])
    a = (g * jax.nn.sigmoid(g) * u).astype(BF16)
    contrib = _dot(a, wd_ref[...])

    @pl.when(j == 0)
    def _():
        o_ref[...] = contrib

    @pl.when(j > 0)
    def _():
        o_ref[...] += contrib


def _ffn(h, wg, wu, wd, tm=512, th=512):
    n, d = h.shape
    hid = wg.shape[-1]
    tm, th = min(tm, n), min(th, hid)
    return pl.pallas_call(
        _ffn_kernel,
        grid=(n // tm, hid // th),
        in_specs=[pl.BlockSpec((tm, d), lambda i, j: (i, 0)),
                  pl.BlockSpec((d, th), lambda i, j: (0, j)),
                  pl.BlockSpec((d, th), lambda i, j: (0, j)),
                  pl.BlockSpec((th, d), lambda i, j: (j, 0))],
        out_specs=pl.BlockSpec((tm, d), lambda i, j: (i, 0)),
        out_shape=jax.ShapeDtypeStruct((n, d), F32),
        compiler_params=_cparams("parallel", "arbitrary"),
        name="swiglu_ffn",
    )(h, wg, wu, wd)


def _final_kernel(x_ref, f_ref, gate_ref, g_ref, o_ref):
    x = x_ref[...] + gate_ref[...] * f_ref[...]
    ms = jnp.mean(x * x, axis=-1, keepdims=True)
    o_ref[...] = x * lax.rsqrt(ms + EPS) * g_ref[...]


def _final(x1, f, gate, g, tm=256):
    bsz, length, d = x1.shape
    tm = min(tm, length)
    return pl.pallas_call(
        _final_kernel,
        grid=(bsz, length // tm),
        in_specs=[pl.BlockSpec((None, tm, d), lambda b, i: (b, i, 0)),
                  pl.BlockSpec((None, tm, d), lambda b, i: (b, i, 0)),
                  pl.BlockSpec((None, 1, d), lambda b, i: (b, 0, 0)),
                  pl.BlockSpec((1, d), lambda b, i: (0, 0))],
        out_specs=pl.BlockSpec((None, tm, d), lambda b, i: (b, i, 0)),
        out_shape=jax.ShapeDtypeStruct((bsz, length, d), F32),
        compiler_params=_cparams("parallel", "parallel"),
        name="residual_final_norm",
    )(x1, f, gate.reshape(bsz, 1, d), g.reshape(1, d))


def _pad_to(a, axis, mult):
    size = a.shape[axis]
    pad = (-size) % mult
    if pad == 0:
        return a
    widths = [(0, 0)] * a.ndim
    widths[axis] = (0, pad)
    return jnp.pad(a, widths)


def _dft_split(length):
    n2 = 64
    while length // n2 < 8 or (length // n2) % 8:
        n2 //= 2
    return n2


def kernel(x, c, ctx, c_ctx, ada_w, ada_b, norm1_g, norm2_g, w_in, w_out, fourier_w, s5_lam_re, s5_lam_im, s5_log_dt, s5_b_re, s5_b_im, s5_c_re, s5_c_im, s5_d, glu_w_a, glu_b_a, glu_w_b, glu_b_b, ffn_w_gate, ffn_w_up, ffn_w_down, final_g):
    bsz, length, d = x.shape
    depth = ada_w.shape[0]
    assert depth == 1, "single-layer block"
    lyr = 0
    heads, hd, _ = fourier_w.shape[1:]
    fw = heads * hd
    _, g, p, hgrp = s5_b_re.shape[1:]
    sw = g * hgrp
    assert w_in.shape[-1] == fw + sw and length % CHUNK == 0 and ctx.shape[1] % CHUNK == 0
    n = bsz * length

    a8 = jnp.zeros((8, d), F32).at[:bsz].set(c.astype(F32)).at[bsz].set(c_ctx.astype(F32))
    mods = _ada(jnp.concatenate([a8, a8], axis=0), ada_w[lyr], ada_b[lyr]).reshape(8, N_MOD, d)
    sh1, sc1, g1, sh2, sc2, g2 = (mods[:bsz, i] for i in range(N_MOD))
    csh1, csc1 = mods[bsz:bsz + 1, 0], mods[bsz:bsz + 1, 1]

    w_in_b = w_in[lyr].astype(BF16)
    w_s_t = jnp.transpose(w_in[lyr][:, fw:]).astype(BF16)
    ang = (2.0 * np.pi / hd) * ((np.arange(hd)[:, None] * np.arange(hd)[None, :]) % hd).astype(np.float64)
    cd = jnp.asarray(np.cos(ang) / math.sqrt(hd), F32)
    sd = jnp.asarray(np.sin(ang) / math.sqrt(hd), F32)
    csd = jnp.broadcast_to(jnp.stack([cd, sd])[:, None], (2, heads, hd, hd)).reshape(2 * heads, hd, hd)
    wf2 = jnp.concatenate([fourier_w[lyr], fourier_w[lyr]], axis=0).astype(F32)
    folded = _fold(csd, wf2)
    wcs = jnp.concatenate([folded[:heads], folded[heads:]], axis=-1).astype(BF16)
    toep, e_mat, d_mat, dec = _s5_tables(s5_lam_re[lyr], s5_lam_im[lyr], s5_log_dt[lyr], s5_b_re[lyr],
                                         s5_b_im[lyr], s5_c_re[lyr], s5_c_im[lyr])
    dsk = jnp.repeat(s5_d[lyr].astype(F32).reshape(g, 1, hgrp), CHUNK, axis=-1)

    hc = _norm_mod(ctx, norm1_g[lyr], csh1, csc1)
    nctx_tok = bsz * ctx.shape[1]
    zc_t = _proj_t(w_s_t, hc.reshape(nctx_tok, d))
    ctx_s = _s5_states(zc_t.reshape(sw, nctx_tok // CHUNK, CHUNK), e_mat)

    hm = _norm_mod(x, norm1_g[lyr], sh1, sc1).reshape(n, d)
    pc, ps = _four_in(hm, w_in_b, wcs)
    n2 = _dft_split(length)
    g_tab, cs_tab = _dft_tables(length, n2)
    y_four = _dft2(_dft1(pc, ps, g_tab, bsz, length), cs_tab).reshape(n, fw)

    z3 = _proj_t(w_s_t, hm).reshape(sw, n // CHUNK, CHUNK)
    lat_s = _s5_states(z3, e_mat)
    h_in = _s5_scan(dec, ctx_s, lat_s, bsz)
    g_t = _s5_out(z3, toep, d_mat, h_in, dsk).reshape(sw, n)
    y_s = _glu(g_t, glu_w_a[lyr].astype(BF16), glu_w_b[lyr].astype(BF16), glu_b_a[lyr], glu_b_b[lyr])

    x1 = _out_proj(y_four, y_s, w_out[lyr].astype(BF16), x.reshape(n, d), g1, length)

    hm2 = _norm_mod(x1.reshape(bsz, length, d), norm2_g[lyr], sh2, sc2).reshape(n, d)
    hid_mult = 512
    wg = _pad_to(ffn_w_gate[lyr].astype(BF16), 1, hid_mult)
    wu = _pad_to(ffn_w_up[lyr].astype(BF16), 1, hid_mult)
    wd = _pad_to(ffn_w_down[lyr].astype(BF16), 0, hid_mult)
    f = _ffn(hm2, wg, wu, wd)
    return _final(x1.reshape(bsz, length, d), f.reshape(bsz, length, d), g2, final_g)
```

```python
import functools
import math

import numpy as np
import jax
import jax.numpy as jnp
from jax import lax
from jax.experimental import pallas as pl
from jax.experimental.pallas import tpu as pltpu

F32 = jnp.float32
BF16 = jnp.bfloat16
EPS = 1e-6
CHUNK = 64
N_MOD = 6
V7X_VMEM_LIMIT = 56 * 1024 * 1024
HI = lax.Precision.HIGHEST


def _cparams(*sem):
    return pltpu.CompilerParams(dimension_semantics=sem, vmem_limit_bytes=V7X_VMEM_LIMIT)


def _dot(a, b):
    return jnp.dot(a, b, preferred_element_type=F32)


def _ada_kernel(a_ref, w_ref, b_ref, o_ref):
    a = a_ref[...]
    s = a * jax.nn.sigmoid(a)
    s_hi = s.astype(BF16).astype(F32)
    row = lax.broadcasted_iota(jnp.int32, s.shape, 0)
    lhs = jnp.where(row < 8, s_hi, s - s_hi).astype(BF16)
    w = w_ref[...]
    w_hi = w.astype(BF16)
    w_lo = (w - w_hi.astype(F32)).astype(BF16)
    r = _dot(lhs, w_hi) + _dot(lhs, w_lo)
    o_ref[...] = r[0:8] + r[8:16] + b_ref[...]


def _ada(a16, w, b, tn=1024):
    d, n = w.shape
    tn = min(tn, n)
    return pl.pallas_call(
        _ada_kernel,
        grid=(n // tn,),
        in_specs=[pl.BlockSpec((16, d), lambda j: (0, 0)),
                  pl.BlockSpec((d, tn), lambda j: (0, j)),
                  pl.BlockSpec((1, tn), lambda j: (0, j))],
        out_specs=pl.BlockSpec((8, tn), lambda j: (0, j)),
        out_shape=jax.ShapeDtypeStruct((8, n), F32),
        compiler_params=_cparams("parallel"),
        name="ada_matvec",
    )(a16, w, b.reshape(1, n))


def _norm_mod_kernel(x_ref, g_ref, sh_ref, sc_ref, o_ref):
    x = x_ref[...]
    ms = jnp.mean(x * x, axis=-1, keepdims=True)
    y = x * lax.rsqrt(ms + EPS) * g_ref[...]
    o_ref[...] = (y * (1.0 + sc_ref[...]) + sh_ref[...]).astype(o_ref.dtype)


def _norm_mod(x, g, sh, sc, tm=256):
    bsz, length, d = x.shape
    tm = min(tm, length)
    bm = sh.shape[0]
    mod_map = (lambda b, i: (b, 0, 0)) if bm == bsz else (lambda b, i: (0, 0, 0))
    return pl.pallas_call(
        _norm_mod_kernel,
        grid=(bsz, length // tm),
        in_specs=[pl.BlockSpec((None, tm, d), lambda b, i: (b, i, 0)),
                  pl.BlockSpec((1, d), lambda b, i: (0, 0)),
                  pl.BlockSpec((None, 1, d), mod_map),
                  pl.BlockSpec((None, 1, d), mod_map)],
        out_specs=pl.BlockSpec((None, tm, d), lambda b, i: (b, i, 0)),
        out_shape=jax.ShapeDtypeStruct((bsz, length, d), BF16),
        compiler_params=_cparams("parallel", "parallel"),
        name="norm_mod",
    )(x, g.reshape(1, d), sh.reshape(bm, 1, d), sc.reshape(bm, 1, d))


def _fold_kernel(a_ref, b_ref, o_ref):
    o_ref[...] = jnp.dot(a_ref[...], b_ref[...], preferred_element_type=F32, precision=HI)


def _fold(a, b):
    hh, m, k = a.shape
    n = b.shape[-1]
    return pl.pallas_call(
        _fold_kernel,
        grid=(hh,),
        in_specs=[pl.BlockSpec((None, m, k), lambda h: (h, 0, 0)),
                  pl.BlockSpec((None, k, n), lambda h: (h, 0, 0))],
        out_specs=pl.BlockSpec((None, m, n), lambda h: (h, 0, 0)),
        out_shape=jax.ShapeDtypeStruct((hh, m, n), F32),
        compiler_params=_cparams("parallel"),
        name="weight_fold",
    )(a, b)


LANES = 128
PACK_ROWS = 16
PITCH_PAD = 8


def _four_in_kernel(h_ref, w_ref, wcs_ref, pc_ref, ps_ref, p_scr):
    n2, tn1, hd = pc_ref.shape
    pitch = n2 + PITCH_PAD
    nq = hd // LANES
    z = _dot(h_ref[...], w_ref[...]).astype(BF16)
    p = _dot(z, wcs_ref[...])
    for q in range(2 * nq):
        for i1 in range(tn1):
            p_scr[q, i1 * pitch:i1 * pitch + n2, :] = p[i1 * n2:(i1 + 1) * n2, q * LANES:(q + 1) * LANES]

    def emit(j2, carry):
        for q in range(nq):
            pc_ref[j2, :, q * LANES:(q + 1) * LANES] = p_scr[q, pl.ds(j2, tn1, stride=pitch), :].astype(BF16)
            ps_ref[j2, :, q * LANES:(q + 1) * LANES] = p_scr[nq + q, pl.ds(j2, tn1, stride=pitch), :].astype(BF16)
        return carry

    lax.fori_loop(0, n2, emit, 0)


def _four_in(h, w_f, wcs, bsz, length, n2):
    n, d = h.shape
    heads, hd, _ = wcs.shape
    n1 = length // n2
    tn1 = PACK_ROWS
    tm = tn1 * n2
    tpb = length // tm
    fw = heads * hd
    out = jax.ShapeDtypeStruct((bsz, n2, n1, fw), BF16)
    ospec = pl.BlockSpec((None, n2, tn1, hd), lambda i, j: (i // tpb, 0, i % tpb, j))
    return pl.pallas_call(
        _four_in_kernel,
        grid=(n // tm, heads),
        in_specs=[pl.BlockSpec((tm, d), lambda i, j: (i, 0)),
                  pl.BlockSpec((d, hd), lambda i, j: (0, j)),
                  pl.BlockSpec((None, hd, 2 * hd), lambda i, j: (j, 0, 0))],
        out_specs=[ospec, ospec],
        out_shape=[out, out],
        scratch_shapes=[pltpu.VMEM((2 * hd // LANES, tn1 * (n2 + PITCH_PAD), LANES), F32)],
        compiler_params=_cparams("parallel", "arbitrary"),
        name="fourier_in_proj",
    )(h, w_f, wcs)


def _dft_tables(length, n2):
    n1 = length // n2
    k1 = np.arange(n1)[:, None, None]
    i1 = np.arange(n1)[None, :, None]
    i2 = np.arange(n2)[None, None, :]
    phase = (k1 * (n2 * i1 + i2)) % length
    phi = (2.0 * np.pi / length) * phase.astype(np.float64)
    c = np.cos(phi).transpose(2, 0, 1) / math.sqrt(n1)
    s = np.sin(phi).transpose(2, 0, 1) / math.sqrt(n1)
    g = np.concatenate([np.concatenate([c, -s], axis=2),
                        np.concatenate([-s, -c], axis=2)], axis=1)
    k2 = np.arange(n2)[:, None]
    j2 = np.arange(n2)[None, :]
    th = (2.0 * np.pi / n2) * ((k2 * j2) % n2).astype(np.float64)
    cs = np.concatenate([np.cos(th), np.sin(th)], axis=1) / math.sqrt(n2)
    return jnp.asarray(g, dtype=BF16), jnp.asarray(cs, dtype=BF16)


def _dft_kernel(g_ref, cs_ref, pc_ref, ps_ref, o_ref, t_scr, y_scr):
    n2, n1, _ = pc_ref.shape
    m = 2 * n1
    p1 = m + PITCH_PAD
    p2 = n1 + PITCH_PAD

    def stage1(j2, carry):
        rhs = jnp.concatenate([pc_ref[j2], ps_ref[j2]], axis=0)
        t_scr[pl.ds(pl.multiple_of(j2 * p1, 8), m), :] = _dot(g_ref[j2], rhs)
        return carry

    lax.fori_loop(0, n2, stage1, 0, unroll=2)

    cs = cs_ref[...]

    def stage2(i, carry):
        k1 = 2 * i
        cols = []
        for dk in range(2):
            re = t_scr[pl.ds(k1 + dk, n2, stride=p1), :]
            im = t_scr[pl.ds(n1 + k1 + dk, n2, stride=p1), :]
            cols.append(jnp.concatenate([re, im], axis=0).astype(BF16))
        res = _dot(cs, jnp.concatenate(cols, axis=1))
        for dk in range(2):
            y_scr[pl.ds(k1 + dk, n2, stride=p2), :] = res[:, dk * LANES:(dk + 1) * LANES]
        return carry

    lax.fori_loop(0, n1 // 2, stage2, 0, unroll=4)
    for k2 in range(n2):
        o_ref[k2 * n1:(k2 + 1) * n1, :] = y_scr[k2 * p2:k2 * p2 + n1, :].astype(o_ref.dtype)


def _dft(pc4, ps4, g, cs):
    bsz, n2, n1, w = pc4.shape
    m = 2 * n1
    ispec = pl.BlockSpec((None, n2, n1, LANES), lambda b, j: (b, 0, 0, j))
    return pl.pallas_call(
        _dft_kernel,
        grid=(bsz, w // LANES),
        in_specs=[pl.BlockSpec((n2, m, m), lambda b, j: (0, 0, 0), pipeline_mode=pl.Buffered(1)),
                  pl.BlockSpec((n2, 2 * n2), lambda b, j: (0, 0)),
                  ispec, ispec],
        out_specs=pl.BlockSpec((None, n2 * n1, LANES), lambda b, j: (b, 0, j)),
        out_shape=jax.ShapeDtypeStruct((bsz, n2 * n1, w), BF16),
        scratch_shapes=[pltpu.VMEM((n2 * (m + PITCH_PAD), LANES), F32),
                        pltpu.VMEM((n2 * (n1 + PITCH_PAD), LANES), F32)],
        compiler_params=_cparams("parallel", "parallel"),
        name="position_dft",
    )(g, cs, pc4, ps4)


def _nt_kernel(w_ref, h_ref, o_ref):
    acc = lax.dot_general(w_ref[...], h_ref[...], (((1,), (1,)), ((), ())), preferred_element_type=F32)
    for q in range(o_ref.shape[1]):
        o_ref[:, q, :] = acc[:, q * LANES:(q + 1) * LANES]


def _proj_t(w_t, h, tmc=1024, tn=1024):
    c, d = w_t.shape
    n = h.shape[0]
    tmc, tn = min(tmc, c), min(tn, n)
    return pl.pallas_call(
        _nt_kernel,
        grid=(n // tn, c // tmc),
        in_specs=[pl.BlockSpec((tmc, d), lambda i, j: (j, 0)),
                  pl.BlockSpec((tn, d), lambda i, j: (i, 0))],
        out_specs=pl.BlockSpec((tmc, tn // LANES, LANES), lambda i, j: (j, i, 0)),
        out_shape=jax.ShapeDtypeStruct((c, n // LANES, LANES), F32),
        compiler_params=_cparams("parallel", "arbitrary"),
        name="s5_in_proj_t",
    )(w_t, h)


def _s5_tables(lam_re, lam_im, log_dt, b_re, b_im, c_re, c_im):
    t = CHUNK
    _, g, p = lam_re.shape
    h = b_re.shape[-1]
    dt = jnp.exp(log_dt.astype(F32))[..., None]
    lr, li = jnp.minimum(lam_re.astype(F32), -1e-4), lam_im.astype(F32)
    ar, ai = lr * dt, li * dt
    steps = jnp.arange(t + 1, dtype=F32)
    mag = jnp.exp(ar[..., None] * steps)
    ang = ai[..., None] * steps
    pr, pi = mag * jnp.cos(ang), mag * jnp.sin(ang)
    nr, ni = pr[..., 1] - 1.0, pi[..., 1]
    den = lr * lr + li * li
    qr, qi = (nr * lr + ni * li) / den, (ni * lr - nr * li) / den
    br, bi = b_re.astype(F32), b_im.astype(F32)
    bbr = qr[..., None] * br - qi[..., None] * bi
    bbi = qr[..., None] * bi + qi[..., None] * br
    cr, ci = c_re.astype(F32), c_im.astype(F32)

    crt, cit = jnp.swapaxes(cr, -1, -2)[..., None, :], jnp.swapaxes(ci, -1, -2)[..., None, :]
    mr = (bbr[..., None] * crt - bbi[..., None] * cit).reshape(2, g, p, h * h)
    mi = (bbr[..., None] * cit + bbi[..., None] * crt).reshape(2, g, p, h * h)
    kern = (jnp.einsum('dgpx,dgpt->dgxt', mr, pr[..., :t], precision=HI)
            - jnp.einsum('dgpx,dgpt->dgxt', mi, pi[..., :t], precision=HI))
    kf, kb = kern[0], kern[1]
    kall = jnp.concatenate([jnp.flip(kb[..., 1:], axis=-1), kf[..., :1] + kb[..., :1], kf[..., 1:],
                            jnp.zeros_like(kf[..., :1])], axis=-1)

    def outer_e(wr, wi, d):
        wr_, wi_ = jnp.swapaxes(wr, 1, 2)[:, None], jnp.swapaxes(wi, 1, 2)[:, None]
        b_r, b_i = jnp.swapaxes(bbr[d], 1, 2)[:, :, None], jnp.swapaxes(bbi[d], 1, 2)[:, :, None]
        return wr_ * b_r - wi_ * b_i, wr_ * b_i + wi_ * b_r
    efr, efi = outer_e(jnp.flip(pr[0, :, :, :t], -1), jnp.flip(pi[0, :, :, :t], -1), 0)
    ebr, ebi = outer_e(pr[1, :, :, :t], pi[1, :, :, :t], 1)
    e = jnp.concatenate([efr, efi, ebr, ebi], axis=-1).astype(BF16).reshape(g, h * t, 4 * p)

    def outer_d(wr, wi, d):
        wr_, wi_ = wr[:, :, None], wi[:, :, None]
        c_r, c_i = jnp.swapaxes(cr[d], 1, 2)[..., None], jnp.swapaxes(ci[d], 1, 2)[..., None]
        return c_r * wr_ - c_i * wi_, -(c_r * wi_ + c_i * wr_)
    dfr, dfi = outer_d(pr[0, :, :, 1:], pi[0, :, :, 1:], 0)
    dbr, dbi = outer_d(jnp.flip(pr[1, :, :, 1:], -1), jnp.flip(pi[1, :, :, 1:], -1), 1)
    dm = jnp.concatenate([dfr, dfi, dbr, dbi], axis=1).astype(BF16).reshape(g, 4 * p, h * t)

    dec = jnp.stack([pr[0, :, :, t], pi[0, :, :, t], pr[1, :, :, t], pi[1, :, :, t]]).reshape(4, g * p)
    return kall, e, dm, dec


def _toeplitz_select(t):
    sel = np.zeros((4 * t, (t // 8) * 2 * t), np.float32)
    for q in range(t // 8):
        for tt in range(t):
            j = tt - 8 * q + t - 8
            sel[j, q * 2 * t + tt] = 1.0
            sel[2 * t + j, q * 2 * t + t + tt] = 1.0
    return jnp.asarray(sel, dtype=BF16)


GPS = 2


def _gather_chunks(z_ref, gi, h):
    t = z_ref.shape[-1] // 2
    lo = lax.broadcasted_iota(jnp.int32, z_ref.shape[1:], 1) < t
    ev, od = [], []
    for k in range(0, h, 2):
        za, zb = z_ref[gi * h + k], z_ref[gi * h + k + 1]
        ev.append(jnp.where(lo, za, pltpu.roll(zb, t, axis=1)))
        od.append(jnp.where(lo, pltpu.roll(za, t, axis=1), zb))
    return jnp.concatenate([jnp.concatenate(ev, axis=1), jnp.concatenate(od, axis=1)], axis=0)


def _s5_state_kernel(z_ref, e_ref, fre_ref, fim_ref, bre_ref, bim_ref):
    h = z_ref.shape[0] // GPS
    p = e_ref.shape[-1] // 4
    outs = (fre_ref, fim_ref, bre_ref, bim_ref)
    for gi in range(GPS):
        a2 = _gather_chunks(z_ref, gi, h).astype(BF16)
        s = _dot(a2, e_ref[gi])
        for k in range(4):
            outs[k][:, gi * p:(gi + 1) * p] = s[:, k * p:(k + 1) * p]


def _s5_states(z3, e):
    gh, npair, t2 = z3.shape
    nchunk, t = 2 * npair, t2 // 2
    g, ht, p4 = e.shape
    h, p = gh // g, p4 // 4
    out = jax.ShapeDtypeStruct((nchunk, g * p), F32)
    ospec = pl.BlockSpec((nchunk, GPS * p), lambda i: (0, i))
    return pl.pallas_call(
        _s5_state_kernel,
        grid=(g // GPS,),
        in_specs=[pl.BlockSpec((GPS * h, npair, t2), lambda i: (i, 0, 0)),
                  pl.BlockSpec((GPS, ht, p4), lambda i: (i, 0, 0))],
        out_specs=[ospec] * 4,
        out_shape=[out] * 4,
        compiler_params=_cparams("parallel"),
        name="s5_chunk_states",
    )(z3, e)


def _s5_scan_kernel(bsz, dec_ref, cfre, cfim, cbre, cbim, sfre, sfim, sbre, sbim,
                    hfre, hfim, hbre, hbim):
    nctx = cfre.shape[0] // bsz
    nlat = sfre.shape[0] // bsz
    width = dec_ref.shape[-1]
    fr, fi = dec_ref[0:1, :], dec_ref[1:2, :]
    br, bi = dec_ref[2:3, :], dec_ref[3:4, :]
    zero = jnp.zeros((1, width), F32)

    def step(ar, ai, hr, hi, sr, si):
        return ar * hr - ai * hi + sr, ar * hi + ai * hr + si

    def row(nchunk, b, j):
        return (j % 2) * (bsz * nchunk // 2) + b * (nchunk // 2) + j // 2

    for b in range(bsz):
        hr, hi = zero, zero
        for j in range(nctx):
            r = row(nctx, b, j)
            hr, hi = step(fr, fi, hr, hi, cfre[r:r + 1, :], cfim[r:r + 1, :])

        def fwd(j, carry, b=b):
            hr, hi = carry
            r = row(nlat, b, j)
            hfre[pl.ds(r, 1), :] = hr
            hfim[pl.ds(r, 1), :] = hi
            return step(fr, fi, hr, hi, sfre[pl.ds(r, 1), :], sfim[pl.ds(r, 1), :])

        lax.fori_loop(0, nlat, fwd, (hr, hi))

        hr, hi = zero, zero
        for j in range(nctx - 1, -1, -1):
            r = row(nctx, b, j)
            hr, hi = step(br, bi, hr, hi, cbre[r:r + 1, :], cbim[r:r + 1, :])

        def bwd(k, carry, b=b):
            hr, hi = carry
            r = row(nlat, b, nlat - 1 - k)
            hbre[pl.ds(r, 1), :] = hr
            hbim[pl.ds(r, 1), :] = hi
            return step(br, bi, hr, hi, sbre[pl.ds(r, 1), :], sbim[pl.ds(r, 1), :])

        lax.fori_loop(0, nlat, bwd, (hr, hi))


def _s5_scan(dec, ctx_s, lat_s, bsz, tw=1024):
    gp = dec.shape[-1]
    rc, rl = ctx_s[0].shape[0], lat_s[0].shape[0]
    assert (rc // bsz) % 2 == 0 and (rl // bsz) % 2 == 0
    tw = min(tw, gp)
    cspec = pl.BlockSpec((rc, tw), lambda i: (0, i))
    lspec = pl.BlockSpec((rl, tw), lambda i: (0, i))
    out = jax.ShapeDtypeStruct((rl, gp), F32)
    return pl.pallas_call(
        functools.partial(_s5_scan_kernel, bsz),
        grid=(gp // tw,),
        in_specs=[pl.BlockSpec((4, tw), lambda i: (0, i))] + [cspec] * 4 + [lspec] * 4,
        out_specs=[lspec] * 4,
        out_shape=[out] * 4,
        compiler_params=_cparams("parallel"),
        name="s5_state_scan",
    )(dec, *ctx_s, *lat_s)


def _gelu_tanh(x):
    return 0.5 * x * (1.0 + jnp.tanh(math.sqrt(2.0 / math.pi) * (x + 0.044715 * (x * x * x))))


def _expand_toeplitz(k_ref, gi, sel_ref, lhs_ref, res_ref, w_ref, h, t):
    half = h // 2
    rows_per_hi = half * 8

    def fill(hi, carry):
        for hp in range(half):
            for par in range(2):
                v = k_ref[gi, pl.ds(hi * h + 2 * hp + par, 1), :]
                b = pltpu.roll(jnp.broadcast_to(v, (8, 2 * t)), 2 * t - 7, axis=1, stride=1, stride_axis=0)
                lhs_ref[pl.ds(pl.multiple_of(hi * rows_per_hi + hp * 8, 8), 8), par * 2 * t:(par + 1) * 2 * t] = b
        return carry

    lax.fori_loop(0, h, fill, 0)
    res_ref[...] = _dot(lhs_ref[...].astype(BF16), sel_ref[...])

    def shuffle(hi, carry):
        base = pl.multiple_of(hi * rows_per_hi, rows_per_hi)
        r_hi = res_ref[pl.ds(base, rows_per_hi), :]
        rows = [jnp.concatenate([r_hi[hp * 8:(hp + 1) * 8, q * 2 * t:(q + 1) * 2 * t] for hp in range(half)], axis=1)
                for q in range(t // 8)]
        w_ref[pl.ds(base, t), :] = jnp.concatenate(rows, axis=0).astype(BF16)
        return carry

    lax.fori_loop(0, h, shuffle, 0)


def _s5_out_kernel(z_ref, k_ref, sel_ref, d_ref, hfre, hfim, hbre, hbim, dsk_ref, o_ref, lhs_ref, res_ref, w_ref):
    h = z_ref.shape[0] // GPS
    npair = z_ref.shape[1]
    t = z_ref.shape[-1] // 2
    p = d_ref.shape[1] // 4
    lo = lax.broadcasted_iota(jnp.int32, z_ref.shape[1:], 1) < t
    for gi in range(GPS):
        _expand_toeplitz(k_ref, gi, sel_ref, lhs_ref, res_ref, w_ref, h, t)
        u = _gather_chunks(z_ref, gi, h)
        sl = slice(gi * p, (gi + 1) * p)
        hp = jnp.concatenate([hfre[:, sl], hfim[:, sl], hbre[:, sl], hbim[:, sl]], axis=-1).astype(BF16)
        y = _dot(u.astype(BF16), w_ref[...]) + _dot(hp, d_ref[gi])
        gl = _gelu_tanh(y + u * dsk_ref[gi])
        for k in range(0, h, 2):
            te = gl[0:npair, k * t:(k + 2) * t]
            to = gl[npair:2 * npair, k * t:(k + 2) * t]
            o_ref[gi * h + k] = jnp.where(lo, te, pltpu.roll(to, t, axis=1))
            o_ref[gi * h + k + 1] = jnp.where(lo, pltpu.roll(te, t, axis=1), to)


def _s5_out(z3, kall, sel, dm, hin, dsk):
    gh, npair, t2z = z3.shape
    nchunk, t = 2 * npair, t2z // 2
    g, hh, t2 = kall.shape
    h = gh // g
    ht = h * t
    assert hh == h * h and t2 == 2 * t and (h // 2) * 8 == t and t2 == LANES
    p4 = dm.shape[1]
    p = p4 // 4
    hspec = pl.BlockSpec((nchunk, GPS * p), lambda i: (0, i))
    return pl.pallas_call(
        _s5_out_kernel,
        grid=(g // GPS,),
        in_specs=[pl.BlockSpec((GPS * h, npair, t2z), lambda i: (i, 0, 0)),
                  pl.BlockSpec((GPS, hh, t2), lambda i: (i, 0, 0)),
                  pl.BlockSpec(sel.shape, lambda i: (0, 0)),
                  pl.BlockSpec((GPS, p4, ht), lambda i: (i, 0, 0))] + [hspec] * 4 +
                 [pl.BlockSpec((GPS, 1, ht), lambda i: (i, 0, 0))],
        out_specs=pl.BlockSpec((GPS * h, npair, t2z), lambda i: (i, 0, 0)),
        out_shape=jax.ShapeDtypeStruct((gh, npair, t2z), F32),
        scratch_shapes=[pltpu.VMEM((ht, 4 * t), F32), pltpu.VMEM((ht, ht), F32), pltpu.VMEM((ht, ht), BF16)],
        compiler_params=_cparams("parallel"),
        name="s5_chunk_out",
    )(z3, kall, sel, dm, *hin, dsk)


def _glu_kernel(gt_ref, wa_ref, wb_ref, ba_ref, bb_ref, o_ref, g_scr):
    @pl.when(pl.program_id(1) == 0)
    def _():
        for q in range(gt_ref.shape[1]):
            g_scr[:, q * LANES:(q + 1) * LANES] = gt_ref[:, q, :].astype(BF16)

    dn = (((0,), (0,)), ((), ()))
    gt = g_scr[...]
    a = lax.dot_general(gt, wa_ref[...], dn, preferred_element_type=F32) + ba_ref[...]
    b = lax.dot_general(gt, wb_ref[...], dn, preferred_element_type=F32) + bb_ref[...]
    o_ref[...] = (a * jax.nn.sigmoid(b)).astype(o_ref.dtype)


def _glu(g3, wa, wb, ba, bb, tm=1024, tn=1024):
    c, nq, _ = g3.shape
    n = nq * LANES
    co = wa.shape[-1]
    tm, tn = min(tm, n), min(tn, co)
    return pl.pallas_call(
        _glu_kernel,
        grid=(n // tm, co // tn),
        in_specs=[pl.BlockSpec((c, tm // LANES, LANES), lambda i, j: (0, i, 0)),
                  pl.BlockSpec((c, tn), lambda i, j: (0, j)),
                  pl.BlockSpec((c, tn), lambda i, j: (0, j)),
                  pl.BlockSpec((1, tn), lambda i, j: (0, j)),
                  pl.BlockSpec((1, tn), lambda i, j: (0, j))],
        out_specs=pl.BlockSpec((tm, tn), lambda i, j: (i, j)),
        out_shape=jax.ShapeDtypeStruct((n, co), BF16),
        scratch_shapes=[pltpu.VMEM((c, tm), BF16)],
        compiler_params=_cparams("parallel", "arbitrary"),
        name="s5_glu",
    )(g3, wa, wb, ba.reshape(1, co), bb.reshape(1, co))


def _out_proj_kernel(yf_ref, ys_ref, wf_ref, ws_ref, x_ref, g_ref, o_ref):
    acc = _dot(yf_ref[...], wf_ref[...]) + _dot(ys_ref[...], ws_ref[...])
    o_ref[...] = x_ref[...] + g_ref[...] * acc


def _out_proj(yf, ys, w_out, x2, gate, length, tm=1024, tn=1024):
    n, fw = yf.shape
    sw = ys.shape[-1]
    d = w_out.shape[-1]
    tm, tn = min(tm, length), min(tn, d)
    assert fw % sw == 0
    bsz = gate.shape[0]
    return pl.pallas_call(
        _out_proj_kernel,
        grid=(n // tm, d // tn),
        in_specs=[pl.BlockSpec((tm, fw), lambda i, j: (i, 0)),
                  pl.BlockSpec((tm, sw), lambda i, j: (i, 0)),
                  pl.BlockSpec((fw, tn), lambda i, j: (0, j)),
                  pl.BlockSpec((sw, tn), lambda i, j: (fw // sw, j)),
                  pl.BlockSpec((tm, tn), lambda i, j: (i, j)),
                  pl.BlockSpec((None, 1, tn), lambda i, j: ((i * tm) // length, 0, j))],
        out_specs=pl.BlockSpec((tm, tn), lambda i, j: (i, j)),
        out_shape=jax.ShapeDtypeStruct((n, d), F32),
        compiler_params=_cparams("parallel", "arbitrary"),
        name="out_proj_residual",
    )(yf, ys, w_out, w_out, x2, gate.reshape(bsz, 1, d))


def _ffn_kernel(h_ref, wg_ref, wu_ref, wd_ref, o_ref):
    j = pl.program_id(1)
    hh = h_ref[...]
    g = _dot(hh, wg_ref[...])
    u = _dot(hh, wu_ref[...])
    a = (g * jax.nn.sigmoid(g) * u).astype(BF16)
    d = o_ref.shape[-1]
    nc = min(FFN_DOWN_CHUNK, d)

    @pl.when(j == 0)
    def _():
        for c0 in range(0, d, nc):
            o_ref[:, c0:c0 + nc] = _dot(a, wd_ref[:, c0:c0 + nc])

    @pl.when(j > 0)
    def _():
        for c0 in range(0, d, nc):
            o_ref[:, c0:c0 + nc] += _dot(a, wd_ref[:, c0:c0 + nc])


FFN_DOWN_CHUNK = 512


def _ffn(h, wg, wu, wd, tm=1024, th=256):
    n, d = h.shape
    hid = wg.shape[-1]
    tm, th = min(tm, n), min(th, hid)
    assert hid % th == 0
    return pl.pallas_call(
        _ffn_kernel,
        grid=(n // tm, hid // th),
        in_specs=[pl.BlockSpec((tm, d), lambda i, j: (i, 0), pipeline_mode=pl.Buffered(1)),
                  pl.BlockSpec((d, th), lambda i, j: (0, j)),
                  pl.BlockSpec((d, th), lambda i, j: (0, j)),
                  pl.BlockSpec((th, d), lambda i, j: (j, 0))],
        out_specs=pl.BlockSpec((tm, d), lambda i, j: (i, 0), pipeline_mode=pl.Buffered(1)),
        out_shape=jax.ShapeDtypeStruct((n, d), F32),
        compiler_params=_cparams("parallel", "arbitrary"),
        name="swiglu_ffn",
    )(h, wg, wu, wd)


def _final_kernel(x_ref, f_ref, gate_ref, g_ref, o_ref):
    x = x_ref[...] + gate_ref[...] * f_ref[...]
    ms = jnp.mean(x * x, axis=-1, keepdims=True)
    o_ref[...] = x * lax.rsqrt(ms + EPS) * g_ref[...]


def _final(x1, f, gate, g, tm=256):
    bsz, length, d = x1.shape
    tm = min(tm, length)
    return pl.pallas_call(
        _final_kernel,
        grid=(bsz, length // tm),
        in_specs=[pl.BlockSpec((None, tm, d), lambda b, i: (b, i, 0)),
                  pl.BlockSpec((None, tm, d), lambda b, i: (b, i, 0)),
                  pl.BlockSpec((None, 1, d), lambda b, i: (b, 0, 0)),
                  pl.BlockSpec((1, d), lambda b, i: (0, 0))],
        out_specs=pl.BlockSpec((None, tm, d), lambda b, i: (b, i, 0)),
        out_shape=jax.ShapeDtypeStruct((bsz, length, d), F32),
        compiler_params=_cparams("parallel", "parallel"),
        name="residual_final_norm",
    )(x1, f, gate.reshape(bsz, 1, d), g.reshape(1, d))


def _dft_split(length):
    n2 = 64
    while (length // n2) % PACK_ROWS:
        n2 //= 2
    assert n2 >= 8 and length % n2 == 0
    return n2


def kernel(x, c, ctx, c_ctx, ada_w, ada_b, norm1_g, norm2_g, w_in, w_out, fourier_w, s5_lam_re, s5_lam_im, s5_log_dt, s5_b_re, s5_b_im, s5_c_re, s5_c_im, s5_d, glu_w_a, glu_b_a, glu_w_b, glu_b_b, ffn_w_gate, ffn_w_up, ffn_w_down, final_g):
    bsz, length, d = x.shape
    depth = ada_w.shape[0]
    assert depth == 1, "single-layer block"
    lyr = 0
    heads, hd, _ = fourier_w.shape[1:]
    fw = heads * hd
    _, g, p, hgrp = s5_b_re.shape[1:]
    sw = g * hgrp
    assert w_in.shape[-1] == fw + sw and length % (2 * CHUNK) == 0 and ctx.shape[1] % (2 * CHUNK) == 0
    n = bsz * length

    a8 = jnp.zeros((8, d), F32).at[:bsz].set(c.astype(F32)).at[bsz].set(c_ctx.astype(F32))
    mods = _ada(jnp.concatenate([a8, a8], axis=0), ada_w[lyr], ada_b[lyr]).reshape(8, N_MOD, d)
    sh1, sc1, g1, sh2, sc2, g2 = (mods[:bsz, i] for i in range(N_MOD))
    csh1, csc1 = mods[bsz:bsz + 1, 0], mods[bsz:bsz + 1, 1]

    w_in_b = w_in[lyr].astype(BF16)
    w_s_t = jnp.transpose(w_in[lyr][:, fw:]).astype(BF16)
    ang = (2.0 * np.pi / hd) * ((np.arange(hd)[:, None] * np.arange(hd)[None, :]) % hd).astype(np.float64)
    cd = jnp.asarray(np.cos(ang) / math.sqrt(hd), F32)
    sd = jnp.asarray(np.sin(ang) / math.sqrt(hd), F32)
    csd = jnp.broadcast_to(jnp.stack([cd, sd])[:, None], (2, heads, hd, hd)).reshape(2 * heads, hd, hd)
    wf2 = jnp.concatenate([fourier_w[lyr], fourier_w[lyr]], axis=0).astype(F32)
    folded = _fold(csd, wf2)
    wcs = jnp.concatenate([folded[:heads], folded[heads:]], axis=-1).astype(BF16)
    kall, e_mat, d_mat, dec = _s5_tables(s5_lam_re[lyr], s5_lam_im[lyr], s5_log_dt[lyr], s5_b_re[lyr],
                                         s5_b_im[lyr], s5_c_re[lyr], s5_c_im[lyr])
    sel = _toeplitz_select(CHUNK)
    dsk = jnp.repeat(s5_d[lyr].astype(F32).reshape(g, 1, hgrp), CHUNK, axis=-1)

    hc = _norm_mod(ctx, norm1_g[lyr], csh1, csc1)
    nctx_tok = bsz * ctx.shape[1]
    ctx_s = _s5_states(_proj_t(w_s_t, hc.reshape(nctx_tok, d)), e_mat)

    hm = _norm_mod(x, norm1_g[lyr], sh1, sc1).reshape(n, d)
    n2 = _dft_split(length)
    pc4, ps4 = _four_in(hm, w_in_b, wcs, bsz, length, n2)
    g_tab, cs_tab = _dft_tables(length, n2)
    y_four = _dft(pc4, ps4, g_tab, cs_tab).reshape(n, fw)

    z3 = _proj_t(w_s_t, hm)
    lat_s = _s5_states(z3, e_mat)
    h_in = _s5_scan(dec, ctx_s, lat_s, bsz)
    g3 = _s5_out(z3, kall, sel, d_mat, h_in, dsk)
    y_s = _glu(g3, glu_w_a[lyr].astype(BF16), glu_w_b[lyr].astype(BF16), glu_b_a[lyr], glu_b_b[lyr])

    x1 = _out_proj(y_four, y_s, w_out[lyr].astype(BF16), x.reshape(n, d), g1, length)

    hm2 = _norm_mod(x1.reshape(bsz, length, d), norm2_g[lyr], sh2, sc2).reshape(n, d)
    f = _ffn(hm2, ffn_w_gate[lyr].astype(BF16), ffn_w_up[lyr].astype(BF16), ffn_w_down[lyr].astype(BF16))
    return _final(x1.reshape(bsz, length, d), f.reshape(bsz, length, d), g2, final_g)
```

```python
import functools
import math

import numpy as np
import jax
import jax.numpy as jnp
from jax import lax
from jax.experimental import pallas as pl
from jax.experimental.pallas import tpu as pltpu

F32 = jnp.float32
BF16 = jnp.bfloat16
EPS = 1e-6
CHUNK = 64
N_MOD = 6
V7X_VMEM_LIMIT = 56 * 1024 * 1024
HI = lax.Precision.HIGHEST


def _cparams(*sem, vmem=V7X_VMEM_LIMIT):
    return pltpu.CompilerParams(dimension_semantics=sem, vmem_limit_bytes=vmem)


def _dot(a, b):
    return jnp.dot(a, b, preferred_element_type=F32)


def _ada_kernel(a_ref, w_ref, b_ref, o_ref):
    a = a_ref[...]
    s = a * jax.nn.sigmoid(a)
    s_hi = s.astype(BF16).astype(F32)
    row = lax.broadcasted_iota(jnp.int32, s.shape, 0)
    lhs = jnp.where(row < 8, s_hi, s - s_hi).astype(BF16)
    w = w_ref[...]
    w_hi = w.astype(BF16)
    w_lo = (w - w_hi.astype(F32)).astype(BF16)
    r = _dot(lhs, w_hi) + _dot(lhs, w_lo)
    o_ref[...] = r[0:8] + r[8:16] + b_ref[...]


def _ada(a16, w, b, tn=1024):
    d, n = w.shape
    tn = min(tn, n)
    return pl.pallas_call(
        _ada_kernel,
        grid=(n // tn,),
        in_specs=[pl.BlockSpec((16, d), lambda j: (0, 0)),
                  pl.BlockSpec((d, tn), lambda j: (0, j)),
                  pl.BlockSpec((1, tn), lambda j: (0, j))],
        out_specs=pl.BlockSpec((8, tn), lambda j: (0, j)),
        out_shape=jax.ShapeDtypeStruct((8, n), F32),
        compiler_params=_cparams("parallel"),
        name="ada_matvec",
    )(a16, w, b.reshape(1, n))


def _norm_mod_kernel(x_ref, g_ref, sh_ref, sc_ref, o_ref):
    x = x_ref[...]
    ms = jnp.mean(x * x, axis=-1, keepdims=True)
    y = x * lax.rsqrt(ms + EPS) * g_ref[...]
    o_ref[...] = (y * (1.0 + sc_ref[...]) + sh_ref[...]).astype(o_ref.dtype)


def _norm_mod(x, g, sh, sc, tm=256):
    bsz, length, d = x.shape
    tm = min(tm, length)
    bm = sh.shape[0]
    mod_map = (lambda b, i: (b, 0, 0)) if bm == bsz else (lambda b, i: (0, 0, 0))
    return pl.pallas_call(
        _norm_mod_kernel,
        grid=(bsz, length // tm),
        in_specs=[pl.BlockSpec((None, tm, d), lambda b, i: (b, i, 0)),
                  pl.BlockSpec((1, d), lambda b, i: (0, 0)),
                  pl.BlockSpec((None, 1, d), mod_map),
                  pl.BlockSpec((None, 1, d), mod_map)],
        out_specs=pl.BlockSpec((None, tm, d), lambda b, i: (b, i, 0)),
        out_shape=jax.ShapeDtypeStruct((bsz, length, d), BF16),
        compiler_params=_cparams("parallel", "parallel"),
        name="norm_mod",
    )(x, g.reshape(1, d), sh.reshape(bm, 1, d), sc.reshape(bm, 1, d))


def _fold_kernel(a_ref, b_ref, o_ref):
    o_ref[...] = jnp.dot(a_ref[...], b_ref[...], preferred_element_type=F32, precision=HI)


def _fold(a, b):
    hh, m, k = a.shape
    n = b.shape[-1]
    return pl.pallas_call(
        _fold_kernel,
        grid=(hh,),
        in_specs=[pl.BlockSpec((None, m, k), lambda h: (h, 0, 0)),
                  pl.BlockSpec((None, k, n), lambda h: (h, 0, 0))],
        out_specs=pl.BlockSpec((None, m, n), lambda h: (h, 0, 0)),
        out_shape=jax.ShapeDtypeStruct((hh, m, n), F32),
        compiler_params=_cparams("parallel"),
        name="weight_fold",
    )(a, b)


LANES = 128
PACK_ROWS = 16
PITCH_PAD = 8


def _four_in_kernel(h_ref, w_ref, wcs_ref, pc_ref, ps_ref, p_scr):
    n2, tn1, hd = pc_ref.shape
    pitch = n2 + PITCH_PAD
    nq = hd // LANES
    z = _dot(h_ref[...], w_ref[...]).astype(BF16)
    p = _dot(z, wcs_ref[...])
    for q in range(2 * nq):
        for i1 in range(tn1):
            p_scr[q, i1 * pitch:i1 * pitch + n2, :] = p[i1 * n2:(i1 + 1) * n2, q * LANES:(q + 1) * LANES]

    def emit(j2, carry):
        for q in range(nq):
            pc_ref[j2, :, q * LANES:(q + 1) * LANES] = p_scr[q, pl.ds(j2, tn1, stride=pitch), :].astype(BF16)
            ps_ref[j2, :, q * LANES:(q + 1) * LANES] = p_scr[nq + q, pl.ds(j2, tn1, stride=pitch), :].astype(BF16)
        return carry

    lax.fori_loop(0, n2, emit, 0)


def _four_in(h, w_f, wcs, bsz, length, n2):
    n, d = h.shape
    heads, hd, _ = wcs.shape
    n1 = length // n2
    tn1 = PACK_ROWS
    tm = tn1 * n2
    tpb = length // tm
    fw = heads * hd
    out = jax.ShapeDtypeStruct((bsz, n2, n1, fw), BF16)
    ospec = pl.BlockSpec((None, n2, tn1, hd), lambda i, j: (i // tpb, 0, i % tpb, j))
    return pl.pallas_call(
        _four_in_kernel,
        grid=(n // tm, heads),
        in_specs=[pl.BlockSpec((tm, d), lambda i, j: (i, 0)),
                  pl.BlockSpec((d, hd), lambda i, j: (0, j)),
                  pl.BlockSpec((None, hd, 2 * hd), lambda i, j: (j, 0, 0))],
        out_specs=[ospec, ospec],
        out_shape=[out, out],
        scratch_shapes=[pltpu.VMEM((2 * hd // LANES, tn1 * (n2 + PITCH_PAD), LANES), F32)],
        compiler_params=_cparams("parallel", "arbitrary"),
        name="fourier_in_proj",
    )(h, w_f, wcs)


def _dft_tables(length, n2):
    n1 = length // n2
    k1 = np.arange(n1)[:, None, None]
    i1 = np.arange(n1)[None, :, None]
    i2 = np.arange(n2)[None, None, :]
    phase = (k1 * (n2 * i1 + i2)) % length
    phi = (2.0 * np.pi / length) * phase.astype(np.float64)
    c = np.cos(phi).transpose(2, 0, 1) / math.sqrt(n1)
    s = np.sin(phi).transpose(2, 0, 1) / math.sqrt(n1)
    g = np.concatenate([np.concatenate([c, -s], axis=2),
                        np.concatenate([-s, -c], axis=2)], axis=1)
    k2 = np.arange(n2)[:, None]
    j2 = np.arange(n2)[None, :]
    th = (2.0 * np.pi / n2) * ((k2 * j2) % n2).astype(np.float64)
    cs = np.concatenate([np.cos(th), np.sin(th)], axis=1) / math.sqrt(n2)
    return jnp.asarray(g, dtype=BF16), jnp.asarray(cs, dtype=BF16)


def _dft_kernel(g_ref, cs_ref, pc_ref, ps_ref, o_ref, t_scr, y_scr):
    n2, n1, _ = pc_ref.shape
    m = 2 * n1
    p1 = m + PITCH_PAD
    p2 = n1 + PITCH_PAD

    def stage1(j2, carry):
        rhs = jnp.concatenate([pc_ref[j2], ps_ref[j2]], axis=0)
        t_scr[pl.ds(pl.multiple_of(j2 * p1, 8), m), :] = _dot(g_ref[j2], rhs)
        return carry

    lax.fori_loop(0, n2, stage1, 0, unroll=8)

    cs = cs_ref[...]

    def stage2(i, carry):
        k1 = 2 * i
        cols = []
        for dk in range(2):
            re = t_scr[pl.ds(k1 + dk, n2, stride=p1), :]
            im = t_scr[pl.ds(n1 + k1 + dk, n2, stride=p1), :]
            cols.append(jnp.concatenate([re, im], axis=0).astype(BF16))
        res = _dot(cs, jnp.concatenate(cols, axis=1))
        for dk in range(2):
            y_scr[pl.ds(k1 + dk, n2, stride=p2), :] = res[:, dk * LANES:(dk + 1) * LANES]
        return carry

    lax.fori_loop(0, n1 // 2, stage2, 0, unroll=8)
    for k2 in range(n2):
        o_ref[k2 * n1:(k2 + 1) * n1, :] = y_scr[k2 * p2:k2 * p2 + n1, :].astype(o_ref.dtype)


def _dft(pc4, ps4, g, cs):
    bsz, n2, n1, w = pc4.shape
    m = 2 * n1
    ispec = pl.BlockSpec((None, n2, n1, LANES), lambda b, j: (b, 0, 0, j))
    return pl.pallas_call(
        _dft_kernel,
        grid=(bsz, w // LANES),
        in_specs=[pl.BlockSpec((n2, m, m), lambda b, j: (0, 0, 0), pipeline_mode=pl.Buffered(1)),
                  pl.BlockSpec((n2, 2 * n2), lambda b, j: (0, 0)),
                  ispec, ispec],
        out_specs=pl.BlockSpec((None, n2 * n1, LANES), lambda b, j: (b, 0, j)),
        out_shape=jax.ShapeDtypeStruct((bsz, n2 * n1, w), BF16),
        scratch_shapes=[pltpu.VMEM((n2 * (m + PITCH_PAD), LANES), F32),
                        pltpu.VMEM((n2 * (n1 + PITCH_PAD), LANES), F32)],
        compiler_params=_cparams("parallel", "parallel"),
        name="position_dft",
    )(g, cs, pc4, ps4)


def _nt_kernel(w_ref, h_ref, o_ref, stage_ref):
    acc = lax.dot_general(w_ref[...], h_ref[...], (((1,), (1,)), ((), ())), preferred_element_type=F32)
    rows, tq, _ = o_ref.shape
    if tq % 8 == 0:
        for q in range(tq):
            stage_ref[pl.ds(q, rows, stride=tq), :] = acc[:, q * LANES:(q + 1) * LANES]
        o_ref[...] = stage_ref[...].reshape(rows, tq, LANES)
    else:
        for q in range(tq):
            o_ref[:, q, :] = acc[:, q * LANES:(q + 1) * LANES]


def _proj_t(w_t, h, tmc=1024, tn=1024):
    c, d = w_t.shape
    n = h.shape[0]
    tmc, tn = min(tmc, c), min(tn, n)
    return pl.pallas_call(
        _nt_kernel,
        grid=(n // tn, c // tmc),
        in_specs=[pl.BlockSpec((tmc, d), lambda i, j: (j, 0)),
                  pl.BlockSpec((tn, d), lambda i, j: (i, 0))],
        out_specs=pl.BlockSpec((tmc, tn // LANES, LANES), lambda i, j: (j, i, 0)),
        out_shape=jax.ShapeDtypeStruct((c, n // LANES, LANES), F32),
        scratch_shapes=[pltpu.VMEM((tmc * (tn // LANES), LANES), F32)],
        compiler_params=_cparams("parallel", "arbitrary"),
        name="s5_in_proj_t",
    )(w_t, h)


def _s5_tables(lam_re, lam_im, log_dt, b_re, b_im, c_re, c_im):
    t = CHUNK
    _, g, p = lam_re.shape
    h = b_re.shape[-1]
    dt = jnp.exp(log_dt.astype(F32))[..., None]
    lr, li = jnp.minimum(lam_re.astype(F32), -1e-4), lam_im.astype(F32)
    ar, ai = lr * dt, li * dt

    def powers(d, first, step):
        ks = first + step * jnp.arange(t, dtype=F32)
        mag = jnp.exp(ar[d][..., None] * ks)
        ang = ai[d][..., None] * ks
        return mag * jnp.cos(ang), mag * jnp.sin(ang)

    lbm = jnp.exp(ar)
    nr, ni = lbm * jnp.cos(ai) - 1.0, lbm * jnp.sin(ai)
    den = lr * lr + li * li
    qr, qi = (nr * lr + ni * li) / den, (ni * lr - nr * li) / den
    br, bi = b_re.astype(F32), b_im.astype(F32)
    bbr = qr[..., None] * br - qi[..., None] * bi
    bbi = qr[..., None] * bi + qi[..., None] * br
    cr, ci = c_re.astype(F32), c_im.astype(F32)

    crt, cit = jnp.swapaxes(cr, -1, -2)[..., None, :], jnp.swapaxes(ci, -1, -2)[..., None, :]
    mr = (bbr[..., None] * crt - bbi[..., None] * cit).reshape(2, g, p, h * h)
    mi = (bbr[..., None] * cit + bbi[..., None] * crt).reshape(2, g, p, h * h)
    def lag_kernel(d, first, step):
        wr, wi = powers(d, first, step)
        return (jnp.einsum('gpx,gpt->gxt', mr[d], wr, precision=HI)
                - jnp.einsum('gpx,gpt->gxt', mi[d], wi, precision=HI))

    kf = lag_kernel(0, 0.0, 1.0)
    kbr = lag_kernel(1, t - 1.0, -1.0)
    kall = jnp.concatenate([kbr[..., :t - 1], kf[..., :1] + kbr[..., t - 1:], kf[..., 1:],
                            jnp.zeros_like(kf[..., :1])], axis=-1)

    def outer_e(wr, wi, d):
        wr_, wi_ = jnp.swapaxes(wr, 1, 2)[:, None], jnp.swapaxes(wi, 1, 2)[:, None]
        b_r, b_i = jnp.swapaxes(bbr[d], 1, 2)[:, :, None], jnp.swapaxes(bbi[d], 1, 2)[:, :, None]
        return wr_ * b_r - wi_ * b_i, wr_ * b_i + wi_ * b_r
    efr, efi = outer_e(*powers(0, t - 1.0, -1.0), 0)
    ebr, ebi = outer_e(*powers(1, 0.0, 1.0), 1)
    e = jnp.concatenate([efr, efi, ebr, ebi], axis=-1).astype(BF16).reshape(g, h * t, 4 * p)

    def outer_d(wr, wi, d):
        wr_, wi_ = wr[:, :, None], wi[:, :, None]
        c_r, c_i = jnp.swapaxes(cr[d], 1, 2)[..., None], jnp.swapaxes(ci[d], 1, 2)[..., None]
        return c_r * wr_ - c_i * wi_, -(c_r * wi_ + c_i * wr_)
    dfr, dfi = outer_d(*powers(0, 1.0, 1.0), 0)
    dbr, dbi = outer_d(*powers(1, float(t), -1.0), 1)
    dm = jnp.concatenate([dfr, dfi, dbr, dbi], axis=1).astype(BF16).reshape(g, 4 * p, h * t)

    mag_t = jnp.exp(ar * float(t))
    dec = jnp.stack([mag_t[0] * jnp.cos(ai[0] * t), mag_t[0] * jnp.sin(ai[0] * t),
                     mag_t[1] * jnp.cos(ai[1] * t), mag_t[1] * jnp.sin(ai[1] * t)]).reshape(4, g * p)
    return kall, e, dm, dec


def _toeplitz_select(t):
    sel = np.zeros((4 * t, (t // 8) * 2 * t), np.float32)
    for q in range(t // 8):
        for tt in range(t):
            j = tt - 8 * q + t - 8
            sel[j, q * 2 * t + tt] = 1.0
            sel[2 * t + j, q * 2 * t + t + tt] = 1.0
    return jnp.asarray(sel, dtype=BF16)


GPS = 2


def _gather_chunks(z_ref, gi, h):
    t = z_ref.shape[-1] // 2
    lo = lax.broadcasted_iota(jnp.int32, z_ref.shape[1:], 1) < t
    ev, od = [], []
    for k in range(0, h, 2):
        za, zb = z_ref[gi * h + k], z_ref[gi * h + k + 1]
        ev.append(jnp.where(lo, za, pltpu.roll(zb, t, axis=1)))
        od.append(jnp.where(lo, pltpu.roll(za, t, axis=1), zb))
    return jnp.concatenate([jnp.concatenate(ev, axis=1), jnp.concatenate(od, axis=1)], axis=0)


def _s5_state_kernel(z_ref, e_ref, fre_ref, fim_ref, bre_ref, bim_ref):
    h = z_ref.shape[0] // GPS
    p = e_ref.shape[-1] // 4
    outs = (fre_ref, fim_ref, bre_ref, bim_ref)
    for gi in range(GPS):
        a2 = _gather_chunks(z_ref, gi, h).astype(BF16)
        s = _dot(a2, e_ref[gi])
        for k in range(4):
            outs[k][:, gi * p:(gi + 1) * p] = s[:, k * p:(k + 1) * p]


def _s5_states(z3, e):
    gh, npair, t2 = z3.shape
    nchunk, t = 2 * npair, t2 // 2
    g, ht, p4 = e.shape
    h, p = gh // g, p4 // 4
    out = jax.ShapeDtypeStruct((nchunk, g * p), F32)
    ospec = pl.BlockSpec((nchunk, GPS * p), lambda i: (0, i))
    return pl.pallas_call(
        _s5_state_kernel,
        grid=(g // GPS,),
        in_specs=[pl.BlockSpec((GPS * h, npair, t2), lambda i: (i, 0, 0)),
                  pl.BlockSpec((GPS, ht, p4), lambda i: (i, 0, 0))],
        out_specs=[ospec] * 4,
        out_shape=[out] * 4,
        compiler_params=_cparams("parallel"),
        name="s5_chunk_states",
    )(z3, e)


def _s5_scan_kernel(bsz, dec_ref, cfre, cfim, cbre, cbim, sfre, sfim, sbre, sbim,
                    hfre, hfim, hbre, hbim):
    nctx = cfre.shape[0] // bsz
    nlat = sfre.shape[0] // bsz
    width = dec_ref.shape[-1]
    fr, fi = dec_ref[0:1, :], dec_ref[1:2, :]
    br, bi = dec_ref[2:3, :], dec_ref[3:4, :]
    zero = jnp.zeros((1, width), F32)

    def step(ar, ai, hr, hi, sr, si):
        return ar * hr - ai * hi + sr, ar * hi + ai * hr + si

    def row(nchunk, b, j):
        return (j % 2) * (bsz * nchunk // 2) + b * (nchunk // 2) + j // 2

    for b in range(bsz):
        hr, hi = zero, zero
        for j in range(nctx):
            r = row(nctx, b, j)
            hr, hi = step(fr, fi, hr, hi, cfre[r:r + 1, :], cfim[r:r + 1, :])

        def fwd(j, carry, b=b):
            hr, hi = carry
            r = row(nlat, b, j)
            hfre[pl.ds(r, 1), :] = hr
            hfim[pl.ds(r, 1), :] = hi
            return step(fr, fi, hr, hi, sfre[pl.ds(r, 1), :], sfim[pl.ds(r, 1), :])

        lax.fori_loop(0, nlat, fwd, (hr, hi))

        hr, hi = zero, zero
        for j in range(nctx - 1, -1, -1):
            r = row(nctx, b, j)
            hr, hi = step(br, bi, hr, hi, cbre[r:r + 1, :], cbim[r:r + 1, :])

        def bwd(k, carry, b=b):
            hr, hi = carry
            r = row(nlat, b, nlat - 1 - k)
            hbre[pl.ds(r, 1), :] = hr
            hbim[pl.ds(r, 1), :] = hi
            return step(br, bi, hr, hi, sbre[pl.ds(r, 1), :], sbim[pl.ds(r, 1), :])

        lax.fori_loop(0, nlat, bwd, (hr, hi))


def _s5_scan(dec, ctx_s, lat_s, bsz, tw=1024):
    gp = dec.shape[-1]
    rc, rl = ctx_s[0].shape[0], lat_s[0].shape[0]
    assert (rc // bsz) % 2 == 0 and (rl // bsz) % 2 == 0
    tw = min(tw, gp)
    cspec = pl.BlockSpec((rc, tw), lambda i: (0, i))
    lspec = pl.BlockSpec((rl, tw), lambda i: (0, i))
    out = jax.ShapeDtypeStruct((rl, gp), F32)
    return pl.pallas_call(
        functools.partial(_s5_scan_kernel, bsz),
        grid=(gp // tw,),
        in_specs=[pl.BlockSpec((4, tw), lambda i: (0, i))] + [cspec] * 4 + [lspec] * 4,
        out_specs=[lspec] * 4,
        out_shape=[out] * 4,
        compiler_params=_cparams("parallel"),
        name="s5_state_scan",
    )(dec, *ctx_s, *lat_s)


def _gelu_tanh(x):
    return 0.5 * x * (1.0 + jnp.tanh(math.sqrt(2.0 / math.pi) * (x + 0.044715 * (x * x * x))))


def _expand_toeplitz(k_ref, gi, sel_ref, lhs_ref, res_ref, w_ref, h, t):
    half = h // 2
    rows_per_hi = half * 8

    def fill(hi, carry):
        for hp in range(half):
            for par in range(2):
                v = k_ref[gi, pl.ds(hi * h + 2 * hp + par, 1), :]
                b = pltpu.roll(jnp.broadcast_to(v, (8, 2 * t)), 2 * t - 7, axis=1, stride=1, stride_axis=0)
                lhs_ref[pl.ds(pl.multiple_of(hi * rows_per_hi + hp * 8, 8), 8), par * 2 * t:(par + 1) * 2 * t] = b
        return carry

    lax.fori_loop(0, h, fill, 0, unroll=4)
    res_ref[...] = _dot(lhs_ref[...].astype(BF16), sel_ref[...])

    def shuffle(hi, carry):
        base = pl.multiple_of(hi * rows_per_hi, rows_per_hi)
        r_hi = res_ref[pl.ds(base, rows_per_hi), :]
        rows = [jnp.concatenate([r_hi[hp * 8:(hp + 1) * 8, q * 2 * t:(q + 1) * 2 * t] for hp in range(half)], axis=1)
                for q in range(t // 8)]
        w_ref[pl.ds(base, t), :] = jnp.concatenate(rows, axis=0).astype(BF16)
        return carry

    lax.fori_loop(0, h, shuffle, 0)


def _s5_out_kernel(z_ref, k_ref, sel_ref, d_ref, hfre, hfim, hbre, hbim, dsk_ref, o_ref, lhs_ref, res_ref, w_ref):
    h = z_ref.shape[0] // GPS
    npair = z_ref.shape[1]
    t = z_ref.shape[-1] // 2
    p = d_ref.shape[1] // 4
    lo = lax.broadcasted_iota(jnp.int32, z_ref.shape[1:], 1) < t
    for gi in range(GPS):
        _expand_toeplitz(k_ref, gi, sel_ref, lhs_ref, res_ref, w_ref, h, t)
        u = _gather_chunks(z_ref, gi, h)
        sl = slice(gi * p, (gi + 1) * p)
        hp = jnp.concatenate([hfre[:, sl], hfim[:, sl], hbre[:, sl], hbim[:, sl]], axis=-1).astype(BF16)
        y = _dot(u.astype(BF16), w_ref[...]) + _dot(hp, d_ref[gi])
        gl = _gelu_tanh(y + u * dsk_ref[gi])
        for k in range(0, h, 2):
            te = gl[0:npair, k * t:(k + 2) * t]
            to = gl[npair:2 * npair, k * t:(k + 2) * t]
            o_ref[gi * h + k] = jnp.where(lo, te, pltpu.roll(to, t, axis=1))
            o_ref[gi * h + k + 1] = jnp.where(lo, pltpu.roll(te, t, axis=1), to)


def _s5_out(z3, kall, sel, dm, hin, dsk):
    gh, npair, t2z = z3.shape
    nchunk, t = 2 * npair, t2z // 2
    g, hh, t2 = kall.shape
    h = gh // g
    ht = h * t
    assert hh == h * h and t2 == 2 * t and (h // 2) * 8 == t and t2 == LANES
    p4 = dm.shape[1]
    p = p4 // 4
    hspec = pl.BlockSpec((nchunk, GPS * p), lambda i: (0, i))
    return pl.pallas_call(
        _s5_out_kernel,
        grid=(g // GPS,),
        in_specs=[pl.BlockSpec((GPS * h, npair, t2z), lambda i: (i, 0, 0)),
                  pl.BlockSpec((GPS, hh, t2), lambda i: (i, 0, 0)),
                  pl.BlockSpec(sel.shape, lambda i: (0, 0)),
                  pl.BlockSpec((GPS, p4, ht), lambda i: (i, 0, 0))] + [hspec] * 4 +
                 [pl.BlockSpec((GPS, 1, ht), lambda i: (i, 0, 0))],
        out_specs=pl.BlockSpec((GPS * h, npair, t2z), lambda i: (i, 0, 0)),
        out_shape=jax.ShapeDtypeStruct((gh, npair, t2z), F32),
        scratch_shapes=[pltpu.VMEM((ht, 4 * t), F32), pltpu.VMEM((ht, ht), F32), pltpu.VMEM((ht, ht), BF16)],
        compiler_params=_cparams("parallel"),
        name="s5_chunk_out",
    )(z3, kall, sel, dm, *hin, dsk)


def _glu_kernel(gt_ref, wa_ref, wb_ref, ba_ref, bb_ref, o_ref, g_scr, stage_ref):
    @pl.when(pl.program_id(1) == 0)
    def _():
        c, tq, _ = gt_ref.shape
        half = stage_ref.shape[0] // tq
        for c0 in range(0, c, half):
            stage_ref[...] = gt_ref[c0:c0 + half].reshape(half * tq, LANES)
            for q in range(tq):
                g_scr[c0:c0 + half, q * LANES:(q + 1) * LANES] = stage_ref[pl.ds(q, half, stride=tq), :].astype(BF16)

    dn = (((0,), (0,)), ((), ()))
    gt = g_scr[...]
    a = lax.dot_general(gt, wa_ref[...], dn, preferred_element_type=F32) + ba_ref[...]
    b = lax.dot_general(gt, wb_ref[...], dn, preferred_element_type=F32) + bb_ref[...]
    o_ref[...] = (a * jax.nn.sigmoid(b)).astype(o_ref.dtype)


def _glu(g3, wa, wb, ba, bb, tm=1024, tn=512):
    c, nq, _ = g3.shape
    n = nq * LANES
    co = wa.shape[-1]
    tm, tn = min(tm, n), min(tn, co)
    stage_rows = (c // 2) * (tm // LANES)
    return pl.pallas_call(
        _glu_kernel,
        grid=(n // tm, co // tn),
        in_specs=[pl.BlockSpec((c, tm // LANES, LANES), lambda i, j: (0, i, 0)),
                  pl.BlockSpec((c, tn), lambda i, j: (0, j)),
                  pl.BlockSpec((c, tn), lambda i, j: (0, j)),
                  pl.BlockSpec((1, tn), lambda i, j: (0, j)),
                  pl.BlockSpec((1, tn), lambda i, j: (0, j))],
        out_specs=pl.BlockSpec((tm, tn), lambda i, j: (i, j)),
        out_shape=jax.ShapeDtypeStruct((n, co), BF16),
        scratch_shapes=[pltpu.VMEM((c, tm), BF16), pltpu.VMEM((stage_rows, LANES), F32)],
        compiler_params=_cparams("parallel", "arbitrary"),
        name="s5_glu",
    )(g3, wa, wb, ba.reshape(1, co), bb.reshape(1, co))


def _out_proj_kernel(yf_ref, ys_ref, wf_ref, ws_ref, x_ref, g_ref, o_ref):
    acc = _dot(yf_ref[...], wf_ref[...]) + _dot(ys_ref[...], ws_ref[...])
    o_ref[...] = x_ref[...] + g_ref[...] * acc


def _out_proj(yf, ys, w_out, x2, gate, length, tm=1024, tn=1024):
    n, fw = yf.shape
    sw = ys.shape[-1]
    d = w_out.shape[-1]
    tm, tn = min(tm, length), min(tn, d)
    assert fw % sw == 0
    bsz = gate.shape[0]
    return pl.pallas_call(
        _out_proj_kernel,
        grid=(n // tm, d // tn),
        in_specs=[pl.BlockSpec((tm, fw), lambda i, j: (i, 0)),
                  pl.BlockSpec((tm, sw), lambda i, j: (i, 0)),
                  pl.BlockSpec((fw, tn), lambda i, j: (0, j)),
                  pl.BlockSpec((sw, tn), lambda i, j: (fw // sw, j)),
                  pl.BlockSpec((tm, tn), lambda i, j: (i, j)),
                  pl.BlockSpec((None, 1, tn), lambda i, j: ((i * tm) // length, 0, j))],
        out_specs=pl.BlockSpec((tm, tn), lambda i, j: (i, j)),
        out_shape=jax.ShapeDtypeStruct((n, d), F32),
        compiler_params=_cparams("parallel", "arbitrary"),
        name="out_proj_residual",
    )(yf, ys, w_out, w_out, x2, gate.reshape(bsz, 1, d))


def _ffn_hidden_start(j, th, hid):
    return pl.multiple_of(jnp.minimum(j * th, hid - th), math.gcd(th, hid - th))


def _ffn_kernel(hid, h_ref, wg_ref, wu_ref, wd_ref, o_ref):
    j = pl.program_id(1)
    th = wg_ref.shape[-1]
    hh = h_ref[...]
    g = _dot(hh, wg_ref[...])
    u = _dot(hh, wu_ref[...])
    unit = _ffn_hidden_start(j, th, hid) + lax.broadcasted_iota(jnp.int32, (1, th), 1)
    a = jnp.where(unit >= j * th, g * jax.nn.sigmoid(g) * u, 0.0).astype(BF16)
    d = o_ref.shape[-1]
    nc = min(FFN_DOWN_CHUNK, d)

    @pl.when(j == 0)
    def _():
        for c0 in range(0, d, nc):
            o_ref[:, c0:c0 + nc] = _dot(a, wd_ref[:, c0:c0 + nc])

    @pl.when(j > 0)
    def _():
        for c0 in range(0, d, nc):
            o_ref[:, c0:c0 + nc] += _dot(a, wd_ref[:, c0:c0 + nc])


FFN_DOWN_CHUNK = 512


FFN_TH = 512
FFN_VMEM_LIMIT = 60 * 1024 * 1024


def _ffn(h, wg, wu, wd, tm=1024):
    n, d = h.shape
    hid = wg.shape[-1]
    tm, th = min(tm, n), min(FFN_TH, hid)
    col_map = lambda i, j: (0, _ffn_hidden_start(j, th, hid))
    return pl.pallas_call(
        functools.partial(_ffn_kernel, hid),
        grid=(n // tm, pl.cdiv(hid, th)),
        in_specs=[pl.BlockSpec((tm, d), lambda i, j: (i, 0), pipeline_mode=pl.Buffered(1)),
                  pl.BlockSpec((pl.Element(d), pl.Element(th)), col_map),
                  pl.BlockSpec((pl.Element(d), pl.Element(th)), col_map),
                  pl.BlockSpec((pl.Element(th), pl.Element(d)), lambda i, j: (_ffn_hidden_start(j, th, hid), 0))],
        out_specs=pl.BlockSpec((tm, d), lambda i, j: (i, 0), pipeline_mode=pl.Buffered(1)),
        out_shape=jax.ShapeDtypeStruct((n, d), F32),
        compiler_params=_cparams("parallel", "arbitrary", vmem=FFN_VMEM_LIMIT),
        name="swiglu_ffn",
    )(h, wg, wu, wd)


def _final_kernel(x_ref, f_ref, gate_ref, g_ref, o_ref):
    x = x_ref[...] + gate_ref[...] * f_ref[...]
    ms = jnp.mean(x * x, axis=-1, keepdims=True)
    o_ref[...] = x * lax.rsqrt(ms + EPS) * g_ref[...]


def _final(x1, f, gate, g, tm=256):
    bsz, length, d = x1.shape
    tm = min(tm, length)
    return pl.pallas_call(
        _final_kernel,
        grid=(bsz, length // tm),
        in_specs=[pl.BlockSpec((None, tm, d), lambda b, i: (b, i, 0)),
                  pl.BlockSpec((None, tm, d), lambda b, i: (b, i, 0)),
                  pl.BlockSpec((None, 1, d), lambda b, i: (b, 0, 0)),
                  pl.BlockSpec((1, d), lambda b, i: (0, 0))],
        out_specs=pl.BlockSpec((None, tm, d), lambda b, i: (b, i, 0)),
        out_shape=jax.ShapeDtypeStruct((bsz, length, d), F32),
        compiler_params=_cparams("parallel", "parallel"),
        name="residual_final_norm",
    )(x1, f, gate.reshape(bsz, 1, d), g.reshape(1, d))


def _dft_split(length):
    n2 = 64
    while (length // n2) % PACK_ROWS:
        n2 //= 2
    assert n2 >= 8 and length % n2 == 0
    return n2


def kernel(x, c, ctx, c_ctx, ada_w, ada_b, norm1_g, norm2_g, w_in, w_out, fourier_w, s5_lam_re, s5_lam_im, s5_log_dt, s5_b_re, s5_b_im, s5_c_re, s5_c_im, s5_d, glu_w_a, glu_b_a, glu_w_b, glu_b_b, ffn_w_gate, ffn_w_up, ffn_w_down, final_g):
    bsz, length, d = x.shape
    depth = ada_w.shape[0]
    assert depth == 1, "single-layer block"
    lyr = 0
    heads, hd, _ = fourier_w.shape[1:]
    fw = heads * hd
    _, g, p, hgrp = s5_b_re.shape[1:]
    sw = g * hgrp
    assert w_in.shape[-1] == fw + sw and length % (2 * CHUNK) == 0 and ctx.shape[1] % (2 * CHUNK) == 0
    n = bsz * length

    a8 = jnp.zeros((8, d), F32).at[:bsz].set(c.astype(F32)).at[bsz].set(c_ctx.astype(F32))
    mods = _ada(jnp.concatenate([a8, a8], axis=0), ada_w[lyr], ada_b[lyr]).reshape(8, N_MOD, d)
    sh1, sc1, g1, sh2, sc2, g2 = (mods[:bsz, i] for i in range(N_MOD))
    csh1, csc1 = mods[bsz:bsz + 1, 0], mods[bsz:bsz + 1, 1]

    w_in_b = w_in[lyr].astype(BF16)
    w_s_t = jnp.transpose(w_in_b[:, fw:])
    ang = (2.0 * np.pi / hd) * ((np.arange(hd)[:, None] * np.arange(hd)[None, :]) % hd).astype(np.float64)
    cd = jnp.asarray(np.cos(ang) / math.sqrt(hd), F32)
    sd = jnp.asarray(np.sin(ang) / math.sqrt(hd), F32)
    csd = jnp.broadcast_to(jnp.stack([cd, sd])[:, None], (2, heads, hd, hd)).reshape(2 * heads, hd, hd)
    wf2 = jnp.concatenate([fourier_w[lyr], fourier_w[lyr]], axis=0).astype(F32)
    folded = _fold(csd, wf2)
    wcs = jnp.concatenate([folded[:heads], folded[heads:]], axis=-1).astype(BF16)
    kall, e_mat, d_mat, dec = _s5_tables(s5_lam_re[lyr], s5_lam_im[lyr], s5_log_dt[lyr], s5_b_re[lyr],
                                         s5_b_im[lyr], s5_c_re[lyr], s5_c_im[lyr])
    sel = _toeplitz_select(CHUNK)
    dsk = jnp.repeat(s5_d[lyr].astype(F32).reshape(g, 1, hgrp), CHUNK, axis=-1)

    hc = _norm_mod(ctx, norm1_g[lyr], csh1, csc1)
    nctx_tok = bsz * ctx.shape[1]
    ctx_s = _s5_states(_proj_t(w_s_t, hc.reshape(nctx_tok, d)), e_mat)

    hm = _norm_mod(x, norm1_g[lyr], sh1, sc1).reshape(n, d)
    n2 = _dft_split(length)
    pc4, ps4 = _four_in(hm, w_in_b, wcs, bsz, length, n2)
    g_tab, cs_tab = _dft_tables(length, n2)
    y_four = _dft(pc4, ps4, g_tab, cs_tab).reshape(n, fw)

    z3 = _proj_t(w_s_t, hm)
    lat_s = _s5_states(z3, e_mat)
    h_in = _s5_scan(dec, ctx_s, lat_s, bsz)
    g3 = _s5_out(z3, kall, sel, d_mat, h_in, dsk)
    y_s = _glu(g3, glu_w_a[lyr].astype(BF16), glu_w_b[lyr].astype(BF16), glu_b_a[lyr], glu_b_b[lyr])

    x1 = _out_proj(y_four, y_s, w_out[lyr].astype(BF16), x.reshape(n, d), g1, length)

    hm2 = _norm_mod(x1.reshape(bsz, length, d), norm2_g[lyr], sh2, sc2).reshape(n, d)
    f = _ffn(hm2, ffn_w_gate[lyr].astype(BF16), ffn_w_up[lyr].astype(BF16), ffn_w_down[lyr].astype(BF16))
    return _final(x1.reshape(bsz, length, d), f.reshape(bsz, length, d), g2, final_g)
```

```python
import functools
import math

import numpy as np
import jax
import jax.numpy as jnp
from jax import lax
from jax.experimental import pallas as pl
from jax.experimental.pallas import tpu as pltpu

F32 = jnp.float32
BF16 = jnp.bfloat16
EPS = 1e-6
CHUNK = 64
N_MOD = 6
V7X_VMEM_LIMIT = 56 * 1024 * 1024
HI = lax.Precision.HIGHEST


def _cparams(*sem, vmem=V7X_VMEM_LIMIT):
    return pltpu.CompilerParams(dimension_semantics=sem, vmem_limit_bytes=vmem)


def _dot(a, b):
    return jnp.dot(a, b, preferred_element_type=F32)


def _ada_kernel(a_ref, w_ref, b_ref, o_ref):
    a = a_ref[...]
    s = a * jax.nn.sigmoid(a)
    s_hi = s.astype(BF16).astype(F32)
    row = lax.broadcasted_iota(jnp.int32, s.shape, 0)
    lhs = jnp.where(row < 8, s_hi, s - s_hi).astype(BF16)
    w = w_ref[...]
    w_hi = w.astype(BF16)
    w_lo = (w - w_hi.astype(F32)).astype(BF16)
    r = _dot(lhs, w_hi) + _dot(lhs, w_lo)
    o_ref[...] = r[0:8] + r[8:16] + b_ref[...]


def _ada(a16, w, b, tn=1024):
    d, n = w.shape
    tn = min(tn, n)
    return pl.pallas_call(
        _ada_kernel,
        grid=(n // tn,),
        in_specs=[pl.BlockSpec((16, d), lambda j: (0, 0)),
                  pl.BlockSpec((d, tn), lambda j: (0, j)),
                  pl.BlockSpec((1, tn), lambda j: (0, j))],
        out_specs=pl.BlockSpec((8, tn), lambda j: (0, j)),
        out_shape=jax.ShapeDtypeStruct((8, n), F32),
        compiler_params=_cparams("parallel"),
        name="ada_matvec",
    )(a16, w, b.reshape(1, n))


def _norm_mod_kernel(x_ref, g_ref, sh_ref, sc_ref, o_ref):
    x = x_ref[...]
    ms = jnp.mean(x * x, axis=-1, keepdims=True)
    y = x * lax.rsqrt(ms + EPS) * g_ref[...]
    o_ref[...] = (y * (1.0 + sc_ref[...]) + sh_ref[...]).astype(o_ref.dtype)


def _norm_mod(x, g, sh, sc, tm=256):
    bsz, length, d = x.shape
    tm = min(tm, length)
    bm = sh.shape[0]
    mod_map = (lambda b, i: (b, 0, 0)) if bm == bsz else (lambda b, i: (0, 0, 0))
    return pl.pallas_call(
        _norm_mod_kernel,
        grid=(bsz, length // tm),
        in_specs=[pl.BlockSpec((None, tm, d), lambda b, i: (b, i, 0)),
                  pl.BlockSpec((1, d), lambda b, i: (0, 0)),
                  pl.BlockSpec((None, 1, d), mod_map),
                  pl.BlockSpec((None, 1, d), mod_map)],
        out_specs=pl.BlockSpec((None, tm, d), lambda b, i: (b, i, 0)),
        out_shape=jax.ShapeDtypeStruct((bsz, length, d), BF16),
        compiler_params=_cparams("parallel", "parallel"),
        name="norm_mod",
    )(x, g.reshape(1, d), sh.reshape(bm, 1, d), sc.reshape(bm, 1, d))


def _fold_kernel(a_ref, b_ref, o_ref):
    o_ref[...] = jnp.dot(a_ref[...], b_ref[...], preferred_element_type=F32, precision=HI)


def _fold(a, b):
    hh, m, k = a.shape
    n = b.shape[-1]
    return pl.pallas_call(
        _fold_kernel,
        grid=(hh,),
        in_specs=[pl.BlockSpec((None, m, k), lambda h: (h, 0, 0)),
                  pl.BlockSpec((None, k, n), lambda h: (h, 0, 0))],
        out_specs=pl.BlockSpec((None, m, n), lambda h: (h, 0, 0)),
        out_shape=jax.ShapeDtypeStruct((hh, m, n), F32),
        compiler_params=_cparams("parallel"),
        name="weight_fold",
    )(a, b)


LANES = 128
PACK_ROWS = 16
PITCH_PAD = 8


def _four_in_kernel(h_ref, w_ref, wcs_ref, pc_ref, ps_ref, p_scr):
    n2, tn1, hd = pc_ref.shape
    pitch = n2 + PITCH_PAD
    nq = hd // LANES
    z = _dot(h_ref[...], w_ref[...]).astype(BF16)
    p = _dot(z, wcs_ref[...])
    for q in range(2 * nq):
        for i1 in range(tn1):
            p_scr[q, i1 * pitch:i1 * pitch + n2, :] = p[i1 * n2:(i1 + 1) * n2, q * LANES:(q + 1) * LANES]

    def emit(j2, carry):
        for q in range(nq):
            pc_ref[j2, :, q * LANES:(q + 1) * LANES] = p_scr[q, pl.ds(j2, tn1, stride=pitch), :].astype(BF16)
            ps_ref[j2, :, q * LANES:(q + 1) * LANES] = p_scr[nq + q, pl.ds(j2, tn1, stride=pitch), :].astype(BF16)
        return carry

    lax.fori_loop(0, n2, emit, 0)


def _four_in(h, w_f, wcs, bsz, length, n2):
    n, d = h.shape
    heads, hd, _ = wcs.shape
    n1 = length // n2
    tn1 = PACK_ROWS
    tm = tn1 * n2
    tpb = length // tm
    fw = heads * hd
    out = jax.ShapeDtypeStruct((bsz, n2, n1, fw), BF16)
    ospec = pl.BlockSpec((None, n2, tn1, hd), lambda i, j: (i // tpb, 0, i % tpb, j))
    return pl.pallas_call(
        _four_in_kernel,
        grid=(n // tm, heads),
        in_specs=[pl.BlockSpec((tm, d), lambda i, j: (i, 0)),
                  pl.BlockSpec((d, hd), lambda i, j: (0, j)),
                  pl.BlockSpec((None, hd, 2 * hd), lambda i, j: (j, 0, 0))],
        out_specs=[ospec, ospec],
        out_shape=[out, out],
        scratch_shapes=[pltpu.VMEM((2 * hd // LANES, tn1 * (n2 + PITCH_PAD), LANES), F32)],
        compiler_params=_cparams("parallel", "arbitrary"),
        name="fourier_in_proj",
    )(h, w_f, wcs)


def _dft_tables(length, n2):
    n1 = length // n2
    k1 = np.arange(n1)[:, None, None]
    i1 = np.arange(n1)[None, :, None]
    i2 = np.arange(n2)[None, None, :]
    phase = (k1 * (n2 * i1 + i2)) % length
    phi = (2.0 * np.pi / length) * phase.astype(np.float64)
    c = np.cos(phi).transpose(2, 0, 1) / math.sqrt(n1)
    s = np.sin(phi).transpose(2, 0, 1) / math.sqrt(n1)
    g = np.concatenate([np.concatenate([c, -s], axis=2),
                        np.concatenate([-s, -c], axis=2)], axis=1)
    k2 = np.arange(n2)[:, None]
    j2 = np.arange(n2)[None, :]
    th = (2.0 * np.pi / n2) * ((k2 * j2) % n2).astype(np.float64)
    cs = np.concatenate([np.cos(th), np.sin(th)], axis=1) / math.sqrt(n2)
    return jnp.asarray(g, dtype=BF16), jnp.asarray(cs, dtype=BF16)


def _dft_kernel(g_ref, cs_ref, pc_ref, ps_ref, o_ref, t_scr, y_scr):
    n2, n1, _ = pc_ref.shape
    m = 2 * n1
    p1 = m + PITCH_PAD
    p2 = n1 + PITCH_PAD

    def stage1(j2, carry):
        rhs = jnp.concatenate([pc_ref[j2], ps_ref[j2]], axis=0)
        t_scr[pl.ds(pl.multiple_of(j2 * p1, 8), m), :] = _dot(g_ref[j2], rhs)
        return carry

    lax.fori_loop(0, n2, stage1, 0, unroll=8)

    cs = cs_ref[...]

    def stage2(i, carry):
        k1 = 2 * i
        cols = []
        for dk in range(2):
            re = t_scr[pl.ds(k1 + dk, n2, stride=p1), :]
            im = t_scr[pl.ds(n1 + k1 + dk, n2, stride=p1), :]
            cols.append(jnp.concatenate([re, im], axis=0).astype(BF16))
        res = _dot(cs, jnp.concatenate(cols, axis=1))
        for dk in range(2):
            y_scr[pl.ds(k1 + dk, n2, stride=p2), :] = res[:, dk * LANES:(dk + 1) * LANES]
        return carry

    lax.fori_loop(0, n1 // 2, stage2, 0, unroll=8)
    for k2 in range(n2):
        o_ref[k2 * n1:(k2 + 1) * n1, :] = y_scr[k2 * p2:k2 * p2 + n1, :].astype(o_ref.dtype)


def _dft(pc4, ps4, g, cs):
    bsz, n2, n1, w = pc4.shape
    m = 2 * n1
    ispec = pl.BlockSpec((None, n2, n1, LANES), lambda b, j: (b, 0, 0, j))
    return pl.pallas_call(
        _dft_kernel,
        grid=(bsz, w // LANES),
        in_specs=[pl.BlockSpec((n2, m, m), lambda b, j: (0, 0, 0), pipeline_mode=pl.Buffered(1)),
                  pl.BlockSpec((n2, 2 * n2), lambda b, j: (0, 0)),
                  ispec, ispec],
        out_specs=pl.BlockSpec((None, n2 * n1, LANES), lambda b, j: (b, 0, j)),
        out_shape=jax.ShapeDtypeStruct((bsz, n2 * n1, w), BF16),
        scratch_shapes=[pltpu.VMEM((n2 * (m + PITCH_PAD), LANES), F32),
                        pltpu.VMEM((n2 * (n1 + PITCH_PAD), LANES), F32)],
        compiler_params=_cparams("parallel", "parallel"),
        name="position_dft",
    )(g, cs, pc4, ps4)


def _nt_kernel(w_ref, h_ref, o_ref, stage_ref):
    acc = lax.dot_general(w_ref[...], h_ref[...], (((1,), (1,)), ((), ())), preferred_element_type=F32)
    rows, tq, _ = o_ref.shape
    if tq % 8 == 0:
        for q in range(tq):
            stage_ref[pl.ds(q, rows, stride=tq), :] = acc[:, q * LANES:(q + 1) * LANES]
        o_ref[...] = stage_ref[...].reshape(rows, tq, LANES)
    else:
        for q in range(tq):
            o_ref[:, q, :] = acc[:, q * LANES:(q + 1) * LANES]


def _proj_t(w_t, h, tmc=1024, tn=1024):
    c, d = w_t.shape
    n = h.shape[0]
    tmc, tn = min(tmc, c), min(tn, n)
    return pl.pallas_call(
        _nt_kernel,
        grid=(n // tn, c // tmc),
        in_specs=[pl.BlockSpec((tmc, d), lambda i, j: (j, 0)),
                  pl.BlockSpec((tn, d), lambda i, j: (i, 0))],
        out_specs=pl.BlockSpec((tmc, tn // LANES, LANES), lambda i, j: (j, i, 0)),
        out_shape=jax.ShapeDtypeStruct((c, n // LANES, LANES), F32),
        scratch_shapes=[pltpu.VMEM((tmc * (tn // LANES), LANES), F32)],
        compiler_params=_cparams("parallel", "arbitrary"),
        name="s5_in_proj_t",
    )(w_t, h)


def _s5_tables(lam_re, lam_im, log_dt, b_re, b_im, c_re, c_im):
    t = CHUNK
    _, g, p = lam_re.shape
    h = b_re.shape[-1]
    dt = jnp.exp(log_dt.astype(F32))[..., None]
    lr, li = jnp.minimum(lam_re.astype(F32), -1e-4), lam_im.astype(F32)
    ar, ai = lr * dt, li * dt

    def powers(d, first, step):
        ks = first + step * jnp.arange(t, dtype=F32)
        mag = jnp.exp(ar[d][..., None] * ks)
        ang = ai[d][..., None] * ks
        return mag * jnp.cos(ang), mag * jnp.sin(ang)

    lbm = jnp.exp(ar)
    nr, ni = lbm * jnp.cos(ai) - 1.0, lbm * jnp.sin(ai)
    den = lr * lr + li * li
    qr, qi = (nr * lr + ni * li) / den, (ni * lr - nr * li) / den
    br, bi = b_re.astype(F32), b_im.astype(F32)
    bbr = qr[..., None] * br - qi[..., None] * bi
    bbi = qr[..., None] * bi + qi[..., None] * br
    cr, ci = c_re.astype(F32), c_im.astype(F32)

    crt, cit = jnp.swapaxes(cr, -1, -2)[..., None, :], jnp.swapaxes(ci, -1, -2)[..., None, :]
    mr = (bbr[..., None] * crt - bbi[..., None] * cit).reshape(2, g, p, h * h)
    mi = (bbr[..., None] * cit + bbi[..., None] * crt).reshape(2, g, p, h * h)
    def lag_kernel(d, first, step):
        wr, wi = powers(d, first, step)
        return (jnp.einsum('gpx,gpt->gxt', mr[d], wr, precision=HI)
                - jnp.einsum('gpx,gpt->gxt', mi[d], wi, precision=HI))

    kf = lag_kernel(0, 0.0, 1.0)
    kbr = lag_kernel(1, t - 1.0, -1.0)
    kall = jnp.concatenate([kbr[..., :t - 1], kf[..., :1] + kbr[..., t - 1:], kf[..., 1:],
                            jnp.zeros_like(kf[..., :1])], axis=-1)

    pfr, pfi = (jnp.swapaxes(w, 1, 2) for w in powers(0, t - 1.0, -1.0))
    pbr, pbi = (jnp.swapaxes(w, 1, 2) for w in powers(1, 0.0, 1.0))
    ew_a = jnp.concatenate([pfr, pfr, pbr, pbr], axis=-1)
    ew_b = jnp.concatenate([pfi, pfi, pbi, pbi], axis=-1)
    bt = lambda a: jnp.swapaxes(a, 1, 2)
    eb_a = jnp.concatenate([bt(bbr[0]), bt(bbi[0]), bt(bbr[1]), bt(bbi[1])], axis=-1)
    eb_b = jnp.concatenate([-bt(bbi[0]), bt(bbr[0]), -bt(bbi[1]), bt(bbr[1])], axis=-1)

    qfr, qfi = powers(0, 1.0, 1.0)
    qbr, qbi = powers(1, float(t), -1.0)
    crf, cif, crb, cib = bt(cr[0]), bt(ci[0]), bt(cr[1]), bt(ci[1])
    dc = jnp.concatenate([jnp.concatenate([crf, -crf, crb, -crb], axis=1),
                          jnp.concatenate([-cif, -cif, -cib, -cib], axis=1)], axis=-1)
    dq = jnp.concatenate([jnp.concatenate([qfr, qfi, qbr, qbi], axis=1),
                          jnp.concatenate([qfi, qfr, qbi, qbr], axis=1)], axis=-1)

    mag_t = jnp.exp(ar * float(t))
    dec = jnp.stack([mag_t[0] * jnp.cos(ai[0] * t), mag_t[0] * jnp.sin(ai[0] * t),
                     mag_t[1] * jnp.cos(ai[1] * t), mag_t[1] * jnp.sin(ai[1] * t)]).reshape(4, g * p)
    return kall, (ew_a, ew_b, eb_a, eb_b), (dc.astype(BF16), dq.astype(BF16)), dec


def _carry_expanders(h, t):
    ht = h * t
    rep = np.zeros((2 * h, 2 * ht), np.float32)
    til = np.zeros((2 * t, 2 * ht), np.float32)
    for half in range(2):
        for ho in range(h):
            for tt in range(t):
                rep[half * h + ho, half * ht + ho * t + tt] = 1.0
                til[half * t + tt, half * ht + ho * t + tt] = 1.0
    return jnp.asarray(rep, dtype=BF16), jnp.asarray(til, dtype=BF16)


def _toeplitz_select(t):
    sel = np.zeros((4 * t, (t // 8) * 2 * t), np.float32)
    for q in range(t // 8):
        for tt in range(t):
            j = tt - 8 * q + t - 8
            sel[j, q * 2 * t + tt] = 1.0
            sel[2 * t + j, q * 2 * t + t + tt] = 1.0
    return jnp.asarray(sel, dtype=BF16)


GPS = 2


def _gather_chunks(z_ref, gi, h):
    t = z_ref.shape[-1] // 2
    lo = lax.broadcasted_iota(jnp.int32, z_ref.shape[1:], 1) < t
    ev, od = [], []
    for k in range(0, h, 2):
        za, zb = z_ref[gi * h + k], z_ref[gi * h + k + 1]
        ev.append(jnp.where(lo, za, pltpu.roll(zb, t, axis=1)))
        od.append(jnp.where(lo, pltpu.roll(za, t, axis=1), zb))
    return jnp.concatenate([jnp.concatenate(ev, axis=1), jnp.concatenate(od, axis=1)], axis=0)


def _s5_state_kernel(z_ref, ewa_ref, ewb_ref, eba_ref, ebb_ref, fre_ref, fim_ref, bre_ref, bim_ref, e_scr):
    h = z_ref.shape[0] // GPS
    t = ewa_ref.shape[1]
    p = ewa_ref.shape[-1] // 4
    outs = (fre_ref, fim_ref, bre_ref, bim_ref)
    for gi in range(GPS):
        wa, wb = ewa_ref[gi], ewb_ref[gi]
        for k in range(h):
            e_scr[k * t:(k + 1) * t, :] = (wa * eba_ref[gi, k:k + 1, :] + wb * ebb_ref[gi, k:k + 1, :]).astype(BF16)
        a2 = _gather_chunks(z_ref, gi, h).astype(BF16)
        s = _dot(a2, e_scr[...])
        for k in range(4):
            outs[k][:, gi * p:(gi + 1) * p] = s[:, k * p:(k + 1) * p]


def _s5_states(z3, e_tabs):
    gh, npair, t2 = z3.shape
    nchunk, t = 2 * npair, t2 // 2
    g, _, p4 = e_tabs[0].shape
    h, p = gh // g, p4 // 4
    out = jax.ShapeDtypeStruct((nchunk, g * p), F32)
    ospec = pl.BlockSpec((nchunk, GPS * p), lambda i: (0, i))
    wspec = pl.BlockSpec((GPS, t, p4), lambda i: (i, 0, 0))
    bspec = pl.BlockSpec((GPS, h, p4), lambda i: (i, 0, 0))
    return pl.pallas_call(
        _s5_state_kernel,
        grid=(g // GPS,),
        in_specs=[pl.BlockSpec((GPS * h, npair, t2), lambda i: (i, 0, 0)), wspec, wspec, bspec, bspec],
        out_specs=[ospec] * 4,
        out_shape=[out] * 4,
        scratch_shapes=[pltpu.VMEM((h * t, p4), BF16)],
        compiler_params=_cparams("parallel"),
        name="s5_chunk_states",
    )(z3, *e_tabs)


def _s5_scan_kernel(bsz, dec_ref, cfre, cfim, cbre, cbim, sfre, sfim, sbre, sbim,
                    hfre, hfim, hbre, hbim):
    nctx = cfre.shape[0] // bsz
    nlat = sfre.shape[0] // bsz
    width = dec_ref.shape[-1]
    fr, fi = dec_ref[0:1, :], dec_ref[1:2, :]
    br, bi = dec_ref[2:3, :], dec_ref[3:4, :]
    zero = jnp.zeros((1, width), F32)

    def step(ar, ai, hr, hi, sr, si):
        return ar * hr - ai * hi + sr, ar * hi + ai * hr + si

    def row(nchunk, b, j):
        return (j % 2) * (bsz * nchunk // 2) + b * (nchunk // 2) + j // 2

    for b in range(bsz):
        hr, hi = zero, zero
        for j in range(nctx):
            r = row(nctx, b, j)
            hr, hi = step(fr, fi, hr, hi, cfre[r:r + 1, :], cfim[r:r + 1, :])

        def fwd(j, carry, b=b):
            hr, hi = carry
            r = row(nlat, b, j)
            hfre[pl.ds(r, 1), :] = hr
            hfim[pl.ds(r, 1), :] = hi
            return step(fr, fi, hr, hi, sfre[pl.ds(r, 1), :], sfim[pl.ds(r, 1), :])

        lax.fori_loop(0, nlat, fwd, (hr, hi))

        hr, hi = zero, zero
        for j in range(nctx - 1, -1, -1):
            r = row(nctx, b, j)
            hr, hi = step(br, bi, hr, hi, cbre[r:r + 1, :], cbim[r:r + 1, :])

        def bwd(k, carry, b=b):
            hr, hi = carry
            r = row(nlat, b, nlat - 1 - k)
            hbre[pl.ds(r, 1), :] = hr
            hbim[pl.ds(r, 1), :] = hi
            return step(br, bi, hr, hi, sbre[pl.ds(r, 1), :], sbim[pl.ds(r, 1), :])

        lax.fori_loop(0, nlat, bwd, (hr, hi))


def _s5_scan(dec, ctx_s, lat_s, bsz, tw=1024):
    gp = dec.shape[-1]
    rc, rl = ctx_s[0].shape[0], lat_s[0].shape[0]
    assert (rc // bsz) % 2 == 0 and (rl // bsz) % 2 == 0
    tw = min(tw, gp)
    cspec = pl.BlockSpec((rc, tw), lambda i: (0, i))
    lspec = pl.BlockSpec((rl, tw), lambda i: (0, i))
    out = jax.ShapeDtypeStruct((rl, gp), F32)
    return pl.pallas_call(
        functools.partial(_s5_scan_kernel, bsz),
        grid=(gp // tw,),
        in_specs=[pl.BlockSpec((4, tw), lambda i: (0, i))] + [cspec] * 4 + [lspec] * 4,
        out_specs=[lspec] * 4,
        out_shape=[out] * 4,
        compiler_params=_cparams("parallel"),
        name="s5_state_scan",
    )(dec, *ctx_s, *lat_s)


def _gelu_tanh(x):
    return 0.5 * x * (1.0 + jnp.tanh(math.sqrt(2.0 / math.pi) * (x + 0.044715 * (x * x * x))))


def _expand_toeplitz(k_ref, gi, sel_ref, lhs_ref, res_ref, w_ref, h, t):
    half = h // 2
    rows_per_hi = half * 8

    def fill(hi, carry):
        for hp in range(half):
            for par in range(2):
                v = k_ref[gi, pl.ds(hi * h + 2 * hp + par, 1), :]
                b = pltpu.roll(jnp.broadcast_to(v, (8, 2 * t)), 2 * t - 7, axis=1, stride=1, stride_axis=0)
                lhs_ref[pl.ds(pl.multiple_of(hi * rows_per_hi + hp * 8, 8), 8), par * 2 * t:(par + 1) * 2 * t] = b
        return carry

    lax.fori_loop(0, h, fill, 0, unroll=4)
    res_ref[...] = _dot(lhs_ref[...].astype(BF16), sel_ref[...])

    def shuffle(hi, carry):
        base = pl.multiple_of(hi * rows_per_hi, rows_per_hi)
        r_hi = res_ref[pl.ds(base, rows_per_hi), :]
        rows = [jnp.concatenate([r_hi[hp * 8:(hp + 1) * 8, q * 2 * t:(q + 1) * 2 * t] for hp in range(half)], axis=1)
                for q in range(t // 8)]
        w_ref[pl.ds(base, t), :] = jnp.concatenate(rows, axis=0).astype(BF16)
        return carry

    lax.fori_loop(0, h, shuffle, 0)


def _s5_out_kernel(z_ref, k_ref, sel_ref, dc_ref, dq_ref, rep_ref, til_ref, hfre, hfim, hbre, hbim, dsk_ref,
                   o_ref, lhs_ref, res_ref, w_ref):
    h = z_ref.shape[0] // GPS
    npair = z_ref.shape[1]
    t = z_ref.shape[-1] // 2
    ht = h * t
    p = dc_ref.shape[1] // 4
    lo = lax.broadcasted_iota(jnp.int32, z_ref.shape[1:], 1) < t
    for gi in range(GPS):
        _expand_toeplitz(k_ref, gi, sel_ref, lhs_ref, res_ref, w_ref, h, t)
        ce = _dot(dc_ref[gi], rep_ref[...])
        qe = _dot(dq_ref[gi], til_ref[...])
        dmat = (ce[:, :ht] * qe[:, :ht] + ce[:, ht:] * qe[:, ht:]).astype(BF16)
        u = _gather_chunks(z_ref, gi, h)
        sl = slice(gi * p, (gi + 1) * p)
        hp = jnp.concatenate([hfre[:, sl], hfim[:, sl], hbre[:, sl], hbim[:, sl]], axis=-1).astype(BF16)
        y = _dot(u.astype(BF16), w_ref[...]) + _dot(hp, dmat)
        gl = _gelu_tanh(y + u * dsk_ref[gi])
        for k in range(0, h, 2):
            te = gl[0:npair, k * t:(k + 2) * t]
            to = gl[npair:2 * npair, k * t:(k + 2) * t]
            o_ref[gi * h + k] = jnp.where(lo, te, pltpu.roll(to, t, axis=1))
            o_ref[gi * h + k + 1] = jnp.where(lo, pltpu.roll(te, t, axis=1), to)


def _s5_out(z3, kall, sel, d_tabs, expanders, hin, dsk):
    gh, npair, t2z = z3.shape
    nchunk, t = 2 * npair, t2z // 2
    g, hh, t2 = kall.shape
    h = gh // g
    ht = h * t
    assert hh == h * h and t2 == 2 * t and (h // 2) * 8 == t and t2 == LANES
    dc, dq = d_tabs
    rep, til = expanders
    p4 = dc.shape[1]
    p = p4 // 4
    hspec = pl.BlockSpec((nchunk, GPS * p), lambda i: (0, i))
    const = lambda a: pl.BlockSpec(a.shape, lambda i: (0, 0))
    return pl.pallas_call(
        _s5_out_kernel,
        grid=(g // GPS,),
        in_specs=[pl.BlockSpec((GPS * h, npair, t2z), lambda i: (i, 0, 0)),
                  pl.BlockSpec((GPS, hh, t2), lambda i: (i, 0, 0)),
                  const(sel),
                  pl.BlockSpec((GPS, p4, 2 * h), lambda i: (i, 0, 0)),
                  pl.BlockSpec((GPS, p4, 2 * t), lambda i: (i, 0, 0)),
                  const(rep), const(til)] + [hspec] * 4 +
                 [pl.BlockSpec((GPS, 1, ht), lambda i: (i, 0, 0))],
        out_specs=pl.BlockSpec((GPS * h, npair, t2z), lambda i: (i, 0, 0)),
        out_shape=jax.ShapeDtypeStruct((gh, npair, t2z), F32),
        scratch_shapes=[pltpu.VMEM((ht, 4 * t), F32), pltpu.VMEM((ht, ht), F32), pltpu.VMEM((ht, ht), BF16)],
        compiler_params=_cparams("parallel"),
        name="s5_chunk_out",
    )(z3, kall, sel, dc, dq, rep, til, *hin, dsk)


def _glu_kernel(gt_ref, wa_ref, wb_ref, ba_ref, bb_ref, o_ref, g_scr, stage_ref):
    @pl.when(pl.program_id(1) == 0)
    def _():
        c, tq, _ = gt_ref.shape
        half = stage_ref.shape[0] // tq
        for c0 in range(0, c, half):
            stage_ref[...] = gt_ref[c0:c0 + half].reshape(half * tq, LANES)
            for q in range(tq):
                g_scr[c0:c0 + half, q * LANES:(q + 1) * LANES] = stage_ref[pl.ds(q, half, stride=tq), :].astype(BF16)

    dn = (((0,), (0,)), ((), ()))
    gt = g_scr[...]
    a = lax.dot_general(gt, wa_ref[...], dn, preferred_element_type=F32) + ba_ref[...]
    b = lax.dot_general(gt, wb_ref[...], dn, preferred_element_type=F32) + bb_ref[...]
    o_ref[...] = (a * jax.nn.sigmoid(b)).astype(o_ref.dtype)


def _glu(g3, wa, wb, ba, bb, tm=1024, tn=512):
    c, nq, _ = g3.shape
    n = nq * LANES
    co = wa.shape[-1]
    tm, tn = min(tm, n), min(tn, co)
    stage_rows = (c // 2) * (tm // LANES)
    return pl.pallas_call(
        _glu_kernel,
        grid=(n // tm, co // tn),
        in_specs=[pl.BlockSpec((c, tm // LANES, LANES), lambda i, j: (0, i, 0)),
                  pl.BlockSpec((c, tn), lambda i, j: (0, j)),
                  pl.BlockSpec((c, tn), lambda i, j: (0, j)),
                  pl.BlockSpec((1, tn), lambda i, j: (0, j)),
                  pl.BlockSpec((1, tn), lambda i, j: (0, j))],
        out_specs=pl.BlockSpec((tm, tn), lambda i, j: (i, j)),
        out_shape=jax.ShapeDtypeStruct((n, co), BF16),
        scratch_shapes=[pltpu.VMEM((c, tm), BF16), pltpu.VMEM((stage_rows, LANES), F32)],
        compiler_params=_cparams("parallel", "arbitrary"),
        name="s5_glu",
    )(g3, wa, wb, ba.reshape(1, co), bb.reshape(1, co))


def _out_proj_kernel(yf_ref, ys_ref, wf_ref, ws_ref, x_ref, g_ref, o_ref):
    acc = _dot(yf_ref[...], wf_ref[...]) + _dot(ys_ref[...], ws_ref[...])
    o_ref[...] = x_ref[...] + g_ref[...] * acc


def _out_proj(yf, ys, w_out, x2, gate, length, tm=1024, tn=1024):
    n, fw = yf.shape
    sw = ys.shape[-1]
    d = w_out.shape[-1]
    tm, tn = min(tm, length), min(tn, d)
    assert fw % sw == 0
    bsz = gate.shape[0]
    return pl.pallas_call(
        _out_proj_kernel,
        grid=(n // tm, d // tn),
        in_specs=[pl.BlockSpec((tm, fw), lambda i, j: (i, 0)),
                  pl.BlockSpec((tm, sw), lambda i, j: (i, 0)),
                  pl.BlockSpec((fw, tn), lambda i, j: (0, j)),
                  pl.BlockSpec((sw, tn), lambda i, j: (fw // sw, j)),
                  pl.BlockSpec((tm, tn), lambda i, j: (i, j)),
                  pl.BlockSpec((None, 1, tn), lambda i, j: ((i * tm) // length, 0, j))],
        out_specs=pl.BlockSpec((tm, tn), lambda i, j: (i, j)),
        out_shape=jax.ShapeDtypeStruct((n, d), F32),
        compiler_params=_cparams("parallel", "arbitrary"),
        name="out_proj_residual",
    )(yf, ys, w_out, w_out, x2, gate.reshape(bsz, 1, d))


def _ffn_hidden_start(j, th, hid):
    return pl.multiple_of(jnp.minimum(j * th, hid - th), math.gcd(th, hid - th))


def _ffn_kernel(hid, h_ref, wg_ref, wu_ref, wd_ref, o_ref):
    j = pl.program_id(1)
    th = wg_ref.shape[-1]
    hh = h_ref[...]
    g = _dot(hh, wg_ref[...].astype(BF16))
    u = _dot(hh, wu_ref[...].astype(BF16))
    unit = _ffn_hidden_start(j, th, hid) + lax.broadcasted_iota(jnp.int32, (1, th), 1)
    a = jnp.where(unit >= j * th, g * jax.nn.sigmoid(g) * u, 0.0).astype(BF16)
    d = o_ref.shape[-1]
    nc = min(FFN_DOWN_CHUNK, d)
    wd = wd_ref[...].astype(BF16)

    @pl.when(j == 0)
    def _():
        for c0 in range(0, d, nc):
            o_ref[:, c0:c0 + nc] = _dot(a, wd[:, c0:c0 + nc])

    @pl.when(j > 0)
    def _():
        for c0 in range(0, d, nc):
            o_ref[:, c0:c0 + nc] += _dot(a, wd[:, c0:c0 + nc])


FFN_DOWN_CHUNK = 512


FFN_TH = 256
FFN_VMEM_LIMIT = 60 * 1024 * 1024


def _ffn(h, wg, wu, wd, tm=1024):
    n, d = h.shape
    hid = wg.shape[-1]
    tm, th = min(tm, n), min(FFN_TH, hid)
    col_map = lambda i, j: (0, _ffn_hidden_start(j, th, hid))
    return pl.pallas_call(
        functools.partial(_ffn_kernel, hid),
        grid=(n // tm, pl.cdiv(hid, th)),
        in_specs=[pl.BlockSpec((tm, d), lambda i, j: (i, 0), pipeline_mode=pl.Buffered(1)),
                  pl.BlockSpec((pl.Element(d), pl.Element(th)), col_map),
                  pl.BlockSpec((pl.Element(d), pl.Element(th)), col_map),
                  pl.BlockSpec((pl.Element(th), pl.Element(d)), lambda i, j: (_ffn_hidden_start(j, th, hid), 0))],
        out_specs=pl.BlockSpec((tm, d), lambda i, j: (i, 0), pipeline_mode=pl.Buffered(1)),
        out_shape=jax.ShapeDtypeStruct((n, d), F32),
        compiler_params=_cparams("parallel", "arbitrary", vmem=FFN_VMEM_LIMIT),
        name="swiglu_ffn",
    )(h, wg, wu, wd)


def _final_kernel(x_ref, f_ref, gate_ref, g_ref, o_ref):
    x = x_ref[...] + gate_ref[...] * f_ref[...]
    ms = jnp.mean(x * x, axis=-1, keepdims=True)
    o_ref[...] = x * lax.rsqrt(ms + EPS) * g_ref[...]


def _final(x1, f, gate, g, tm=256):
    bsz, length, d = x1.shape
    tm = min(tm, length)
    return pl.pallas_call(
        _final_kernel,
        grid=(bsz, length // tm),
        in_specs=[pl.BlockSpec((None, tm, d), lambda b, i: (b, i, 0)),
                  pl.BlockSpec((None, tm, d), lambda b, i: (b, i, 0)),
                  pl.BlockSpec((None, 1, d), lambda b, i: (b, 0, 0)),
                  pl.BlockSpec((1, d), lambda b, i: (0, 0))],
        out_specs=pl.BlockSpec((None, tm, d), lambda b, i: (b, i, 0)),
        out_shape=jax.ShapeDtypeStruct((bsz, length, d), F32),
        compiler_params=_cparams("parallel", "parallel"),
        name="residual_final_norm",
    )(x1, f, gate.reshape(bsz, 1, d), g.reshape(1, d))


def _dft_split(length):
    n2 = 64
    while (length // n2) % PACK_ROWS:
        n2 //= 2
    assert n2 >= 8 and length % n2 == 0
    return n2


def kernel(x, c, ctx, c_ctx, ada_w, ada_b, norm1_g, norm2_g, w_in, w_out, fourier_w, s5_lam_re, s5_lam_im, s5_log_dt, s5_b_re, s5_b_im, s5_c_re, s5_c_im, s5_d, glu_w_a, glu_b_a, glu_w_b, glu_b_b, ffn_w_gate, ffn_w_up, ffn_w_down, final_g):
    bsz, length, d = x.shape
    depth = ada_w.shape[0]
    assert depth == 1, "single-layer block"
    lyr = 0
    heads, hd, _ = fourier_w.shape[1:]
    fw = heads * hd
    _, g, p, hgrp = s5_b_re.shape[1:]
    sw = g * hgrp
    assert w_in.shape[-1] == fw + sw and length % (2 * CHUNK) == 0 and ctx.shape[1] % (2 * CHUNK) == 0
    n = bsz * length

    a8 = jnp.zeros((8, d), F32).at[:bsz].set(c.astype(F32)).at[bsz].set(c_ctx.astype(F32))
    mods = _ada(jnp.concatenate([a8, a8], axis=0), ada_w[lyr], ada_b[lyr]).reshape(8, N_MOD, d)
    sh1, sc1, g1, sh2, sc2, g2 = (mods[:bsz, i] for i in range(N_MOD))
    csh1, csc1 = mods[bsz:bsz + 1, 0], mods[bsz:bsz + 1, 1]

    w_in_b = w_in[lyr].astype(BF16)
    w_s_t = jnp.transpose(w_in_b[:, fw:])
    ang = (2.0 * np.pi / hd) * ((np.arange(hd)[:, None] * np.arange(hd)[None, :]) % hd).astype(np.float64)
    cd = jnp.asarray(np.cos(ang) / math.sqrt(hd), F32)
    sd = jnp.asarray(np.sin(ang) / math.sqrt(hd), F32)
    csd = jnp.broadcast_to(jnp.stack([cd, sd])[:, None], (2, heads, hd, hd)).reshape(2 * heads, hd, hd)
    wf2 = jnp.concatenate([fourier_w[lyr], fourier_w[lyr]], axis=0).astype(F32)
    folded = _fold(csd, wf2)
    wcs = jnp.concatenate([folded[:heads], folded[heads:]], axis=-1).astype(BF16)
    kall, e_mat, d_mat, dec = _s5_tables(s5_lam_re[lyr], s5_lam_im[lyr], s5_log_dt[lyr], s5_b_re[lyr],
                                         s5_b_im[lyr], s5_c_re[lyr], s5_c_im[lyr])
    sel = _toeplitz_select(CHUNK)
    dsk = jnp.repeat(s5_d[lyr].astype(F32).reshape(g, 1, hgrp), CHUNK, axis=-1)

    hc = _norm_mod(ctx, norm1_g[lyr], csh1, csc1)
    nctx_tok = bsz * ctx.shape[1]
    ctx_s = _s5_states(_proj_t(w_s_t, hc.reshape(nctx_tok, d)), e_mat)

    hm = _norm_mod(x, norm1_g[lyr], sh1, sc1).reshape(n, d)
    n2 = _dft_split(length)
    pc4, ps4 = _four_in(hm, w_in_b, wcs, bsz, length, n2)
    g_tab, cs_tab = _dft_tables(length, n2)
    y_four = _dft(pc4, ps4, g_tab, cs_tab).reshape(n, fw)

    z3 = _proj_t(w_s_t, hm)
    lat_s = _s5_states(z3, e_mat)
    h_in = _s5_scan(dec, ctx_s, lat_s, bsz)
    g3 = _s5_out(z3, kall, sel, d_mat, _carry_expanders(hgrp, CHUNK), h_in, dsk)
    y_s = _glu(g3, glu_w_a[lyr].astype(BF16), glu_w_b[lyr].astype(BF16), glu_b_a[lyr], glu_b_b[lyr])

    x1 = _out_proj(y_four, y_s, w_out[lyr].astype(BF16), x.reshape(n, d), g1, length)

    hm2 = _norm_mod(x1.reshape(bsz, length, d), norm2_g[lyr], sh2, sc2).reshape(n, d)
    f = _ffn(hm2, ffn_w_gate[lyr], ffn_w_up[lyr], ffn_w_down[lyr])
    return _final(x1.reshape(bsz, length, d), f.reshape(bsz, length, d), g2, final_g)
```

```python
import functools
import math

import numpy as np
import jax
import jax.numpy as jnp
from jax import lax
from jax.experimental import pallas as pl
from jax.experimental.pallas import tpu as pltpu

F32 = jnp.float32
BF16 = jnp.bfloat16
EPS = 1e-6
CHUNK = 64
N_MOD = 6
V7X_VMEM_LIMIT = 56 * 1024 * 1024
HI = lax.Precision.HIGHEST


def _cparams(*sem, vmem=V7X_VMEM_LIMIT):
    return pltpu.CompilerParams(dimension_semantics=sem, vmem_limit_bytes=vmem)


def _dot(a, b):
    return jnp.dot(a, b, preferred_element_type=F32)


def _ada_kernel(a_ref, w_ref, b_ref, o_ref):
    a = a_ref[...]
    s = a * jax.nn.sigmoid(a)
    s_hi = s.astype(BF16).astype(F32)
    row = lax.broadcasted_iota(jnp.int32, s.shape, 0)
    lhs = jnp.where(row < 8, s_hi, s - s_hi).astype(BF16)
    w = w_ref[...]
    w_hi = w.astype(BF16)
    w_lo = (w - w_hi.astype(F32)).astype(BF16)
    r = _dot(lhs, w_hi) + _dot(lhs, w_lo)
    o_ref[...] = r[0:8] + r[8:16] + b_ref[...]


def _ada(a16, w, b, tn=1024):
    d, n = w.shape
    tn = min(tn, n)
    return pl.pallas_call(
        _ada_kernel,
        grid=(n // tn,),
        in_specs=[pl.BlockSpec((16, d), lambda j: (0, 0)),
                  pl.BlockSpec((d, tn), lambda j: (0, j)),
                  pl.BlockSpec((1, tn), lambda j: (0, j))],
        out_specs=pl.BlockSpec((8, tn), lambda j: (0, j)),
        out_shape=jax.ShapeDtypeStruct((8, n), F32),
        compiler_params=_cparams("parallel"),
        name="ada_matvec",
    )(a16, w, b.reshape(1, n))


def _norm_mod_kernel(x_ref, g_ref, sh_ref, sc_ref, o_ref):
    x = x_ref[...]
    ms = jnp.mean(x * x, axis=-1, keepdims=True)
    y = x * lax.rsqrt(ms + EPS) * g_ref[...]
    o_ref[...] = (y * (1.0 + sc_ref[...]) + sh_ref[...]).astype(o_ref.dtype)


def _norm_mod(x, g, sh, sc, tm=256):
    bsz, length, d = x.shape
    tm = min(tm, length)
    bm = sh.shape[0]
    mod_map = (lambda b, i: (b, 0, 0)) if bm == bsz else (lambda b, i: (0, 0, 0))
    return pl.pallas_call(
        _norm_mod_kernel,
        grid=(bsz, length // tm),
        in_specs=[pl.BlockSpec((None, tm, d), lambda b, i: (b, i, 0)),
                  pl.BlockSpec((1, d), lambda b, i: (0, 0)),
                  pl.BlockSpec((None, 1, d), mod_map),
                  pl.BlockSpec((None, 1, d), mod_map)],
        out_specs=pl.BlockSpec((None, tm, d), lambda b, i: (b, i, 0)),
        out_shape=jax.ShapeDtypeStruct((bsz, length, d), BF16),
        compiler_params=_cparams("parallel", "parallel"),
        name="norm_mod",
    )(x, g.reshape(1, d), sh.reshape(bm, 1, d), sc.reshape(bm, 1, d))


def _fold_kernel(a_ref, b_ref, o_ref):
    o_ref[...] = jnp.dot(a_ref[...], b_ref[...], preferred_element_type=F32, precision=HI)


def _fold(a, b):
    hh, m, k = a.shape
    n = b.shape[-1]
    return pl.pallas_call(
        _fold_kernel,
        grid=(hh,),
        in_specs=[pl.BlockSpec((None, m, k), lambda h: (h, 0, 0)),
                  pl.BlockSpec((None, k, n), lambda h: (h, 0, 0))],
        out_specs=pl.BlockSpec((None, m, n), lambda h: (h, 0, 0)),
        out_shape=jax.ShapeDtypeStruct((hh, m, n), F32),
        compiler_params=_cparams("parallel"),
        name="weight_fold",
    )(a, b)


LANES = 128
PACK_ROWS = 16
PITCH_PAD = 8


def _four_in_kernel(h_ref, w_ref, wcs_ref, pc_ref, ps_ref, p_scr):
    n2, tn1, hd = pc_ref.shape
    pitch = n2 + PITCH_PAD
    nq = hd // LANES
    z = _dot(h_ref[...], w_ref[...]).astype(BF16)
    p = _dot(z, wcs_ref[...])
    for q in range(2 * nq):
        for i1 in range(tn1):
            p_scr[q, i1 * pitch:i1 * pitch + n2, :] = p[i1 * n2:(i1 + 1) * n2, q * LANES:(q + 1) * LANES]

    def emit(j2, carry):
        for q in range(nq):
            pc_ref[j2, :, q * LANES:(q + 1) * LANES] = p_scr[q, pl.ds(j2, tn1, stride=pitch), :].astype(BF16)
            ps_ref[j2, :, q * LANES:(q + 1) * LANES] = p_scr[nq + q, pl.ds(j2, tn1, stride=pitch), :].astype(BF16)
        return carry

    lax.fori_loop(0, n2, emit, 0)


def _four_in(h, w_f, wcs, bsz, length, n2):
    n, d = h.shape
    heads, hd, _ = wcs.shape
    n1 = length // n2
    tn1 = PACK_ROWS
    tm = tn1 * n2
    tpb = length // tm
    fw = heads * hd
    out = jax.ShapeDtypeStruct((bsz, n2, n1, fw), BF16)
    ospec = pl.BlockSpec((None, n2, tn1, hd), lambda i, j: (i // tpb, 0, i % tpb, j))
    return pl.pallas_call(
        _four_in_kernel,
        grid=(n // tm, heads),
        in_specs=[pl.BlockSpec((tm, d), lambda i, j: (i, 0)),
                  pl.BlockSpec((d, hd), lambda i, j: (0, j)),
                  pl.BlockSpec((None, hd, 2 * hd), lambda i, j: (j, 0, 0))],
        out_specs=[ospec, ospec],
        out_shape=[out, out],
        scratch_shapes=[pltpu.VMEM((2 * hd // LANES, tn1 * (n2 + PITCH_PAD), LANES), F32)],
        compiler_params=_cparams("parallel", "arbitrary"),
        name="fourier_in_proj",
    )(h, w_f, wcs)


def _dft_tables(length, n2):
    n1 = length // n2
    k1 = np.arange(n1)[:, None, None]
    i1 = np.arange(n1)[None, :, None]
    i2 = np.arange(n2)[None, None, :]
    phase = (k1 * (n2 * i1 + i2)) % length
    phi = (2.0 * np.pi / length) * phase.astype(np.float64)
    c = np.cos(phi).transpose(2, 0, 1) / math.sqrt(n1)
    s = np.sin(phi).transpose(2, 0, 1) / math.sqrt(n1)
    g = np.concatenate([np.concatenate([c, -s], axis=2),
                        np.concatenate([-s, -c], axis=2)], axis=1)
    k2 = np.arange(n2)[:, None]
    j2 = np.arange(n2)[None, :]
    th = (2.0 * np.pi / n2) * ((k2 * j2) % n2).astype(np.float64)
    cs = np.concatenate([np.cos(th), np.sin(th)], axis=1) / math.sqrt(n2)
    return jnp.asarray(g, dtype=BF16), jnp.asarray(cs, dtype=BF16)


def _dft_kernel(g_ref, cs_ref, pc_ref, ps_ref, o_ref, t_scr, y_scr):
    n2, n1, _ = pc_ref.shape
    m = 2 * n1
    p1 = m + PITCH_PAD
    p2 = n1 + PITCH_PAD

    def stage1(j2, carry):
        rhs = jnp.concatenate([pc_ref[j2], ps_ref[j2]], axis=0)
        t_scr[pl.ds(pl.multiple_of(j2 * p1, 8), m), :] = _dot(g_ref[j2], rhs)
        return carry

    lax.fori_loop(0, n2, stage1, 0, unroll=8)

    cs = cs_ref[...]

    def stage2(i, carry):
        k1 = 2 * i
        cols = []
        for dk in range(2):
            re = t_scr[pl.ds(k1 + dk, n2, stride=p1), :]
            im = t_scr[pl.ds(n1 + k1 + dk, n2, stride=p1), :]
            cols.append(jnp.concatenate([re, im], axis=0).astype(BF16))
        res = _dot(cs, jnp.concatenate(cols, axis=1))
        for dk in range(2):
            y_scr[pl.ds(k1 + dk, n2, stride=p2), :] = res[:, dk * LANES:(dk + 1) * LANES]
        return carry

    lax.fori_loop(0, n1 // 2, stage2, 0, unroll=8)
    for k2 in range(n2):
        o_ref[k2 * n1:(k2 + 1) * n1, :] = y_scr[k2 * p2:k2 * p2 + n1, :].astype(o_ref.dtype)


def _dft(pc4, ps4, g, cs):
    bsz, n2, n1, w = pc4.shape
    m = 2 * n1
    ispec = pl.BlockSpec((None, n2, n1, LANES), lambda b, j: (b, 0, 0, j))
    return pl.pallas_call(
        _dft_kernel,
        grid=(bsz, w // LANES),
        in_specs=[pl.BlockSpec((n2, m, m), lambda b, j: (0, 0, 0), pipeline_mode=pl.Buffered(1)),
                  pl.BlockSpec((n2, 2 * n2), lambda b, j: (0, 0)),
                  ispec, ispec],
        out_specs=pl.BlockSpec((None, n2 * n1, LANES), lambda b, j: (b, 0, j)),
        out_shape=jax.ShapeDtypeStruct((bsz, n2 * n1, w), BF16),
        scratch_shapes=[pltpu.VMEM((n2 * (m + PITCH_PAD), LANES), F32),
                        pltpu.VMEM((n2 * (n1 + PITCH_PAD), LANES), F32)],
        compiler_params=_cparams("parallel", "parallel"),
        name="position_dft",
    )(g, cs, pc4, ps4)


def _nt_kernel(w_ref, h_ref, o_ref, stage_ref):
    acc = lax.dot_general(w_ref[...], h_ref[...], (((1,), (1,)), ((), ())), preferred_element_type=F32)
    rows, tq, _ = o_ref.shape
    if tq % 8 == 0:
        for q in range(tq):
            stage_ref[pl.ds(q, rows, stride=tq), :] = acc[:, q * LANES:(q + 1) * LANES]
        o_ref[...] = stage_ref[...].reshape(rows, tq, LANES)
    else:
        for q in range(tq):
            o_ref[:, q, :] = acc[:, q * LANES:(q + 1) * LANES]


def _proj_t(w_t, h, tmc=1024, tn=1024):
    c, d = w_t.shape
    n = h.shape[0]
    tmc, tn = min(tmc, c), min(tn, n)
    return pl.pallas_call(
        _nt_kernel,
        grid=(n // tn, c // tmc),
        in_specs=[pl.BlockSpec((tmc, d), lambda i, j: (j, 0)),
                  pl.BlockSpec((tn, d), lambda i, j: (i, 0))],
        out_specs=pl.BlockSpec((tmc, tn // LANES, LANES), lambda i, j: (j, i, 0)),
        out_shape=jax.ShapeDtypeStruct((c, n // LANES, LANES), F32),
        scratch_shapes=[pltpu.VMEM((tmc * (tn // LANES), LANES), F32)],
        compiler_params=_cparams("parallel", "arbitrary"),
        name="s5_in_proj_t",
    )(w_t, h)


def _s5_tables(lam_re, lam_im, log_dt, b_re, b_im, c_re, c_im):
    t = CHUNK
    _, g, p = lam_re.shape
    h = b_re.shape[-1]
    dt = jnp.exp(log_dt.astype(F32))[..., None]
    lr, li = jnp.minimum(lam_re.astype(F32), -1e-4), lam_im.astype(F32)
    ar, ai = lr * dt, li * dt

    def powers(d, first, step, steps_major=False):
        ks = first + step * jnp.arange(t, dtype=F32)
        if steps_major:
            ea, ia = ar[d][:, None, :] * ks[None, :, None], ai[d][:, None, :] * ks[None, :, None]
        else:
            ea, ia = ar[d][..., None] * ks, ai[d][..., None] * ks
        mag = jnp.exp(ea)
        return mag * jnp.cos(ia), mag * jnp.sin(ia)

    lbm = jnp.exp(ar)
    nr, ni = lbm * jnp.cos(ai) - 1.0, lbm * jnp.sin(ai)
    den = lr * lr + li * li
    qr, qi = (nr * lr + ni * li) / den, (ni * lr - nr * li) / den
    br, bi = (jnp.swapaxes(a.astype(F32), -1, -2) for a in (b_re, b_im))
    bbr = qr[:, :, None] * br - qi[:, :, None] * bi
    bbi = qr[:, :, None] * bi + qi[:, :, None] * br
    cr, ci = c_re.astype(F32), c_im.astype(F32)

    mr = (bbr[:, :, :, None] * cr[:, :, None] - bbi[:, :, :, None] * ci[:, :, None]).reshape(2, g, h * h, p)
    mi = (bbr[:, :, :, None] * ci[:, :, None] + bbi[:, :, :, None] * cr[:, :, None]).reshape(2, g, h * h, p)

    def lag_kernel(d, first, step):
        wr, wi = powers(d, first, step)
        return (jnp.einsum('gxp,gpt->gxt', mr[d], wr, precision=HI)
                - jnp.einsum('gxp,gpt->gxt', mi[d], wi, precision=HI))

    kf = lag_kernel(0, 0.0, 1.0)
    kbr = lag_kernel(1, t - 1.0, -1.0)
    kall = jnp.concatenate([kbr[..., :t - 1], kf[..., :1] + kbr[..., t - 1:], kf[..., 1:],
                            jnp.zeros_like(kf[..., :1])], axis=-1)

    pfr, pfi = powers(0, t - 1.0, -1.0, steps_major=True)
    pbr, pbi = powers(1, 0.0, 1.0, steps_major=True)
    ew_a = jnp.concatenate([pfr, pfr, pbr, pbr], axis=-1)
    ew_b = jnp.concatenate([pfi, pfi, pbi, pbi], axis=-1)
    eb_a = jnp.concatenate([bbr[0], bbi[0], bbr[1], bbi[1]], axis=-1)
    eb_b = jnp.concatenate([-bbi[0], bbr[0], -bbi[1], bbr[1]], axis=-1)

    qfr, qfi = powers(0, 1.0, 1.0)
    qbr, qbi = powers(1, float(t), -1.0)
    crf, cif, crb, cib = (jnp.swapaxes(a, 1, 2) for a in (cr[0], ci[0], cr[1], ci[1]))
    dc = jnp.concatenate([jnp.concatenate([crf, -crf, crb, -crb], axis=1),
                          jnp.concatenate([-cif, -cif, -cib, -cib], axis=1)], axis=-1)
    dq = jnp.concatenate([jnp.concatenate([qfr, qfi, qbr, qbi], axis=1),
                          jnp.concatenate([qfi, qfr, qbi, qbr], axis=1)], axis=-1)

    mag_t = jnp.exp(ar * float(t))
    dec = jnp.stack([mag_t[0] * jnp.cos(ai[0] * t), mag_t[0] * jnp.sin(ai[0] * t),
                     mag_t[1] * jnp.cos(ai[1] * t), mag_t[1] * jnp.sin(ai[1] * t)]).reshape(4, g * p)
    return kall, (ew_a, ew_b, eb_a, eb_b), (dc.astype(BF16), dq.astype(BF16)), dec


def _carry_expanders(h, t):
    ht = h * t
    rep = np.zeros((2 * h, 2 * ht), np.float32)
    til = np.zeros((2 * t, 2 * ht), np.float32)
    for half in range(2):
        for ho in range(h):
            for tt in range(t):
                rep[half * h + ho, half * ht + ho * t + tt] = 1.0
                til[half * t + tt, half * ht + ho * t + tt] = 1.0
    return jnp.asarray(rep, dtype=BF16), jnp.asarray(til, dtype=BF16)


def _toeplitz_select(t):
    sel = np.zeros((4 * t, (t // 8) * 2 * t), np.float32)
    for q in range(t // 8):
        for tt in range(t):
            j = tt - 8 * q + t - 8
            sel[j, q * 2 * t + tt] = 1.0
            sel[2 * t + j, q * 2 * t + t + tt] = 1.0
    return jnp.asarray(sel, dtype=BF16)


GPS = 2


def _gather_chunks(z_ref, gi, h):
    t = z_ref.shape[-1] // 2
    lo = lax.broadcasted_iota(jnp.int32, z_ref.shape[1:], 1) < t
    ev, od = [], []
    for k in range(0, h, 2):
        za, zb = z_ref[gi * h + k], z_ref[gi * h + k + 1]
        ev.append(jnp.where(lo, za, pltpu.roll(zb, t, axis=1)))
        od.append(jnp.where(lo, pltpu.roll(za, t, axis=1), zb))
    return jnp.concatenate([jnp.concatenate(ev, axis=1), jnp.concatenate(od, axis=1)], axis=0)


def _s5_state_kernel(z_ref, ewa_ref, ewb_ref, eba_ref, ebb_ref, fre_ref, fim_ref, bre_ref, bim_ref, e_scr):
    h = z_ref.shape[0] // GPS
    t = ewa_ref.shape[1]
    p = ewa_ref.shape[-1] // 4
    outs = (fre_ref, fim_ref, bre_ref, bim_ref)
    for gi in range(GPS):
        wa, wb = ewa_ref[gi], ewb_ref[gi]
        for k in range(h):
            e_scr[k * t:(k + 1) * t, :] = (wa * eba_ref[gi, k:k + 1, :] + wb * ebb_ref[gi, k:k + 1, :]).astype(BF16)
        a2 = _gather_chunks(z_ref, gi, h).astype(BF16)
        s = _dot(a2, e_scr[...])
        for k in range(4):
            outs[k][:, gi * p:(gi + 1) * p] = s[:, k * p:(k + 1) * p]


def _s5_states(z3, e_tabs):
    gh, npair, t2 = z3.shape
    nchunk, t = 2 * npair, t2 // 2
    g, _, p4 = e_tabs[0].shape
    h, p = gh // g, p4 // 4
    out = jax.ShapeDtypeStruct((nchunk, g * p), F32)
    ospec = pl.BlockSpec((nchunk, GPS * p), lambda i: (0, i))
    wspec = pl.BlockSpec((GPS, t, p4), lambda i: (i, 0, 0))
    bspec = pl.BlockSpec((GPS, h, p4), lambda i: (i, 0, 0))
    return pl.pallas_call(
        _s5_state_kernel,
        grid=(g // GPS,),
        in_specs=[pl.BlockSpec((GPS * h, npair, t2), lambda i: (i, 0, 0)), wspec, wspec, bspec, bspec],
        out_specs=[ospec] * 4,
        out_shape=[out] * 4,
        scratch_shapes=[pltpu.VMEM((h * t, p4), BF16)],
        compiler_params=_cparams("parallel"),
        name="s5_chunk_states",
    )(z3, *e_tabs)


def _s5_scan_kernel(bsz, dec_ref, cfre, cfim, cbre, cbim, sfre, sfim, sbre, sbim,
                    hfre, hfim, hbre, hbim):
    nctx = cfre.shape[0] // bsz
    nlat = sfre.shape[0] // bsz
    width = dec_ref.shape[-1]
    fr, fi = dec_ref[0:1, :], dec_ref[1:2, :]
    br, bi = dec_ref[2:3, :], dec_ref[3:4, :]
    zero = jnp.zeros((1, width), F32)

    def step(ar, ai, hr, hi, sr, si):
        return ar * hr - ai * hi + sr, ar * hi + ai * hr + si

    def row(nchunk, b, j):
        return (j % 2) * (bsz * nchunk // 2) + b * (nchunk // 2) + j // 2

    for b in range(bsz):
        hr, hi = zero, zero
        for j in range(nctx):
            r = row(nctx, b, j)
            hr, hi = step(fr, fi, hr, hi, cfre[r:r + 1, :], cfim[r:r + 1, :])

        def fwd(j, carry, b=b):
            hr, hi = carry
            r = row(nlat, b, j)
            hfre[pl.ds(r, 1), :] = hr
            hfim[pl.ds(r, 1), :] = hi
            return step(fr, fi, hr, hi, sfre[pl.ds(r, 1), :], sfim[pl.ds(r, 1), :])

        lax.fori_loop(0, nlat, fwd, (hr, hi))

        hr, hi = zero, zero
        for j in range(nctx - 1, -1, -1):
            r = row(nctx, b, j)
            hr, hi = step(br, bi, hr, hi, cbre[r:r + 1, :], cbim[r:r + 1, :])

        def bwd(k, carry, b=b):
            hr, hi = carry
            r = row(nlat, b, nlat - 1 - k)
            hbre[pl.ds(r, 1), :] = hr
            hbim[pl.ds(r, 1), :] = hi
            return step(br, bi, hr, hi, sbre[pl.ds(r, 1), :], sbim[pl.ds(r, 1), :])

        lax.fori_loop(0, nlat, bwd, (hr, hi))


def _s5_scan(dec, ctx_s, lat_s, bsz, tw=1024):
    gp = dec.shape[-1]
    rc, rl = ctx_s[0].shape[0], lat_s[0].shape[0]
    assert (rc // bsz) % 2 == 0 and (rl // bsz) % 2 == 0
    tw = min(tw, gp)
    cspec = pl.BlockSpec((rc, tw), lambda i: (0, i))
    lspec = pl.BlockSpec((rl, tw), lambda i: (0, i))
    out = jax.ShapeDtypeStruct((rl, gp), F32)
    return pl.pallas_call(
        functools.partial(_s5_scan_kernel, bsz),
        grid=(gp // tw,),
        in_specs=[pl.BlockSpec((4, tw), lambda i: (0, i))] + [cspec] * 4 + [lspec] * 4,
        out_specs=[lspec] * 4,
        out_shape=[out] * 4,
        compiler_params=_cparams("parallel"),
        name="s5_state_scan",
    )(dec, *ctx_s, *lat_s)


def _gelu_tanh(x):
    return 0.5 * x * (1.0 + jnp.tanh(math.sqrt(2.0 / math.pi) * (x + 0.044715 * (x * x * x))))


def _expand_toeplitz(k_ref, gi, sel_ref, lhs_ref, res_ref, w_ref, h, t):
    half = h // 2
    rows_per_hi = half * 8

    def fill(hi, carry):
        for hp in range(half):
            for par in range(2):
                v = k_ref[gi, pl.ds(hi * h + 2 * hp + par, 1), :]
                b = pltpu.roll(jnp.broadcast_to(v, (8, 2 * t)), 2 * t - 7, axis=1, stride=1, stride_axis=0)
                lhs_ref[pl.ds(pl.multiple_of(hi * rows_per_hi + hp * 8, 8), 8), par * 2 * t:(par + 1) * 2 * t] = b
        return carry

    lax.fori_loop(0, h, fill, 0, unroll=4)
    res_ref[...] = _dot(lhs_ref[...].astype(BF16), sel_ref[...])

    def shuffle(hi, carry):
        base = pl.multiple_of(hi * rows_per_hi, rows_per_hi)
        r_hi = res_ref[pl.ds(base, rows_per_hi), :]
        rows = [jnp.concatenate([r_hi[hp * 8:(hp + 1) * 8, q * 2 * t:(q + 1) * 2 * t] for hp in range(half)], axis=1)
                for q in range(t // 8)]
        w_ref[pl.ds(base, t), :] = jnp.concatenate(rows, axis=0).astype(BF16)
        return carry

    lax.fori_loop(0, h, shuffle, 0)


def _s5_out_kernel(z_ref, k_ref, sel_ref, dc_ref, dq_ref, rep_ref, til_ref, hfre, hfim, hbre, hbim, dsk_ref,
                   o_ref, lhs_ref, res_ref, w_ref):
    h = z_ref.shape[0] // GPS
    npair = z_ref.shape[1]
    t = z_ref.shape[-1] // 2
    ht = h * t
    p = dc_ref.shape[1] // 4
    lo = lax.broadcasted_iota(jnp.int32, z_ref.shape[1:], 1) < t
    for gi in range(GPS):
        _expand_toeplitz(k_ref, gi, sel_ref, lhs_ref, res_ref, w_ref, h, t)
        ce = _dot(dc_ref[gi], rep_ref[...])
        qe = _dot(dq_ref[gi], til_ref[...])
        dmat = (ce[:, :ht] * qe[:, :ht] + ce[:, ht:] * qe[:, ht:]).astype(BF16)
        u = _gather_chunks(z_ref, gi, h)
        sl = slice(gi * p, (gi + 1) * p)
        hp = jnp.concatenate([hfre[:, sl], hfim[:, sl], hbre[:, sl], hbim[:, sl]], axis=-1).astype(BF16)
        y = _dot(u.astype(BF16), w_ref[...]) + _dot(hp, dmat)
        gl = _gelu_tanh(y + u * dsk_ref[gi])
        for k in range(0, h, 2):
            te = gl[0:npair, k * t:(k + 2) * t]
            to = gl[npair:2 * npair, k * t:(k + 2) * t]
            o_ref[gi * h + k] = jnp.where(lo, te, pltpu.roll(to, t, axis=1))
            o_ref[gi * h + k + 1] = jnp.where(lo, pltpu.roll(te, t, axis=1), to)


def _s5_out(z3, kall, sel, d_tabs, expanders, hin, dsk):
    gh, npair, t2z = z3.shape
    nchunk, t = 2 * npair, t2z // 2
    g, hh, t2 = kall.shape
    h = gh // g
    ht = h * t
    assert hh == h * h and t2 == 2 * t and (h // 2) * 8 == t and t2 == LANES
    dc, dq = d_tabs
    rep, til = expanders
    p4 = dc.shape[1]
    p = p4 // 4
    hspec = pl.BlockSpec((nchunk, GPS * p), lambda i: (0, i))
    const = lambda a: pl.BlockSpec(a.shape, lambda i: (0, 0))
    return pl.pallas_call(
        _s5_out_kernel,
        grid=(g // GPS,),
        in_specs=[pl.BlockSpec((GPS * h, npair, t2z), lambda i: (i, 0, 0)),
                  pl.BlockSpec((GPS, hh, t2), lambda i: (i, 0, 0)),
                  const(sel),
                  pl.BlockSpec((GPS, p4, 2 * h), lambda i: (i, 0, 0)),
                  pl.BlockSpec((GPS, p4, 2 * t), lambda i: (i, 0, 0)),
                  const(rep), const(til)] + [hspec] * 4 +
                 [pl.BlockSpec((GPS, 1, ht), lambda i: (i, 0, 0))],
        out_specs=pl.BlockSpec((GPS * h, npair, t2z), lambda i: (i, 0, 0)),
        out_shape=jax.ShapeDtypeStruct((gh, npair, t2z), F32),
        scratch_shapes=[pltpu.VMEM((ht, 4 * t), F32), pltpu.VMEM((ht, ht), F32), pltpu.VMEM((ht, ht), BF16)],
        compiler_params=_cparams("parallel"),
        name="s5_chunk_out",
    )(z3, kall, sel, dc, dq, rep, til, *hin, dsk)


def _glu_kernel(gt_ref, wa_ref, wb_ref, ba_ref, bb_ref, o_ref, g_scr, stage_ref):
    @pl.when(pl.program_id(1) == 0)
    def _():
        c, tq, _ = gt_ref.shape
        half = stage_ref.shape[0] // tq
        for c0 in range(0, c, half):
            stage_ref[...] = gt_ref[c0:c0 + half].reshape(half * tq, LANES)
            for q in range(tq):
                g_scr[c0:c0 + half, q * LANES:(q + 1) * LANES] = stage_ref[pl.ds(q, half, stride=tq), :].astype(BF16)

    dn = (((0,), (0,)), ((), ()))
    gt = g_scr[...]
    a = lax.dot_general(gt, wa_ref[...], dn, preferred_element_type=F32) + ba_ref[...]
    b = lax.dot_general(gt, wb_ref[...], dn, preferred_element_type=F32) + bb_ref[...]
    o_ref[...] = (a * jax.nn.sigmoid(b)).astype(o_ref.dtype)


def _glu(g3, wa, wb, ba, bb, tm=1024, tn=512):
    c, nq, _ = g3.shape
    n = nq * LANES
    co = wa.shape[-1]
    tm, tn = min(tm, n), min(tn, co)
    stage_rows = (c // 2) * (tm // LANES)
    return pl.pallas_call(
        _glu_kernel,
        grid=(n // tm, co // tn),
        in_specs=[pl.BlockSpec((c, tm // LANES, LANES), lambda i, j: (0, i, 0)),
                  pl.BlockSpec((c, tn), lambda i, j: (0, j)),
                  pl.BlockSpec((c, tn), lambda i, j: (0, j)),
                  pl.BlockSpec((1, tn), lambda i, j: (0, j)),
                  pl.BlockSpec((1, tn), lambda i, j: (0, j))],
        out_specs=pl.BlockSpec((tm, tn), lambda i, j: (i, j)),
        out_shape=jax.ShapeDtypeStruct((n, co), BF16),
        scratch_shapes=[pltpu.VMEM((c, tm), BF16), pltpu.VMEM((stage_rows, LANES), F32)],
        compiler_params=_cparams("parallel", "arbitrary"),
        name="s5_glu",
    )(g3, wa, wb, ba.reshape(1, co), bb.reshape(1, co))


def _out_proj_kernel(yf_ref, ys_ref, wf_ref, ws_ref, x_ref, g_ref, o_ref):
    acc = _dot(yf_ref[...], wf_ref[...]) + _dot(ys_ref[...], ws_ref[...])
    o_ref[...] = x_ref[...] + g_ref[...] * acc


def _out_proj(yf, ys, w_out, x2, gate, length, tm=1024, tn=1024):
    n, fw = yf.shape
    sw = ys.shape[-1]
    d = w_out.shape[-1]
    tm, tn = min(tm, length), min(tn, d)
    assert fw % sw == 0
    bsz = gate.shape[0]
    return pl.pallas_call(
        _out_proj_kernel,
        grid=(n // tm, d // tn),
        in_specs=[pl.BlockSpec((tm, fw), lambda i, j: (i, 0)),
                  pl.BlockSpec((tm, sw), lambda i, j: (i, 0)),
                  pl.BlockSpec((fw, tn), lambda i, j: (0, j)),
                  pl.BlockSpec((sw, tn), lambda i, j: (fw // sw, j)),
                  pl.BlockSpec((tm, tn), lambda i, j: (i, j)),
                  pl.BlockSpec((None, 1, tn), lambda i, j: ((i * tm) // length, 0, j))],
        out_specs=pl.BlockSpec((tm, tn), lambda i, j: (i, j)),
        out_shape=jax.ShapeDtypeStruct((n, d), F32),
        compiler_params=_cparams("parallel", "arbitrary"),
        name="out_proj_residual",
    )(yf, ys, w_out, w_out, x2, gate.reshape(bsz, 1, d))


def _ffn_hidden_start(j, th, hid):
    return pl.multiple_of(jnp.minimum(j * th, hid - th), math.gcd(th, hid - th))


def _ffn_kernel(hid, h_ref, wg_ref, wu_ref, wd_ref, o_ref):
    j = pl.program_id(1)
    th = wg_ref.shape[-1]
    hh = h_ref[...]
    g = _dot(hh, wg_ref[...].astype(BF16))
    u = _dot(hh, wu_ref[...].astype(BF16))
    unit = _ffn_hidden_start(j, th, hid) + lax.broadcasted_iota(jnp.int32, (1, th), 1)
    a = jnp.where(unit >= j * th, g * jax.nn.sigmoid(g) * u, 0.0).astype(BF16)
    d = o_ref.shape[-1]
    nc = min(FFN_DOWN_CHUNK, d)
    wd = wd_ref[...].astype(BF16)

    @pl.when(j == 0)
    def _():
        for c0 in range(0, d, nc):
            o_ref[:, c0:c0 + nc] = _dot(a, wd[:, c0:c0 + nc])

    @pl.when(j > 0)
    def _():
        for c0 in range(0, d, nc):
            o_ref[:, c0:c0 + nc] += _dot(a, wd[:, c0:c0 + nc])


FFN_DOWN_CHUNK = 512


FFN_TH = 256
FFN_VMEM_LIMIT = 60 * 1024 * 1024


def _ffn(h, wg, wu, wd, tm=1024):
    n, d = h.shape
    hid = wg.shape[-1]
    tm, th = min(tm, n), min(FFN_TH, hid)
    col_map = lambda i, j: (0, _ffn_hidden_start(j, th, hid))
    return pl.pallas_call(
        functools.partial(_ffn_kernel, hid),
        grid=(n // tm, pl.cdiv(hid, th)),
        in_specs=[pl.BlockSpec((tm, d), lambda i, j: (i, 0), pipeline_mode=pl.Buffered(1)),
                  pl.BlockSpec((pl.Element(d), pl.Element(th)), col_map),
                  pl.BlockSpec((pl.Element(d), pl.Element(th)), col_map),
                  pl.BlockSpec((pl.Element(th), pl.Element(d)), lambda i, j: (_ffn_hidden_start(j, th, hid), 0))],
        out_specs=pl.BlockSpec((tm, d), lambda i, j: (i, 0), pipeline_mode=pl.Buffered(1)),
        out_shape=jax.ShapeDtypeStruct((n, d), F32),
        compiler_params=_cparams("parallel", "arbitrary", vmem=FFN_VMEM_LIMIT),
        name="swiglu_ffn",
    )(h, wg, wu, wd)


def _final_kernel(x_ref, f_ref, gate_ref, g_ref, o_ref):
    x = x_ref[...] + gate_ref[...] * f_ref[...]
    ms = jnp.mean(x * x, axis=-1, keepdims=True)
    o_ref[...] = x * lax.rsqrt(ms + EPS) * g_ref[...]


def _final(x1, f, gate, g, tm=256):
    bsz, length, d = x1.shape
    tm = min(tm, length)
    return pl.pallas_call(
        _final_kernel,
        grid=(bsz, length // tm),
        in_specs=[pl.BlockSpec((None, tm, d), lambda b, i: (b, i, 0)),
                  pl.BlockSpec((None, tm, d), lambda b, i: (b, i, 0)),
                  pl.BlockSpec((None, 1, d), lambda b, i: (b, 0, 0)),
                  pl.BlockSpec((1, d), lambda b, i: (0, 0))],
        out_specs=pl.BlockSpec((None, tm, d), lambda b, i: (b, i, 0)),
        out_shape=jax.ShapeDtypeStruct((bsz, length, d), F32),
        compiler_params=_cparams("parallel", "parallel"),
        name="residual_final_norm",
    )(x1, f, gate.reshape(bsz, 1, d), g.reshape(1, d))


def _dft_split(length):
    n2 = 64
    while (length // n2) % PACK_ROWS:
        n2 //= 2
    assert n2 >= 8 and length % n2 == 0
    return n2


def kernel(x, c, ctx, c_ctx, ada_w, ada_b, norm1_g, norm2_g, w_in, w_out, fourier_w, s5_lam_re, s5_lam_im, s5_log_dt, s5_b_re, s5_b_im, s5_c_re, s5_c_im, s5_d, glu_w_a, glu_b_a, glu_w_b, glu_b_b, ffn_w_gate, ffn_w_up, ffn_w_down, final_g):
    bsz, length, d = x.shape
    depth = ada_w.shape[0]
    assert depth == 1, "single-layer block"
    lyr = 0
    heads, hd, _ = fourier_w.shape[1:]
    fw = heads * hd
    _, g, p, hgrp = s5_b_re.shape[1:]
    sw = g * hgrp
    assert w_in.shape[-1] == fw + sw and length % (2 * CHUNK) == 0 and ctx.shape[1] % (2 * CHUNK) == 0
    n = bsz * length

    a8 = jnp.zeros((8, d), F32).at[:bsz].set(c.astype(F32)).at[bsz].set(c_ctx.astype(F32))
    mods = _ada(jnp.concatenate([a8, a8], axis=0), ada_w[lyr], ada_b[lyr]).reshape(8, N_MOD, d)
    sh1, sc1, g1, sh2, sc2, g2 = (mods[:bsz, i] for i in range(N_MOD))
    csh1, csc1 = mods[bsz:bsz + 1, 0], mods[bsz:bsz + 1, 1]

    w_in_b = w_in[lyr].astype(BF16)
    w_s_t = jnp.transpose(w_in_b[:, fw:])
    ang = (2.0 * np.pi / hd) * ((np.arange(hd)[:, None] * np.arange(hd)[None, :]) % hd).astype(np.float64)
    cd = jnp.asarray(np.cos(ang) / math.sqrt(hd), F32)
    sd = jnp.asarray(np.sin(ang) / math.sqrt(hd), F32)
    csd = jnp.broadcast_to(jnp.stack([cd, sd])[:, None], (2, heads, hd, hd)).reshape(2 * heads, hd, hd)
    wf2 = jnp.concatenate([fourier_w[lyr], fourier_w[lyr]], axis=0).astype(F32)
    folded = _fold(csd, wf2)
    wcs = jnp.concatenate([folded[:heads], folded[heads:]], axis=-1).astype(BF16)
    kall, e_mat, d_mat, dec = _s5_tables(s5_lam_re[lyr], s5_lam_im[lyr], s5_log_dt[lyr], s5_b_re[lyr],
                                         s5_b_im[lyr], s5_c_re[lyr], s5_c_im[lyr])
    sel = _toeplitz_select(CHUNK)
    dsk = jnp.repeat(s5_d[lyr].astype(F32).reshape(g, 1, hgrp), CHUNK, axis=-1)

    hc = _norm_mod(ctx, norm1_g[lyr], csh1, csc1)
    nctx_tok = bsz * ctx.shape[1]
    ctx_s = _s5_states(_proj_t(w_s_t, hc.reshape(nctx_tok, d)), e_mat)

    hm = _norm_mod(x, norm1_g[lyr], sh1, sc1).reshape(n, d)
    n2 = _dft_split(length)
    pc4, ps4 = _four_in(hm, w_in_b, wcs, bsz, length, n2)
    g_tab, cs_tab = _dft_tables(length, n2)
    y_four = _dft(pc4, ps4, g_tab, cs_tab).reshape(n, fw)

    z3 = _proj_t(w_s_t, hm)
    lat_s = _s5_states(z3, e_mat)
    h_in = _s5_scan(dec, ctx_s, lat_s, bsz)
    g3 = _s5_out(z3, kall, sel, d_mat, _carry_expanders(hgrp, CHUNK), h_in, dsk)
    y_s = _glu(g3, glu_w_a[lyr].astype(BF16), glu_w_b[lyr].astype(BF16), glu_b_a[lyr], glu_b_b[lyr])

    x1 = _out_proj(y_four, y_s, w_out[lyr].astype(BF16), x.reshape(n, d), g1, length)

    hm2 = _norm_mod(x1.reshape(bsz, length, d), norm2_g[lyr], sh2, sc2).reshape(n, d)
    f = _ffn(hm2, ffn_w_gate[lyr], ffn_w_up[lyr], ffn_w_down[lyr])
    return _final(x1.reshape(bsz, length, d), f.reshape(bsz, length, d), g2, final_g)
```

```python
import functools
import math

import numpy as np
import jax
import jax.numpy as jnp
from jax import lax
from jax.experimental import pallas as pl
from jax.experimental.pallas import tpu as pltpu

F32 = jnp.float32
BF16 = jnp.bfloat16
EPS = 1e-6
CHUNK = 64
N_MOD = 6
V7X_VMEM_LIMIT = 56 * 1024 * 1024
HI = lax.Precision.HIGHEST


def _cparams(*sem, vmem=V7X_VMEM_LIMIT):
    return pltpu.CompilerParams(dimension_semantics=sem, vmem_limit_bytes=vmem)


def _dot(a, b):
    return jnp.dot(a, b, preferred_element_type=F32)


def _ada_kernel(a_ref, w_ref, b_ref, o_ref):
    a = a_ref[...]
    s = a * jax.nn.sigmoid(a)
    s_hi = s.astype(BF16).astype(F32)
    row = lax.broadcasted_iota(jnp.int32, s.shape, 0)
    lhs = jnp.where(row < 8, s_hi, s - s_hi).astype(BF16)
    w = w_ref[...]
    w_hi = w.astype(BF16)
    w_lo = (w - w_hi.astype(F32)).astype(BF16)
    r = _dot(lhs, w_hi) + _dot(lhs, w_lo)
    o_ref[...] = r[0:8] + r[8:16] + b_ref[...]


def _ada(a16, w, b, tn=1024):
    d, n = w.shape
    tn = min(tn, n)
    return pl.pallas_call(
        _ada_kernel,
        grid=(n // tn,),
        in_specs=[pl.BlockSpec((16, d), lambda j: (0, 0)),
                  pl.BlockSpec((d, tn), lambda j: (0, j)),
                  pl.BlockSpec((1, tn), lambda j: (0, j))],
        out_specs=pl.BlockSpec((8, tn), lambda j: (0, j)),
        out_shape=jax.ShapeDtypeStruct((8, n), F32),
        compiler_params=_cparams("parallel"),
        name="ada_matvec",
    )(a16, w, b.reshape(1, n))


def _norm_mod_kernel(x_ref, g_ref, sh_ref, sc_ref, o_ref):
    x = x_ref[...]
    ms = jnp.mean(x * x, axis=-1, keepdims=True)
    y = x * lax.rsqrt(ms + EPS) * g_ref[...]
    o_ref[...] = (y * (1.0 + sc_ref[...]) + sh_ref[...]).astype(o_ref.dtype)


def _norm_mod(x, g, sh, sc, tm=512):
    bsz, length, d = x.shape
    tm = min(tm, length)
    bm = sh.shape[0]
    mod_map = (lambda b, i: (b, 0, 0)) if bm == bsz else (lambda b, i: (0, 0, 0))
    return pl.pallas_call(
        _norm_mod_kernel,
        grid=(bsz, length // tm),
        in_specs=[pl.BlockSpec((None, tm, d), lambda b, i: (b, i, 0)),
                  pl.BlockSpec((1, d), lambda b, i: (0, 0)),
                  pl.BlockSpec((None, 1, d), mod_map),
                  pl.BlockSpec((None, 1, d), mod_map)],
        out_specs=pl.BlockSpec((None, tm, d), lambda b, i: (b, i, 0)),
        out_shape=jax.ShapeDtypeStruct((bsz, length, d), BF16),
        compiler_params=_cparams("parallel", "parallel"),
        name="norm_mod",
    )(x, g.reshape(1, d), sh.reshape(bm, 1, d), sc.reshape(bm, 1, d))


def _fold_kernel(a_ref, b_ref, o_ref):
    o_ref[...] = jnp.dot(a_ref[...], b_ref[...], preferred_element_type=F32, precision=HI)


def _fold(a, b):
    hh, m, k = a.shape
    n = b.shape[-1]
    return pl.pallas_call(
        _fold_kernel,
        grid=(hh,),
        in_specs=[pl.BlockSpec((None, m, k), lambda h: (h, 0, 0)),
                  pl.BlockSpec((None, k, n), lambda h: (h, 0, 0))],
        out_specs=pl.BlockSpec((None, m, n), lambda h: (h, 0, 0)),
        out_shape=jax.ShapeDtypeStruct((hh, m, n), F32),
        compiler_params=_cparams("parallel"),
        name="weight_fold",
    )(a, b)


LANES = 128
PACK_ROWS = 16
PITCH_PAD = 8


def _four_in_kernel(h_ref, w_ref, wcs_ref, pc_ref, ps_ref, p_scr):
    n2, tn1, hd = pc_ref.shape
    pitch = n2 + PITCH_PAD
    nq = hd // LANES
    z = _dot(h_ref[...], w_ref[...]).astype(BF16)
    p = _dot(z, wcs_ref[...])
    for q in range(2 * nq):
        for i1 in range(tn1):
            p_scr[q, i1 * pitch:i1 * pitch + n2, :] = p[i1 * n2:(i1 + 1) * n2, q * LANES:(q + 1) * LANES]

    def emit(j2, carry):
        for q in range(nq):
            pc_ref[j2, :, q * LANES:(q + 1) * LANES] = p_scr[q, pl.ds(j2, tn1, stride=pitch), :].astype(BF16)
            ps_ref[j2, :, q * LANES:(q + 1) * LANES] = p_scr[nq + q, pl.ds(j2, tn1, stride=pitch), :].astype(BF16)
        return carry

    lax.fori_loop(0, n2, emit, 0, unroll=True)


def _four_in(h, w_f, wcs, bsz, length, n2):
    n, d = h.shape
    heads, hd, _ = wcs.shape
    n1 = length // n2
    tn1 = PACK_ROWS
    tm = tn1 * n2
    tpb = length // tm
    fw = heads * hd
    out = jax.ShapeDtypeStruct((bsz, n2, n1, fw), BF16)
    ospec = pl.BlockSpec((None, n2, tn1, hd), lambda i, j: (i // tpb, 0, i % tpb, j))
    return pl.pallas_call(
        _four_in_kernel,
        grid=(n // tm, heads),
        in_specs=[pl.BlockSpec((tm, d), lambda i, j: (i, 0)),
                  pl.BlockSpec((d, hd), lambda i, j: (0, j)),
                  pl.BlockSpec((None, hd, 2 * hd), lambda i, j: (j, 0, 0))],
        out_specs=[ospec, ospec],
        out_shape=[out, out],
        scratch_shapes=[pltpu.VMEM((2 * hd // LANES, tn1 * (n2 + PITCH_PAD), LANES), F32)],
        compiler_params=_cparams("parallel", "arbitrary"),
        name="fourier_in_proj",
    )(h, w_f, wcs)


def _dft_tables(length, n2):
    n1 = length // n2
    k1 = np.arange(n1)[:, None, None]
    i1 = np.arange(n1)[None, :, None]
    i2 = np.arange(n2)[None, None, :]
    phase = (k1 * (n2 * i1 + i2)) % length
    phi = (2.0 * np.pi / length) * phase.astype(np.float64)
    c = np.cos(phi).transpose(2, 0, 1) / math.sqrt(n1)
    s = np.sin(phi).transpose(2, 0, 1) / math.sqrt(n1)
    g = np.concatenate([np.concatenate([c, -s], axis=2),
                        np.concatenate([-s, -c], axis=2)], axis=1)
    k2 = np.arange(n2)[:, None]
    j2 = np.arange(n2)[None, :]
    th = (2.0 * np.pi / n2) * ((k2 * j2) % n2).astype(np.float64)
    cs = np.concatenate([np.cos(th), np.sin(th)], axis=1) / math.sqrt(n2)
    return jnp.asarray(g, dtype=BF16), jnp.asarray(cs, dtype=BF16)


def _dft_kernel(g_ref, cs_ref, pc_ref, ps_ref, o_ref, t_scr, y_scr):
    n2, n1, _ = pc_ref.shape
    m = 2 * n1
    p1 = m + PITCH_PAD
    p2 = n1 + PITCH_PAD

    def stage1(j2, carry):
        rhs = jnp.concatenate([pc_ref[j2], ps_ref[j2]], axis=0)
        t_scr[pl.ds(pl.multiple_of(j2 * p1, 8), m), :] = _dot(g_ref[j2], rhs)
        return carry

    lax.fori_loop(0, n2, stage1, 0, unroll=True)

    cs = cs_ref[...]

    def stage2(i, carry):
        k1 = 2 * i
        cols = []
        for dk in range(2):
            re = t_scr[pl.ds(k1 + dk, n2, stride=p1), :]
            im = t_scr[pl.ds(n1 + k1 + dk, n2, stride=p1), :]
            cols.append(jnp.concatenate([re, im], axis=0).astype(BF16))
        res = _dot(cs, jnp.concatenate(cols, axis=1))
        for dk in range(2):
            y_scr[pl.ds(k1 + dk, n2, stride=p2), :] = res[:, dk * LANES:(dk + 1) * LANES]
        return carry

    lax.fori_loop(0, n1 // 2, stage2, 0, unroll=True)
    for k2 in range(n2):
        o_ref[k2 * n1:(k2 + 1) * n1, :] = y_scr[k2 * p2:k2 * p2 + n1, :].astype(o_ref.dtype)


def _dft(pc4, ps4, g, cs):
    bsz, n2, n1, w = pc4.shape
    m = 2 * n1
    ispec = pl.BlockSpec((None, n2, n1, LANES), lambda b, j: (b, 0, 0, j))
    return pl.pallas_call(
        _dft_kernel,
        grid=(bsz, w // LANES),
        in_specs=[pl.BlockSpec((n2, m, m), lambda b, j: (0, 0, 0), pipeline_mode=pl.Buffered(1)),
                  pl.BlockSpec((n2, 2 * n2), lambda b, j: (0, 0)),
                  ispec, ispec],
        out_specs=pl.BlockSpec((None, n2 * n1, LANES), lambda b, j: (b, 0, j)),
        out_shape=jax.ShapeDtypeStruct((bsz, n2 * n1, w), BF16),
        scratch_shapes=[pltpu.VMEM((n2 * (m + PITCH_PAD), LANES), F32),
                        pltpu.VMEM((n2 * (n1 + PITCH_PAD), LANES), F32)],
        compiler_params=_cparams("parallel", "parallel"),
        name="position_dft",
    )(g, cs, pc4, ps4)


def _nt_kernel(w_ref, h_ref, o_ref, stage_ref):
    acc = lax.dot_general(w_ref[...], h_ref[...], (((0,), (1,)), ((), ())), preferred_element_type=F32)
    rows, tq, _ = o_ref.shape
    if tq % 8 == 0:
        for q in range(tq):
            stage_ref[pl.ds(q, rows, stride=tq), :] = acc[:, q * LANES:(q + 1) * LANES]
        o_ref[...] = stage_ref[...].reshape(rows, tq, LANES)
    else:
        for q in range(tq):
            o_ref[:, q, :] = acc[:, q * LANES:(q + 1) * LANES]


def _proj_t(w, col0, c, h, tmc=1024, tn=1024):
    d = w.shape[0]
    n = h.shape[0]
    tmc, tn = min(tmc, c), min(tn, n)
    assert col0 % tmc == 0 and c % tmc == 0
    return pl.pallas_call(
        _nt_kernel,
        grid=(n // tn, c // tmc),
        in_specs=[pl.BlockSpec((d, tmc), lambda i, j: (0, col0 // tmc + j)),
                  pl.BlockSpec((tn, d), lambda i, j: (i, 0))],
        out_specs=pl.BlockSpec((tmc, tn // LANES, LANES), lambda i, j: (j, i, 0)),
        out_shape=jax.ShapeDtypeStruct((c, n // LANES, LANES), F32),
        scratch_shapes=[pltpu.VMEM((tmc * (tn // LANES), LANES), F32)],
        compiler_params=_cparams("parallel", "arbitrary"),
        name="s5_in_proj_t",
    )(w, h)


def _s5_tables(lam_re, lam_im, log_dt, b_re, b_im, c_re, c_im):
    t = CHUNK
    _, g, p = lam_re.shape
    h = b_re.shape[-1]
    dt = jnp.exp(log_dt.astype(F32))[..., None]
    lr, li = jnp.minimum(lam_re.astype(F32), -1e-4), lam_im.astype(F32)
    ar, ai = lr * dt, li * dt

    lbm = jnp.exp(ar)
    lbr, lbi = lbm * jnp.cos(ai), lbm * jnp.sin(ai)

    tab_r, tab_i = jnp.ones((2, g, p, 1), F32), jnp.zeros((2, g, p, 1), F32)
    er, ei = lbr[..., None], lbi[..., None]
    while tab_r.shape[-1] < 2 * t:
        tab_r, tab_i = (jnp.concatenate([tab_r, tab_r * er - tab_i * ei], axis=-1),
                        jnp.concatenate([tab_i, tab_r * ei + tab_i * er], axis=-1))
        er, ei = er * er - ei * ei, 2.0 * er * ei

    def powers(d, first, step, steps_major=False):
        first = int(first)
        if step > 0:
            out = tab_r[d, :, :, first:first + t], tab_i[d, :, :, first:first + t]
        else:
            out = (jnp.flip(tab_r[d, :, :, first - t + 1:first + 1], -1),
                   jnp.flip(tab_i[d, :, :, first - t + 1:first + 1], -1))
        return tuple(jnp.swapaxes(a, 1, 2) for a in out) if steps_major else out

    nr, ni = lbr - 1.0, lbi
    den = lr * lr + li * li
    qr, qi = (nr * lr + ni * li) / den, (ni * lr - nr * li) / den
    br, bi = (jnp.swapaxes(a.astype(F32), -1, -2) for a in (b_re, b_im))
    bbr = qr[:, :, None] * br - qi[:, :, None] * bi
    bbi = qr[:, :, None] * bi + qi[:, :, None] * br
    cr, ci = c_re.astype(F32), c_im.astype(F32)

    mr = (bbr[:, :, :, None] * cr[:, :, None] - bbi[:, :, :, None] * ci[:, :, None]).reshape(2, g, h * h, p)
    mi = (bbr[:, :, :, None] * ci[:, :, None] + bbi[:, :, :, None] * cr[:, :, None]).reshape(2, g, h * h, p)

    def lag_kernel(d, first, step):
        wr, wi = powers(d, first, step)
        return (jnp.einsum('gxp,gpt->gxt', mr[d], wr, precision=HI)
                - jnp.einsum('gxp,gpt->gxt', mi[d], wi, precision=HI))

    kf = lag_kernel(0, 0.0, 1.0)
    kbr = lag_kernel(1, t - 1.0, -1.0)
    kall = jnp.concatenate([kbr[..., :t - 1], kf[..., :1] + kbr[..., t - 1:], kf[..., 1:],
                            jnp.zeros_like(kf[..., :1])], axis=-1)

    pfr, pfi = powers(0, t - 1.0, -1.0, steps_major=True)
    pbr, pbi = powers(1, 0.0, 1.0, steps_major=True)
    ew_a = jnp.concatenate([pfr, pfr, pbr, pbr], axis=-1)
    ew_b = jnp.concatenate([pfi, pfi, pbi, pbi], axis=-1)
    eb_a = jnp.concatenate([bbr[0], bbi[0], bbr[1], bbi[1]], axis=-1)
    eb_b = jnp.concatenate([-bbi[0], bbr[0], -bbi[1], bbr[1]], axis=-1)

    qfr, qfi = powers(0, 1.0, 1.0)
    qbr, qbi = powers(1, float(t), -1.0)
    crf, cif, crb, cib = (jnp.swapaxes(a, 1, 2) for a in (cr[0], ci[0], cr[1], ci[1]))
    dc = jnp.concatenate([jnp.concatenate([crf, -crf, crb, -crb], axis=1),
                          jnp.concatenate([-cif, -cif, -cib, -cib], axis=1)], axis=-1)
    dq = jnp.concatenate([jnp.concatenate([qfr, qfi, qbr, qbi], axis=1),
                          jnp.concatenate([qfi, qfr, qbi, qbr], axis=1)], axis=-1)

    dec = jnp.stack([tab_r[0, :, :, t], tab_i[0, :, :, t],
                     tab_r[1, :, :, t], tab_i[1, :, :, t]]).reshape(4, g * p)
    return kall, (ew_a, ew_b, eb_a, eb_b), (dc.astype(BF16), dq.astype(BF16)), dec


def _carry_expanders(h, t):
    ht = h * t
    rep = np.zeros((2 * h, 2 * ht), np.float32)
    til = np.zeros((2 * t, 2 * ht), np.float32)
    for half in range(2):
        for ho in range(h):
            for tt in range(t):
                rep[half * h + ho, half * ht + ho * t + tt] = 1.0
                til[half * t + tt, half * ht + ho * t + tt] = 1.0
    return jnp.asarray(rep, dtype=BF16), jnp.asarray(til, dtype=BF16)


def _toeplitz_select(t):
    sel = np.zeros((4 * t, (t // 8) * 2 * t), np.float32)
    for q in range(t // 8):
        for tt in range(t):
            j = tt - 8 * q + t - 8
            sel[j, q * 2 * t + tt] = 1.0
            sel[2 * t + j, q * 2 * t + t + tt] = 1.0
    return jnp.asarray(sel, dtype=BF16)


GPS = 2


def _gather_chunks(z_ref, gi, h):
    t = z_ref.shape[-1] // 2
    lo = lax.broadcasted_iota(jnp.int32, z_ref.shape[1:], 1) < t
    ev, od = [], []
    for k in range(0, h, 2):
        za, zb = z_ref[gi * h + k], z_ref[gi * h + k + 1]
        ev.append(jnp.where(lo, za, pltpu.roll(zb, t, axis=1)))
        od.append(jnp.where(lo, pltpu.roll(za, t, axis=1), zb))
    return jnp.concatenate([jnp.concatenate(ev, axis=1), jnp.concatenate(od, axis=1)], axis=0)


def _s5_state_kernel(z_ref, ewa_ref, ewb_ref, eba_ref, ebb_ref, fre_ref, fim_ref, bre_ref, bim_ref, e_scr):
    h = z_ref.shape[0] // GPS
    t = ewa_ref.shape[1]
    p = ewa_ref.shape[-1] // 4
    outs = (fre_ref, fim_ref, bre_ref, bim_ref)
    for gi in range(GPS):
        wa, wb = ewa_ref[gi], ewb_ref[gi]
        for k in range(h):
            e_scr[k * t:(k + 1) * t, :] = (wa * eba_ref[gi, k:k + 1, :] + wb * ebb_ref[gi, k:k + 1, :]).astype(BF16)
        a2 = _gather_chunks(z_ref, gi, h).astype(BF16)
        s = _dot(a2, e_scr[...])
        for k in range(4):
            outs[k][:, gi * p:(gi + 1) * p] = s[:, k * p:(k + 1) * p]


def _s5_states(z3, e_tabs):
    gh, npair, t2 = z3.shape
    nchunk, t = 2 * npair, t2 // 2
    g, _, p4 = e_tabs[0].shape
    h, p = gh // g, p4 // 4
    out = jax.ShapeDtypeStruct((nchunk, g * p), F32)
    ospec = pl.BlockSpec((nchunk, GPS * p), lambda i: (0, i))
    wspec = pl.BlockSpec((GPS, t, p4), lambda i: (i, 0, 0))
    bspec = pl.BlockSpec((GPS, h, p4), lambda i: (i, 0, 0))
    return pl.pallas_call(
        _s5_state_kernel,
        grid=(g // GPS,),
        in_specs=[pl.BlockSpec((GPS * h, npair, t2), lambda i: (i, 0, 0)), wspec, wspec, bspec, bspec],
        out_specs=[ospec] * 4,
        out_shape=[out] * 4,
        scratch_shapes=[pltpu.VMEM((h * t, p4), BF16)],
        compiler_params=_cparams("parallel"),
        name="s5_chunk_states",
    )(z3, *e_tabs)


def _s5_scan_kernel(bsz, dec_ref, cfre, cfim, cbre, cbim, sfre, sfim, sbre, sbim,
                    hfre, hfim, hbre, hbim):
    nctx = cfre.shape[0] // bsz
    nlat = sfre.shape[0] // bsz
    width = dec_ref.shape[-1]
    fr, fi = dec_ref[0:1, :], dec_ref[1:2, :]
    br, bi = dec_ref[2:3, :], dec_ref[3:4, :]
    zero = jnp.zeros((1, width), F32)

    def step(ar, ai, hr, hi, sr, si):
        return ar * hr - ai * hi + sr, ar * hi + ai * hr + si

    def row(nchunk, b, j):
        return (j % 2) * (bsz * nchunk // 2) + b * (nchunk // 2) + j // 2

    for b in range(bsz):
        hr, hi = zero, zero
        for j in range(nctx):
            r = row(nctx, b, j)
            hr, hi = step(fr, fi, hr, hi, cfre[r:r + 1, :], cfim[r:r + 1, :])

        def fwd(j, carry, b=b):
            hr, hi = carry
            r = row(nlat, b, j)
            hfre[pl.ds(r, 1), :] = hr
            hfim[pl.ds(r, 1), :] = hi
            return step(fr, fi, hr, hi, sfre[pl.ds(r, 1), :], sfim[pl.ds(r, 1), :])

        lax.fori_loop(0, nlat, fwd, (hr, hi))

        hr, hi = zero, zero
        for j in range(nctx - 1, -1, -1):
            r = row(nctx, b, j)
            hr, hi = step(br, bi, hr, hi, cbre[r:r + 1, :], cbim[r:r + 1, :])

        def bwd(k, carry, b=b):
            hr, hi = carry
            r = row(nlat, b, nlat - 1 - k)
            hbre[pl.ds(r, 1), :] = hr
            hbim[pl.ds(r, 1), :] = hi
            return step(br, bi, hr, hi, sbre[pl.ds(r, 1), :], sbim[pl.ds(r, 1), :])

        lax.fori_loop(0, nlat, bwd, (hr, hi))


def _s5_scan(dec, ctx_s, lat_s, bsz, tw=1024):
    gp = dec.shape[-1]
    rc, rl = ctx_s[0].shape[0], lat_s[0].shape[0]
    assert (rc // bsz) % 2 == 0 and (rl // bsz) % 2 == 0
    tw = min(tw, gp)
    cspec = pl.BlockSpec((rc, tw), lambda i: (0, i))
    lspec = pl.BlockSpec((rl, tw), lambda i: (0, i))
    out = jax.ShapeDtypeStruct((rl, gp), F32)
    return pl.pallas_call(
        functools.partial(_s5_scan_kernel, bsz),
        grid=(gp // tw,),
        in_specs=[pl.BlockSpec((4, tw), lambda i: (0, i))] + [cspec] * 4 + [lspec] * 4,
        out_specs=[lspec] * 4,
        out_shape=[out] * 4,
        compiler_params=_cparams("parallel"),
        name="s5_state_scan",
    )(dec, *ctx_s, *lat_s)


def _gelu_tanh(x):
    return 0.5 * x * (1.0 + jnp.tanh(math.sqrt(2.0 / math.pi) * (x + 0.044715 * (x * x * x))))


def _expand_toeplitz(k_ref, gi, sel_ref, lhs_ref, res_ref, w_ref, h, t):
    half = h // 2
    rows_per_hi = half * 8

    def fill(hi, carry):
        for hp in range(half):
            for par in range(2):
                v = k_ref[gi, pl.ds(hi * h + 2 * hp + par, 1), :]
                b = pltpu.roll(jnp.broadcast_to(v, (8, 2 * t)), 2 * t - 7, axis=1, stride=1, stride_axis=0)
                lhs_ref[pl.ds(pl.multiple_of(hi * rows_per_hi + hp * 8, 8), 8), par * 2 * t:(par + 1) * 2 * t] = b
        return carry

    lax.fori_loop(0, h, fill, 0, unroll=True)
    res_ref[...] = _dot(lhs_ref[...].astype(BF16), sel_ref[...])

    def shuffle(hi, carry):
        base = pl.multiple_of(hi * rows_per_hi, rows_per_hi)
        r_hi = res_ref[pl.ds(base, rows_per_hi), :]
        rows = [jnp.concatenate([r_hi[hp * 8:(hp + 1) * 8, q * 2 * t:(q + 1) * 2 * t] for hp in range(half)], axis=1)
                for q in range(t // 8)]
        w_ref[pl.ds(base, t), :] = jnp.concatenate(rows, axis=0).astype(BF16)
        return carry

    lax.fori_loop(0, h, shuffle, 0, unroll=True)


def _s5_out_kernel(z_ref, k_ref, sel_ref, dc_ref, dq_ref, rep_ref, til_ref, hfre, hfim, hbre, hbim, dsk_ref,
                   o_ref, lhs_ref, res_ref, w_ref):
    h = z_ref.shape[0] // GPS
    npair = z_ref.shape[1]
    t = z_ref.shape[-1] // 2
    ht = h * t
    p = dc_ref.shape[1] // 4
    lo = lax.broadcasted_iota(jnp.int32, z_ref.shape[1:], 1) < t
    for gi in range(GPS):
        _expand_toeplitz(k_ref, gi, sel_ref, lhs_ref, res_ref, w_ref, h, t)
        ce = _dot(dc_ref[gi], rep_ref[...])
        qe = _dot(dq_ref[gi], til_ref[...])
        dmat = (ce[:, :ht] * qe[:, :ht] + ce[:, ht:] * qe[:, ht:]).astype(BF16)
        u = _gather_chunks(z_ref, gi, h)
        sl = slice(gi * p, (gi + 1) * p)
        hp = jnp.concatenate([hfre[:, sl], hfim[:, sl], hbre[:, sl], hbim[:, sl]], axis=-1).astype(BF16)
        y = _dot(u.astype(BF16), w_ref[...]) + _dot(hp, dmat)
        gl = _gelu_tanh(y + u * dsk_ref[gi])
        for k in range(0, h, 2):
            te = gl[0:npair, k * t:(k + 2) * t]
            to = gl[npair:2 * npair, k * t:(k + 2) * t]
            o_ref[gi * h + k] = jnp.where(lo, te, pltpu.roll(to, t, axis=1))
            o_ref[gi * h + k + 1] = jnp.where(lo, pltpu.roll(te, t, axis=1), to)


def _s5_out(z3, kall, sel, d_tabs, expanders, hin, dsk):
    gh, npair, t2z = z3.shape
    nchunk, t = 2 * npair, t2z // 2
    g, hh, t2 = kall.shape
    h = gh // g
    ht = h * t
    assert hh == h * h and t2 == 2 * t and (h // 2) * 8 == t and t2 == LANES
    dc, dq = d_tabs
    rep, til = expanders
    p4 = dc.shape[1]
    p = p4 // 4
    hspec = pl.BlockSpec((nchunk, GPS * p), lambda i: (0, i))
    const = lambda a: pl.BlockSpec(a.shape, lambda i: (0, 0))
    return pl.pallas_call(
        _s5_out_kernel,
        grid=(g // GPS,),
        in_specs=[pl.BlockSpec((GPS * h, npair, t2z), lambda i: (i, 0, 0)),
                  pl.BlockSpec((GPS, hh, t2), lambda i: (i, 0, 0)),
                  const(sel),
                  pl.BlockSpec((GPS, p4, 2 * h), lambda i: (i, 0, 0)),
                  pl.BlockSpec((GPS, p4, 2 * t), lambda i: (i, 0, 0)),
                  const(rep), const(til)] + [hspec] * 4 +
                 [pl.BlockSpec((GPS, 1, ht), lambda i: (i, 0, 0))],
        out_specs=pl.BlockSpec((GPS * h, npair, t2z), lambda i: (i, 0, 0)),
        out_shape=jax.ShapeDtypeStruct((gh, npair, t2z), F32),
        scratch_shapes=[pltpu.VMEM((ht, 4 * t), F32), pltpu.VMEM((ht, ht), F32), pltpu.VMEM((ht, ht), BF16)],
        compiler_params=_cparams("parallel"),
        name="s5_chunk_out",
    )(z3, kall, sel, dc, dq, rep, til, *hin, dsk)


def _glu_kernel(gt_ref, wa_ref, wb_ref, ba_ref, bb_ref, o_ref, g_scr, stage_ref):
    @pl.when(pl.program_id(1) == 0)
    def _():
        c, tq, _ = gt_ref.shape
        half = stage_ref.shape[0] // tq
        for c0 in range(0, c, half):
            stage_ref[...] = gt_ref[c0:c0 + half].reshape(half * tq, LANES)
            for q in range(tq):
                g_scr[c0:c0 + half, q * LANES:(q + 1) * LANES] = stage_ref[pl.ds(q, half, stride=tq), :].astype(BF16)

    dn = (((0,), (0,)), ((), ()))
    gt = g_scr[...]
    a = lax.dot_general(gt, wa_ref[...], dn, preferred_element_type=F32) + ba_ref[...]
    b = lax.dot_general(gt, wb_ref[...], dn, preferred_element_type=F32) + bb_ref[...]
    o_ref[...] = (a * jax.nn.sigmoid(b)).astype(o_ref.dtype)


def _glu(g3, wa, wb, ba, bb, tm=1024, tn=512):
    c, nq, _ = g3.shape
    n = nq * LANES
    co = wa.shape[-1]
    tm, tn = min(tm, n), min(tn, co)
    stage_rows = (c // 2) * (tm // LANES)
    return pl.pallas_call(
        _glu_kernel,
        grid=(n // tm, co // tn),
        in_specs=[pl.BlockSpec((c, tm // LANES, LANES), lambda i, j: (0, i, 0)),
                  pl.BlockSpec((c, tn), lambda i, j: (0, j)),
                  pl.BlockSpec((c, tn), lambda i, j: (0, j)),
                  pl.BlockSpec((1, tn), lambda i, j: (0, j)),
                  pl.BlockSpec((1, tn), lambda i, j: (0, j))],
        out_specs=pl.BlockSpec((tm, tn), lambda i, j: (i, j)),
        out_shape=jax.ShapeDtypeStruct((n, co), BF16),
        scratch_shapes=[pltpu.VMEM((c, tm), BF16), pltpu.VMEM((stage_rows, LANES), F32)],
        compiler_params=_cparams("parallel", "arbitrary"),
        name="s5_glu",
    )(g3, wa, wb, ba.reshape(1, co), bb.reshape(1, co))


def _out_proj_kernel(yf_ref, ys_ref, wf_ref, ws_ref, x_ref, g_ref, o_ref):
    acc = _dot(yf_ref[...], wf_ref[...]) + _dot(ys_ref[...], ws_ref[...])
    o_ref[...] = x_ref[...] + g_ref[...] * acc


def _out_proj(yf, ys, w_out, x2, gate, length, tm=1024, tn=1024):
    n, fw = yf.shape
    sw = ys.shape[-1]
    d = w_out.shape[-1]
    tm, tn = min(tm, length), min(tn, d)
    assert fw % sw == 0
    bsz = gate.shape[0]
    return pl.pallas_call(
        _out_proj_kernel,
        grid=(n // tm, d // tn),
        in_specs=[pl.BlockSpec((tm, fw), lambda i, j: (i, 0)),
                  pl.BlockSpec((tm, sw), lambda i, j: (i, 0)),
                  pl.BlockSpec((fw, tn), lambda i, j: (0, j)),
                  pl.BlockSpec((sw, tn), lambda i, j: (fw // sw, j)),
                  pl.BlockSpec((tm, tn), lambda i, j: (i, j)),
                  pl.BlockSpec((None, 1, tn), lambda i, j: ((i * tm) // length, 0, j))],
        out_specs=pl.BlockSpec((tm, tn), lambda i, j: (i, j)),
        out_shape=jax.ShapeDtypeStruct((n, d), F32),
        compiler_params=_cparams("parallel", "arbitrary"),
        name="out_proj_residual",
    )(yf, ys, w_out, w_out, x2, gate.reshape(bsz, 1, d))


def _ffn_hidden_start(j, th, hid):
    return pl.multiple_of(jnp.minimum(j * th, hid - th), math.gcd(th, hid - th))


def _ffn_kernel(hid, h_ref, wg_ref, wu_ref, wd_ref, o_ref):
    j = pl.program_id(1)
    th = wg_ref.shape[-1]
    hh = h_ref[...]
    g = _dot(hh, wg_ref[...].astype(BF16))
    u = _dot(hh, wu_ref[...].astype(BF16))
    unit = _ffn_hidden_start(j, th, hid) + lax.broadcasted_iota(jnp.int32, (1, th), 1)
    a = jnp.where(unit >= j * th, g * jax.nn.sigmoid(g) * u, 0.0).astype(BF16)
    d = o_ref.shape[-1]
    nc = min(FFN_DOWN_CHUNK, d)
    wd = wd_ref[...].astype(BF16)

    @pl.when(j == 0)
    def _():
        for c0 in range(0, d, nc):
            o_ref[:, c0:c0 + nc] = _dot(a, wd[:, c0:c0 + nc])

    @pl.when(j > 0)
    def _():
        for c0 in range(0, d, nc):
            o_ref[:, c0:c0 + nc] += _dot(a, wd[:, c0:c0 + nc])


FFN_DOWN_CHUNK = 512


FFN_TH = 256
FFN_VMEM_LIMIT = 60 * 1024 * 1024


def _ffn(h, wg, wu, wd, tm=1024):
    n, d = h.shape
    hid = wg.shape[-1]
    tm, th = min(tm, n), min(FFN_TH, hid)
    col_map = lambda i, j: (0, _ffn_hidden_start(j, th, hid))
    return pl.pallas_call(
        functools.partial(_ffn_kernel, hid),
        grid=(n // tm, pl.cdiv(hid, th)),
        in_specs=[pl.BlockSpec((tm, d), lambda i, j: (i, 0), pipeline_mode=pl.Buffered(1)),
                  pl.BlockSpec((pl.Element(d), pl.Element(th)), col_map),
                  pl.BlockSpec((pl.Element(d), pl.Element(th)), col_map),
                  pl.BlockSpec((pl.Element(th), pl.Element(d)), lambda i, j: (_ffn_hidden_start(j, th, hid), 0))],
        out_specs=pl.BlockSpec((tm, d), lambda i, j: (i, 0), pipeline_mode=pl.Buffered(1)),
        out_shape=jax.ShapeDtypeStruct((n, d), F32),
        compiler_params=_cparams("parallel", "arbitrary", vmem=FFN_VMEM_LIMIT),
        name="swiglu_ffn",
    )(h, wg, wu, wd)


def _final_kernel(x_ref, f_ref, gate_ref, g_ref, o_ref):
    x = x_ref[...] + gate_ref[...] * f_ref[...]
    ms = jnp.mean(x * x, axis=-1, keepdims=True)
    o_ref[...] = x * lax.rsqrt(ms + EPS) * g_ref[...]


def _final(x1, f, gate, g, tm=256):
    bsz, length, d = x1.shape
    tm = min(tm, length)
    return pl.pallas_call(
        _final_kernel,
        grid=(bsz, length // tm),
        in_specs=[pl.BlockSpec((None, tm, d), lambda b, i: (b, i, 0)),
                  pl.BlockSpec((None, tm, d), lambda b, i: (b, i, 0)),
                  pl.BlockSpec((None, 1, d), lambda b, i: (b, 0, 0)),
                  pl.BlockSpec((1, d), lambda b, i: (0, 0))],
        out_specs=pl.BlockSpec((None, tm, d), lambda b, i: (b, i, 0)),
        out_shape=jax.ShapeDtypeStruct((bsz, length, d), F32),
        compiler_params=_cparams("parallel", "parallel"),
        name="residual_final_norm",
    )(x1, f, gate.reshape(bsz, 1, d), g.reshape(1, d))


def _dft_split(length):
    n2 = 64
    while (length // n2) % PACK_ROWS:
        n2 //= 2
    assert n2 >= 8 and length % n2 == 0
    return n2


def kernel(x, c, ctx, c_ctx, ada_w, ada_b, norm1_g, norm2_g, w_in, w_out, fourier_w, s5_lam_re, s5_lam_im, s5_log_dt, s5_b_re, s5_b_im, s5_c_re, s5_c_im, s5_d, glu_w_a, glu_b_a, glu_w_b, glu_b_b, ffn_w_gate, ffn_w_up, ffn_w_down, final_g):
    bsz, length, d = x.shape
    depth = ada_w.shape[0]
    assert depth == 1, "single-layer block"
    lyr = 0
    heads, hd, _ = fourier_w.shape[1:]
    fw = heads * hd
    _, g, p, hgrp = s5_b_re.shape[1:]
    sw = g * hgrp
    assert w_in.shape[-1] == fw + sw and length % (2 * CHUNK) == 0 and ctx.shape[1] % (2 * CHUNK) == 0
    n = bsz * length

    a8 = jnp.zeros((8, d), F32).at[:bsz].set(c.astype(F32)).at[bsz].set(c_ctx.astype(F32))
    mods = _ada(jnp.concatenate([a8, a8], axis=0), ada_w[lyr], ada_b[lyr]).reshape(8, N_MOD, d)
    sh1, sc1, g1, sh2, sc2, g2 = (mods[:bsz, i] for i in range(N_MOD))
    csh1, csc1 = mods[bsz:bsz + 1, 0], mods[bsz:bsz + 1, 1]

    w_in_b = w_in[lyr].astype(BF16)
    ang = (2.0 * np.pi / hd) * ((np.arange(hd)[:, None] * np.arange(hd)[None, :]) % hd).astype(np.float64)
    cd = jnp.asarray(np.cos(ang) / math.sqrt(hd), F32)
    sd = jnp.asarray(np.sin(ang) / math.sqrt(hd), F32)
    csd = jnp.broadcast_to(jnp.stack([cd, sd])[:, None], (2, heads, hd, hd)).reshape(2 * heads, hd, hd)
    wf2 = jnp.concatenate([fourier_w[lyr], fourier_w[lyr]], axis=0).astype(F32)
    folded = _fold(csd, wf2)
    wcs = jnp.concatenate([folded[:heads], folded[heads:]], axis=-1).astype(BF16)
    kall, e_mat, d_mat, dec = _s5_tables(s5_lam_re[lyr], s5_lam_im[lyr], s5_log_dt[lyr], s5_b_re[lyr],
                                         s5_b_im[lyr], s5_c_re[lyr], s5_c_im[lyr])
    sel = _toeplitz_select(CHUNK)
    dsk = jnp.repeat(s5_d[lyr].astype(F32).reshape(g, 1, hgrp), CHUNK, axis=-1)

    hc = _norm_mod(ctx, norm1_g[lyr], csh1, csc1)
    nctx_tok = bsz * ctx.shape[1]
    ctx_s = _s5_states(_proj_t(w_in_b, fw, sw, hc.reshape(nctx_tok, d)), e_mat)

    hm = _norm_mod(x, norm1_g[lyr], sh1, sc1).reshape(n, d)
    n2 = _dft_split(length)
    pc4, ps4 = _four_in(hm, w_in_b, wcs, bsz, length, n2)
    g_tab, cs_tab = _dft_tables(length, n2)
    y_four = _dft(pc4, ps4, g_tab, cs_tab).reshape(n, fw)

    z3 = _proj_t(w_in_b, fw, sw, hm)
    lat_s = _s5_states(z3, e_mat)
    h_in = _s5_scan(dec, ctx_s, lat_s, bsz)
    g3 = _s5_out(z3, kall, sel, d_mat, _carry_expanders(hgrp, CHUNK), h_in, dsk)
    y_s = _glu(g3, glu_w_a[lyr].astype(BF16), glu_w_b[lyr].astype(BF16), glu_b_a[lyr], glu_b_b[lyr])

    x1 = _out_proj(y_four, y_s, w_out[lyr].astype(BF16), x.reshape(n, d), g1, length)

    hm2 = _norm_mod(x1.reshape(bsz, length, d), norm2_g[lyr], sh2, sc2).reshape(n, d)
    f = _ffn(hm2, ffn_w_gate[lyr], ffn_w_up[lyr], ffn_w_down[lyr])
    return _final(x1.reshape(bsz, length, d), f.reshape(bsz, length, d), g2, final_g)
```

```python
import functools
import math

import numpy as np
import jax
import jax.numpy as jnp
from jax import lax
from jax.experimental import pallas as pl
from jax.experimental.pallas import tpu as pltpu

F32 = jnp.float32
BF16 = jnp.bfloat16
EPS = 1e-6
CHUNK = 64
N_MOD = 6
V7X_VMEM_LIMIT = 56 * 1024 * 1024
HI = lax.Precision.HIGHEST


def _cparams(*sem, vmem=V7X_VMEM_LIMIT):
    return pltpu.CompilerParams(dimension_semantics=sem, vmem_limit_bytes=vmem)


def _dot(a, b):
    return jnp.dot(a, b, preferred_element_type=F32)


def _ada_kernel(a_ref, w_ref, b_ref, o_ref):
    a = a_ref[...]
    s = a * jax.nn.sigmoid(a)
    s_hi = s.astype(BF16).astype(F32)
    row = lax.broadcasted_iota(jnp.int32, s.shape, 0)
    lhs = jnp.where(row < 8, s_hi, s - s_hi).astype(BF16)
    w = w_ref[...]
    w_hi = w.astype(BF16)
    w_lo = (w - w_hi.astype(F32)).astype(BF16)
    r = _dot(lhs, w_hi) + _dot(lhs, w_lo)
    o_ref[...] = r[0:8] + r[8:16] + b_ref[...]


def _ada(a16, w, b, tn=1024):
    d, n = w.shape
    tn = min(tn, n)
    return pl.pallas_call(
        _ada_kernel,
        grid=(n // tn,),
        in_specs=[pl.BlockSpec((16, d), lambda j: (0, 0)),
                  pl.BlockSpec((d, tn), lambda j: (0, j)),
                  pl.BlockSpec((1, tn), lambda j: (0, j))],
        out_specs=pl.BlockSpec((8, tn), lambda j: (0, j)),
        out_shape=jax.ShapeDtypeStruct((8, n), F32),
        compiler_params=_cparams("parallel"),
        name="ada_matvec",
    )(a16, w, b.reshape(1, n))


def _norm_mod_kernel(x_ref, g_ref, sh_ref, sc_ref, o_ref):
    x = x_ref[...]
    ms = jnp.mean(x * x, axis=-1, keepdims=True)
    y = x * lax.rsqrt(ms + EPS) * g_ref[...]
    o_ref[...] = (y * (1.0 + sc_ref[...]) + sh_ref[...]).astype(o_ref.dtype)


def _norm_mod(x, g, sh, sc, tm=512):
    bsz, length, d = x.shape
    tm = min(tm, length)
    bm = sh.shape[0]
    mod_map = (lambda b, i: (b, 0, 0)) if bm == bsz else (lambda b, i: (0, 0, 0))
    return pl.pallas_call(
        _norm_mod_kernel,
        grid=(bsz, length // tm),
        in_specs=[pl.BlockSpec((None, tm, d), lambda b, i: (b, i, 0)),
                  pl.BlockSpec((1, d), lambda b, i: (0, 0)),
                  pl.BlockSpec((None, 1, d), mod_map),
                  pl.BlockSpec((None, 1, d), mod_map)],
        out_specs=pl.BlockSpec((None, tm, d), lambda b, i: (b, i, 0)),
        out_shape=jax.ShapeDtypeStruct((bsz, length, d), BF16),
        compiler_params=_cparams("parallel", "parallel"),
        name="norm_mod",
    )(x, g.reshape(1, d), sh.reshape(bm, 1, d), sc.reshape(bm, 1, d))


def _fold_kernel(a_ref, b_ref, o_ref):
    o_ref[...] = jnp.dot(a_ref[...], b_ref[...], preferred_element_type=F32, precision=HI)


def _fold(a, b):
    hh, m, k = a.shape
    n = b.shape[-1]
    return pl.pallas_call(
        _fold_kernel,
        grid=(hh,),
        in_specs=[pl.BlockSpec((None, m, k), lambda h: (h, 0, 0)),
                  pl.BlockSpec((None, k, n), lambda h: (h, 0, 0))],
        out_specs=pl.BlockSpec((None, m, n), lambda h: (h, 0, 0)),
        out_shape=jax.ShapeDtypeStruct((hh, m, n), F32),
        compiler_params=_cparams("parallel"),
        name="weight_fold",
    )(a, b)


LANES = 128
PACK_ROWS = 16
PITCH_PAD = 8


def _four_in_kernel(h_ref, w_ref, wcs_ref, pc_ref, ps_ref, p_scr):
    n2, tn1, hd = pc_ref.shape
    pitch = n2 + PITCH_PAD
    nq = hd // LANES
    z = _dot(h_ref[...], w_ref[...]).astype(BF16)
    p = _dot(z, wcs_ref[...])
    for q in range(2 * nq):
        for i1 in range(tn1):
            p_scr[q, i1 * pitch:i1 * pitch + n2, :] = p[i1 * n2:(i1 + 1) * n2, q * LANES:(q + 1) * LANES]

    def emit(j2, carry):
        for q in range(nq):
            pc_ref[j2, :, q * LANES:(q + 1) * LANES] = p_scr[q, pl.ds(j2, tn1, stride=pitch), :].astype(BF16)
            ps_ref[j2, :, q * LANES:(q + 1) * LANES] = p_scr[nq + q, pl.ds(j2, tn1, stride=pitch), :].astype(BF16)
        return carry

    lax.fori_loop(0, n2, emit, 0, unroll=True)


def _four_in(h, w_f, wcs, bsz, length, n2):
    n, d = h.shape
    heads, hd, _ = wcs.shape
    n1 = length // n2
    tn1 = PACK_ROWS
    tm = tn1 * n2
    tpb = length // tm
    fw = heads * hd
    out = jax.ShapeDtypeStruct((bsz, n2, n1, fw), BF16)
    ospec = pl.BlockSpec((None, n2, tn1, hd), lambda i, j: (i // tpb, 0, i % tpb, j))
    return pl.pallas_call(
        _four_in_kernel,
        grid=(n // tm, heads),
        in_specs=[pl.BlockSpec((tm, d), lambda i, j: (i, 0)),
                  pl.BlockSpec((d, hd), lambda i, j: (0, j)),
                  pl.BlockSpec((None, hd, 2 * hd), lambda i, j: (j, 0, 0))],
        out_specs=[ospec, ospec],
        out_shape=[out, out],
        scratch_shapes=[pltpu.VMEM((2 * hd // LANES, tn1 * (n2 + PITCH_PAD), LANES), F32)],
        compiler_params=_cparams("parallel", "arbitrary"),
        name="fourier_in_proj",
    )(h, w_f, wcs)


def _dft_tables(length, n2):
    n1 = length // n2
    k1 = np.arange(n1)[:, None, None]
    i1 = np.arange(n1)[None, :, None]
    i2 = np.arange(n2)[None, None, :]
    phase = (k1 * (n2 * i1 + i2)) % length
    phi = (2.0 * np.pi / length) * phase.astype(np.float64)
    c = np.cos(phi).transpose(2, 0, 1) / math.sqrt(n1)
    s = np.sin(phi).transpose(2, 0, 1) / math.sqrt(n1)
    g = np.concatenate([np.concatenate([c, -s], axis=2),
                        np.concatenate([-s, -c], axis=2)], axis=1)
    k2 = np.arange(n2)[:, None]
    j2 = np.arange(n2)[None, :]
    th = (2.0 * np.pi / n2) * ((k2 * j2) % n2).astype(np.float64)
    cs = np.concatenate([np.cos(th), np.sin(th)], axis=1) / math.sqrt(n2)
    return jnp.asarray(g, dtype=BF16), jnp.asarray(cs, dtype=BF16)


def _dft_kernel(g_ref, cs_ref, pc_ref, ps_ref, o_ref, t_scr, y_scr):
    n2, n1, _ = pc_ref.shape
    m = 2 * n1
    p1 = m + PITCH_PAD
    p2 = n1 + PITCH_PAD

    def stage1(j2, carry):
        rhs = jnp.concatenate([pc_ref[j2], ps_ref[j2]], axis=0)
        t_scr[pl.ds(pl.multiple_of(j2 * p1, 8), m), :] = _dot(g_ref[j2], rhs)
        return carry

    lax.fori_loop(0, n2, stage1, 0, unroll=True)

    cs = cs_ref[...]

    def stage2(i, carry):
        k1 = 2 * i
        cols = []
        for dk in range(2):
            re = t_scr[pl.ds(k1 + dk, n2, stride=p1), :]
            im = t_scr[pl.ds(n1 + k1 + dk, n2, stride=p1), :]
            cols.append(jnp.concatenate([re, im], axis=0).astype(BF16))
        res = _dot(cs, jnp.concatenate(cols, axis=1))
        for dk in range(2):
            y_scr[pl.ds(k1 + dk, n2, stride=p2), :] = res[:, dk * LANES:(dk + 1) * LANES]
        return carry

    lax.fori_loop(0, n1 // 2, stage2, 0, unroll=True)
    for k2 in range(n2):
        o_ref[k2 * n1:(k2 + 1) * n1, :] = y_scr[k2 * p2:k2 * p2 + n1, :].astype(o_ref.dtype)


def _dft(pc4, ps4, g, cs):
    bsz, n2, n1, w = pc4.shape
    m = 2 * n1
    ispec = pl.BlockSpec((None, n2, n1, LANES), lambda b, j: (b, 0, 0, j))
    return pl.pallas_call(
        _dft_kernel,
        grid=(bsz, w // LANES),
        in_specs=[pl.BlockSpec((n2, m, m), lambda b, j: (0, 0, 0), pipeline_mode=pl.Buffered(1)),
                  pl.BlockSpec((n2, 2 * n2), lambda b, j: (0, 0)),
                  ispec, ispec],
        out_specs=pl.BlockSpec((None, n2 * n1, LANES), lambda b, j: (b, 0, j)),
        out_shape=jax.ShapeDtypeStruct((bsz, n2 * n1, w), BF16),
        scratch_shapes=[pltpu.VMEM((n2 * (m + PITCH_PAD), LANES), F32),
                        pltpu.VMEM((n2 * (n1 + PITCH_PAD), LANES), F32)],
        compiler_params=_cparams("parallel", "parallel"),
        name="position_dft",
    )(g, cs, pc4, ps4)


def _nt_kernel(w_ref, h_ref, o_ref, stage_ref):
    acc = lax.dot_general(w_ref[...], h_ref[...], (((0,), (1,)), ((), ())), preferred_element_type=F32)
    rows, tq, _ = o_ref.shape
    if tq % 8 == 0:
        for q in range(tq):
            stage_ref[pl.ds(q, rows, stride=tq), :] = acc[:, q * LANES:(q + 1) * LANES]
        o_ref[...] = stage_ref[...].reshape(rows, tq, LANES)
    else:
        for q in range(tq):
            o_ref[:, q, :] = acc[:, q * LANES:(q + 1) * LANES]


def _proj_t(w, col0, c, h, tmc=1024, tn=1024):
    d = w.shape[0]
    n = h.shape[0]
    tmc, tn = min(tmc, c), min(tn, n)
    assert col0 % tmc == 0 and c % tmc == 0
    return pl.pallas_call(
        _nt_kernel,
        grid=(n // tn, c // tmc),
        in_specs=[pl.BlockSpec((d, tmc), lambda i, j: (0, col0 // tmc + j)),
                  pl.BlockSpec((tn, d), lambda i, j: (i, 0))],
        out_specs=pl.BlockSpec((tmc, tn // LANES, LANES), lambda i, j: (j, i, 0)),
        out_shape=jax.ShapeDtypeStruct((c, n // LANES, LANES), F32),
        scratch_shapes=[pltpu.VMEM((tmc * (tn // LANES), LANES), F32)],
        compiler_params=_cparams("parallel", "arbitrary"),
        name="s5_in_proj_t",
    )(w, h)


def _s5_tables(lam_re, lam_im, log_dt, b_re, b_im, c_re, c_im):
    t = CHUNK
    _, g, p = lam_re.shape
    h = b_re.shape[-1]
    dt = jnp.exp(log_dt.astype(F32))[..., None]
    lr, li = jnp.minimum(lam_re.astype(F32), -1e-4), lam_im.astype(F32)
    ar, ai = lr * dt, li * dt

    lbm = jnp.exp(ar)
    lbr, lbi = lbm * jnp.cos(ai), lbm * jnp.sin(ai)

    ks = jnp.arange(2 * t, dtype=F32)
    tab_m = jnp.exp(ar[..., None] * ks)
    tab_r, tab_i = tab_m * jnp.cos(ai[..., None] * ks), tab_m * jnp.sin(ai[..., None] * ks)

    def powers(d, first, step, steps_major=False):
        first = int(first)
        if step > 0:
            out = tab_r[d, :, :, first:first + t], tab_i[d, :, :, first:first + t]
        else:
            out = (jnp.flip(tab_r[d, :, :, first - t + 1:first + 1], -1),
                   jnp.flip(tab_i[d, :, :, first - t + 1:first + 1], -1))
        return tuple(jnp.swapaxes(a, 1, 2) for a in out) if steps_major else out

    nr, ni = lbr - 1.0, lbi
    den = lr * lr + li * li
    qr, qi = (nr * lr + ni * li) / den, (ni * lr - nr * li) / den
    br, bi = (jnp.swapaxes(a.astype(F32), -1, -2) for a in (b_re, b_im))
    bbr = qr[:, :, None] * br - qi[:, :, None] * bi
    bbi = qr[:, :, None] * bi + qi[:, :, None] * br
    cr, ci = c_re.astype(F32), c_im.astype(F32)

    mr = (bbr[:, :, :, None] * cr[:, :, None] - bbi[:, :, :, None] * ci[:, :, None]).reshape(2, g, h * h, p)
    mi = (bbr[:, :, :, None] * ci[:, :, None] + bbi[:, :, :, None] * cr[:, :, None]).reshape(2, g, h * h, p)

    km = jnp.concatenate([mr[0], mi[0], mr[1], mi[1]], axis=-1)
    wfr, wfi = powers(0, 0.0, 1.0)
    wbr, wbi = powers(1, t - 1.0, -1.0)
    fwd = lambda a: jnp.pad(a, ((0, 0), (0, 0), (t - 1, 1)))
    bwd = lambda a: jnp.pad(a, ((0, 0), (0, 0), (0, t)))
    kw = jnp.concatenate([fwd(wfr), fwd(-wfi), bwd(wbr), bwd(-wbi)], axis=1)

    pfr, pfi = powers(0, t - 1.0, -1.0, steps_major=True)
    pbr, pbi = powers(1, 0.0, 1.0, steps_major=True)
    ew_a = jnp.concatenate([pfr, pfr, pbr, pbr], axis=-1)
    ew_b = jnp.concatenate([pfi, pfi, pbi, pbi], axis=-1)
    eb_a = jnp.concatenate([bbr[0], bbi[0], bbr[1], bbi[1]], axis=-1)
    eb_b = jnp.concatenate([-bbi[0], bbr[0], -bbi[1], bbr[1]], axis=-1)

    qfr, qfi = powers(0, 1.0, 1.0)
    qbr, qbi = powers(1, float(t), -1.0)
    crf, cif, crb, cib = (jnp.swapaxes(a, 1, 2) for a in (cr[0], ci[0], cr[1], ci[1]))
    dc = jnp.concatenate([jnp.concatenate([crf, -crf, crb, -crb], axis=1),
                          jnp.concatenate([-cif, -cif, -cib, -cib], axis=1)], axis=-1)
    dq = jnp.concatenate([jnp.concatenate([qfr, qfi, qbr, qbi], axis=1),
                          jnp.concatenate([qfi, qfr, qbi, qbr], axis=1)], axis=-1)

    dec = jnp.stack([tab_r[0, :, :, t], tab_i[0, :, :, t],
                     tab_r[1, :, :, t], tab_i[1, :, :, t]]).reshape(4, g * p)
    return (km, kw), (ew_a, ew_b, eb_a, eb_b), (dc.astype(BF16), dq.astype(BF16)), dec


def _carry_expanders(h, t):
    ht = h * t
    rep = np.zeros((2 * h, 2 * ht), np.float32)
    til = np.zeros((2 * t, 2 * ht), np.float32)
    for half in range(2):
        for ho in range(h):
            for tt in range(t):
                rep[half * h + ho, half * ht + ho * t + tt] = 1.0
                til[half * t + tt, half * ht + ho * t + tt] = 1.0
    return jnp.asarray(rep, dtype=BF16), jnp.asarray(til, dtype=BF16)


def _toeplitz_select(t):
    sel = np.zeros((4 * t, (t // 8) * 2 * t), np.float32)
    for q in range(t // 8):
        for tt in range(t):
            j = tt - 8 * q + t - 8
            sel[j, q * 2 * t + tt] = 1.0
            sel[2 * t + j, q * 2 * t + t + tt] = 1.0
    return jnp.asarray(sel, dtype=BF16)


GPS = 2


def _gather_chunks(z_ref, gi, h):
    t = z_ref.shape[-1] // 2
    lo = lax.broadcasted_iota(jnp.int32, z_ref.shape[1:], 1) < t
    ev, od = [], []
    for k in range(0, h, 2):
        za, zb = z_ref[gi * h + k], z_ref[gi * h + k + 1]
        ev.append(jnp.where(lo, za, pltpu.roll(zb, t, axis=1)))
        od.append(jnp.where(lo, pltpu.roll(za, t, axis=1), zb))
    return jnp.concatenate([jnp.concatenate(ev, axis=1), jnp.concatenate(od, axis=1)], axis=0)


def _s5_state_kernel(z_ref, ewa_ref, ewb_ref, eba_ref, ebb_ref, fre_ref, fim_ref, bre_ref, bim_ref, e_scr):
    h = z_ref.shape[0] // GPS
    t = ewa_ref.shape[1]
    p = ewa_ref.shape[-1] // 4
    outs = (fre_ref, fim_ref, bre_ref, bim_ref)
    for gi in range(GPS):
        wa, wb = ewa_ref[gi], ewb_ref[gi]
        for k in range(h):
            e_scr[k * t:(k + 1) * t, :] = (wa * eba_ref[gi, k:k + 1, :] + wb * ebb_ref[gi, k:k + 1, :]).astype(BF16)
        a2 = _gather_chunks(z_ref, gi, h).astype(BF16)
        s = _dot(a2, e_scr[...])
        for k in range(4):
            outs[k][:, gi * p:(gi + 1) * p] = s[:, k * p:(k + 1) * p]


def _s5_states(z3, e_tabs):
    gh, npair, t2 = z3.shape
    nchunk, t = 2 * npair, t2 // 2
    g, _, p4 = e_tabs[0].shape
    h, p = gh // g, p4 // 4
    out = jax.ShapeDtypeStruct((nchunk, g * p), F32)
    ospec = pl.BlockSpec((nchunk, GPS * p), lambda i: (0, i))
    wspec = pl.BlockSpec((GPS, t, p4), lambda i: (i, 0, 0))
    bspec = pl.BlockSpec((GPS, h, p4), lambda i: (i, 0, 0))
    return pl.pallas_call(
        _s5_state_kernel,
        grid=(g // GPS,),
        in_specs=[pl.BlockSpec((GPS * h, npair, t2), lambda i: (i, 0, 0)), wspec, wspec, bspec, bspec],
        out_specs=[ospec] * 4,
        out_shape=[out] * 4,
        scratch_shapes=[pltpu.VMEM((h * t, p4), BF16)],
        compiler_params=_cparams("parallel"),
        name="s5_chunk_states",
    )(z3, *e_tabs)


def _s5_scan_kernel(bsz, dec_ref, cfre, cfim, cbre, cbim, sfre, sfim, sbre, sbim,
                    hfre, hfim, hbre, hbim):
    nctx = cfre.shape[0] // bsz
    nlat = sfre.shape[0] // bsz
    width = dec_ref.shape[-1]
    fr, fi = dec_ref[0:1, :], dec_ref[1:2, :]
    br, bi = dec_ref[2:3, :], dec_ref[3:4, :]
    zero = jnp.zeros((1, width), F32)

    def step(ar, ai, hr, hi, sr, si):
        return ar * hr - ai * hi + sr, ar * hi + ai * hr + si

    def row(nchunk, b, j):
        return (j % 2) * (bsz * nchunk // 2) + b * (nchunk // 2) + j // 2

    for b in range(bsz):
        hr, hi = zero, zero
        for j in range(nctx):
            r = row(nctx, b, j)
            hr, hi = step(fr, fi, hr, hi, cfre[r:r + 1, :], cfim[r:r + 1, :])

        def fwd(j, carry, b=b):
            hr, hi = carry
            r = row(nlat, b, j)
            hfre[pl.ds(r, 1), :] = hr
            hfim[pl.ds(r, 1), :] = hi
            return step(fr, fi, hr, hi, sfre[pl.ds(r, 1), :], sfim[pl.ds(r, 1), :])

        lax.fori_loop(0, nlat, fwd, (hr, hi))

        hr, hi = zero, zero
        for j in range(nctx - 1, -1, -1):
            r = row(nctx, b, j)
            hr, hi = step(br, bi, hr, hi, cbre[r:r + 1, :], cbim[r:r + 1, :])

        def bwd(k, carry, b=b):
            hr, hi = carry
            r = row(nlat, b, nlat - 1 - k)
            hbre[pl.ds(r, 1), :] = hr
            hbim[pl.ds(r, 1), :] = hi
            return step(br, bi, hr, hi, sbre[pl.ds(r, 1), :], sbim[pl.ds(r, 1), :])

        lax.fori_loop(0, nlat, bwd, (hr, hi))


def _s5_scan(dec, ctx_s, lat_s, bsz, tw=1024):
    gp = dec.shape[-1]
    rc, rl = ctx_s[0].shape[0], lat_s[0].shape[0]
    assert (rc // bsz) % 2 == 0 and (rl // bsz) % 2 == 0
    tw = min(tw, gp)
    cspec = pl.BlockSpec((rc, tw), lambda i: (0, i))
    lspec = pl.BlockSpec((rl, tw), lambda i: (0, i))
    out = jax.ShapeDtypeStruct((rl, gp), F32)
    return pl.pallas_call(
        functools.partial(_s5_scan_kernel, bsz),
        grid=(gp // tw,),
        in_specs=[pl.BlockSpec((4, tw), lambda i: (0, i))] + [cspec] * 4 + [lspec] * 4,
        out_specs=[lspec] * 4,
        out_shape=[out] * 4,
        compiler_params=_cparams("parallel"),
        name="s5_state_scan",
    )(dec, *ctx_s, *lat_s)


def _gelu_tanh(x):
    return 0.5 * x * (1.0 + jnp.tanh(math.sqrt(2.0 / math.pi) * (x + 0.044715 * (x * x * x))))


def _expand_toeplitz(k_ref, sel_ref, lhs_ref, res_ref, w_ref, h, t):
    half = h // 2
    rows_per_hi = half * 8

    def fill(hi, carry):
        for hp in range(half):
            for par in range(2):
                v = k_ref[pl.ds(hi * h + 2 * hp + par, 1), :]
                b = pltpu.roll(jnp.broadcast_to(v, (8, 2 * t)), 2 * t - 7, axis=1, stride=1, stride_axis=0)
                lhs_ref[pl.ds(pl.multiple_of(hi * rows_per_hi + hp * 8, 8), 8), par * 2 * t:(par + 1) * 2 * t] = b
        return carry

    lax.fori_loop(0, h, fill, 0, unroll=True)
    res_ref[...] = _dot(lhs_ref[...].astype(BF16), sel_ref[...])

    def shuffle(hi, carry):
        base = pl.multiple_of(hi * rows_per_hi, rows_per_hi)
        r_hi = res_ref[pl.ds(base, rows_per_hi), :]
        rows = [jnp.concatenate([r_hi[hp * 8:(hp + 1) * 8, q * 2 * t:(q + 1) * 2 * t] for hp in range(half)], axis=1)
                for q in range(t // 8)]
        w_ref[pl.ds(base, t), :] = jnp.concatenate(rows, axis=0).astype(BF16)
        return carry

    lax.fori_loop(0, h, shuffle, 0, unroll=True)


def _s5_out_kernel(z_ref, km_ref, kw_ref, sel_ref, dc_ref, dq_ref, rep_ref, til_ref, hfre, hfim, hbre, hbim,
                   dsk_ref, o_ref, k_scr, lhs_ref, res_ref, w_ref):
    h = z_ref.shape[0] // GPS
    npair = z_ref.shape[1]
    t = z_ref.shape[-1] // 2
    ht = h * t
    p = dc_ref.shape[1] // 4
    lo = lax.broadcasted_iota(jnp.int32, z_ref.shape[1:], 1) < t
    for gi in range(GPS):
        k_scr[...] = jnp.dot(km_ref[gi], kw_ref[gi], preferred_element_type=F32, precision=HI)
        _expand_toeplitz(k_scr, sel_ref, lhs_ref, res_ref, w_ref, h, t)
        ce = _dot(dc_ref[gi], rep_ref[...])
        qe = _dot(dq_ref[gi], til_ref[...])
        dmat = (ce[:, :ht] * qe[:, :ht] + ce[:, ht:] * qe[:, ht:]).astype(BF16)
        u = _gather_chunks(z_ref, gi, h)
        sl = slice(gi * p, (gi + 1) * p)
        hp = jnp.concatenate([hfre[:, sl], hfim[:, sl], hbre[:, sl], hbim[:, sl]], axis=-1).astype(BF16)
        y = _dot(u.astype(BF16), w_ref[...]) + _dot(hp, dmat)
        gl = _gelu_tanh(y + u * dsk_ref[gi])
        for k in range(0, h, 2):
            te = gl[0:npair, k * t:(k + 2) * t]
            to = gl[npair:2 * npair, k * t:(k + 2) * t]
            o_ref[gi * h + k] = jnp.where(lo, te, pltpu.roll(to, t, axis=1))
            o_ref[gi * h + k + 1] = jnp.where(lo, pltpu.roll(te, t, axis=1), to)


def _s5_out(z3, k_tabs, sel, d_tabs, expanders, hin, dsk):
    gh, npair, t2z = z3.shape
    nchunk, t = 2 * npair, t2z // 2
    km, kw = k_tabs
    g, hh, kp = km.shape
    t2 = kw.shape[-1]
    h = gh // g
    ht = h * t
    assert hh == h * h and t2 == 2 * t and (h // 2) * 8 == t and t2 == LANES
    dc, dq = d_tabs
    rep, til = expanders
    p4 = dc.shape[1]
    p = p4 // 4
    hspec = pl.BlockSpec((nchunk, GPS * p), lambda i: (0, i))
    const = lambda a: pl.BlockSpec(a.shape, lambda i: (0, 0))
    return pl.pallas_call(
        _s5_out_kernel,
        grid=(g // GPS,),
        in_specs=[pl.BlockSpec((GPS * h, npair, t2z), lambda i: (i, 0, 0)),
                  pl.BlockSpec((GPS, hh, kp), lambda i: (i, 0, 0)),
                  pl.BlockSpec((GPS, kp, t2), lambda i: (i, 0, 0)),
                  const(sel),
                  pl.BlockSpec((GPS, p4, 2 * h), lambda i: (i, 0, 0)),
                  pl.BlockSpec((GPS, p4, 2 * t), lambda i: (i, 0, 0)),
                  const(rep), const(til)] + [hspec] * 4 +
                 [pl.BlockSpec((GPS, 1, ht), lambda i: (i, 0, 0))],
        out_specs=pl.BlockSpec((GPS * h, npair, t2z), lambda i: (i, 0, 0)),
        out_shape=jax.ShapeDtypeStruct((gh, npair, t2z), F32),
        scratch_shapes=[pltpu.VMEM((hh, t2), F32), pltpu.VMEM((ht, 4 * t), F32), pltpu.VMEM((ht, ht), F32),
                        pltpu.VMEM((ht, ht), BF16)],
        compiler_params=_cparams("parallel"),
        name="s5_chunk_out",
    )(z3, km, kw, sel, dc, dq, rep, til, *hin, dsk)


def _glu_kernel(gt_ref, wa_ref, wb_ref, ba_ref, bb_ref, o_ref, g_scr, stage_ref):
    @pl.when(pl.program_id(1) == 0)
    def _():
        c, tq, _ = gt_ref.shape
        half = stage_ref.shape[0] // tq
        for c0 in range(0, c, half):
            stage_ref[...] = gt_ref[c0:c0 + half].reshape(half * tq, LANES)
            for q in range(tq):
                g_scr[c0:c0 + half, q * LANES:(q + 1) * LANES] = stage_ref[pl.ds(q, half, stride=tq), :].astype(BF16)

    dn = (((0,), (0,)), ((), ()))
    gt = g_scr[...]
    a = lax.dot_general(gt, wa_ref[...], dn, preferred_element_type=F32) + ba_ref[...]
    b = lax.dot_general(gt, wb_ref[...], dn, preferred_element_type=F32) + bb_ref[...]
    o_ref[...] = (a * jax.nn.sigmoid(b)).astype(o_ref.dtype)


def _glu(g3, wa, wb, ba, bb, tm=1024, tn=512):
    c, nq, _ = g3.shape
    n = nq * LANES
    co = wa.shape[-1]
    tm, tn = min(tm, n), min(tn, co)
    stage_rows = (c // 2) * (tm // LANES)
    return pl.pallas_call(
        _glu_kernel,
        grid=(n // tm, co // tn),
        in_specs=[pl.BlockSpec((c, tm // LANES, LANES), lambda i, j: (0, i, 0)),
                  pl.BlockSpec((c, tn), lambda i, j: (0, j)),
                  pl.BlockSpec((c, tn), lambda i, j: (0, j)),
                  pl.BlockSpec((1, tn), lambda i, j: (0, j)),
                  pl.BlockSpec((1, tn), lambda i, j: (0, j))],
        out_specs=pl.BlockSpec((tm, tn), lambda i, j: (i, j)),
        out_shape=jax.ShapeDtypeStruct((n, co), BF16),
        scratch_shapes=[pltpu.VMEM((c, tm), BF16), pltpu.VMEM((stage_rows, LANES), F32)],
        compiler_params=_cparams("parallel", "arbitrary"),
        name="s5_glu",
    )(g3, wa, wb, ba.reshape(1, co), bb.reshape(1, co))


def _out_proj_kernel(yf_ref, ys_ref, wf_ref, ws_ref, x_ref, g_ref, o_ref):
    acc = _dot(yf_ref[...], wf_ref[...]) + _dot(ys_ref[...], ws_ref[...])
    o_ref[...] = x_ref[...] + g_ref[...] * acc


def _out_proj(yf, ys, w_out, x2, gate, length, tm=1024, tn=1024):
    n, fw = yf.shape
    sw = ys.shape[-1]
    d = w_out.shape[-1]
    tm, tn = min(tm, length), min(tn, d)
    assert fw % sw == 0
    bsz = gate.shape[0]
    return pl.pallas_call(
        _out_proj_kernel,
        grid=(n // tm, d // tn),
        in_specs=[pl.BlockSpec((tm, fw), lambda i, j: (i, 0)),
                  pl.BlockSpec((tm, sw), lambda i, j: (i, 0)),
                  pl.BlockSpec((fw, tn), lambda i, j: (0, j)),
                  pl.BlockSpec((sw, tn), lambda i, j: (fw // sw, j)),
                  pl.BlockSpec((tm, tn), lambda i, j: (i, j)),
                  pl.BlockSpec((None, 1, tn), lambda i, j: ((i * tm) // length, 0, j))],
        out_specs=pl.BlockSpec((tm, tn), lambda i, j: (i, j)),
        out_shape=jax.ShapeDtypeStruct((n, d), F32),
        compiler_params=_cparams("parallel", "arbitrary"),
        name="out_proj_residual",
    )(yf, ys, w_out, w_out, x2, gate.reshape(bsz, 1, d))


def _ffn_hidden_start(j, th, hid):
    return pl.multiple_of(jnp.minimum(j * th, hid - th), math.gcd(th, hid - th))


def _ffn_kernel(hid, h_ref, wg_ref, wu_ref, wd_ref, o_ref):
    j = pl.program_id(1)
    th = wg_ref.shape[-1]
    hh = h_ref[...]
    g = _dot(hh, wg_ref[...].astype(BF16))
    u = _dot(hh, wu_ref[...].astype(BF16))
    unit = _ffn_hidden_start(j, th, hid) + lax.broadcasted_iota(jnp.int32, (1, th), 1)
    a = jnp.where(unit >= j * th, g * jax.nn.sigmoid(g) * u, 0.0).astype(BF16)
    d = o_ref.shape[-1]
    nc = min(FFN_DOWN_CHUNK, d)
    wd = wd_ref[...].astype(BF16)

    @pl.when(j == 0)
    def _():
        for c0 in range(0, d, nc):
            o_ref[:, c0:c0 + nc] = _dot(a, wd[:, c0:c0 + nc])

    @pl.when(j > 0)
    def _():
        for c0 in range(0, d, nc):
            o_ref[:, c0:c0 + nc] += _dot(a, wd[:, c0:c0 + nc])


FFN_DOWN_CHUNK = 512


FFN_TH = 256
FFN_VMEM_LIMIT = 60 * 1024 * 1024


def _ffn(h, wg, wu, wd, tm=1024):
    n, d = h.shape
    hid = wg.shape[-1]
    tm, th = min(tm, n), min(FFN_TH, hid)
    col_map = lambda i, j: (0, _ffn_hidden_start(j, th, hid))
    return pl.pallas_call(
        functools.partial(_ffn_kernel, hid),
        grid=(n // tm, pl.cdiv(hid, th)),
        in_specs=[pl.BlockSpec((tm, d), lambda i, j: (i, 0), pipeline_mode=pl.Buffered(1)),
                  pl.BlockSpec((pl.Element(d), pl.Element(th)), col_map),
                  pl.BlockSpec((pl.Element(d), pl.Element(th)), col_map),
                  pl.BlockSpec((pl.Element(th), pl.Element(d)), lambda i, j: (_ffn_hidden_start(j, th, hid), 0))],
        out_specs=pl.BlockSpec((tm, d), lambda i, j: (i, 0), pipeline_mode=pl.Buffered(1)),
        out_shape=jax.ShapeDtypeStruct((n, d), F32),
        compiler_params=_cparams("parallel", "arbitrary", vmem=FFN_VMEM_LIMIT),
        name="swiglu_ffn",
    )(h, wg, wu, wd)


def _final_kernel(x_ref, f_ref, gate_ref, g_ref, o_ref):
    x = x_ref[...] + gate_ref[...] * f_ref[...]
    ms = jnp.mean(x * x, axis=-1, keepdims=True)
    o_ref[...] = x * lax.rsqrt(ms + EPS) * g_ref[...]


def _final(x1, f, gate, g, tm=256):
    bsz, length, d = x1.shape
    tm = min(tm, length)
    return pl.pallas_call(
        _final_kernel,
        grid=(bsz, length // tm),
        in_specs=[pl.BlockSpec((None, tm, d), lambda b, i: (b, i, 0)),
                  pl.BlockSpec((None, tm, d), lambda b, i: (b, i, 0)),
                  pl.BlockSpec((None, 1, d), lambda b, i: (b, 0, 0)),
                  pl.BlockSpec((1, d), lambda b, i: (0, 0))],
        out_specs=pl.BlockSpec((None, tm, d), lambda b, i: (b, i, 0)),
        out_shape=jax.ShapeDtypeStruct((bsz, length, d), F32),
        compiler_params=_cparams("parallel", "parallel"),
        name="residual_final_norm",
    )(x1, f, gate.reshape(bsz, 1, d), g.reshape(1, d))


def _dft_split(length):
    n2 = 64
    while (length // n2) % PACK_ROWS:
        n2 //= 2
    assert n2 >= 8 and length % n2 == 0
    return n2


def kernel(x, c, ctx, c_ctx, ada_w, ada_b, norm1_g, norm2_g, w_in, w_out, fourier_w, s5_lam_re, s5_lam_im, s5_log_dt, s5_b_re, s5_b_im, s5_c_re, s5_c_im, s5_d, glu_w_a, glu_b_a, glu_w_b, glu_b_b, ffn_w_gate, ffn_w_up, ffn_w_down, final_g):
    bsz, length, d = x.shape
    depth = ada_w.shape[0]
    assert depth == 1, "single-layer block"
    lyr = 0
    heads, hd, _ = fourier_w.shape[1:]
    fw = heads * hd
    _, g, p, hgrp = s5_b_re.shape[1:]
    sw = g * hgrp
    assert w_in.shape[-1] == fw + sw and length % (2 * CHUNK) == 0 and ctx.shape[1] % (2 * CHUNK) == 0
    n = bsz * length

    a8 = jnp.zeros((8, d), F32).at[:bsz].set(c.astype(F32)).at[bsz].set(c_ctx.astype(F32))
    mods = _ada(jnp.concatenate([a8, a8], axis=0), ada_w[lyr], ada_b[lyr]).reshape(8, N_MOD, d)
    sh1, sc1, g1, sh2, sc2, g2 = (mods[:bsz, i] for i in range(N_MOD))
    csh1, csc1 = mods[bsz:bsz + 1, 0], mods[bsz:bsz + 1, 1]

    w_in_b = w_in[lyr].astype(BF16)
    ang = (2.0 * np.pi / hd) * ((np.arange(hd)[:, None] * np.arange(hd)[None, :]) % hd).astype(np.float64)
    cd = jnp.asarray(np.cos(ang) / math.sqrt(hd), F32)
    sd = jnp.asarray(np.sin(ang) / math.sqrt(hd), F32)
    csd = jnp.broadcast_to(jnp.stack([cd, sd])[:, None], (2, heads, hd, hd)).reshape(2 * heads, hd, hd)
    wf2 = jnp.concatenate([fourier_w[lyr], fourier_w[lyr]], axis=0).astype(F32)
    folded = _fold(csd, wf2)
    wcs = jnp.concatenate([folded[:heads], folded[heads:]], axis=-1).astype(BF16)
    k_tabs, e_mat, d_mat, dec = _s5_tables(s5_lam_re[lyr], s5_lam_im[lyr], s5_log_dt[lyr], s5_b_re[lyr],
                                         s5_b_im[lyr], s5_c_re[lyr], s5_c_im[lyr])
    sel = _toeplitz_select(CHUNK)
    dsk = jnp.repeat(s5_d[lyr].astype(F32).reshape(g, 1, hgrp), CHUNK, axis=-1)

    hc = _norm_mod(ctx, norm1_g[lyr], csh1, csc1)
    nctx_tok = bsz * ctx.shape[1]
    ctx_s = _s5_states(_proj_t(w_in_b, fw, sw, hc.reshape(nctx_tok, d)), e_mat)

    hm = _norm_mod(x, norm1_g[lyr], sh1, sc1).reshape(n, d)
    n2 = _dft_split(length)
    pc4, ps4 = _four_in(hm, w_in_b, wcs, bsz, length, n2)
    g_tab, cs_tab = _dft_tables(length, n2)
    y_four = _dft(pc4, ps4, g_tab, cs_tab).reshape(n, fw)

    z3 = _proj_t(w_in_b, fw, sw, hm)
    lat_s = _s5_states(z3, e_mat)
    h_in = _s5_scan(dec, ctx_s, lat_s, bsz)
    g3 = _s5_out(z3, k_tabs, sel, d_mat, _carry_expanders(hgrp, CHUNK), h_in, dsk)
    y_s = _glu(g3, glu_w_a[lyr].astype(BF16), glu_w_b[lyr].astype(BF16), glu_b_a[lyr], glu_b_b[lyr])

    x1 = _out_proj(y_four, y_s, w_out[lyr].astype(BF16), x.reshape(n, d), g1, length)

    hm2 = _norm_mod(x1.reshape(bsz, length, d), norm2_g[lyr], sh2, sc2).reshape(n, d)
    f = _ffn(hm2, ffn_w_gate[lyr], ffn_w_up[lyr], ffn_w_down[lyr])
    return _final(x1.reshape(bsz, length, d), f.reshape(bsz, length, d), g2, final_g)
```

```python
import functools
import math

import numpy as np
import jax
import jax.numpy as jnp
from jax import lax
from jax.experimental import pallas as pl
from jax.experimental.pallas import tpu as pltpu

F32 = jnp.float32
BF16 = jnp.bfloat16
EPS = 1e-6
CHUNK = 64
N_MOD = 6
V7X_VMEM_LIMIT = 56 * 1024 * 1024
HI = lax.Precision.HIGHEST


def _cparams(*sem, vmem=V7X_VMEM_LIMIT):
    return pltpu.CompilerParams(dimension_semantics=sem, vmem_limit_bytes=vmem)


def _dot(a, b):
    return jnp.dot(a, b, preferred_element_type=F32)


def _ada_kernel(a_ref, w_ref, b_ref, o_ref):
    a = a_ref[...]
    s = a * jax.nn.sigmoid(a)
    s_hi = s.astype(BF16).astype(F32)
    row = lax.broadcasted_iota(jnp.int32, s.shape, 0)
    lhs = jnp.where(row < 8, s_hi, s - s_hi).astype(BF16)
    w = w_ref[...]
    w_hi = w.astype(BF16)
    w_lo = (w - w_hi.astype(F32)).astype(BF16)
    r = _dot(lhs, w_hi) + _dot(lhs, w_lo)
    o_ref[...] = r[0:8] + r[8:16] + b_ref[...]


def _ada(a16, w, b, tn=1024):
    d, n = w.shape
    tn = min(tn, n)
    return pl.pallas_call(
        _ada_kernel,
        grid=(n // tn,),
        in_specs=[pl.BlockSpec((16, d), lambda j: (0, 0)),
                  pl.BlockSpec((d, tn), lambda j: (0, j)),
                  pl.BlockSpec((1, tn), lambda j: (0, j))],
        out_specs=pl.BlockSpec((8, tn), lambda j: (0, j)),
        out_shape=jax.ShapeDtypeStruct((8, n), F32),
        compiler_params=_cparams("parallel"),
        name="ada_matvec",
    )(a16, w, b.reshape(1, n))


def _norm_mod_kernel(x_ref, g_ref, sh_ref, sc_ref, o_ref):
    x = x_ref[...]
    ms = jnp.mean(x * x, axis=-1, keepdims=True)
    y = x * lax.rsqrt(ms + EPS) * g_ref[...]
    o_ref[...] = (y * (1.0 + sc_ref[...]) + sh_ref[...]).astype(o_ref.dtype)


def _norm_mod(x, g, sh, sc, tm=512):
    bsz, length, d = x.shape
    tm = min(tm, length)
    bm = sh.shape[0]
    mod_map = (lambda b, i: (b, 0, 0)) if bm == bsz else (lambda b, i: (0, 0, 0))
    return pl.pallas_call(
        _norm_mod_kernel,
        grid=(bsz, length // tm),
        in_specs=[pl.BlockSpec((None, tm, d), lambda b, i: (b, i, 0)),
                  pl.BlockSpec((1, d), lambda b, i: (0, 0)),
                  pl.BlockSpec((None, 1, d), mod_map),
                  pl.BlockSpec((None, 1, d), mod_map)],
        out_specs=pl.BlockSpec((None, tm, d), lambda b, i: (b, i, 0)),
        out_shape=jax.ShapeDtypeStruct((bsz, length, d), BF16),
        compiler_params=_cparams("parallel", "parallel"),
        name="norm_mod",
    )(x, g.reshape(1, d), sh.reshape(bm, 1, d), sc.reshape(bm, 1, d))


def _fold_kernel(a_ref, b_ref, o_ref):
    o_ref[...] = jnp.dot(a_ref[...], b_ref[...], preferred_element_type=F32, precision=HI)


def _fold(a, b):
    hh, m, k = a.shape
    n = b.shape[-1]
    return pl.pallas_call(
        _fold_kernel,
        grid=(hh,),
        in_specs=[pl.BlockSpec((None, m, k), lambda h: (h, 0, 0)),
                  pl.BlockSpec((None, k, n), lambda h: (h, 0, 0))],
        out_specs=pl.BlockSpec((None, m, n), lambda h: (h, 0, 0)),
        out_shape=jax.ShapeDtypeStruct((hh, m, n), F32),
        compiler_params=_cparams("parallel"),
        name="weight_fold",
    )(a, b)


LANES = 128
PACK_ROWS = 16
PITCH_PAD = 8


def _four_in_kernel(h_ref, w_ref, wcs_ref, pc_ref, ps_ref, p_scr):
    n2, tn1, hd = pc_ref.shape
    pitch = n2 + PITCH_PAD
    nq = hd // LANES
    z = _dot(h_ref[...], w_ref[...]).astype(BF16)
    p = _dot(z, wcs_ref[...])
    for q in range(2 * nq):
        for i1 in range(tn1):
            p_scr[q, i1 * pitch:i1 * pitch + n2, :] = p[i1 * n2:(i1 + 1) * n2, q * LANES:(q + 1) * LANES]

    def emit(j2, carry):
        for q in range(nq):
            pc_ref[j2, :, q * LANES:(q + 1) * LANES] = p_scr[q, pl.ds(j2, tn1, stride=pitch), :].astype(BF16)
            ps_ref[j2, :, q * LANES:(q + 1) * LANES] = p_scr[nq + q, pl.ds(j2, tn1, stride=pitch), :].astype(BF16)
        return carry

    lax.fori_loop(0, n2, emit, 0, unroll=True)


def _four_in(h, w_f, wcs, bsz, length, n2):
    n, d = h.shape
    heads, hd, _ = wcs.shape
    n1 = length // n2
    tn1 = PACK_ROWS
    tm = tn1 * n2
    tpb = length // tm
    fw = heads * hd
    out = jax.ShapeDtypeStruct((bsz, n2, n1, fw), BF16)
    ospec = pl.BlockSpec((None, n2, tn1, hd), lambda i, j: (i // tpb, 0, i % tpb, j))
    return pl.pallas_call(
        _four_in_kernel,
        grid=(n // tm, heads),
        in_specs=[pl.BlockSpec((tm, d), lambda i, j: (i, 0)),
                  pl.BlockSpec((d, hd), lambda i, j: (0, j)),
                  pl.BlockSpec((None, hd, 2 * hd), lambda i, j: (j, 0, 0))],
        out_specs=[ospec, ospec],
        out_shape=[out, out],
        scratch_shapes=[pltpu.VMEM((2 * hd // LANES, tn1 * (n2 + PITCH_PAD), LANES), F32)],
        compiler_params=_cparams("parallel", "arbitrary"),
        name="fourier_in_proj",
    )(h, w_f, wcs)


def _dft_tables(length, n2):
    n1 = length // n2
    k1 = np.arange(n1)[:, None, None]
    i1 = np.arange(n1)[None, :, None]
    i2 = np.arange(n2)[None, None, :]
    phase = (k1 * (n2 * i1 + i2)) % length
    phi = (2.0 * np.pi / length) * phase.astype(np.float64)
    c = np.cos(phi).transpose(2, 0, 1) / math.sqrt(n1)
    s = np.sin(phi).transpose(2, 0, 1) / math.sqrt(n1)
    g = np.concatenate([np.concatenate([c, -s], axis=2),
                        np.concatenate([-s, -c], axis=2)], axis=1)
    k2 = np.arange(n2)[:, None]
    j2 = np.arange(n2)[None, :]
    th = (2.0 * np.pi / n2) * ((k2 * j2) % n2).astype(np.float64)
    cs = np.concatenate([np.cos(th), np.sin(th)], axis=1) / math.sqrt(n2)
    return jnp.asarray(g, dtype=BF16), jnp.asarray(cs, dtype=BF16)


def _dft_kernel(g_ref, cs_ref, pc_ref, ps_ref, o_ref, t_scr, y_scr):
    n2, n1, _ = pc_ref.shape
    m = 2 * n1
    p1 = m + PITCH_PAD
    p2 = n1 + PITCH_PAD

    def stage1(j2, carry):
        rhs = jnp.concatenate([pc_ref[j2], ps_ref[j2]], axis=0)
        t_scr[pl.ds(pl.multiple_of(j2 * p1, 8), m), :] = _dot(g_ref[j2], rhs)
        return carry

    lax.fori_loop(0, n2, stage1, 0, unroll=True)

    cs = cs_ref[...]

    def stage2(i, carry):
        k1 = 2 * i
        cols = []
        for dk in range(2):
            re = t_scr[pl.ds(k1 + dk, n2, stride=p1), :]
            im = t_scr[pl.ds(n1 + k1 + dk, n2, stride=p1), :]
            cols.append(jnp.concatenate([re, im], axis=0).astype(BF16))
        res = _dot(cs, jnp.concatenate(cols, axis=1))
        for dk in range(2):
            y_scr[pl.ds(k1 + dk, n2, stride=p2), :] = res[:, dk * LANES:(dk + 1) * LANES]
        return carry

    lax.fori_loop(0, n1 // 2, stage2, 0, unroll=True)
    for k2 in range(n2):
        o_ref[k2 * n1:(k2 + 1) * n1, :] = y_scr[k2 * p2:k2 * p2 + n1, :].astype(o_ref.dtype)


def _dft(pc4, ps4, g, cs):
    bsz, n2, n1, w = pc4.shape
    m = 2 * n1
    ispec = pl.BlockSpec((None, n2, n1, LANES), lambda b, j: (b, 0, 0, j))
    return pl.pallas_call(
        _dft_kernel,
        grid=(bsz, w // LANES),
        in_specs=[pl.BlockSpec((n2, m, m), lambda b, j: (0, 0, 0), pipeline_mode=pl.Buffered(1)),
                  pl.BlockSpec((n2, 2 * n2), lambda b, j: (0, 0)),
                  ispec, ispec],
        out_specs=pl.BlockSpec((None, n2 * n1, LANES), lambda b, j: (b, 0, j)),
        out_shape=jax.ShapeDtypeStruct((bsz, n2 * n1, w), BF16),
        scratch_shapes=[pltpu.VMEM((n2 * (m + PITCH_PAD), LANES), F32),
                        pltpu.VMEM((n2 * (n1 + PITCH_PAD), LANES), F32)],
        compiler_params=_cparams("parallel", "parallel"),
        name="position_dft",
    )(g, cs, pc4, ps4)


def _nt_kernel(w_ref, h_ref, o_ref, stage_ref):
    acc = lax.dot_general(w_ref[...], h_ref[...], (((0,), (1,)), ((), ())), preferred_element_type=F32)
    rows, tq, _ = o_ref.shape
    if tq % 8 == 0:
        for q in range(tq):
            stage_ref[pl.ds(q, rows, stride=tq), :] = acc[:, q * LANES:(q + 1) * LANES]
        o_ref[...] = stage_ref[...].reshape(rows, tq, LANES)
    else:
        for q in range(tq):
            o_ref[:, q, :] = acc[:, q * LANES:(q + 1) * LANES]


def _proj_t(w, col0, c, h, tmc=1024, tn=1024):
    d = w.shape[0]
    n = h.shape[0]
    tmc, tn = min(tmc, c), min(tn, n)
    assert col0 % tmc == 0 and c % tmc == 0
    return pl.pallas_call(
        _nt_kernel,
        grid=(n // tn, c // tmc),
        in_specs=[pl.BlockSpec((d, tmc), lambda i, j: (0, col0 // tmc + j)),
                  pl.BlockSpec((tn, d), lambda i, j: (i, 0))],
        out_specs=pl.BlockSpec((tmc, tn // LANES, LANES), lambda i, j: (j, i, 0)),
        out_shape=jax.ShapeDtypeStruct((c, n // LANES, LANES), F32),
        scratch_shapes=[pltpu.VMEM((tmc * (tn // LANES), LANES), F32)],
        compiler_params=_cparams("parallel", "arbitrary"),
        name="s5_in_proj_t",
    )(w, h)


def _s5_tables(lam_re, lam_im, log_dt, b_re, b_im, c_re, c_im):
    t = CHUNK
    _, g, p = lam_re.shape
    h = b_re.shape[-1]
    dt = jnp.exp(log_dt.astype(F32))[..., None]
    lr, li = jnp.minimum(lam_re.astype(F32), -1e-4), lam_im.astype(F32)
    ar, ai = lr * dt, li * dt

    lbm = jnp.exp(ar)
    lbr, lbi = lbm * jnp.cos(ai), lbm * jnp.sin(ai)

    ks = jnp.arange(2 * t, dtype=F32)
    tab_m = jnp.exp(ar[..., None] * ks)
    tab_r, tab_i = tab_m * jnp.cos(ai[..., None] * ks), tab_m * jnp.sin(ai[..., None] * ks)

    def powers(d, first, step, steps_major=False):
        first = int(first)
        if step > 0:
            out = tab_r[d, :, :, first:first + t], tab_i[d, :, :, first:first + t]
        else:
            out = (jnp.flip(tab_r[d, :, :, first - t + 1:first + 1], -1),
                   jnp.flip(tab_i[d, :, :, first - t + 1:first + 1], -1))
        return tuple(jnp.swapaxes(a, 1, 2) for a in out) if steps_major else out

    nr, ni = lbr - 1.0, lbi
    den = lr * lr + li * li
    qr, qi = (nr * lr + ni * li) / den, (ni * lr - nr * li) / den
    br, bi = (jnp.swapaxes(a.astype(F32), -1, -2) for a in (b_re, b_im))
    bbr = qr[:, :, None] * br - qi[:, :, None] * bi
    bbi = qr[:, :, None] * bi + qi[:, :, None] * br
    cr, ci = c_re.astype(F32), c_im.astype(F32)

    kb_a = jnp.concatenate([bbr[0], bbr[0], bbr[1], bbr[1]], axis=-1)
    kb_b = jnp.concatenate([-bbi[0], bbi[0], -bbi[1], bbi[1]], axis=-1)
    kc_a = jnp.concatenate([cr[0], ci[0], cr[1], ci[1]], axis=-1)
    kc_b = jnp.concatenate([ci[0], cr[0], ci[1], cr[1]], axis=-1)
    wfr, wfi = powers(0, 0.0, 1.0)
    wbr, wbi = powers(1, t - 1.0, -1.0)
    fwd = lambda a: jnp.pad(a, ((0, 0), (0, 0), (t - 1, 1)))
    bwd = lambda a: jnp.pad(a, ((0, 0), (0, 0), (0, t)))
    kw = jnp.concatenate([fwd(wfr), fwd(-wfi), bwd(wbr), bwd(-wbi)], axis=1)

    pfr, pfi = powers(0, t - 1.0, -1.0, steps_major=True)
    pbr, pbi = powers(1, 0.0, 1.0, steps_major=True)
    ew_a = jnp.concatenate([pfr, pfr, pbr, pbr], axis=-1)
    ew_b = jnp.concatenate([pfi, pfi, pbi, pbi], axis=-1)
    eb_a = jnp.concatenate([bbr[0], bbi[0], bbr[1], bbi[1]], axis=-1)
    eb_b = jnp.concatenate([-bbi[0], bbr[0], -bbi[1], bbr[1]], axis=-1)

    qfr, qfi = powers(0, 1.0, 1.0)
    qbr, qbi = powers(1, float(t), -1.0)
    crf, cif, crb, cib = (jnp.swapaxes(a, 1, 2) for a in (cr[0], ci[0], cr[1], ci[1]))
    dc = jnp.concatenate([jnp.concatenate([crf, -crf, crb, -crb], axis=1),
                          jnp.concatenate([-cif, -cif, -cib, -cib], axis=1)], axis=-1)
    dq = jnp.concatenate([jnp.concatenate([qfr, qfi, qbr, qbi], axis=1),
                          jnp.concatenate([qfi, qfr, qbi, qbr], axis=1)], axis=-1)

    dec = jnp.stack([tab_r[0, :, :, t], tab_i[0, :, :, t],
                     tab_r[1, :, :, t], tab_i[1, :, :, t]]).reshape(4, g * p)
    return (kb_a, kb_b, kc_a, kc_b, kw), (ew_a, ew_b, eb_a, eb_b), (dc.astype(BF16), dq.astype(BF16)), dec


def _carry_expanders(h, t):
    ht = h * t
    rep = np.zeros((2 * h, 2 * ht), np.float32)
    til = np.zeros((2 * t, 2 * ht), np.float32)
    for half in range(2):
        for ho in range(h):
            for tt in range(t):
                rep[half * h + ho, half * ht + ho * t + tt] = 1.0
                til[half * t + tt, half * ht + ho * t + tt] = 1.0
    return jnp.asarray(rep, dtype=BF16), jnp.asarray(til, dtype=BF16)


def _toeplitz_select(t):
    sel = np.zeros((4 * t, (t // 8) * 2 * t), np.float32)
    for q in range(t // 8):
        for tt in range(t):
            j = tt - 8 * q + t - 8
            sel[j, q * 2 * t + tt] = 1.0
            sel[2 * t + j, q * 2 * t + t + tt] = 1.0
    return jnp.asarray(sel, dtype=BF16)


GPS = 4


def _gather_chunks(z_ref, gi, h):
    t = z_ref.shape[-1] // 2
    lo = lax.broadcasted_iota(jnp.int32, z_ref.shape[1:], 1) < t
    ev, od = [], []
    for k in range(0, h, 2):
        za, zb = z_ref[gi * h + k], z_ref[gi * h + k + 1]
        ev.append(jnp.where(lo, za, pltpu.roll(zb, t, axis=1)))
        od.append(jnp.where(lo, pltpu.roll(za, t, axis=1), zb))
    return jnp.concatenate([jnp.concatenate(ev, axis=1), jnp.concatenate(od, axis=1)], axis=0)


def _s5_state_kernel(zc_ref, z_ref, ewa_ref, ewb_ref, eba_ref, ebb_ref, *refs):
    couts, louts, e_scr = refs[0:4], refs[4:8], refs[8]
    h = z_ref.shape[0] // GPS
    t = ewa_ref.shape[1]
    p = ewa_ref.shape[-1] // 4
    for gi in range(GPS):
        wa, wb = ewa_ref[gi], ewb_ref[gi]
        for k in range(h):
            e_scr[k * t:(k + 1) * t, :] = (wa * eba_ref[gi, k:k + 1, :] + wb * ebb_ref[gi, k:k + 1, :]).astype(BF16)
        for src, outs in ((zc_ref, couts), (z_ref, louts)):
            a2 = _gather_chunks(src, gi, h).astype(BF16)
            s = _dot(a2, e_scr[...])
            for k in range(4):
                outs[k][:, gi * p:(gi + 1) * p] = s[:, k * p:(k + 1) * p]


def _s5_states(zc3, z3, e_tabs):
    gh, npair, t2 = z3.shape
    cpair = zc3.shape[1]
    t = t2 // 2
    g, _, p4 = e_tabs[0].shape
    h, p = gh // g, p4 // 4
    wspec = pl.BlockSpec((GPS, t, p4), lambda i: (i, 0, 0))
    bspec = pl.BlockSpec((GPS, h, p4), lambda i: (i, 0, 0))
    res = pl.pallas_call(
        _s5_state_kernel,
        grid=(g // GPS,),
        in_specs=[pl.BlockSpec((GPS * h, cpair, t2), lambda i: (i, 0, 0)),
                  pl.BlockSpec((GPS * h, npair, t2), lambda i: (i, 0, 0)), wspec, wspec, bspec, bspec],
        out_specs=[pl.BlockSpec((2 * cpair, GPS * p), lambda i: (0, i))] * 4 +
                  [pl.BlockSpec((2 * npair, GPS * p), lambda i: (0, i))] * 4,
        out_shape=[jax.ShapeDtypeStruct((2 * cpair, g * p), F32)] * 4 +
                  [jax.ShapeDtypeStruct((2 * npair, g * p), F32)] * 4,
        scratch_shapes=[pltpu.VMEM((h * t, p4), BF16)],
        compiler_params=_cparams("parallel"),
        name="s5_chunk_states",
    )(zc3, z3, *e_tabs)
    return res[:4], res[4:]


def _s5_scan_kernel(bsz, dec_ref, cfre, cfim, cbre, cbim, sfre, sfim, sbre, sbim,
                    hfre, hfim, hbre, hbim):
    nctx = cfre.shape[0] // bsz
    nlat = sfre.shape[0] // bsz
    width = dec_ref.shape[-1]
    fr, fi = dec_ref[0:1, :], dec_ref[1:2, :]
    br, bi = dec_ref[2:3, :], dec_ref[3:4, :]
    zero = jnp.zeros((1, width), F32)

    def step(ar, ai, hr, hi, sr, si):
        return ar * hr - ai * hi + sr, ar * hi + ai * hr + si

    def row(nchunk, b, j):
        return (j % 2) * (bsz * nchunk // 2) + b * (nchunk // 2) + j // 2

    init = []
    for b in range(bsz):
        hr, hi = zero, zero
        for j in range(nctx):
            r = row(nctx, b, j)
            hr, hi = step(fr, fi, hr, hi, cfre[r:r + 1, :], cfim[r:r + 1, :])
        init += [hr, hi]
        hr, hi = zero, zero
        for j in range(nctx - 1, -1, -1):
            r = row(nctx, b, j)
            hr, hi = step(br, bi, hr, hi, cbre[r:r + 1, :], cbim[r:r + 1, :])
        init += [hr, hi]

    def body(k, carry):
        out = []
        for b in range(bsz):
            hr, hi, gr, gi = carry[4 * b:4 * b + 4]
            rf = row(nlat, b, k)
            hfre[pl.ds(rf, 1), :] = hr
            hfim[pl.ds(rf, 1), :] = hi
            out += list(step(fr, fi, hr, hi, sfre[pl.ds(rf, 1), :], sfim[pl.ds(rf, 1), :]))
            rb = row(nlat, b, nlat - 1 - k)
            hbre[pl.ds(rb, 1), :] = gr
            hbim[pl.ds(rb, 1), :] = gi
            out += list(step(br, bi, gr, gi, sbre[pl.ds(rb, 1), :], sbim[pl.ds(rb, 1), :]))
        return tuple(out)

    lax.fori_loop(0, nlat, body, tuple(init), unroll=2)


def _s5_scan(dec, ctx_s, lat_s, bsz, tw=1024):
    gp = dec.shape[-1]
    rc, rl = ctx_s[0].shape[0], lat_s[0].shape[0]
    assert (rc // bsz) % 2 == 0 and (rl // bsz) % 2 == 0
    tw = min(tw, gp)
    cspec = pl.BlockSpec((rc, tw), lambda i: (0, i))
    lspec = pl.BlockSpec((rl, tw), lambda i: (0, i))
    out = jax.ShapeDtypeStruct((rl, gp), F32)
    return pl.pallas_call(
        functools.partial(_s5_scan_kernel, bsz),
        grid=(gp // tw,),
        in_specs=[pl.BlockSpec((4, tw), lambda i: (0, i))] + [cspec] * 4 + [lspec] * 4,
        out_specs=[lspec] * 4,
        out_shape=[out] * 4,
        compiler_params=_cparams("parallel"),
        name="s5_state_scan",
    )(dec, *ctx_s, *lat_s)


def _gelu_tanh(x):
    return 0.5 * x * (1.0 + jnp.tanh(math.sqrt(2.0 / math.pi) * (x + 0.044715 * (x * x * x))))


def _expand_toeplitz(k_ref, sel_ref, lhs_ref, res_ref, w_ref, h, t):
    half = h // 2
    rows_per_hi = half * 8

    def fill(hi, carry):
        for hp in range(half):
            for par in range(2):
                v = k_ref[pl.ds(hi * h + 2 * hp + par, 1), :]
                b = pltpu.roll(jnp.broadcast_to(v, (8, 2 * t)), 2 * t - 7, axis=1, stride=1, stride_axis=0)
                lhs_ref[pl.ds(pl.multiple_of(hi * rows_per_hi + hp * 8, 8), 8), par * 2 * t:(par + 1) * 2 * t] = b
        return carry

    lax.fori_loop(0, h, fill, 0, unroll=True)
    res_ref[...] = _dot(lhs_ref[...].astype(BF16), sel_ref[...])

    def shuffle(hi, carry):
        base = pl.multiple_of(hi * rows_per_hi, rows_per_hi)
        r_hi = res_ref[pl.ds(base, rows_per_hi), :]
        rows = [jnp.concatenate([r_hi[hp * 8:(hp + 1) * 8, q * 2 * t:(q + 1) * 2 * t] for hp in range(half)], axis=1)
                for q in range(t // 8)]
        w_ref[pl.ds(base, t), :] = jnp.concatenate(rows, axis=0).astype(BF16)
        return carry

    lax.fori_loop(0, h, shuffle, 0, unroll=True)


def _s5_out_kernel(z_ref, kba_ref, kbb_ref, kca_ref, kcb_ref, kw_ref, sel_ref, dc_ref, dq_ref, rep_ref, til_ref,
                   hfre, hfim, hbre, hbim, dsk_ref, o_ref, km_scr, k_scr, lhs_ref, res_ref, w_ref):
    h = z_ref.shape[0] // GPS
    npair = z_ref.shape[1]
    t = z_ref.shape[-1] // 2
    ht = h * t
    p = dc_ref.shape[1] // 4
    lo = lax.broadcasted_iota(jnp.int32, z_ref.shape[1:], 1) < t
    for gi in range(GPS):
        for hi in range(h):
            km_scr[hi * h:(hi + 1) * h, :] = (kba_ref[gi, hi:hi + 1, :] * kca_ref[gi]
                                              + kbb_ref[gi, hi:hi + 1, :] * kcb_ref[gi])
        k_scr[...] = jnp.dot(km_scr[...], kw_ref[gi], preferred_element_type=F32, precision=HI)
        _expand_toeplitz(k_scr, sel_ref, lhs_ref, res_ref, w_ref, h, t)
        ce = _dot(dc_ref[gi], rep_ref[...])
        qe = _dot(dq_ref[gi], til_ref[...])
        dmat = (ce[:, :ht] * qe[:, :ht] + ce[:, ht:] * qe[:, ht:]).astype(BF16)
        u = _gather_chunks(z_ref, gi, h)
        sl = slice(gi * p, (gi + 1) * p)
        hp = jnp.concatenate([hfre[:, sl], hfim[:, sl], hbre[:, sl], hbim[:, sl]], axis=-1).astype(BF16)
        y = _dot(u.astype(BF16), w_ref[...]) + _dot(hp, dmat)
        gl = _gelu_tanh(y + u * dsk_ref[gi])
        for k in range(0, h, 2):
            te = gl[0:npair, k * t:(k + 2) * t]
            to = gl[npair:2 * npair, k * t:(k + 2) * t]
            o_ref[gi * h + k] = jnp.where(lo, te, pltpu.roll(to, t, axis=1))
            o_ref[gi * h + k + 1] = jnp.where(lo, pltpu.roll(te, t, axis=1), to)


def _s5_out(z3, k_tabs, sel, d_tabs, expanders, hin, dsk):
    gh, npair, t2z = z3.shape
    nchunk, t = 2 * npair, t2z // 2
    kb_a, kb_b, kc_a, kc_b, kw = k_tabs
    g, kp, t2 = kw.shape
    h = gh // g
    hh, ht = h * h, h * t
    assert t2 == 2 * t and (h // 2) * 8 == t and t2 == LANES
    dc, dq = d_tabs
    rep, til = expanders
    p4 = dc.shape[1]
    p = p4 // 4
    hspec = pl.BlockSpec((nchunk, GPS * p), lambda i: (0, i))
    const = lambda a: pl.BlockSpec(a.shape, lambda i: (0, 0))
    return pl.pallas_call(
        _s5_out_kernel,
        grid=(g // GPS,),
        in_specs=[pl.BlockSpec((GPS * h, npair, t2z), lambda i: (i, 0, 0)),
                  ] + [pl.BlockSpec((GPS, h, kp), lambda i: (i, 0, 0))] * 4 + [
                  pl.BlockSpec((GPS, kp, t2), lambda i: (i, 0, 0)),
                  const(sel),
                  pl.BlockSpec((GPS, p4, 2 * h), lambda i: (i, 0, 0)),
                  pl.BlockSpec((GPS, p4, 2 * t), lambda i: (i, 0, 0)),
                  const(rep), const(til)] + [hspec] * 4 +
                 [pl.BlockSpec((GPS, 1, ht), lambda i: (i, 0, 0))],
        out_specs=pl.BlockSpec((GPS * h, npair, t2z), lambda i: (i, 0, 0)),
        out_shape=jax.ShapeDtypeStruct((gh, npair, t2z), F32),
        scratch_shapes=[pltpu.VMEM((hh, kp), F32), pltpu.VMEM((hh, t2), F32), pltpu.VMEM((ht, 4 * t), F32),
                        pltpu.VMEM((ht, ht), F32), pltpu.VMEM((ht, ht), BF16)],
        compiler_params=_cparams("parallel"),
        name="s5_chunk_out",
    )(z3, kb_a, kb_b, kc_a, kc_b, kw, sel, dc, dq, rep, til, *hin, dsk)


def _glu_kernel(gt_ref, wa_ref, wb_ref, ba_ref, bb_ref, o_ref, g_scr, stage_ref):
    @pl.when(pl.program_id(1) == 0)
    def _():
        c, tq, _ = gt_ref.shape
        half = stage_ref.shape[0] // tq
        for c0 in range(0, c, half):
            stage_ref[...] = gt_ref[c0:c0 + half].reshape(half * tq, LANES)
            for q in range(tq):
                g_scr[c0:c0 + half, q * LANES:(q + 1) * LANES] = stage_ref[pl.ds(q, half, stride=tq), :].astype(BF16)

    dn = (((0,), (0,)), ((), ()))
    gt = g_scr[...]
    a = lax.dot_general(gt, wa_ref[...], dn, preferred_element_type=F32) + ba_ref[...]
    b = lax.dot_general(gt, wb_ref[...], dn, preferred_element_type=F32) + bb_ref[...]
    o_ref[...] = (a * jax.nn.sigmoid(b)).astype(o_ref.dtype)


def _glu(g3, wa, wb, ba, bb, tm=1024, tn=512):
    c, nq, _ = g3.shape
    n = nq * LANES
    co = wa.shape[-1]
    tm, tn = min(tm, n), min(tn, co)
    stage_rows = (c // 2) * (tm // LANES)
    return pl.pallas_call(
        _glu_kernel,
        grid=(n // tm, co // tn),
        in_specs=[pl.BlockSpec((c, tm // LANES, LANES), lambda i, j: (0, i, 0)),
                  pl.BlockSpec((c, tn), lambda i, j: (0, j)),
                  pl.BlockSpec((c, tn), lambda i, j: (0, j)),
                  pl.BlockSpec((1, tn), lambda i, j: (0, j)),
                  pl.BlockSpec((1, tn), lambda i, j: (0, j))],
        out_specs=pl.BlockSpec((tm, tn), lambda i, j: (i, j)),
        out_shape=jax.ShapeDtypeStruct((n, co), BF16),
        scratch_shapes=[pltpu.VMEM((c, tm), BF16), pltpu.VMEM((stage_rows, LANES), F32)],
        compiler_params=_cparams("parallel", "arbitrary"),
        name="s5_glu",
    )(g3, wa, wb, ba.reshape(1, co), bb.reshape(1, co))


def _out_proj_kernel(yf_ref, ys_ref, wf_ref, ws_ref, x_ref, g_ref, o_ref):
    acc = _dot(yf_ref[...], wf_ref[...]) + _dot(ys_ref[...], ws_ref[...])
    o_ref[...] = x_ref[...] + g_ref[...] * acc


def _out_proj(yf, ys, w_out, x2, gate, length, tm=1024, tn=1024):
    n, fw = yf.shape
    sw = ys.shape[-1]
    d = w_out.shape[-1]
    tm, tn = min(tm, length), min(tn, d)
    assert fw % sw == 0
    bsz = gate.shape[0]
    return pl.pallas_call(
        _out_proj_kernel,
        grid=(n // tm, d // tn),
        in_specs=[pl.BlockSpec((tm, fw), lambda i, j: (i, 0)),
                  pl.BlockSpec((tm, sw), lambda i, j: (i, 0)),
                  pl.BlockSpec((fw, tn), lambda i, j: (0, j)),
                  pl.BlockSpec((sw, tn), lambda i, j: (fw // sw, j)),
                  pl.BlockSpec((tm, tn), lambda i, j: (i, j)),
                  pl.BlockSpec((None, 1, tn), lambda i, j: ((i * tm) // length, 0, j))],
        out_specs=pl.BlockSpec((tm, tn), lambda i, j: (i, j)),
        out_shape=jax.ShapeDtypeStruct((n, d), F32),
        compiler_params=_cparams("parallel", "arbitrary"),
        name="out_proj_residual",
    )(yf, ys, w_out, w_out, x2, gate.reshape(bsz, 1, d))


def _ffn_hidden_start(j, th, hid):
    return pl.multiple_of(jnp.minimum(j * th, hid - th), math.gcd(th, hid - th))


def _ffn_kernel(hid, h_ref, wg_ref, wu_ref, wd_ref, o_ref):
    j = pl.program_id(1)
    th = wg_ref.shape[-1]
    hh = h_ref[...]
    g = _dot(hh, wg_ref[...].astype(BF16))
    u = _dot(hh, wu_ref[...].astype(BF16))
    unit = _ffn_hidden_start(j, th, hid) + lax.broadcasted_iota(jnp.int32, (1, th), 1)
    a = jnp.where(unit >= j * th, g * jax.nn.sigmoid(g) * u, 0.0).astype(BF16)
    d = o_ref.shape[-1]
    nc = min(FFN_DOWN_CHUNK, d)
    wd = wd_ref[...].astype(BF16)

    @pl.when(j == 0)
    def _():
        for c0 in range(0, d, nc):
            o_ref[:, c0:c0 + nc] = _dot(a, wd[:, c0:c0 + nc])

    @pl.when(j > 0)
    def _():
        for c0 in range(0, d, nc):
            o_ref[:, c0:c0 + nc] += _dot(a, wd[:, c0:c0 + nc])


FFN_DOWN_CHUNK = 512


FFN_TH = 256
FFN_VMEM_LIMIT = 60 * 1024 * 1024


def _ffn(h, wg, wu, wd, tm=1024):
    n, d = h.shape
    hid = wg.shape[-1]
    tm, th = min(tm, n), min(FFN_TH, hid)
    col_map = lambda i, j: (0, _ffn_hidden_start(j, th, hid))
    return pl.pallas_call(
        functools.partial(_ffn_kernel, hid),
        grid=(n // tm, pl.cdiv(hid, th)),
        in_specs=[pl.BlockSpec((tm, d), lambda i, j: (i, 0), pipeline_mode=pl.Buffered(1)),
                  pl.BlockSpec((pl.Element(d), pl.Element(th)), col_map),
                  pl.BlockSpec((pl.Element(d), pl.Element(th)), col_map),
                  pl.BlockSpec((pl.Element(th), pl.Element(d)), lambda i, j: (_ffn_hidden_start(j, th, hid), 0))],
        out_specs=pl.BlockSpec((tm, d), lambda i, j: (i, 0), pipeline_mode=pl.Buffered(1)),
        out_shape=jax.ShapeDtypeStruct((n, d), F32),
        compiler_params=_cparams("parallel", "arbitrary", vmem=FFN_VMEM_LIMIT),
        name="swiglu_ffn",
    )(h, wg, wu, wd)


def _final_kernel(x_ref, f_ref, gate_ref, g_ref, o_ref):
    x = x_ref[...] + gate_ref[...] * f_ref[...]
    ms = jnp.mean(x * x, axis=-1, keepdims=True)
    o_ref[...] = x * lax.rsqrt(ms + EPS) * g_ref[...]


def _final(x1, f, gate, g, tm=256):
    bsz, length, d = x1.shape
    tm = min(tm, length)
    return pl.pallas_call(
        _final_kernel,
        grid=(bsz, length // tm),
        in_specs=[pl.BlockSpec((None, tm, d), lambda b, i: (b, i, 0)),
                  pl.BlockSpec((None, tm, d), lambda b, i: (b, i, 0)),
                  pl.BlockSpec((None, 1, d), lambda b, i: (b, 0, 0)),
                  pl.BlockSpec((1, d), lambda b, i: (0, 0))],
        out_specs=pl.BlockSpec((None, tm, d), lambda b, i: (b, i, 0)),
        out_shape=jax.ShapeDtypeStruct((bsz, length, d), F32),
        compiler_params=_cparams("parallel", "parallel"),
        name="residual_final_norm",
    )(x1, f, gate.reshape(bsz, 1, d), g.reshape(1, d))


def _dft_split(length):
    n2 = 64
    while (length // n2) % PACK_ROWS:
        n2 //= 2
    assert n2 >= 8 and length % n2 == 0
    return n2


def kernel(x, c, ctx, c_ctx, ada_w, ada_b, norm1_g, norm2_g, w_in, w_out, fourier_w, s5_lam_re, s5_lam_im, s5_log_dt, s5_b_re, s5_b_im, s5_c_re, s5_c_im, s5_d, glu_w_a, glu_b_a, glu_w_b, glu_b_b, ffn_w_gate, ffn_w_up, ffn_w_down, final_g):
    bsz, length, d = x.shape
    depth = ada_w.shape[0]
    assert depth == 1, "single-layer block"
    lyr = 0
    heads, hd, _ = fourier_w.shape[1:]
    fw = heads * hd
    _, g, p, hgrp = s5_b_re.shape[1:]
    sw = g * hgrp
    assert w_in.shape[-1] == fw + sw and length % (2 * CHUNK) == 0 and ctx.shape[1] % (2 * CHUNK) == 0
    n = bsz * length

    a8 = jnp.zeros((8, d), F32).at[:bsz].set(c.astype(F32)).at[bsz].set(c_ctx.astype(F32))
    mods = _ada(jnp.concatenate([a8, a8], axis=0), ada_w[lyr], ada_b[lyr]).reshape(8, N_MOD, d)
    sh1, sc1, g1, sh2, sc2, g2 = (mods[:bsz, i] for i in range(N_MOD))
    csh1, csc1 = mods[bsz:bsz + 1, 0], mods[bsz:bsz + 1, 1]

    w_in_b = w_in[lyr].astype(BF16)
    ang = (2.0 * np.pi / hd) * ((np.arange(hd)[:, None] * np.arange(hd)[None, :]) % hd).astype(np.float64)
    cd = jnp.asarray(np.cos(ang) / math.sqrt(hd), F32)
    sd = jnp.asarray(np.sin(ang) / math.sqrt(hd), F32)
    csd = jnp.broadcast_to(jnp.stack([cd, sd])[:, None], (2, heads, hd, hd)).reshape(2 * heads, hd, hd)
    wf2 = jnp.concatenate([fourier_w[lyr], fourier_w[lyr]], axis=0).astype(F32)
    folded = _fold(csd, wf2)
    wcs = jnp.concatenate([folded[:heads], folded[heads:]], axis=-1).astype(BF16)
    k_tabs, e_mat, d_mat, dec = _s5_tables(s5_lam_re[lyr], s5_lam_im[lyr], s5_log_dt[lyr], s5_b_re[lyr],
                                         s5_b_im[lyr], s5_c_re[lyr], s5_c_im[lyr])
    sel = _toeplitz_select(CHUNK)
    dsk = jnp.repeat(s5_d[lyr].astype(F32).reshape(g, 1, hgrp), CHUNK, axis=-1)

    hc = _norm_mod(ctx, norm1_g[lyr], csh1, csc1)
    nctx_tok = bsz * ctx.shape[1]
    zc3 = _proj_t(w_in_b, fw, sw, hc.reshape(nctx_tok, d))

    hm = _norm_mod(x, norm1_g[lyr], sh1, sc1).reshape(n, d)
    n2 = _dft_split(length)
    pc4, ps4 = _four_in(hm, w_in_b, wcs, bsz, length, n2)
    g_tab, cs_tab = _dft_tables(length, n2)
    y_four = _dft(pc4, ps4, g_tab, cs_tab).reshape(n, fw)

    z3 = _proj_t(w_in_b, fw, sw, hm)
    ctx_s, lat_s = _s5_states(zc3, z3, e_mat)
    h_in = _s5_scan(dec, ctx_s, lat_s, bsz)
    g3 = _s5_out(z3, k_tabs, sel, d_mat, _carry_expanders(hgrp, CHUNK), h_in, dsk)
    y_s = _glu(g3, glu_w_a[lyr].astype(BF16), glu_w_b[lyr].astype(BF16), glu_b_a[lyr], glu_b_b[lyr])

    x1 = _out_proj(y_four, y_s, w_out[lyr].astype(BF16), x.reshape(n, d), g1, length)

    hm2 = _norm_mod(x1.reshape(bsz, length, d), norm2_g[lyr], sh2, sc2).reshape(n, d)
    f = _ffn(hm2, ffn_w_gate[lyr], ffn_w_up[lyr], ffn_w_down[lyr])
    return _final(x1.reshape(bsz, length, d), f.reshape(bsz, length, d), g2, final_g)
```

```python
import functools
import math

import numpy as np
import jax
import jax.numpy as jnp
from jax import lax
from jax.experimental import pallas as pl
from jax.experimental.pallas import tpu as pltpu

F32 = jnp.float32
BF16 = jnp.bfloat16
EPS = 1e-6
CHUNK = 64
N_MOD = 6
V7X_VMEM_LIMIT = 56 * 1024 * 1024
HI = lax.Precision.HIGHEST


def _cparams(*sem, vmem=V7X_VMEM_LIMIT):
    return pltpu.CompilerParams(dimension_semantics=sem, vmem_limit_bytes=vmem)


def _dot(a, b):
    return jnp.dot(a, b, preferred_element_type=F32)


def _ada_kernel(a_ref, w_ref, b_ref, o_ref):
    a = a_ref[...]
    s = a * jax.nn.sigmoid(a)
    s_hi = s.astype(BF16).astype(F32)
    row = lax.broadcasted_iota(jnp.int32, s.shape, 0)
    lhs = jnp.where(row < 8, s_hi, s - s_hi).astype(BF16)
    w = w_ref[...]
    w_hi = w.astype(BF16)
    w_lo = (w - w_hi.astype(F32)).astype(BF16)
    r = _dot(lhs, w_hi) + _dot(lhs, w_lo)
    o_ref[...] = r[0:8] + r[8:16] + b_ref[...]


def _ada(a16, w, b, tn=1024):
    d, n = w.shape
    tn = min(tn, n)
    return pl.pallas_call(
        _ada_kernel,
        grid=(n // tn,),
        in_specs=[pl.BlockSpec((16, d), lambda j: (0, 0)),
                  pl.BlockSpec((d, tn), lambda j: (0, j)),
                  pl.BlockSpec((1, tn), lambda j: (0, j))],
        out_specs=pl.BlockSpec((8, tn), lambda j: (0, j)),
        out_shape=jax.ShapeDtypeStruct((8, n), F32),
        compiler_params=_cparams("parallel"),
        name="ada_matvec",
    )(a16, w, b.reshape(1, n))


def _norm_mod_kernel(x_ref, g_ref, sh_ref, sc_ref, o_ref):
    x = x_ref[...]
    ms = jnp.mean(x * x, axis=-1, keepdims=True)
    y = x * lax.rsqrt(ms + EPS) * g_ref[...]
    o_ref[...] = (y * (1.0 + sc_ref[...]) + sh_ref[...]).astype(o_ref.dtype)


def _norm_mod(x, g, sh, sc, tm=512):
    bsz, length, d = x.shape
    tm = min(tm, length)
    bm = sh.shape[0]
    mod_map = (lambda b, i: (b, 0, 0)) if bm == bsz else (lambda b, i: (0, 0, 0))
    return pl.pallas_call(
        _norm_mod_kernel,
        grid=(bsz, length // tm),
        in_specs=[pl.BlockSpec((None, tm, d), lambda b, i: (b, i, 0)),
                  pl.BlockSpec((1, d), lambda b, i: (0, 0)),
                  pl.BlockSpec((None, 1, d), mod_map),
                  pl.BlockSpec((None, 1, d), mod_map)],
        out_specs=pl.BlockSpec((None, tm, d), lambda b, i: (b, i, 0)),
        out_shape=jax.ShapeDtypeStruct((bsz, length, d), BF16),
        compiler_params=_cparams("parallel", "parallel"),
        name="norm_mod",
    )(x, g.reshape(1, d), sh.reshape(bm, 1, d), sc.reshape(bm, 1, d))


def _fold_kernel(a_ref, b_ref, o_ref):
    o_ref[...] = jnp.dot(a_ref[...], b_ref[...], preferred_element_type=F32, precision=HI)


def _fold(a, b):
    hh, m, k = a.shape
    n = b.shape[-1]
    return pl.pallas_call(
        _fold_kernel,
        grid=(hh,),
        in_specs=[pl.BlockSpec((None, m, k), lambda h: (h, 0, 0)),
                  pl.BlockSpec((None, k, n), lambda h: (h, 0, 0))],
        out_specs=pl.BlockSpec((None, m, n), lambda h: (h, 0, 0)),
        out_shape=jax.ShapeDtypeStruct((hh, m, n), F32),
        compiler_params=_cparams("parallel"),
        name="weight_fold",
    )(a, b)


LANES = 128
PACK_ROWS = 16
PITCH_PAD = 8


def _four_in_kernel(h_ref, w_ref, wcs_ref, pc_ref, ps_ref, p_scr):
    n2, tn1, hd = pc_ref.shape
    pitch = n2 + PITCH_PAD
    nq = hd // LANES
    z = _dot(h_ref[...], w_ref[...]).astype(BF16)
    p = _dot(z, wcs_ref[...])
    for q in range(2 * nq):
        for i1 in range(tn1):
            p_scr[q, i1 * pitch:i1 * pitch + n2, :] = p[i1 * n2:(i1 + 1) * n2, q * LANES:(q + 1) * LANES]

    def emit(j2, carry):
        for q in range(nq):
            pc_ref[j2, :, q * LANES:(q + 1) * LANES] = p_scr[q, pl.ds(j2, tn1, stride=pitch), :].astype(BF16)
            ps_ref[j2, :, q * LANES:(q + 1) * LANES] = p_scr[nq + q, pl.ds(j2, tn1, stride=pitch), :].astype(BF16)
        return carry

    lax.fori_loop(0, n2, emit, 0, unroll=True)


def _four_in(h, w_f, wcs, bsz, length, n2):
    n, d = h.shape
    heads, hd, _ = wcs.shape
    n1 = length // n2
    tn1 = PACK_ROWS
    tm = tn1 * n2
    tpb = length // tm
    fw = heads * hd
    out = jax.ShapeDtypeStruct((bsz, n2, n1, fw), BF16)
    ospec = pl.BlockSpec((None, n2, tn1, hd), lambda i, j: (i // tpb, 0, i % tpb, j))
    return pl.pallas_call(
        _four_in_kernel,
        grid=(n // tm, heads),
        in_specs=[pl.BlockSpec((tm, d), lambda i, j: (i, 0)),
                  pl.BlockSpec((d, hd), lambda i, j: (0, j)),
                  pl.BlockSpec((None, hd, 2 * hd), lambda i, j: (j, 0, 0))],
        out_specs=[ospec, ospec],
        out_shape=[out, out],
        scratch_shapes=[pltpu.VMEM((2 * hd // LANES, tn1 * (n2 + PITCH_PAD), LANES), F32)],
        compiler_params=_cparams("parallel", "arbitrary"),
        name="fourier_in_proj",
    )(h, w_f, wcs)


def _dft_tables(length, n2):
    n1 = length // n2
    k1 = np.arange(n1)[:, None, None]
    i1 = np.arange(n1)[None, :, None]
    i2 = np.arange(n2)[None, None, :]
    phase = (k1 * (n2 * i1 + i2)) % length
    phi = (2.0 * np.pi / length) * phase.astype(np.float64)
    c = np.cos(phi).transpose(2, 0, 1) / math.sqrt(n1)
    s = np.sin(phi).transpose(2, 0, 1) / math.sqrt(n1)
    g = np.concatenate([np.concatenate([c, -s], axis=2),
                        np.concatenate([-s, -c], axis=2)], axis=1)
    k2 = np.arange(n2)[:, None]
    j2 = np.arange(n2)[None, :]
    th = (2.0 * np.pi / n2) * ((k2 * j2) % n2).astype(np.float64)
    cs = np.concatenate([np.cos(th), np.sin(th)], axis=1) / math.sqrt(n2)
    return jnp.asarray(g, dtype=BF16), jnp.asarray(cs, dtype=BF16)


def _dft_kernel(g_ref, cs_ref, pc_ref, ps_ref, o_ref, t_scr, y_scr):
    n2, n1, _ = pc_ref.shape
    m = 2 * n1
    p1 = m + PITCH_PAD
    p2 = n1 + PITCH_PAD

    def stage1(j2, carry):
        rhs = jnp.concatenate([pc_ref[j2], ps_ref[j2]], axis=0)
        t_scr[pl.ds(pl.multiple_of(j2 * p1, 8), m), :] = _dot(g_ref[j2], rhs)
        return carry

    lax.fori_loop(0, n2, stage1, 0, unroll=True)

    cs = cs_ref[...]

    def stage2(i, carry):
        k1 = 2 * i
        cols = []
        for dk in range(2):
            re = t_scr[pl.ds(k1 + dk, n2, stride=p1), :]
            im = t_scr[pl.ds(n1 + k1 + dk, n2, stride=p1), :]
            cols.append(jnp.concatenate([re, im], axis=0).astype(BF16))
        res = _dot(cs, jnp.concatenate(cols, axis=1))
        for dk in range(2):
            y_scr[pl.ds(k1 + dk, n2, stride=p2), :] = res[:, dk * LANES:(dk + 1) * LANES]
        return carry

    lax.fori_loop(0, n1 // 2, stage2, 0, unroll=True)
    for k2 in range(n2):
        o_ref[k2 * n1:(k2 + 1) * n1, :] = y_scr[k2 * p2:k2 * p2 + n1, :].astype(o_ref.dtype)


def _dft(pc4, ps4, g, cs):
    bsz, n2, n1, w = pc4.shape
    m = 2 * n1
    ispec = pl.BlockSpec((None, n2, n1, LANES), lambda b, j: (b, 0, 0, j))
    return pl.pallas_call(
        _dft_kernel,
        grid=(bsz, w // LANES),
        in_specs=[pl.BlockSpec((n2, m, m), lambda b, j: (0, 0, 0), pipeline_mode=pl.Buffered(1)),
                  pl.BlockSpec((n2, 2 * n2), lambda b, j: (0, 0)),
                  ispec, ispec],
        out_specs=pl.BlockSpec((None, n2 * n1, LANES), lambda b, j: (b, 0, j)),
        out_shape=jax.ShapeDtypeStruct((bsz, n2 * n1, w), BF16),
        scratch_shapes=[pltpu.VMEM((n2 * (m + PITCH_PAD), LANES), F32),
                        pltpu.VMEM((n2 * (n1 + PITCH_PAD), LANES), F32)],
        compiler_params=_cparams("parallel", "parallel"),
        name="position_dft",
    )(g, cs, pc4, ps4)


def _nt_kernel(w_ref, h_ref, o_ref, stage_ref):
    acc = lax.dot_general(w_ref[...], h_ref[...], (((0,), (1,)), ((), ())), preferred_element_type=F32)
    rows, tq, _ = o_ref.shape
    if tq % 8 == 0:
        for q in range(tq):
            stage_ref[pl.ds(q, rows, stride=tq), :] = acc[:, q * LANES:(q + 1) * LANES]
        o_ref[...] = stage_ref[...].reshape(rows, tq, LANES)
    else:
        for q in range(tq):
            o_ref[:, q, :] = acc[:, q * LANES:(q + 1) * LANES]


def _proj_t(w, col0, c, h, tmc=1024, tn=1024):
    d = w.shape[0]
    n = h.shape[0]
    tmc, tn = min(tmc, c), min(tn, n)
    assert col0 % tmc == 0 and c % tmc == 0
    return pl.pallas_call(
        _nt_kernel,
        grid=(n // tn, c // tmc),
        in_specs=[pl.BlockSpec((d, tmc), lambda i, j: (0, col0 // tmc + j)),
                  pl.BlockSpec((tn, d), lambda i, j: (i, 0))],
        out_specs=pl.BlockSpec((tmc, tn // LANES, LANES), lambda i, j: (j, i, 0)),
        out_shape=jax.ShapeDtypeStruct((c, n // LANES, LANES), F32),
        scratch_shapes=[pltpu.VMEM((tmc * (tn // LANES), LANES), F32)],
        compiler_params=_cparams("parallel", "arbitrary"),
        name="s5_in_proj_t",
    )(w, h)


def _s5_tables(lam_re, lam_im, log_dt, b_re, b_im, c_re, c_im):
    t = CHUNK
    _, g, p = lam_re.shape
    h = b_re.shape[-1]
    dt = jnp.exp(log_dt.astype(F32))[..., None]
    lr, li = jnp.minimum(lam_re.astype(F32), -1e-4), lam_im.astype(F32)
    ar, ai = lr * dt, li * dt

    lbm = jnp.exp(ar)
    lbr, lbi = lbm * jnp.cos(ai), lbm * jnp.sin(ai)

    ks = jnp.arange(2 * t, dtype=F32)
    tab_m = jnp.exp(ar[..., None] * ks)
    tab_r, tab_i = tab_m * jnp.cos(ai[..., None] * ks), tab_m * jnp.sin(ai[..., None] * ks)

    def powers(d, first, step, steps_major=False):
        first = int(first)
        if step > 0:
            out = tab_r[d, :, :, first:first + t], tab_i[d, :, :, first:first + t]
        else:
            out = (jnp.flip(tab_r[d, :, :, first - t + 1:first + 1], -1),
                   jnp.flip(tab_i[d, :, :, first - t + 1:first + 1], -1))
        return tuple(jnp.swapaxes(a, 1, 2) for a in out) if steps_major else out

    nr, ni = lbr - 1.0, lbi
    den = lr * lr + li * li
    qr, qi = (nr * lr + ni * li) / den, (ni * lr - nr * li) / den
    br, bi = (jnp.swapaxes(a.astype(F32), -1, -2) for a in (b_re, b_im))
    bbr = qr[:, :, None] * br - qi[:, :, None] * bi
    bbi = qr[:, :, None] * bi + qi[:, :, None] * br
    cr, ci = c_re.astype(F32), c_im.astype(F32)

    kb_a = jnp.concatenate([bbr[0], bbr[0], bbr[1], bbr[1]], axis=-1)
    kb_b = jnp.concatenate([-bbi[0], bbi[0], -bbi[1], bbi[1]], axis=-1)
    kc_a = jnp.concatenate([cr[0], ci[0], cr[1], ci[1]], axis=-1)
    kc_b = jnp.concatenate([ci[0], cr[0], ci[1], cr[1]], axis=-1)
    wfr, wfi = powers(0, 0.0, 1.0)
    wbr, wbi = powers(1, t - 1.0, -1.0)
    fwd = lambda a: jnp.pad(a, ((0, 0), (0, 0), (t - 1, 1)))
    bwd = lambda a: jnp.pad(a, ((0, 0), (0, 0), (0, t)))
    kw = jnp.concatenate([fwd(wfr), fwd(-wfi), bwd(wbr), bwd(-wbi)], axis=1)

    pfr, pfi = powers(0, t - 1.0, -1.0, steps_major=True)
    pbr, pbi = powers(1, 0.0, 1.0, steps_major=True)
    ew_a = jnp.concatenate([pfr, pfr, pbr, pbr], axis=-1)
    ew_b = jnp.concatenate([pfi, pfi, pbi, pbi], axis=-1)
    eb_a = jnp.concatenate([bbr[0], bbi[0], bbr[1], bbi[1]], axis=-1)
    eb_b = jnp.concatenate([-bbi[0], bbr[0], -bbi[1], bbr[1]], axis=-1)

    qfr, qfi = powers(0, 1.0, 1.0)
    qbr, qbi = powers(1, float(t), -1.0)
    crf, cif, crb, cib = (jnp.swapaxes(a, 1, 2) for a in (cr[0], ci[0], cr[1], ci[1]))
    dc = jnp.concatenate([jnp.concatenate([crf, -crf, crb, -crb], axis=1),
                          jnp.concatenate([-cif, -cif, -cib, -cib], axis=1)], axis=-1)
    dq = jnp.concatenate([jnp.concatenate([qfr, qfi, qbr, qbi], axis=1),
                          jnp.concatenate([qfi, qfr, qbi, qbr], axis=1)], axis=-1)

    dec = jnp.stack([tab_r[0, :, :, t], tab_i[0, :, :, t],
                     tab_r[1, :, :, t], tab_i[1, :, :, t]]).reshape(4, g * p)
    return (kb_a, kb_b, kc_a, kc_b, kw), (ew_a, ew_b, eb_a, eb_b), (dc.astype(BF16), dq.astype(BF16)), dec


def _carry_expanders(h, t):
    ht = h * t
    rep = np.zeros((2 * h, 2 * ht), np.float32)
    til = np.zeros((2 * t, 2 * ht), np.float32)
    for half in range(2):
        for ho in range(h):
            for tt in range(t):
                rep[half * h + ho, half * ht + ho * t + tt] = 1.0
                til[half * t + tt, half * ht + ho * t + tt] = 1.0
    return jnp.asarray(rep, dtype=BF16), jnp.asarray(til, dtype=BF16)


def _toeplitz_select(t):
    sel = np.zeros((4 * t, (t // 8) * 2 * t), np.float32)
    for q in range(t // 8):
        for tt in range(t):
            j = tt - 8 * q + t - 8
            sel[j, q * 2 * t + tt] = 1.0
            sel[2 * t + j, q * 2 * t + t + tt] = 1.0
    return jnp.asarray(sel, dtype=BF16)


GPS = 4


def _gather_chunks(z_ref, gi, h):
    t = z_ref.shape[-1] // 2
    lo = lax.broadcasted_iota(jnp.int32, z_ref.shape[1:], 1) < t
    ev, od = [], []
    for k in range(0, h, 2):
        za, zb = z_ref[gi * h + k], z_ref[gi * h + k + 1]
        ev.append(jnp.where(lo, za, pltpu.roll(zb, t, axis=1)))
        od.append(jnp.where(lo, pltpu.roll(za, t, axis=1), zb))
    return jnp.concatenate([jnp.concatenate(ev, axis=1), jnp.concatenate(od, axis=1)], axis=0)


def _s5_state_kernel(zc_ref, z_ref, ewa_ref, ewb_ref, eba_ref, ebb_ref, *refs):
    couts, louts, e_scr = refs[0:4], refs[4:8], refs[8]
    h = z_ref.shape[0] // GPS
    t = ewa_ref.shape[1]
    p = ewa_ref.shape[-1] // 4
    for gi in range(GPS):
        wa, wb = ewa_ref[gi], ewb_ref[gi]
        for k in range(h):
            e_scr[k * t:(k + 1) * t, :] = (wa * eba_ref[gi, k:k + 1, :] + wb * ebb_ref[gi, k:k + 1, :]).astype(BF16)
        for src, outs in ((zc_ref, couts), (z_ref, louts)):
            a2 = _gather_chunks(src, gi, h).astype(BF16)
            s = _dot(a2, e_scr[...])
            for k in range(4):
                outs[k][:, gi * p:(gi + 1) * p] = s[:, k * p:(k + 1) * p]


def _s5_states(zc3, z3, e_tabs):
    gh, npair, t2 = z3.shape
    cpair = zc3.shape[1]
    t = t2 // 2
    g, _, p4 = e_tabs[0].shape
    h, p = gh // g, p4 // 4
    wspec = pl.BlockSpec((GPS, t, p4), lambda i: (i, 0, 0))
    bspec = pl.BlockSpec((GPS, h, p4), lambda i: (i, 0, 0))
    res = pl.pallas_call(
        _s5_state_kernel,
        grid=(g // GPS,),
        in_specs=[pl.BlockSpec((GPS * h, cpair, t2), lambda i: (i, 0, 0)),
                  pl.BlockSpec((GPS * h, npair, t2), lambda i: (i, 0, 0)), wspec, wspec, bspec, bspec],
        out_specs=[pl.BlockSpec((2 * cpair, GPS * p), lambda i: (0, i))] * 4 +
                  [pl.BlockSpec((2 * npair, GPS * p), lambda i: (0, i))] * 4,
        out_shape=[jax.ShapeDtypeStruct((2 * cpair, g * p), F32)] * 4 +
                  [jax.ShapeDtypeStruct((2 * npair, g * p), F32)] * 4,
        scratch_shapes=[pltpu.VMEM((h * t, p4), BF16)],
        compiler_params=_cparams("parallel"),
        name="s5_chunk_states",
    )(zc3, z3, *e_tabs)
    return res[:4], res[4:]


def _s5_scan_kernel(bsz, dec_ref, cfre, cfim, cbre, cbim, sfre, sfim, sbre, sbim,
                    hfre, hfim, hbre, hbim):
    nctx = cfre.shape[0] // bsz
    nlat = sfre.shape[0] // bsz
    width = dec_ref.shape[-1]
    fr, fi = dec_ref[0:1, :], dec_ref[1:2, :]
    br, bi = dec_ref[2:3, :], dec_ref[3:4, :]
    zero = jnp.zeros((1, width), F32)

    def step(ar, ai, hr, hi, sr, si):
        return ar * hr - ai * hi + sr, ar * hi + ai * hr + si

    def row(nchunk, b, j):
        return (j % 2) * (bsz * nchunk // 2) + b * (nchunk // 2) + j // 2

    init = []
    for b in range(bsz):
        hr, hi = zero, zero
        for j in range(nctx):
            r = row(nctx, b, j)
            hr, hi = step(fr, fi, hr, hi, cfre[r:r + 1, :], cfim[r:r + 1, :])
        init += [hr, hi]
        hr, hi = zero, zero
        for j in range(nctx - 1, -1, -1):
            r = row(nctx, b, j)
            hr, hi = step(br, bi, hr, hi, cbre[r:r + 1, :], cbim[r:r + 1, :])
        init += [hr, hi]

    def body(k, carry):
        out = []
        for b in range(bsz):
            hr, hi, gr, gi = carry[4 * b:4 * b + 4]
            rf = row(nlat, b, k)
            hfre[pl.ds(rf, 1), :] = hr
            hfim[pl.ds(rf, 1), :] = hi
            out += list(step(fr, fi, hr, hi, sfre[pl.ds(rf, 1), :], sfim[pl.ds(rf, 1), :]))
            rb = row(nlat, b, nlat - 1 - k)
            hbre[pl.ds(rb, 1), :] = gr
            hbim[pl.ds(rb, 1), :] = gi
            out += list(step(br, bi, gr, gi, sbre[pl.ds(rb, 1), :], sbim[pl.ds(rb, 1), :]))
        return tuple(out)

    lax.fori_loop(0, nlat, body, tuple(init), unroll=2)


def _s5_scan(dec, ctx_s, lat_s, bsz, tw=1024):
    gp = dec.shape[-1]
    rc, rl = ctx_s[0].shape[0], lat_s[0].shape[0]
    assert (rc // bsz) % 2 == 0 and (rl // bsz) % 2 == 0
    tw = min(tw, gp)
    cspec = pl.BlockSpec((rc, tw), lambda i: (0, i))
    lspec = pl.BlockSpec((rl, tw), lambda i: (0, i))
    out = jax.ShapeDtypeStruct((rl, gp), F32)
    return pl.pallas_call(
        functools.partial(_s5_scan_kernel, bsz),
        grid=(gp // tw,),
        in_specs=[pl.BlockSpec((4, tw), lambda i: (0, i))] + [cspec] * 4 + [lspec] * 4,
        out_specs=[lspec] * 4,
        out_shape=[out] * 4,
        compiler_params=_cparams("parallel"),
        name="s5_state_scan",
    )(dec, *ctx_s, *lat_s)


def _gelu_tanh(x):
    return 0.5 * x * (1.0 + jnp.tanh(math.sqrt(2.0 / math.pi) * (x + 0.044715 * (x * x * x))))


def _expand_toeplitz(k_ref, sel_ref, lhs_ref, res_ref, w_ref, h, t):
    half = h // 2
    rows_per_hi = half * 8

    def fill(hi, carry):
        for hp in range(half):
            for par in range(2):
                v = k_ref[pl.ds(hi * h + 2 * hp + par, 1), :]
                b = pltpu.roll(jnp.broadcast_to(v, (8, 2 * t)), 2 * t - 7, axis=1, stride=1, stride_axis=0)
                lhs_ref[pl.ds(pl.multiple_of(hi * rows_per_hi + hp * 8, 8), 8), par * 2 * t:(par + 1) * 2 * t] = b
        return carry

    lax.fori_loop(0, h, fill, 0, unroll=True)
    res_ref[...] = _dot(lhs_ref[...].astype(BF16), sel_ref[...])

    def shuffle(hi, carry):
        base = pl.multiple_of(hi * rows_per_hi, rows_per_hi)
        r_hi = res_ref[pl.ds(base, rows_per_hi), :]
        rows = [jnp.concatenate([r_hi[hp * 8:(hp + 1) * 8, q * 2 * t:(q + 1) * 2 * t] for hp in range(half)], axis=1)
                for q in range(t // 8)]
        w_ref[pl.ds(base, t), :] = jnp.concatenate(rows, axis=0).astype(BF16)
        return carry

    lax.fori_loop(0, h, shuffle, 0, unroll=True)


def _s5_out_kernel(z_ref, kba_ref, kbb_ref, kca_ref, kcb_ref, kw_ref, sel_ref, dc_ref, dq_ref, rep_ref, til_ref,
                   hfre, hfim, hbre, hbim, dsk_ref, o_ref, km_scr, k_scr, lhs_ref, res_ref, w_ref):
    h = z_ref.shape[0] // GPS
    npair = z_ref.shape[1]
    t = z_ref.shape[-1] // 2
    ht = h * t
    p = dc_ref.shape[1] // 4
    lo = lax.broadcasted_iota(jnp.int32, z_ref.shape[1:], 1) < t
    for gi in range(GPS):
        for hi in range(h):
            km_scr[hi * h:(hi + 1) * h, :] = (kba_ref[gi, hi:hi + 1, :] * kca_ref[gi]
                                              + kbb_ref[gi, hi:hi + 1, :] * kcb_ref[gi])
        k_scr[...] = jnp.dot(km_scr[...], kw_ref[gi], preferred_element_type=F32, precision=HI)
        _expand_toeplitz(k_scr, sel_ref, lhs_ref, res_ref, w_ref, h, t)
        ce = _dot(dc_ref[gi], rep_ref[...])
        qe = _dot(dq_ref[gi], til_ref[...])
        dmat = (ce[:, :ht] * qe[:, :ht] + ce[:, ht:] * qe[:, ht:]).astype(BF16)
        u = _gather_chunks(z_ref, gi, h)
        sl = slice(gi * p, (gi + 1) * p)
        hp = jnp.concatenate([hfre[:, sl], hfim[:, sl], hbre[:, sl], hbim[:, sl]], axis=-1).astype(BF16)
        y = _dot(u.astype(BF16), w_ref[...]) + _dot(hp, dmat)
        gl = _gelu_tanh(y + u * dsk_ref[gi])
        for k in range(0, h, 2):
            te = gl[0:npair, k * t:(k + 2) * t]
            to = gl[npair:2 * npair, k * t:(k + 2) * t]
            o_ref[gi * h + k] = jnp.where(lo, te, pltpu.roll(to, t, axis=1))
            o_ref[gi * h + k + 1] = jnp.where(lo, pltpu.roll(te, t, axis=1), to)


def _s5_out(z3, k_tabs, sel, d_tabs, expanders, hin, dsk):
    gh, npair, t2z = z3.shape
    nchunk, t = 2 * npair, t2z // 2
    kb_a, kb_b, kc_a, kc_b, kw = k_tabs
    g, kp, t2 = kw.shape
    h = gh // g
    hh, ht = h * h, h * t
    assert t2 == 2 * t and (h // 2) * 8 == t and t2 == LANES
    dc, dq = d_tabs
    rep, til = expanders
    p4 = dc.shape[1]
    p = p4 // 4
    hspec = pl.BlockSpec((nchunk, GPS * p), lambda i: (0, i))
    const = lambda a: pl.BlockSpec(a.shape, lambda i: (0, 0))
    return pl.pallas_call(
        _s5_out_kernel,
        grid=(g // GPS,),
        in_specs=[pl.BlockSpec((GPS * h, npair, t2z), lambda i: (i, 0, 0)),
                  ] + [pl.BlockSpec((GPS, h, kp), lambda i: (i, 0, 0))] * 4 + [
                  pl.BlockSpec((GPS, kp, t2), lambda i: (i, 0, 0)),
                  const(sel),
                  pl.BlockSpec((GPS, p4, 2 * h), lambda i: (i, 0, 0)),
                  pl.BlockSpec((GPS, p4, 2 * t), lambda i: (i, 0, 0)),
                  const(rep), const(til)] + [hspec] * 4 +
                 [pl.BlockSpec((GPS, 1, ht), lambda i: (i, 0, 0))],
        out_specs=pl.BlockSpec((GPS * h, npair, t2z), lambda i: (i, 0, 0)),
        out_shape=jax.ShapeDtypeStruct((gh, npair, t2z), F32),
        scratch_shapes=[pltpu.VMEM((hh, kp), F32), pltpu.VMEM((hh, t2), F32), pltpu.VMEM((ht, 4 * t), F32),
                        pltpu.VMEM((ht, ht), F32), pltpu.VMEM((ht, ht), BF16)],
        compiler_params=_cparams("parallel"),
        name="s5_chunk_out",
    )(z3, kb_a, kb_b, kc_a, kc_b, kw, sel, dc, dq, rep, til, *hin, dsk)


def _glu_kernel(gt_ref, wa_ref, wb_ref, ba_ref, bb_ref, o_ref, g_scr, stage_ref):
    @pl.when(pl.program_id(1) == 0)
    def _():
        c, tq, _ = gt_ref.shape
        half = stage_ref.shape[0] // tq
        for c0 in range(0, c, half):
            stage_ref[...] = gt_ref[c0:c0 + half].reshape(half * tq, LANES)
            for q in range(tq):
                g_scr[c0:c0 + half, q * LANES:(q + 1) * LANES] = stage_ref[pl.ds(q, half, stride=tq), :].astype(BF16)

    dn = (((0,), (0,)), ((), ()))
    gt = g_scr[...]
    a = lax.dot_general(gt, wa_ref[...], dn, preferred_element_type=F32) + ba_ref[...]
    b = lax.dot_general(gt, wb_ref[...], dn, preferred_element_type=F32) + bb_ref[...]
    o_ref[...] = (a * jax.nn.sigmoid(b)).astype(o_ref.dtype)


def _glu(g3, wa, wb, ba, bb, tm=1024, tn=512):
    c, nq, _ = g3.shape
    n = nq * LANES
    co = wa.shape[-1]
    tm, tn = min(tm, n), min(tn, co)
    stage_rows = (c // 2) * (tm // LANES)
    return pl.pallas_call(
        _glu_kernel,
        grid=(n // tm, co // tn),
        in_specs=[pl.BlockSpec((c, tm // LANES, LANES), lambda i, j: (0, i, 0)),
                  pl.BlockSpec((c, tn), lambda i, j: (0, j)),
                  pl.BlockSpec((c, tn), lambda i, j: (0, j)),
                  pl.BlockSpec((1, tn), lambda i, j: (0, j)),
                  pl.BlockSpec((1, tn), lambda i, j: (0, j))],
        out_specs=pl.BlockSpec((tm, tn), lambda i, j: (i, j)),
        out_shape=jax.ShapeDtypeStruct((n, co), BF16),
        scratch_shapes=[pltpu.VMEM((c, tm), BF16), pltpu.VMEM((stage_rows, LANES), F32)],
        compiler_params=_cparams("parallel", "arbitrary"),
        name="s5_glu",
    )(g3, wa, wb, ba.reshape(1, co), bb.reshape(1, co))


def _out_proj_kernel(yf_ref, ys_ref, wf_ref, ws_ref, x_ref, g_ref, o_ref):
    acc = _dot(yf_ref[...], wf_ref[...]) + _dot(ys_ref[...], ws_ref[...])
    o_ref[...] = x_ref[...] + g_ref[...] * acc


def _out_proj(yf, ys, w_out, x2, gate, length, tm=1024, tn=1024):
    n, fw = yf.shape
    sw = ys.shape[-1]
    d = w_out.shape[-1]
    tm, tn = min(tm, length), min(tn, d)
    assert fw % sw == 0
    bsz = gate.shape[0]
    return pl.pallas_call(
        _out_proj_kernel,
        grid=(n // tm, d // tn),
        in_specs=[pl.BlockSpec((tm, fw), lambda i, j: (i, 0)),
                  pl.BlockSpec((tm, sw), lambda i, j: (i, 0)),
                  pl.BlockSpec((fw, tn), lambda i, j: (0, j)),
                  pl.BlockSpec((sw, tn), lambda i, j: (fw // sw, j)),
                  pl.BlockSpec((tm, tn), lambda i, j: (i, j)),
                  pl.BlockSpec((None, 1, tn), lambda i, j: ((i * tm) // length, 0, j))],
        out_specs=pl.BlockSpec((tm, tn), lambda i, j: (i, j)),
        out_shape=jax.ShapeDtypeStruct((n, d), F32),
        compiler_params=_cparams("parallel", "arbitrary"),
        name="out_proj_residual",
    )(yf, ys, w_out, w_out, x2, gate.reshape(bsz, 1, d))


def _ffn_hidden_start(j, th, hid):
    return pl.multiple_of(jnp.minimum(j * th, hid - th), math.gcd(th, hid - th))


def _ffn_kernel(hid, h_ref, wg_ref, wu_ref, wd_ref, x_hbm, gate_ref, fg_ref, o_ref, x_sem):
    i, j = pl.program_id(0), pl.program_id(1)
    tm = o_ref.shape[0]
    th = wg_ref.shape[-1]

    def residual_copy():
        return pltpu.make_async_copy(x_hbm.at[pl.ds(pl.multiple_of(i * tm, tm), tm), :], o_ref, x_sem)

    @pl.when(j == 0)
    def _():
        residual_copy().start()

    hh = h_ref[...]
    g = _dot(hh, wg_ref[...].astype(BF16))
    u = _dot(hh, wu_ref[...].astype(BF16))
    unit = _ffn_hidden_start(j, th, hid) + lax.broadcasted_iota(jnp.int32, (1, th), 1)
    a = jnp.where(unit >= j * th, g * jax.nn.sigmoid(g) * u, 0.0).astype(BF16)
    d = o_ref.shape[-1]
    nc = min(FFN_DOWN_CHUNK, d)
    wd = wd_ref[...].astype(BF16)

    @pl.when(j == 0)
    def _():
        residual_copy().wait()

    for c0 in range(0, d, nc):
        o_ref[:, c0:c0 + nc] += gate_ref[:, c0:c0 + nc] * _dot(a, wd[:, c0:c0 + nc])

    @pl.when(j == pl.num_programs(1) - 1)
    def _():
        fg = fg_ref[...]

        def norm_rows(r, carry):
            rows = pl.ds(pl.multiple_of(r * FFN_NORM_ROWS, FFN_NORM_ROWS), FFN_NORM_ROWS)
            x = o_ref[rows, :]
            ms = jnp.mean(x * x, axis=-1, keepdims=True)
            o_ref[rows, :] = x * lax.rsqrt(ms + EPS) * fg
            return carry

        lax.fori_loop(0, tm // FFN_NORM_ROWS, norm_rows, 0)


FFN_DOWN_CHUNK = 512
FFN_NORM_ROWS = 64


FFN_TH = 256
FFN_VMEM_LIMIT = 60 * 1024 * 1024


def _ffn(h, wg, wu, wd, x, gate, final_g, length, tm=1024):
    n, d = h.shape
    hid = wg.shape[-1]
    bsz = gate.shape[0]
    tm, th = min(tm, length), min(FFN_TH, hid)
    assert length % tm == 0 and tm % FFN_NORM_ROWS == 0
    col_map = lambda i, j: (0, _ffn_hidden_start(j, th, hid))
    return pl.pallas_call(
        functools.partial(_ffn_kernel, hid),
        grid=(n // tm, pl.cdiv(hid, th)),
        in_specs=[pl.BlockSpec((tm, d), lambda i, j: (i, 0), pipeline_mode=pl.Buffered(1)),
                  pl.BlockSpec((pl.Element(d), pl.Element(th)), col_map),
                  pl.BlockSpec((pl.Element(d), pl.Element(th)), col_map),
                  pl.BlockSpec((pl.Element(th), pl.Element(d)), lambda i, j: (_ffn_hidden_start(j, th, hid), 0)),
                  pl.BlockSpec(memory_space=pl.ANY),
                  pl.BlockSpec((None, 1, d), lambda i, j: ((i * tm) // length, 0, 0)),
                  pl.BlockSpec((1, d), lambda i, j: (0, 0))],
        out_specs=pl.BlockSpec((tm, d), lambda i, j: (i, 0), pipeline_mode=pl.Buffered(1)),
        out_shape=jax.ShapeDtypeStruct((n, d), F32),
        scratch_shapes=[pltpu.SemaphoreType.DMA(())],
        compiler_params=_cparams("parallel", "arbitrary", vmem=FFN_VMEM_LIMIT),
        name="swiglu_ffn",
    )(h, wg, wu, wd, x, gate.reshape(bsz, 1, d), final_g.reshape(1, d))


def _dft_split(length):
    n2 = 64
    while (length // n2) % PACK_ROWS:
        n2 //= 2
    assert n2 >= 8 and length % n2 == 0
    return n2


def kernel(x, c, ctx, c_ctx, ada_w, ada_b, norm1_g, norm2_g, w_in, w_out, fourier_w, s5_lam_re, s5_lam_im, s5_log_dt, s5_b_re, s5_b_im, s5_c_re, s5_c_im, s5_d, glu_w_a, glu_b_a, glu_w_b, glu_b_b, ffn_w_gate, ffn_w_up, ffn_w_down, final_g):
    bsz, length, d = x.shape
    depth = ada_w.shape[0]
    assert depth == 1, "single-layer block"
    lyr = 0
    heads, hd, _ = fourier_w.shape[1:]
    fw = heads * hd
    _, g, p, hgrp = s5_b_re.shape[1:]
    sw = g * hgrp
    assert w_in.shape[-1] == fw + sw and length % (2 * CHUNK) == 0 and ctx.shape[1] % (2 * CHUNK) == 0
    n = bsz * length

    a8 = jnp.zeros((8, d), F32).at[:bsz].set(c.astype(F32)).at[bsz].set(c_ctx.astype(F32))
    mods = _ada(jnp.concatenate([a8, a8], axis=0), ada_w[lyr], ada_b[lyr]).reshape(8, N_MOD, d)
    sh1, sc1, g1, sh2, sc2, g2 = (mods[:bsz, i] for i in range(N_MOD))
    csh1, csc1 = mods[bsz:bsz + 1, 0], mods[bsz:bsz + 1, 1]

    w_in_b = w_in[lyr].astype(BF16)
    ang = (2.0 * np.pi / hd) * ((np.arange(hd)[:, None] * np.arange(hd)[None, :]) % hd).astype(np.float64)
    cd = jnp.asarray(np.cos(ang) / math.sqrt(hd), F32)
    sd = jnp.asarray(np.sin(ang) / math.sqrt(hd), F32)
    csd = jnp.broadcast_to(jnp.stack([cd, sd])[:, None], (2, heads, hd, hd)).reshape(2 * heads, hd, hd)
    wf2 = jnp.concatenate([fourier_w[lyr], fourier_w[lyr]], axis=0).astype(F32)
    folded = _fold(csd, wf2)
    wcs = jnp.concatenate([folded[:heads], folded[heads:]], axis=-1).astype(BF16)
    k_tabs, e_mat, d_mat, dec = _s5_tables(s5_lam_re[lyr], s5_lam_im[lyr], s5_log_dt[lyr], s5_b_re[lyr],
                                         s5_b_im[lyr], s5_c_re[lyr], s5_c_im[lyr])
    sel = _toeplitz_select(CHUNK)
    dsk = jnp.repeat(s5_d[lyr].astype(F32).reshape(g, 1, hgrp), CHUNK, axis=-1)

    hc = _norm_mod(ctx, norm1_g[lyr], csh1, csc1)
    nctx_tok = bsz * ctx.shape[1]
    zc3 = _proj_t(w_in_b, fw, sw, hc.reshape(nctx_tok, d))

    hm = _norm_mod(x, norm1_g[lyr], sh1, sc1).reshape(n, d)
    n2 = _dft_split(length)
    pc4, ps4 = _four_in(hm, w_in_b, wcs, bsz, length, n2)
    g_tab, cs_tab = _dft_tables(length, n2)
    y_four = _dft(pc4, ps4, g_tab, cs_tab).reshape(n, fw)

    z3 = _proj_t(w_in_b, fw, sw, hm)
    ctx_s, lat_s = _s5_states(zc3, z3, e_mat)
    h_in = _s5_scan(dec, ctx_s, lat_s, bsz)
    g3 = _s5_out(z3, k_tabs, sel, d_mat, _carry_expanders(hgrp, CHUNK), h_in, dsk)
    y_s = _glu(g3, glu_w_a[lyr].astype(BF16), glu_w_b[lyr].astype(BF16), glu_b_a[lyr], glu_b_b[lyr])

    x1 = _out_proj(y_four, y_s, w_out[lyr].astype(BF16), x.reshape(n, d), g1, length)

    hm2 = _norm_mod(x1.reshape(bsz, length, d), norm2_g[lyr], sh2, sc2).reshape(n, d)
    out = _ffn(hm2, ffn_w_gate[lyr], ffn_w_up[lyr], ffn_w_down[lyr], x1, g2, final_g, length)
    return out.reshape(bsz, length, d)
```

```python
import functools
import math

import numpy as np
import jax
import jax.numpy as jnp
from jax import lax
from jax.experimental import pallas as pl
from jax.experimental.pallas import tpu as pltpu

F32 = jnp.float32
BF16 = jnp.bfloat16
EPS = 1e-6
CHUNK = 64
N_MOD = 6
V7X_VMEM_LIMIT = 56 * 1024 * 1024
HI = lax.Precision.HIGHEST


def _cparams(*sem, vmem=V7X_VMEM_LIMIT):
    return pltpu.CompilerParams(dimension_semantics=sem, vmem_limit_bytes=vmem)


def _dot(a, b):
    return jnp.dot(a, b, preferred_element_type=F32)


def _dot_split(a, b):
    a_hi, b_hi = a.astype(BF16), b.astype(BF16)
    a_lo = (a - a_hi.astype(F32)).astype(BF16)
    b_lo = (b - b_hi.astype(F32)).astype(BF16)
    return _dot(a_hi, b_hi) + _dot(a_hi, b_lo) + _dot(a_lo, b_hi)


def _ada_kernel(a_ref, w_ref, b_ref, o_ref):
    a = a_ref[...]
    s = a * jax.nn.sigmoid(a)
    s_hi = s.astype(BF16).astype(F32)
    row = lax.broadcasted_iota(jnp.int32, s.shape, 0)
    lhs = jnp.where(row < 8, s_hi, s - s_hi).astype(BF16)
    w = w_ref[...]
    w_hi = w.astype(BF16)
    w_lo = (w - w_hi.astype(F32)).astype(BF16)
    r = _dot(lhs, w_hi) + _dot(lhs, w_lo)
    o_ref[...] = r[0:8] + r[8:16] + b_ref[...]


def _ada(a16, w, b, tn=1024):
    d, n = w.shape
    tn = min(tn, n)
    return pl.pallas_call(
        _ada_kernel,
        grid=(n // tn,),
        in_specs=[pl.BlockSpec((16, d), lambda j: (0, 0)),
                  pl.BlockSpec((d, tn), lambda j: (0, j)),
                  pl.BlockSpec((1, tn), lambda j: (0, j))],
        out_specs=pl.BlockSpec((8, tn), lambda j: (0, j)),
        out_shape=jax.ShapeDtypeStruct((8, n), F32),
        compiler_params=_cparams("parallel"),
        name="ada_matvec",
    )(a16, w, b.reshape(1, n))


def _norm_mod_kernel(x_ref, g_ref, sh_ref, sc_ref, o_ref):
    x = x_ref[...]
    ms = jnp.mean(x * x, axis=-1, keepdims=True)
    y = x * lax.rsqrt(ms + EPS) * g_ref[...]
    o_ref[...] = (y * (1.0 + sc_ref[...]) + sh_ref[...]).astype(o_ref.dtype)


def _norm_mod(x, g, sh, sc, tm=512):
    bsz, length, d = x.shape
    tm = min(tm, length)
    bm = sh.shape[0]
    mod_map = (lambda b, i: (b, 0, 0)) if bm == bsz else (lambda b, i: (0, 0, 0))
    return pl.pallas_call(
        _norm_mod_kernel,
        grid=(bsz, length // tm),
        in_specs=[pl.BlockSpec((None, tm, d), lambda b, i: (b, i, 0)),
                  pl.BlockSpec((1, d), lambda b, i: (0, 0)),
                  pl.BlockSpec((None, 1, d), mod_map),
                  pl.BlockSpec((None, 1, d), mod_map)],
        out_specs=pl.BlockSpec((None, tm, d), lambda b, i: (b, i, 0)),
        out_shape=jax.ShapeDtypeStruct((bsz, length, d), BF16),
        compiler_params=_cparams("parallel", "parallel"),
        name="norm_mod",
    )(x, g.reshape(1, d), sh.reshape(bm, 1, d), sc.reshape(bm, 1, d))


def _fold_kernel(a_ref, b_ref, o_ref):
    o_ref[...] = jnp.dot(a_ref[...], b_ref[...], preferred_element_type=F32, precision=HI)


def _fold(a, b):
    hh, m, k = a.shape
    n = b.shape[-1]
    return pl.pallas_call(
        _fold_kernel,
        grid=(hh,),
        in_specs=[pl.BlockSpec((None, m, k), lambda h: (h, 0, 0)),
                  pl.BlockSpec((None, k, n), lambda h: (h, 0, 0))],
        out_specs=pl.BlockSpec((None, m, n), lambda h: (h, 0, 0)),
        out_shape=jax.ShapeDtypeStruct((hh, m, n), F32),
        compiler_params=_cparams("parallel"),
        name="weight_fold",
    )(a, b)


LANES = 128
PACK_ROWS = 16
PITCH_PAD = 8


def _four_in_kernel(h_ref, w_ref, wcs_ref, pc_ref, ps_ref, p_scr):
    n2, tn1, hd = pc_ref.shape
    pitch = n2 + PITCH_PAD
    nq = hd // LANES
    z = _dot(h_ref[...], w_ref[...]).astype(BF16)
    p = _dot(z, wcs_ref[...])
    for q in range(2 * nq):
        for i1 in range(tn1):
            p_scr[q, i1 * pitch:i1 * pitch + n2, :] = p[i1 * n2:(i1 + 1) * n2, q * LANES:(q + 1) * LANES]

    def emit(j2, carry):
        for q in range(nq):
            pc_ref[j2, :, q * LANES:(q + 1) * LANES] = p_scr[q, pl.ds(j2, tn1, stride=pitch), :].astype(BF16)
            ps_ref[j2, :, q * LANES:(q + 1) * LANES] = p_scr[nq + q, pl.ds(j2, tn1, stride=pitch), :].astype(BF16)
        return carry

    lax.fori_loop(0, n2, emit, 0, unroll=True)


def _four_in(h, w_f, wcs, bsz, length, n2):
    n, d = h.shape
    heads, hd, _ = wcs.shape
    n1 = length // n2
    tn1 = PACK_ROWS
    tm = tn1 * n2
    tpb = length // tm
    fw = heads * hd
    out = jax.ShapeDtypeStruct((bsz, n2, n1, fw), BF16)
    ospec = pl.BlockSpec((None, n2, tn1, hd), lambda i, j: (i // tpb, 0, i % tpb, j))
    return pl.pallas_call(
        _four_in_kernel,
        grid=(n // tm, heads),
        in_specs=[pl.BlockSpec((tm, d), lambda i, j: (i, 0)),
                  pl.BlockSpec((d, hd), lambda i, j: (0, j)),
                  pl.BlockSpec((None, hd, 2 * hd), lambda i, j: (j, 0, 0))],
        out_specs=[ospec, ospec],
        out_shape=[out, out],
        scratch_shapes=[pltpu.VMEM((2 * hd // LANES, tn1 * (n2 + PITCH_PAD), LANES), F32)],
        compiler_params=_cparams("parallel", "arbitrary"),
        name="fourier_in_proj",
    )(h, w_f, wcs)


def _dft_tables(length, n2):
    n1 = length // n2
    k1 = np.arange(n1)[:, None, None]
    i1 = np.arange(n1)[None, :, None]
    i2 = np.arange(n2)[None, None, :]
    phase = (k1 * (n2 * i1 + i2)) % length
    phi = (2.0 * np.pi / length) * phase.astype(np.float64)
    c = np.cos(phi).transpose(2, 0, 1) / math.sqrt(n1)
    s = np.sin(phi).transpose(2, 0, 1) / math.sqrt(n1)
    g = np.concatenate([np.concatenate([c, -s], axis=2),
                        np.concatenate([-s, -c], axis=2)], axis=1)
    k2 = np.arange(n2)[:, None]
    j2 = np.arange(n2)[None, :]
    th = (2.0 * np.pi / n2) * ((k2 * j2) % n2).astype(np.float64)
    cs = np.concatenate([np.cos(th), np.sin(th)], axis=1) / math.sqrt(n2)
    return jnp.asarray(g, dtype=BF16), jnp.asarray(cs, dtype=BF16)


def _dft_kernel(g_ref, cs_ref, pc_ref, ps_ref, o_ref, t_scr, y_scr):
    n2, n1, _ = pc_ref.shape
    m = 2 * n1
    p1 = m + PITCH_PAD
    p2 = n1 + PITCH_PAD

    def stage1(j2, carry):
        rhs = jnp.concatenate([pc_ref[j2], ps_ref[j2]], axis=0)
        t_scr[pl.ds(pl.multiple_of(j2 * p1, 8), m), :] = _dot(g_ref[j2], rhs)
        return carry

    lax.fori_loop(0, n2, stage1, 0, unroll=True)

    cs = cs_ref[...]

    def stage2(i, carry):
        k1 = 2 * i
        cols = []
        for dk in range(2):
            re = t_scr[pl.ds(k1 + dk, n2, stride=p1), :]
            im = t_scr[pl.ds(n1 + k1 + dk, n2, stride=p1), :]
            cols.append(jnp.concatenate([re, im], axis=0).astype(BF16))
        res = _dot(cs, jnp.concatenate(cols, axis=1))
        for dk in range(2):
            y_scr[pl.ds(k1 + dk, n2, stride=p2), :] = res[:, dk * LANES:(dk + 1) * LANES]
        return carry

    lax.fori_loop(0, n1 // 2, stage2, 0, unroll=True)
    for k2 in range(n2):
        o_ref[k2 * n1:(k2 + 1) * n1, :] = y_scr[k2 * p2:k2 * p2 + n1, :].astype(o_ref.dtype)


def _dft(pc4, ps4, g, cs):
    bsz, n2, n1, w = pc4.shape
    m = 2 * n1
    ispec = pl.BlockSpec((None, n2, n1, LANES), lambda b, j: (b, 0, 0, j))
    return pl.pallas_call(
        _dft_kernel,
        grid=(bsz, w // LANES),
        in_specs=[pl.BlockSpec((n2, m, m), lambda b, j: (0, 0, 0), pipeline_mode=pl.Buffered(1)),
                  pl.BlockSpec((n2, 2 * n2), lambda b, j: (0, 0)),
                  ispec, ispec],
        out_specs=pl.BlockSpec((None, n2 * n1, LANES), lambda b, j: (b, 0, j)),
        out_shape=jax.ShapeDtypeStruct((bsz, n2 * n1, w), BF16),
        scratch_shapes=[pltpu.VMEM((n2 * (m + PITCH_PAD), LANES), F32),
                        pltpu.VMEM((n2 * (n1 + PITCH_PAD), LANES), F32)],
        compiler_params=_cparams("parallel", "parallel"),
        name="position_dft",
    )(g, cs, pc4, ps4)


def _nt_kernel(w_ref, h_ref, o_ref, stage_ref):
    acc = lax.dot_general(w_ref[...], h_ref[...], (((0,), (1,)), ((), ())), preferred_element_type=F32)
    rows, tq, _ = o_ref.shape
    if tq % 8 == 0:
        for q in range(tq):
            stage_ref[pl.ds(q, rows, stride=tq), :] = acc[:, q * LANES:(q + 1) * LANES]
        o_ref[...] = stage_ref[...].reshape(rows, tq, LANES)
    else:
        for q in range(tq):
            o_ref[:, q, :] = acc[:, q * LANES:(q + 1) * LANES]


def _proj_t(w, col0, c, h, tmc=1024, tn=1024):
    d = w.shape[0]
    n = h.shape[0]
    tmc, tn = min(tmc, c), min(tn, n)
    assert col0 % tmc == 0 and c % tmc == 0
    return pl.pallas_call(
        _nt_kernel,
        grid=(n // tn, c // tmc),
        in_specs=[pl.BlockSpec((d, tmc), lambda i, j: (0, col0 // tmc + j)),
                  pl.BlockSpec((tn, d), lambda i, j: (i, 0))],
        out_specs=pl.BlockSpec((tmc, tn // LANES, LANES), lambda i, j: (j, i, 0)),
        out_shape=jax.ShapeDtypeStruct((c, n // LANES, LANES), F32),
        scratch_shapes=[pltpu.VMEM((tmc * (tn // LANES), LANES), F32)],
        compiler_params=_cparams("parallel", "arbitrary"),
        name="s5_in_proj_t",
    )(w, h)


def _s5_tables(lam_re, lam_im, log_dt, b_re, b_im, c_re, c_im):
    t = CHUNK
    _, g, p = lam_re.shape
    h = b_re.shape[-1]
    dt = jnp.exp(log_dt.astype(F32))[..., None]
    lr, li = jnp.minimum(lam_re.astype(F32), -1e-4), lam_im.astype(F32)
    ar, ai = lr * dt, li * dt

    lbm = jnp.exp(ar)
    lbr, lbi = lbm * jnp.cos(ai), lbm * jnp.sin(ai)

    ks = jnp.arange(2 * t, dtype=F32)
    tab_m = jnp.exp(ar[..., None] * ks)
    tab_r, tab_i = tab_m * jnp.cos(ai[..., None] * ks), tab_m * jnp.sin(ai[..., None] * ks)

    def powers(d, first, step, steps_major=False):
        first = int(first)
        if step > 0:
            out = tab_r[d, :, :, first:first + t], tab_i[d, :, :, first:first + t]
        else:
            out = (jnp.flip(tab_r[d, :, :, first - t + 1:first + 1], -1),
                   jnp.flip(tab_i[d, :, :, first - t + 1:first + 1], -1))
        return tuple(jnp.swapaxes(a, 1, 2) for a in out) if steps_major else out

    nr, ni = lbr - 1.0, lbi
    den = lr * lr + li * li
    qr, qi = (nr * lr + ni * li) / den, (ni * lr - nr * li) / den
    br, bi = (jnp.swapaxes(a.astype(F32), -1, -2) for a in (b_re, b_im))
    bbr = qr[:, :, None] * br - qi[:, :, None] * bi
    bbi = qr[:, :, None] * bi + qi[:, :, None] * br
    cr, ci = c_re.astype(F32), c_im.astype(F32)

    kb_a = jnp.concatenate([bbr[0], bbr[0], bbr[1], bbr[1]], axis=-1)
    kb_b = jnp.concatenate([-bbi[0], bbi[0], -bbi[1], bbi[1]], axis=-1)
    kc_a = jnp.concatenate([cr[0], ci[0], cr[1], ci[1]], axis=-1)
    kc_b = jnp.concatenate([ci[0], cr[0], ci[1], cr[1]], axis=-1)
    wfr, wfi = powers(0, 0.0, 1.0)
    wbr, wbi = powers(1, t - 1.0, -1.0)
    fwd = lambda a: jnp.pad(a, ((0, 0), (0, 0), (t - 1, 1)))
    bwd = lambda a: jnp.pad(a, ((0, 0), (0, 0), (0, t)))
    kw = jnp.concatenate([fwd(wfr), fwd(-wfi), bwd(wbr), bwd(-wbi)], axis=1)

    pfr, pfi = powers(0, t - 1.0, -1.0, steps_major=True)
    pbr, pbi = powers(1, 0.0, 1.0, steps_major=True)
    ew_a = jnp.concatenate([pfr, pfr, pbr, pbr], axis=-1)
    ew_b = jnp.concatenate([pfi, pfi, pbi, pbi], axis=-1)
    eb_a = jnp.concatenate([bbr[0], bbi[0], bbr[1], bbi[1]], axis=-1)
    eb_b = jnp.concatenate([-bbi[0], bbr[0], -bbi[1], bbr[1]], axis=-1)

    qfr, qfi = powers(0, 1.0, 1.0)
    qbr, qbi = powers(1, float(t), -1.0)
    crf, cif, crb, cib = (jnp.swapaxes(a, 1, 2) for a in (cr[0], ci[0], cr[1], ci[1]))
    dc = jnp.concatenate([jnp.concatenate([crf, -crf, crb, -crb], axis=1),
                          jnp.concatenate([-cif, -cif, -cib, -cib], axis=1)], axis=-1)
    dq = jnp.concatenate([jnp.concatenate([qfr, qfi, qbr, qbi], axis=1),
                          jnp.concatenate([qfi, qfr, qbi, qbr], axis=1)], axis=-1)

    dec = jnp.stack([tab_r[0, :, :, t], tab_i[0, :, :, t],
                     tab_r[1, :, :, t], tab_i[1, :, :, t]]).reshape(4, g * p)
    return (kb_a, kb_b, kc_a, kc_b, kw), (ew_a, ew_b, eb_a, eb_b), (dc.astype(BF16), dq.astype(BF16)), dec


def _carry_expanders(h, t):
    ht = h * t
    rep = np.zeros((2 * h, 2 * ht), np.float32)
    til = np.zeros((2 * t, 2 * ht), np.float32)
    for half in range(2):
        for ho in range(h):
            for tt in range(t):
                rep[half * h + ho, half * ht + ho * t + tt] = 1.0
                til[half * t + tt, half * ht + ho * t + tt] = 1.0
    return jnp.asarray(rep, dtype=BF16), jnp.asarray(til, dtype=BF16)


def _toeplitz_select(t):
    sel = np.zeros((4 * t, (t // 8) * 2 * t), np.float32)
    for q in range(t // 8):
        for tt in range(t):
            j = tt - 8 * q + t - 8
            sel[j, q * 2 * t + tt] = 1.0
            sel[2 * t + j, q * 2 * t + t + tt] = 1.0
    return jnp.asarray(sel, dtype=BF16)


GPS = 4


def _gather_chunks(z_ref, gi, h):
    t = z_ref.shape[-1] // 2
    lo = lax.broadcasted_iota(jnp.int32, z_ref.shape[1:], 1) < t
    ev, od = [], []
    for k in range(0, h, 2):
        za, zb = z_ref[gi * h + k], z_ref[gi * h + k + 1]
        ev.append(jnp.where(lo, za, pltpu.roll(zb, t, axis=1)))
        od.append(jnp.where(lo, pltpu.roll(za, t, axis=1), zb))
    return jnp.concatenate([jnp.concatenate(ev, axis=1), jnp.concatenate(od, axis=1)], axis=0)


def _s5_state_kernel(zc_ref, z_ref, ewa_ref, ewb_ref, eba_ref, ebb_ref, *refs):
    couts, louts, e_scr = refs[0:4], refs[4:8], refs[8]
    h = z_ref.shape[0] // GPS
    t = ewa_ref.shape[1]
    p = ewa_ref.shape[-1] // 4
    for gi in range(GPS):
        wa, wb = ewa_ref[gi], ewb_ref[gi]
        for k in range(h):
            e_scr[k * t:(k + 1) * t, :] = (wa * eba_ref[gi, k:k + 1, :] + wb * ebb_ref[gi, k:k + 1, :]).astype(BF16)
        for src, outs in ((zc_ref, couts), (z_ref, louts)):
            a2 = _gather_chunks(src, gi, h).astype(BF16)
            s = _dot(a2, e_scr[...])
            for k in range(4):
                outs[k][:, gi * p:(gi + 1) * p] = s[:, k * p:(k + 1) * p]


def _s5_states(zc3, z3, e_tabs):
    gh, npair, t2 = z3.shape
    cpair = zc3.shape[1]
    t = t2 // 2
    g, _, p4 = e_tabs[0].shape
    h, p = gh // g, p4 // 4
    wspec = pl.BlockSpec((GPS, t, p4), lambda i: (i, 0, 0))
    bspec = pl.BlockSpec((GPS, h, p4), lambda i: (i, 0, 0))
    res = pl.pallas_call(
        _s5_state_kernel,
        grid=(g // GPS,),
        in_specs=[pl.BlockSpec((GPS * h, cpair, t2), lambda i: (i, 0, 0)),
                  pl.BlockSpec((GPS * h, npair, t2), lambda i: (i, 0, 0)), wspec, wspec, bspec, bspec],
        out_specs=[pl.BlockSpec((2 * cpair, GPS * p), lambda i: (0, i))] * 4 +
                  [pl.BlockSpec((2 * npair, GPS * p), lambda i: (0, i))] * 4,
        out_shape=[jax.ShapeDtypeStruct((2 * cpair, g * p), F32)] * 4 +
                  [jax.ShapeDtypeStruct((2 * npair, g * p), F32)] * 4,
        scratch_shapes=[pltpu.VMEM((h * t, p4), BF16)],
        compiler_params=_cparams("parallel"),
        name="s5_chunk_states",
    )(zc3, z3, *e_tabs)
    return res[:4], res[4:]


def _s5_scan_kernel(bsz, dec_ref, cfre, cfim, cbre, cbim, sfre, sfim, sbre, sbim,
                    hfre, hfim, hbre, hbim):
    nctx = cfre.shape[0] // bsz
    nlat = sfre.shape[0] // bsz
    width = dec_ref.shape[-1]
    fr, fi = dec_ref[0:1, :], dec_ref[1:2, :]
    br, bi = dec_ref[2:3, :], dec_ref[3:4, :]
    zero = jnp.zeros((1, width), F32)

    def step(ar, ai, hr, hi, sr, si):
        return ar * hr - ai * hi + sr, ar * hi + ai * hr + si

    def row(nchunk, b, j):
        return (j % 2) * (bsz * nchunk // 2) + b * (nchunk // 2) + j // 2

    init = []
    for b in range(bsz):
        hr, hi = zero, zero
        for j in range(nctx):
            r = row(nctx, b, j)
            hr, hi = step(fr, fi, hr, hi, cfre[r:r + 1, :], cfim[r:r + 1, :])
        init += [hr, hi]
        hr, hi = zero, zero
        for j in range(nctx - 1, -1, -1):
            r = row(nctx, b, j)
            hr, hi = step(br, bi, hr, hi, cbre[r:r + 1, :], cbim[r:r + 1, :])
        init += [hr, hi]

    def body(k, carry):
        out = []
        for b in range(bsz):
            hr, hi, gr, gi = carry[4 * b:4 * b + 4]
            rf = row(nlat, b, k)
            hfre[pl.ds(rf, 1), :] = hr
            hfim[pl.ds(rf, 1), :] = hi
            out += list(step(fr, fi, hr, hi, sfre[pl.ds(rf, 1), :], sfim[pl.ds(rf, 1), :]))
            rb = row(nlat, b, nlat - 1 - k)
            hbre[pl.ds(rb, 1), :] = gr
            hbim[pl.ds(rb, 1), :] = gi
            out += list(step(br, bi, gr, gi, sbre[pl.ds(rb, 1), :], sbim[pl.ds(rb, 1), :]))
        return tuple(out)

    lax.fori_loop(0, nlat, body, tuple(init), unroll=2)


def _s5_scan(dec, ctx_s, lat_s, bsz, tw=1024):
    gp = dec.shape[-1]
    rc, rl = ctx_s[0].shape[0], lat_s[0].shape[0]
    assert (rc // bsz) % 2 == 0 and (rl // bsz) % 2 == 0
    tw = min(tw, gp)
    cspec = pl.BlockSpec((rc, tw), lambda i: (0, i))
    lspec = pl.BlockSpec((rl, tw), lambda i: (0, i))
    out = jax.ShapeDtypeStruct((rl, gp), F32)
    return pl.pallas_call(
        functools.partial(_s5_scan_kernel, bsz),
        grid=(gp // tw,),
        in_specs=[pl.BlockSpec((4, tw), lambda i: (0, i))] + [cspec] * 4 + [lspec] * 4,
        out_specs=[lspec] * 4,
        out_shape=[out] * 4,
        compiler_params=_cparams("parallel"),
        name="s5_state_scan",
    )(dec, *ctx_s, *lat_s)


def _gelu_tanh(x):
    return 0.5 * x * (1.0 + jnp.tanh(math.sqrt(2.0 / math.pi) * (x + 0.044715 * (x * x * x))))


def _expand_toeplitz(k_ref, sel_ref, lhs_ref, res_ref, w_ref, h, t):
    half = h // 2
    rows_per_hi = half * 8

    def fill(hi, carry):
        for hp in range(half):
            for par in range(2):
                v = k_ref[pl.ds(hi * h + 2 * hp + par, 1), :]
                b = pltpu.roll(jnp.broadcast_to(v, (8, 2 * t)), 2 * t - 7, axis=1, stride=1, stride_axis=0)
                lhs_ref[pl.ds(pl.multiple_of(hi * rows_per_hi + hp * 8, 8), 8), par * 2 * t:(par + 1) * 2 * t] = b
        return carry

    lax.fori_loop(0, h, fill, 0, unroll=True)
    res_ref[...] = _dot(lhs_ref[...].astype(BF16), sel_ref[...])

    def shuffle(hi, carry):
        base = pl.multiple_of(hi * rows_per_hi, rows_per_hi)
        r_hi = res_ref[pl.ds(base, rows_per_hi), :]
        rows = [jnp.concatenate([r_hi[hp * 8:(hp + 1) * 8, q * 2 * t:(q + 1) * 2 * t] for hp in range(half)], axis=1)
                for q in range(t // 8)]
        w_ref[pl.ds(base, t), :] = jnp.concatenate(rows, axis=0).astype(BF16)
        return carry

    lax.fori_loop(0, h, shuffle, 0, unroll=True)


def _s5_out_kernel(z_ref, kba_ref, kbb_ref, kca_ref, kcb_ref, kw_ref, sel_ref, dc_ref, dq_ref, rep_ref, til_ref,
                   hfre, hfim, hbre, hbim, dsk_ref, o_ref, km_scr, k_scr, lhs_ref, res_ref, w_ref):
    h = z_ref.shape[0] // GPS
    npair = z_ref.shape[1]
    t = z_ref.shape[-1] // 2
    ht = h * t
    p = dc_ref.shape[1] // 4
    lo = lax.broadcasted_iota(jnp.int32, z_ref.shape[1:], 1) < t
    for gi in range(GPS):
        for hi in range(h):
            km_scr[hi * h:(hi + 1) * h, :] = (kba_ref[gi, hi:hi + 1, :] * kca_ref[gi]
                                              + kbb_ref[gi, hi:hi + 1, :] * kcb_ref[gi])
        k_scr[...] = _dot_split(km_scr[...], kw_ref[gi])
        _expand_toeplitz(k_scr, sel_ref, lhs_ref, res_ref, w_ref, h, t)
        ce = _dot(dc_ref[gi], rep_ref[...])
        qe = _dot(dq_ref[gi], til_ref[...])
        dmat = (ce[:, :ht] * qe[:, :ht] + ce[:, ht:] * qe[:, ht:]).astype(BF16)
        u = _gather_chunks(z_ref, gi, h)
        sl = slice(gi * p, (gi + 1) * p)
        hp = jnp.concatenate([hfre[:, sl], hfim[:, sl], hbre[:, sl], hbim[:, sl]], axis=-1).astype(BF16)
        y = _dot(u.astype(BF16), w_ref[...]) + _dot(hp, dmat)
        gl = _gelu_tanh(y + u * dsk_ref[gi])
        for k in range(0, h, 2):
            te = gl[0:npair, k * t:(k + 2) * t]
            to = gl[npair:2 * npair, k * t:(k + 2) * t]
            o_ref[gi * h + k] = jnp.where(lo, te, pltpu.roll(to, t, axis=1))
            o_ref[gi * h + k + 1] = jnp.where(lo, pltpu.roll(te, t, axis=1), to)


def _s5_out(z3, k_tabs, sel, d_tabs, expanders, hin, dsk):
    gh, npair, t2z = z3.shape
    nchunk, t = 2 * npair, t2z // 2
    kb_a, kb_b, kc_a, kc_b, kw = k_tabs
    g, kp, t2 = kw.shape
    h = gh // g
    hh, ht = h * h, h * t
    assert t2 == 2 * t and (h // 2) * 8 == t and t2 == LANES
    dc, dq = d_tabs
    rep, til = expanders
    p4 = dc.shape[1]
    p = p4 // 4
    hspec = pl.BlockSpec((nchunk, GPS * p), lambda i: (0, i))
    const = lambda a: pl.BlockSpec(a.shape, lambda i: (0, 0))
    return pl.pallas_call(
        _s5_out_kernel,
        grid=(g // GPS,),
        in_specs=[pl.BlockSpec((GPS * h, npair, t2z), lambda i: (i, 0, 0)),
                  ] + [pl.BlockSpec((GPS, h, kp), lambda i: (i, 0, 0))] * 4 + [
                  pl.BlockSpec((GPS, kp, t2), lambda i: (i, 0, 0)),
                  const(sel),
                  pl.BlockSpec((GPS, p4, 2 * h), lambda i: (i, 0, 0)),
                  pl.BlockSpec((GPS, p4, 2 * t), lambda i: (i, 0, 0)),
                  const(rep), const(til)] + [hspec] * 4 +
                 [pl.BlockSpec((GPS, 1, ht), lambda i: (i, 0, 0))],
        out_specs=pl.BlockSpec((GPS * h, npair, t2z), lambda i: (i, 0, 0)),
        out_shape=jax.ShapeDtypeStruct((gh, npair, t2z), F32),
        scratch_shapes=[pltpu.VMEM((hh, kp), F32), pltpu.VMEM((hh, t2), F32), pltpu.VMEM((ht, 4 * t), F32),
                        pltpu.VMEM((ht, ht), F32), pltpu.VMEM((ht, ht), BF16)],
        compiler_params=_cparams("parallel"),
        name="s5_chunk_out",
    )(z3, kb_a, kb_b, kc_a, kc_b, kw, sel, dc, dq, rep, til, *hin, dsk)


def _glu_kernel(gt_ref, wa_ref, wb_ref, ba_ref, bb_ref, o_ref, g_scr, stage_ref):
    @pl.when(pl.program_id(1) == 0)
    def _():
        c, tq, _ = gt_ref.shape
        half = stage_ref.shape[0] // tq
        for c0 in range(0, c, half):
            stage_ref[...] = gt_ref[c0:c0 + half].reshape(half * tq, LANES)
            for q in range(tq):
                g_scr[c0:c0 + half, q * LANES:(q + 1) * LANES] = stage_ref[pl.ds(q, half, stride=tq), :].astype(BF16)

    dn = (((0,), (0,)), ((), ()))
    gt = g_scr[...]
    a = lax.dot_general(gt, wa_ref[...].astype(BF16), dn, preferred_element_type=F32) + ba_ref[...]
    b = lax.dot_general(gt, wb_ref[...].astype(BF16), dn, preferred_element_type=F32) + bb_ref[...]
    o_ref[...] = (a * jax.nn.sigmoid(b)).astype(o_ref.dtype)


def _glu(g3, wa, wb, ba, bb, tm=1024, tn=512):
    c, nq, _ = g3.shape
    n = nq * LANES
    co = wa.shape[-1]
    tm, tn = min(tm, n), min(tn, co)
    stage_rows = (c // 2) * (tm // LANES)
    return pl.pallas_call(
        _glu_kernel,
        grid=(n // tm, co // tn),
        in_specs=[pl.BlockSpec((c, tm // LANES, LANES), lambda i, j: (0, i, 0)),
                  pl.BlockSpec((c, tn), lambda i, j: (0, j)),
                  pl.BlockSpec((c, tn), lambda i, j: (0, j)),
                  pl.BlockSpec((1, tn), lambda i, j: (0, j)),
                  pl.BlockSpec((1, tn), lambda i, j: (0, j))],
        out_specs=pl.BlockSpec((tm, tn), lambda i, j: (i, j)),
        out_shape=jax.ShapeDtypeStruct((n, co), BF16),
        scratch_shapes=[pltpu.VMEM((c, tm), BF16), pltpu.VMEM((stage_rows, LANES), F32)],
        compiler_params=_cparams("parallel", "arbitrary"),
        name="s5_glu",
    )(g3, wa, wb, ba.reshape(1, co), bb.reshape(1, co))


def _out_proj_kernel(yf_ref, ys_ref, wf_ref, ws_ref, x_ref, g_ref, o_ref):
    acc = _dot(yf_ref[...], wf_ref[...]) + _dot(ys_ref[...], ws_ref[...])
    o_ref[...] = x_ref[...] + g_ref[...] * acc


def _out_proj(yf, ys, w_out, x2, gate, length, tm=1024, tn=1024):
    n, fw = yf.shape
    sw = ys.shape[-1]
    d = w_out.shape[-1]
    tm, tn = min(tm, length), min(tn, d)
    assert fw % sw == 0
    bsz = gate.shape[0]
    return pl.pallas_call(
        _out_proj_kernel,
        grid=(n // tm, d // tn),
        in_specs=[pl.BlockSpec((tm, fw), lambda i, j: (i, 0)),
                  pl.BlockSpec((tm, sw), lambda i, j: (i, 0)),
                  pl.BlockSpec((fw, tn), lambda i, j: (0, j)),
                  pl.BlockSpec((sw, tn), lambda i, j: (fw // sw, j)),
                  pl.BlockSpec((tm, tn), lambda i, j: (i, j)),
                  pl.BlockSpec((None, 1, tn), lambda i, j: ((i * tm) // length, 0, j))],
        out_specs=pl.BlockSpec((tm, tn), lambda i, j: (i, j)),
        out_shape=jax.ShapeDtypeStruct((n, d), F32),
        compiler_params=_cparams("parallel", "arbitrary"),
        name="out_proj_residual",
    )(yf, ys, w_out, w_out, x2, gate.reshape(bsz, 1, d))


def _ffn_hidden_start(j, th, hid):
    return pl.multiple_of(jnp.minimum(j * th, hid - th), math.gcd(th, hid - th))


def _ffn_kernel(hid, h_ref, wg_ref, wu_ref, wd_ref, x_hbm, gate_ref, fg_ref, o_ref, x_sem):
    i, j = pl.program_id(0), pl.program_id(1)
    tm = o_ref.shape[0]
    th = wg_ref.shape[-1]

    def residual_copy():
        return pltpu.make_async_copy(x_hbm.at[pl.ds(pl.multiple_of(i * tm, tm), tm), :], o_ref, x_sem)

    @pl.when(j == 0)
    def _():
        residual_copy().start()

    hh = h_ref[...]
    g = _dot(hh, wg_ref[...].astype(BF16))
    u = _dot(hh, wu_ref[...].astype(BF16))
    unit = _ffn_hidden_start(j, th, hid) + lax.broadcasted_iota(jnp.int32, (1, th), 1)
    a = jnp.where(unit >= j * th, g * jax.nn.sigmoid(g) * u, 0.0).astype(BF16)
    d = o_ref.shape[-1]
    nc = min(FFN_DOWN_CHUNK, d)
    wd = wd_ref[...].astype(BF16)

    @pl.when(j == 0)
    def _():
        residual_copy().wait()

    for c0 in range(0, d, nc):
        o_ref[:, c0:c0 + nc] += gate_ref[:, c0:c0 + nc] * _dot(a, wd[:, c0:c0 + nc])

    @pl.when(j == pl.num_programs(1) - 1)
    def _():
        fg = fg_ref[...]

        def norm_rows(r, carry):
            rows = pl.ds(pl.multiple_of(r * FFN_NORM_ROWS, FFN_NORM_ROWS), FFN_NORM_ROWS)
            x = o_ref[rows, :]
            ms = jnp.mean(x * x, axis=-1, keepdims=True)
            o_ref[rows, :] = x * lax.rsqrt(ms + EPS) * fg
            return carry

        lax.fori_loop(0, tm // FFN_NORM_ROWS, norm_rows, 0)


FFN_DOWN_CHUNK = 512
FFN_NORM_ROWS = 64


FFN_TH = 256
FFN_VMEM_LIMIT = 60 * 1024 * 1024


def _ffn(h, wg, wu, wd, x, gate, final_g, length, tm=1024):
    n, d = h.shape
    hid = wg.shape[-1]
    bsz = gate.shape[0]
    tm, th = min(tm, length), min(FFN_TH, hid)
    assert length % tm == 0 and tm % FFN_NORM_ROWS == 0
    col_map = lambda i, j: (0, _ffn_hidden_start(j, th, hid))
    return pl.pallas_call(
        functools.partial(_ffn_kernel, hid),
        grid=(n // tm, pl.cdiv(hid, th)),
        in_specs=[pl.BlockSpec((tm, d), lambda i, j: (i, 0), pipeline_mode=pl.Buffered(1)),
                  pl.BlockSpec((pl.Element(d), pl.Element(th)), col_map),
                  pl.BlockSpec((pl.Element(d), pl.Element(th)), col_map),
                  pl.BlockSpec((pl.Element(th), pl.Element(d)), lambda i, j: (_ffn_hidden_start(j, th, hid), 0)),
                  pl.BlockSpec(memory_space=pl.ANY),
                  pl.BlockSpec((None, 1, d), lambda i, j: ((i * tm) // length, 0, 0)),
                  pl.BlockSpec((1, d), lambda i, j: (0, 0))],
        out_specs=pl.BlockSpec((tm, d), lambda i, j: (i, 0), pipeline_mode=pl.Buffered(1)),
        out_shape=jax.ShapeDtypeStruct((n, d), F32),
        scratch_shapes=[pltpu.SemaphoreType.DMA(())],
        compiler_params=_cparams("parallel", "arbitrary", vmem=FFN_VMEM_LIMIT),
        name="swiglu_ffn",
    )(h, wg, wu, wd, x, gate.reshape(bsz, 1, d), final_g.reshape(1, d))


def _dft_split(length):
    n2 = 64
    while (length // n2) % PACK_ROWS:
        n2 //= 2
    assert n2 >= 8 and length % n2 == 0
    return n2


def kernel(x, c, ctx, c_ctx, ada_w, ada_b, norm1_g, norm2_g, w_in, w_out, fourier_w, s5_lam_re, s5_lam_im, s5_log_dt, s5_b_re, s5_b_im, s5_c_re, s5_c_im, s5_d, glu_w_a, glu_b_a, glu_w_b, glu_b_b, ffn_w_gate, ffn_w_up, ffn_w_down, final_g):
    bsz, length, d = x.shape
    depth = ada_w.shape[0]
    assert depth == 1, "single-layer block"
    lyr = 0
    heads, hd, _ = fourier_w.shape[1:]
    fw = heads * hd
    _, g, p, hgrp = s5_b_re.shape[1:]
    sw = g * hgrp
    assert w_in.shape[-1] == fw + sw and length % (2 * CHUNK) == 0 and ctx.shape[1] % (2 * CHUNK) == 0
    n = bsz * length

    a8 = jnp.zeros((8, d), F32).at[:bsz].set(c.astype(F32)).at[bsz].set(c_ctx.astype(F32))
    mods = _ada(jnp.concatenate([a8, a8], axis=0), ada_w[lyr], ada_b[lyr]).reshape(8, N_MOD, d)
    sh1, sc1, g1, sh2, sc2, g2 = (mods[:bsz, i] for i in range(N_MOD))
    csh1, csc1 = mods[bsz:bsz + 1, 0], mods[bsz:bsz + 1, 1]

    w_in_b = w_in[lyr].astype(BF16)
    ang = (2.0 * np.pi / hd) * ((np.arange(hd)[:, None] * np.arange(hd)[None, :]) % hd).astype(np.float64)
    cd = jnp.asarray(np.cos(ang) / math.sqrt(hd), F32)
    sd = jnp.asarray(np.sin(ang) / math.sqrt(hd), F32)
    csd = jnp.broadcast_to(jnp.stack([cd, sd])[:, None], (2, heads, hd, hd)).reshape(2 * heads, hd, hd)
    wf2 = jnp.concatenate([fourier_w[lyr], fourier_w[lyr]], axis=0).astype(F32)
    folded = _fold(csd, wf2)
    wcs = jnp.concatenate([folded[:heads], folded[heads:]], axis=-1).astype(BF16)
    k_tabs, e_mat, d_mat, dec = _s5_tables(s5_lam_re[lyr], s5_lam_im[lyr], s5_log_dt[lyr], s5_b_re[lyr],
                                         s5_b_im[lyr], s5_c_re[lyr], s5_c_im[lyr])
    sel = _toeplitz_select(CHUNK)
    dsk = jnp.repeat(s5_d[lyr].astype(F32).reshape(g, 1, hgrp), CHUNK, axis=-1)

    hc = _norm_mod(ctx, norm1_g[lyr], csh1, csc1)
    nctx_tok = bsz * ctx.shape[1]
    zc3 = _proj_t(w_in_b, fw, sw, hc.reshape(nctx_tok, d))

    hm = _norm_mod(x, norm1_g[lyr], sh1, sc1).reshape(n, d)
    n2 = _dft_split(length)
    pc4, ps4 = _four_in(hm, w_in_b, wcs, bsz, length, n2)
    g_tab, cs_tab = _dft_tables(length, n2)
    y_four = _dft(pc4, ps4, g_tab, cs_tab).reshape(n, fw)

    z3 = _proj_t(w_in_b, fw, sw, hm)
    ctx_s, lat_s = _s5_states(zc3, z3, e_mat)
    h_in = _s5_scan(dec, ctx_s, lat_s, bsz)
    g3 = _s5_out(z3, k_tabs, sel, d_mat, _carry_expanders(hgrp, CHUNK), h_in, dsk)
    y_s = _glu(g3, glu_w_a[lyr], glu_w_b[lyr], glu_b_a[lyr], glu_b_b[lyr])

    x1 = _out_proj(y_four, y_s, w_out[lyr].astype(BF16), x.reshape(n, d), g1, length)

    hm2 = _norm_mod(x1.reshape(bsz, length, d), norm2_g[lyr], sh2, sc2).reshape(n, d)
    out = _ffn(hm2, ffn_w_gate[lyr], ffn_w_up[lyr], ffn_w_down[lyr], x1, g2, final_g, length)
    return out.reshape(bsz, length, d)
```

```python
import functools
import math

import numpy as np
import jax
import jax.numpy as jnp
from jax import lax
from jax.experimental import pallas as pl
from jax.experimental.pallas import tpu as pltpu

F32 = jnp.float32
BF16 = jnp.bfloat16
EPS = 1e-6
CHUNK = 64
N_MOD = 6
V7X_VMEM_LIMIT = 56 * 1024 * 1024
HI = lax.Precision.HIGHEST
LANES = 128
SUBLANES = 8
PACK_ROWS = 16
PITCH_PAD = 8


def _cparams(*sem, vmem=V7X_VMEM_LIMIT):
    return pltpu.CompilerParams(dimension_semantics=sem, vmem_limit_bytes=vmem)


def _dot(a, b):
    return jnp.dot(a, b, preferred_element_type=F32)


def _dot_split(a, b):
    a_hi, b_hi = a.astype(BF16), b.astype(BF16)
    a_lo = (a - a_hi.astype(F32)).astype(BF16)
    b_lo = (b - b_hi.astype(F32)).astype(BF16)
    return _dot(a_hi, b_hi) + _dot(a_hi, b_lo) + _dot(a_lo, b_hi)


def _ada_kernel(a_ref, w_ref, b_ref, o_ref):
    a = a_ref[...]
    s = a * jax.nn.sigmoid(a)
    s_hi = s.astype(BF16).astype(F32)
    row = lax.broadcasted_iota(jnp.int32, s.shape, 0)
    lhs = jnp.where(row < 8, s_hi, s - s_hi).astype(BF16)
    w = w_ref[...]
    w_hi = w.astype(BF16)
    w_lo = (w - w_hi.astype(F32)).astype(BF16)
    r = _dot(lhs, w_hi) + _dot(lhs, w_lo)
    o_ref[...] = r[0:8] + r[8:16] + b_ref[...]


def _ada(a16, w, b, tn=1024):
    d, n = w.shape
    tn = min(tn, n)
    return pl.pallas_call(
        _ada_kernel,
        grid=(n // tn,),
        in_specs=[pl.BlockSpec((16, d), lambda j: (0, 0)),
                  pl.BlockSpec((d, tn), lambda j: (0, j)),
                  pl.BlockSpec((1, tn), lambda j: (0, j))],
        out_specs=pl.BlockSpec((8, tn), lambda j: (0, j)),
        out_shape=jax.ShapeDtypeStruct((8, n), F32),
        compiler_params=_cparams("parallel"),
        name="ada_matvec",
    )(a16, w, b.reshape(1, n))


def _norm_mod_kernel(x_ref, g_ref, sh_ref, sc_ref, o_ref):
    x = x_ref[...]
    ms = jnp.mean(x * x, axis=-1, keepdims=True)
    y = x * lax.rsqrt(ms + EPS) * g_ref[...]
    o_ref[...] = (y * (1.0 + sc_ref[...]) + sh_ref[...]).astype(o_ref.dtype)


def _norm_mod(x, g, sh, sc, tm=512):
    bsz, length, d = x.shape
    tm = min(tm, length)
    bm = sh.shape[0]
    mod_map = (lambda b, i: (b, 0, 0)) if bm == bsz else (lambda b, i: (0, 0, 0))
    return pl.pallas_call(
        _norm_mod_kernel,
        grid=(bsz, length // tm),
        in_specs=[pl.BlockSpec((None, tm, d), lambda b, i: (b, i, 0)),
                  pl.BlockSpec((1, d), lambda b, i: (0, 0)),
                  pl.BlockSpec((None, 1, d), mod_map),
                  pl.BlockSpec((None, 1, d), mod_map)],
        out_specs=pl.BlockSpec((None, tm, d), lambda b, i: (b, i, 0)),
        out_shape=jax.ShapeDtypeStruct((bsz, length, d), BF16),
        compiler_params=_cparams("parallel", "parallel"),
        name="norm_mod",
    )(x, g.reshape(1, d), sh.reshape(bm, 1, d), sc.reshape(bm, 1, d))


def _fold_kernel(a_ref, b_ref, o_ref):
    o_ref[...] = jnp.dot(a_ref[...], b_ref[...], preferred_element_type=F32, precision=HI)


def _fold(a, b):
    hh, m, k = a.shape
    n = b.shape[-1]
    return pl.pallas_call(
        _fold_kernel,
        grid=(hh,),
        in_specs=[pl.BlockSpec((None, m, k), lambda h: (h, 0, 0)),
                  pl.BlockSpec((None, k, n), lambda h: (h, 0, 0))],
        out_specs=pl.BlockSpec((None, m, n), lambda h: (h, 0, 0)),
        out_shape=jax.ShapeDtypeStruct((hh, m, n), F32),
        compiler_params=_cparams("parallel"),
        name="weight_fold",
    )(a, b)


def _four_in_kernel(h_ref, w_ref, wcs_ref, pc_ref, ps_ref, p_scr):
    n2, tn1, hd = pc_ref.shape
    pitch = n2 + PITCH_PAD
    nq = hd // LANES
    z = _dot(h_ref[...], w_ref[...]).astype(BF16)
    p = _dot(z, wcs_ref[...])
    for q in range(2 * nq):
        for i1 in range(tn1):
            p_scr[q, i1 * pitch:i1 * pitch + n2, :] = p[i1 * n2:(i1 + 1) * n2, q * LANES:(q + 1) * LANES]

    def emit(j2, carry):
        for q in range(nq):
            pc_ref[j2, :, q * LANES:(q + 1) * LANES] = p_scr[q, pl.ds(j2, tn1, stride=pitch), :].astype(BF16)
            ps_ref[j2, :, q * LANES:(q + 1) * LANES] = p_scr[nq + q, pl.ds(j2, tn1, stride=pitch), :].astype(BF16)
        return carry

    lax.fori_loop(0, n2, emit, 0, unroll=True)


def _four_in(h, w_f, wcs, bsz, length, n2):
    n, d = h.shape
    heads, hd, _ = wcs.shape
    n1 = length // n2
    tn1 = PACK_ROWS
    tm = tn1 * n2
    tpb = length // tm
    fw = heads * hd
    out = jax.ShapeDtypeStruct((bsz, n2, n1, fw), BF16)
    ospec = pl.BlockSpec((None, n2, tn1, hd), lambda i, j: (i // tpb, 0, i % tpb, j))
    return pl.pallas_call(
        _four_in_kernel,
        grid=(n // tm, heads),
        in_specs=[pl.BlockSpec((tm, d), lambda i, j: (i, 0)),
                  pl.BlockSpec((d, hd), lambda i, j: (0, j)),
                  pl.BlockSpec((None, hd, 2 * hd), lambda i, j: (j, 0, 0))],
        out_specs=[ospec, ospec],
        out_shape=[out, out],
        scratch_shapes=[pltpu.VMEM((2 * hd // LANES, tn1 * (n2 + PITCH_PAD), LANES), F32)],
        compiler_params=_cparams("parallel", "arbitrary"),
        name="fourier_in_proj",
    )(h, w_f, wcs)


def _dft_tables(length, n2):
    n1 = length // n2
    k1 = np.arange(n1)[:, None, None]
    i1 = np.arange(n1)[None, :, None]
    i2 = np.arange(n2)[None, None, :]
    phase = (k1 * (n2 * i1 + i2)) % length
    phi = (2.0 * np.pi / length) * phase.astype(np.float64)
    c = np.cos(phi).transpose(2, 0, 1) / math.sqrt(n1)
    s = np.sin(phi).transpose(2, 0, 1) / math.sqrt(n1)
    g = np.concatenate([np.concatenate([c, -s], axis=2),
                        np.concatenate([-s, -c], axis=2)], axis=1)
    k2 = np.arange(n2)[:, None]
    j2 = np.arange(n2)[None, :]
    th = (2.0 * np.pi / n2) * ((k2 * j2) % n2).astype(np.float64)
    cs = np.concatenate([np.cos(th), np.sin(th)], axis=1) / math.sqrt(n2)
    return jnp.asarray(g, dtype=BF16), jnp.asarray(cs, dtype=BF16)


def _dft_kernel(g_ref, cs_ref, pc_ref, ps_ref, o_ref, t_scr, y_scr):
    n2, n1, _ = pc_ref.shape
    m = 2 * n1
    p1 = m + PITCH_PAD
    p2 = n1 + PITCH_PAD

    def stage1(j2, carry):
        rhs = jnp.concatenate([pc_ref[j2], ps_ref[j2]], axis=0)
        t_scr[pl.ds(pl.multiple_of(j2 * p1, SUBLANES), m), :] = _dot(g_ref[j2], rhs)
        return carry

    lax.fori_loop(0, n2, stage1, 0, unroll=True)

    cs = cs_ref[...]

    def stage2(i, carry):
        k1 = 2 * i
        cols = []
        for dk in range(2):
            re = t_scr[pl.ds(k1 + dk, n2, stride=p1), :]
            im = t_scr[pl.ds(n1 + k1 + dk, n2, stride=p1), :]
            cols.append(jnp.concatenate([re, im], axis=0).astype(BF16))
        res = _dot(cs, jnp.concatenate(cols, axis=1))
        for dk in range(2):
            y_scr[pl.ds(k1 + dk, n2, stride=p2), :] = res[:, dk * LANES:(dk + 1) * LANES]
        return carry

    lax.fori_loop(0, n1 // 2, stage2, 0, unroll=True)
    for k2 in range(n2):
        o_ref[k2 * n1:(k2 + 1) * n1, :] = y_scr[k2 * p2:k2 * p2 + n1, :].astype(o_ref.dtype)


def _dft(pc4, ps4, g, cs):
    bsz, n2, n1, w = pc4.shape
    m = 2 * n1
    ispec = pl.BlockSpec((None, n2, n1, LANES), lambda b, j: (b, 0, 0, j))
    return pl.pallas_call(
        _dft_kernel,
        grid=(bsz, w // LANES),
        in_specs=[pl.BlockSpec((n2, m, m), lambda b, j: (0, 0, 0), pipeline_mode=pl.Buffered(1)),
                  pl.BlockSpec((n2, 2 * n2), lambda b, j: (0, 0)),
                  ispec, ispec],
        out_specs=pl.BlockSpec((None, n2 * n1, LANES), lambda b, j: (b, 0, j)),
        out_shape=jax.ShapeDtypeStruct((bsz, n2 * n1, w), BF16),
        scratch_shapes=[pltpu.VMEM((n2 * (m + PITCH_PAD), LANES), F32),
                        pltpu.VMEM((n2 * (n1 + PITCH_PAD), LANES), F32)],
        compiler_params=_cparams("parallel", "parallel"),
        name="position_dft",
    )(g, cs, pc4, ps4)


def _nt_kernel(w_ref, h_ref, o_ref, stage_ref):
    acc = lax.dot_general(w_ref[...], h_ref[...], (((0,), (1,)), ((), ())), preferred_element_type=F32)
    rows, tq, _ = o_ref.shape
    if tq % SUBLANES == 0:
        for q in range(tq):
            stage_ref[pl.ds(q, rows, stride=tq), :] = acc[:, q * LANES:(q + 1) * LANES]
        o_ref[...] = stage_ref[...].reshape(rows, tq, LANES)
    else:
        for q in range(tq):
            o_ref[:, q, :] = acc[:, q * LANES:(q + 1) * LANES]


def _proj_t(w, col0, c, h, tmc=1024, tn=1024):
    d = w.shape[0]
    n = h.shape[0]
    tmc, tn = min(tmc, c), min(tn, n)
    assert col0 % tmc == 0 and c % tmc == 0
    return pl.pallas_call(
        _nt_kernel,
        grid=(n // tn, c // tmc),
        in_specs=[pl.BlockSpec((d, tmc), lambda i, j: (0, col0 // tmc + j)),
                  pl.BlockSpec((tn, d), lambda i, j: (i, 0))],
        out_specs=pl.BlockSpec((tmc, tn // LANES, LANES), lambda i, j: (j, i, 0)),
        out_shape=jax.ShapeDtypeStruct((c, n // LANES, LANES), F32),
        scratch_shapes=[pltpu.VMEM((tmc * (tn // LANES), LANES), F32)],
        compiler_params=_cparams("parallel", "arbitrary"),
        name="s5_in_proj_t",
    )(w, h)


def _s5_tables(lam_re, lam_im, log_dt, b_re, b_im, c_re, c_im):
    t = CHUNK
    _, g, p = lam_re.shape
    h = b_re.shape[-1]
    dt = jnp.exp(log_dt.astype(F32))[..., None]
    lr, li = jnp.minimum(lam_re.astype(F32), -1e-4), lam_im.astype(F32)
    ar, ai = lr * dt, li * dt

    lbm = jnp.exp(ar)
    lbr, lbi = lbm * jnp.cos(ai), lbm * jnp.sin(ai)

    ks = jnp.arange(2 * t, dtype=F32)
    tab_m = jnp.exp(ar[..., None] * ks)
    tab_r, tab_i = tab_m * jnp.cos(ai[..., None] * ks), tab_m * jnp.sin(ai[..., None] * ks)

    def powers(d, first, step, steps_major=False):
        first = int(first)
        if step > 0:
            out = tab_r[d, :, :, first:first + t], tab_i[d, :, :, first:first + t]
        else:
            out = (jnp.flip(tab_r[d, :, :, first - t + 1:first + 1], -1),
                   jnp.flip(tab_i[d, :, :, first - t + 1:first + 1], -1))
        return tuple(jnp.swapaxes(a, 1, 2) for a in out) if steps_major else out

    nr, ni = lbr - 1.0, lbi
    den = lr * lr + li * li
    qr, qi = (nr * lr + ni * li) / den, (ni * lr - nr * li) / den
    br, bi = (jnp.swapaxes(a.astype(F32), -1, -2) for a in (b_re, b_im))
    bbr = qr[:, :, None] * br - qi[:, :, None] * bi
    bbi = qr[:, :, None] * bi + qi[:, :, None] * br
    cr, ci = c_re.astype(F32), c_im.astype(F32)

    kb_a = jnp.concatenate([bbr[0], bbr[0], bbr[1], bbr[1]], axis=-1)
    kb_b = jnp.concatenate([-bbi[0], bbi[0], -bbi[1], bbi[1]], axis=-1)
    kc_a = jnp.concatenate([cr[0], ci[0], cr[1], ci[1]], axis=-1)
    kc_b = jnp.concatenate([ci[0], cr[0], ci[1], cr[1]], axis=-1)
    wfr, wfi = powers(0, 0.0, 1.0)
    wbr, wbi = powers(1, t - 1.0, -1.0)
    fwd = lambda a: jnp.pad(a, ((0, 0), (0, 0), (t - 1, 1)))
    bwd = lambda a: jnp.pad(a, ((0, 0), (0, 0), (0, t)))
    kw = jnp.concatenate([fwd(wfr), fwd(-wfi), bwd(wbr), bwd(-wbi)], axis=1)

    pfr, pfi = powers(0, t - 1.0, -1.0, steps_major=True)
    pbr, pbi = powers(1, 0.0, 1.0, steps_major=True)
    ew_a = jnp.concatenate([pfr, pfr, pbr, pbr], axis=-1)
    ew_b = jnp.concatenate([pfi, pfi, pbi, pbi], axis=-1)
    eb_a = jnp.concatenate([bbr[0], bbi[0], bbr[1], bbi[1]], axis=-1)
    eb_b = jnp.concatenate([-bbi[0], bbr[0], -bbi[1], bbr[1]], axis=-1)

    qfr, qfi = powers(0, 1.0, 1.0)
    qbr, qbi = powers(1, float(t), -1.0)
    crf, cif, crb, cib = (jnp.swapaxes(a, 1, 2) for a in (cr[0], ci[0], cr[1], ci[1]))
    dc = jnp.concatenate([jnp.concatenate([crf, -crf, crb, -crb], axis=1),
                          jnp.concatenate([-cif, -cif, -cib, -cib], axis=1)], axis=-1)
    dq = jnp.concatenate([jnp.concatenate([qfr, qfi, qbr, qbi], axis=1),
                          jnp.concatenate([qfi, qfr, qbi, qbr], axis=1)], axis=-1)

    dec = jnp.stack([tab_r[0, :, :, t], tab_i[0, :, :, t],
                     tab_r[1, :, :, t], tab_i[1, :, :, t]]).reshape(4, g * p)
    return (kb_a, kb_b, kc_a, kc_b, kw), (ew_a, ew_b, eb_a, eb_b), (dc.astype(BF16), dq.astype(BF16)), dec


def _carry_expanders(h, t):
    ht = h * t
    rep = np.zeros((2 * h, 2 * ht), np.float32)
    til = np.zeros((2 * t, 2 * ht), np.float32)
    for half in range(2):
        for ho in range(h):
            for tt in range(t):
                rep[half * h + ho, half * ht + ho * t + tt] = 1.0
                til[half * t + tt, half * ht + ho * t + tt] = 1.0
    return jnp.asarray(rep, dtype=BF16), jnp.asarray(til, dtype=BF16)


def _toeplitz_select(t):
    sub = SUBLANES
    sel = np.zeros((4 * t, (t // sub) * 2 * t), np.float32)
    for q in range(t // sub):
        for tt in range(t):
            j = tt - sub * q + t - sub
            sel[j, q * 2 * t + tt] = 1.0
            sel[2 * t + j, q * 2 * t + t + tt] = 1.0
    return jnp.asarray(sel, dtype=BF16)


GPS = 4


def _gather_chunks(z_ref, gi, h):
    t = z_ref.shape[-1] // 2
    lo = lax.broadcasted_iota(jnp.int32, z_ref.shape[1:], 1) < t
    ev, od = [], []
    for k in range(0, h, 2):
        za, zb = z_ref[gi * h + k], z_ref[gi * h + k + 1]
        ev.append(jnp.where(lo, za, pltpu.roll(zb, t, axis=1)))
        od.append(jnp.where(lo, pltpu.roll(za, t, axis=1), zb))
    return jnp.concatenate([jnp.concatenate(ev, axis=1), jnp.concatenate(od, axis=1)], axis=0)


def _s5_state_kernel(zc_ref, z_ref, ewa_ref, ewb_ref, eba_ref, ebb_ref, *refs):
    couts, louts, e_scr = refs[0:4], refs[4:8], refs[8]
    h = z_ref.shape[0] // GPS
    t = ewa_ref.shape[1]
    p = ewa_ref.shape[-1] // 4
    for gi in range(GPS):
        wa, wb = ewa_ref[gi], ewb_ref[gi]
        for k in range(h):
            e_scr[k * t:(k + 1) * t, :] = (wa * eba_ref[gi, k:k + 1, :] + wb * ebb_ref[gi, k:k + 1, :]).astype(BF16)
        for src, outs in ((zc_ref, couts), (z_ref, louts)):
            a2 = _gather_chunks(src, gi, h).astype(BF16)
            s = _dot(a2, e_scr[...])
            for k in range(4):
                outs[k][:, gi * p:(gi + 1) * p] = s[:, k * p:(k + 1) * p]


def _s5_states(zc3, z3, e_tabs):
    gh, npair, t2 = z3.shape
    cpair = zc3.shape[1]
    t = t2 // 2
    g, _, p4 = e_tabs[0].shape
    h, p = gh // g, p4 // 4
    wspec = pl.BlockSpec((GPS, t, p4), lambda i: (i, 0, 0))
    bspec = pl.BlockSpec((GPS, h, p4), lambda i: (i, 0, 0))
    res = pl.pallas_call(
        _s5_state_kernel,
        grid=(g // GPS,),
        in_specs=[pl.BlockSpec((GPS * h, cpair, t2), lambda i: (i, 0, 0)),
                  pl.BlockSpec((GPS * h, npair, t2), lambda i: (i, 0, 0)), wspec, wspec, bspec, bspec],
        out_specs=[pl.BlockSpec((2 * cpair, GPS * p), lambda i: (0, i))] * 4 +
                  [pl.BlockSpec((2 * npair, GPS * p), lambda i: (0, i))] * 4,
        out_shape=[jax.ShapeDtypeStruct((2 * cpair, g * p), F32)] * 4 +
                  [jax.ShapeDtypeStruct((2 * npair, g * p), F32)] * 4,
        scratch_shapes=[pltpu.VMEM((h * t, p4), BF16)],
        compiler_params=_cparams("parallel"),
        name="s5_chunk_states",
    )(zc3, z3, *e_tabs)
    return res[:4], res[4:]


def _s5_scan_kernel(bsz, dec_ref, cfre, cfim, cbre, cbim, sfre, sfim, sbre, sbim,
                    hfre, hfim, hbre, hbim):
    nctx = cfre.shape[0] // bsz
    nlat = sfre.shape[0] // bsz
    width = dec_ref.shape[-1]
    fr, fi = dec_ref[0:1, :], dec_ref[1:2, :]
    br, bi = dec_ref[2:3, :], dec_ref[3:4, :]
    zero = jnp.zeros((1, width), F32)

    def step(ar, ai, hr, hi, sr, si):
        return ar * hr - ai * hi + sr, ar * hi + ai * hr + si

    def row(nchunk, b, j):
        return (j % 2) * (bsz * nchunk // 2) + b * (nchunk // 2) + j // 2

    init = []
    for b in range(bsz):
        hr, hi = zero, zero
        for j in range(nctx):
            r = row(nctx, b, j)
            hr, hi = step(fr, fi, hr, hi, cfre[r:r + 1, :], cfim[r:r + 1, :])
        init += [hr, hi]
        hr, hi = zero, zero
        for j in range(nctx - 1, -1, -1):
            r = row(nctx, b, j)
            hr, hi = step(br, bi, hr, hi, cbre[r:r + 1, :], cbim[r:r + 1, :])
        init += [hr, hi]

    def body(k, carry):
        out = []
        for b in range(bsz):
            hr, hi, gr, gi = carry[4 * b:4 * b + 4]
            rf = row(nlat, b, k)
            hfre[pl.ds(rf, 1), :] = hr
            hfim[pl.ds(rf, 1), :] = hi
            out += list(step(fr, fi, hr, hi, sfre[pl.ds(rf, 1), :], sfim[pl.ds(rf, 1), :]))
            rb = row(nlat, b, nlat - 1 - k)
            hbre[pl.ds(rb, 1), :] = gr
            hbim[pl.ds(rb, 1), :] = gi
            out += list(step(br, bi, gr, gi, sbre[pl.ds(rb, 1), :], sbim[pl.ds(rb, 1), :]))
        return tuple(out)

    lax.fori_loop(0, nlat, body, tuple(init), unroll=2)


def _s5_scan(dec, ctx_s, lat_s, bsz, tw=1024):
    gp = dec.shape[-1]
    rc, rl = ctx_s[0].shape[0], lat_s[0].shape[0]
    assert (rc // bsz) % 2 == 0 and (rl // bsz) % 2 == 0
    tw = min(tw, gp)
    cspec = pl.BlockSpec((rc, tw), lambda i: (0, i))
    lspec = pl.BlockSpec((rl, tw), lambda i: (0, i))
    out = jax.ShapeDtypeStruct((rl, gp), F32)
    return pl.pallas_call(
        functools.partial(_s5_scan_kernel, bsz),
        grid=(gp // tw,),
        in_specs=[pl.BlockSpec((4, tw), lambda i: (0, i))] + [cspec] * 4 + [lspec] * 4,
        out_specs=[lspec] * 4,
        out_shape=[out] * 4,
        compiler_params=_cparams("parallel"),
        name="s5_state_scan",
    )(dec, *ctx_s, *lat_s)


def _gelu_tanh(x):
    return 0.5 * x * (1.0 + jnp.tanh(math.sqrt(2.0 / math.pi) * (x + 0.044715 * (x * x * x))))


def _expand_toeplitz(k_ref, sel_ref, lhs_ref, res_ref, w_ref, h, t):
    half = h // 2
    sub = SUBLANES
    rows_per_hi = half * sub

    def fill(hi, carry):
        for hp in range(half):
            for par in range(2):
                v = k_ref[pl.ds(hi * h + 2 * hp + par, 1), :]
                b = pltpu.roll(jnp.broadcast_to(v, (sub, 2 * t)), 2 * t - (sub - 1), axis=1, stride=1, stride_axis=0)
                rows = pl.ds(pl.multiple_of(hi * rows_per_hi + hp * sub, sub), sub)
                lhs_ref[rows, par * 2 * t:(par + 1) * 2 * t] = b
        return carry

    lax.fori_loop(0, h, fill, 0, unroll=True)
    res_ref[...] = _dot(lhs_ref[...].astype(BF16), sel_ref[...])

    def shuffle(hi, carry):
        base = pl.multiple_of(hi * rows_per_hi, rows_per_hi)
        r_hi = res_ref[pl.ds(base, rows_per_hi), :]
        rows = [jnp.concatenate([r_hi[hp * sub:(hp + 1) * sub, q * 2 * t:(q + 1) * 2 * t] for hp in range(half)],
                                axis=1)
                for q in range(t // sub)]
        w_ref[pl.ds(base, t), :] = jnp.concatenate(rows, axis=0).astype(BF16)
        return carry

    lax.fori_loop(0, h, shuffle, 0, unroll=True)


def _s5_out_kernel(z_ref, kba_ref, kbb_ref, kca_ref, kcb_ref, kw_ref, sel_ref, dc_ref, dq_ref, rep_ref, til_ref,
                   hfre, hfim, hbre, hbim, dsk_ref, o_ref, km_scr, k_scr, lhs_ref, res_ref, w_ref):
    h = z_ref.shape[0] // GPS
    npair = z_ref.shape[1]
    t = z_ref.shape[-1] // 2
    ht = h * t
    p = dc_ref.shape[1] // 4
    lo = lax.broadcasted_iota(jnp.int32, z_ref.shape[1:], 1) < t
    for gi in range(GPS):
        for hi in range(h):
            km_scr[hi * h:(hi + 1) * h, :] = (kba_ref[gi, hi:hi + 1, :] * kca_ref[gi]
                                              + kbb_ref[gi, hi:hi + 1, :] * kcb_ref[gi])
        k_scr[...] = _dot_split(km_scr[...], kw_ref[gi])
        _expand_toeplitz(k_scr, sel_ref, lhs_ref, res_ref, w_ref, h, t)
        ce = _dot(dc_ref[gi], rep_ref[...])
        qe = _dot(dq_ref[gi], til_ref[...])
        dmat = (ce[:, :ht] * qe[:, :ht] + ce[:, ht:] * qe[:, ht:]).astype(BF16)
        u = _gather_chunks(z_ref, gi, h)
        sl = slice(gi * p, (gi + 1) * p)
        hp = jnp.concatenate([hfre[:, sl], hfim[:, sl], hbre[:, sl], hbim[:, sl]], axis=-1).astype(BF16)
        y = _dot(u.astype(BF16), w_ref[...]) + _dot(hp, dmat)
        gl = _gelu_tanh(y + u * dsk_ref[gi])
        for k in range(0, h, 2):
            te = gl[0:npair, k * t:(k + 2) * t]
            to = gl[npair:2 * npair, k * t:(k + 2) * t]
            o_ref[gi * h + k] = jnp.where(lo, te, pltpu.roll(to, t, axis=1))
            o_ref[gi * h + k + 1] = jnp.where(lo, pltpu.roll(te, t, axis=1), to)


def _s5_out(z3, k_tabs, sel, d_tabs, expanders, hin, dsk):
    gh, npair, t2z = z3.shape
    nchunk, t = 2 * npair, t2z // 2
    kb_a, kb_b, kc_a, kc_b, kw = k_tabs
    g, kp, t2 = kw.shape
    h = gh // g
    hh, ht = h * h, h * t
    assert t2 == 2 * t and (h // 2) * SUBLANES == t and t2 == LANES
    dc, dq = d_tabs
    rep, til = expanders
    p4 = dc.shape[1]
    p = p4 // 4
    hspec = pl.BlockSpec((nchunk, GPS * p), lambda i: (0, i))
    const = lambda a: pl.BlockSpec(a.shape, lambda i: (0, 0))
    return pl.pallas_call(
        _s5_out_kernel,
        grid=(g // GPS,),
        in_specs=[pl.BlockSpec((GPS * h, npair, t2z), lambda i: (i, 0, 0)),
                  ] + [pl.BlockSpec((GPS, h, kp), lambda i: (i, 0, 0))] * 4 + [
                  pl.BlockSpec((GPS, kp, t2), lambda i: (i, 0, 0)),
                  const(sel),
                  pl.BlockSpec((GPS, p4, 2 * h), lambda i: (i, 0, 0)),
                  pl.BlockSpec((GPS, p4, 2 * t), lambda i: (i, 0, 0)),
                  const(rep), const(til)] + [hspec] * 4 +
                 [pl.BlockSpec((GPS, 1, ht), lambda i: (i, 0, 0))],
        out_specs=pl.BlockSpec((GPS * h, npair, t2z), lambda i: (i, 0, 0)),
        out_shape=jax.ShapeDtypeStruct((gh, npair, t2z), F32),
        scratch_shapes=[pltpu.VMEM((hh, kp), F32), pltpu.VMEM((hh, t2), F32), pltpu.VMEM((ht, 4 * t), F32),
                        pltpu.VMEM((ht, ht), F32), pltpu.VMEM((ht, ht), BF16)],
        compiler_params=_cparams("parallel"),
        name="s5_chunk_out",
    )(z3, kb_a, kb_b, kc_a, kc_b, kw, sel, dc, dq, rep, til, *hin, dsk)


def _glu_kernel(gt_ref, wa_ref, wb_ref, ba_ref, bb_ref, o_ref, g_scr, stage_ref):
    @pl.when(pl.program_id(1) == 0)
    def _():
        c, tq, _ = gt_ref.shape
        half = stage_ref.shape[0] // tq
        for c0 in range(0, c, half):
            stage_ref[...] = gt_ref[c0:c0 + half].reshape(half * tq, LANES)
            for q in range(tq):
                g_scr[c0:c0 + half, q * LANES:(q + 1) * LANES] = stage_ref[pl.ds(q, half, stride=tq), :].astype(BF16)

    dn = (((0,), (0,)), ((), ()))
    gt = g_scr[...]
    a = lax.dot_general(gt, wa_ref[...].astype(BF16), dn, preferred_element_type=F32) + ba_ref[...]
    b = lax.dot_general(gt, wb_ref[...].astype(BF16), dn, preferred_element_type=F32) + bb_ref[...]
    o_ref[...] = (a * jax.nn.sigmoid(b)).astype(o_ref.dtype)


def _glu(g3, wa, wb, ba, bb, tm=1024, tn=512):
    c, nq, _ = g3.shape
    n = nq * LANES
    co = wa.shape[-1]
    tm, tn = min(tm, n), min(tn, co)
    stage_rows = (c // 2) * (tm // LANES)
    return pl.pallas_call(
        _glu_kernel,
        grid=(n // tm, co // tn),
        in_specs=[pl.BlockSpec((c, tm // LANES, LANES), lambda i, j: (0, i, 0)),
                  pl.BlockSpec((c, tn), lambda i, j: (0, j)),
                  pl.BlockSpec((c, tn), lambda i, j: (0, j)),
                  pl.BlockSpec((1, tn), lambda i, j: (0, j)),
                  pl.BlockSpec((1, tn), lambda i, j: (0, j))],
        out_specs=pl.BlockSpec((tm, tn), lambda i, j: (i, j)),
        out_shape=jax.ShapeDtypeStruct((n, co), BF16),
        scratch_shapes=[pltpu.VMEM((c, tm), BF16), pltpu.VMEM((stage_rows, LANES), F32)],
        compiler_params=_cparams("parallel", "arbitrary"),
        name="s5_glu",
    )(g3, wa, wb, ba.reshape(1, co), bb.reshape(1, co))


def _out_proj_kernel(yf_ref, ys_ref, wf_ref, ws_ref, x_ref, g_ref, o_ref):
    acc = _dot(yf_ref[...], wf_ref[...]) + _dot(ys_ref[...], ws_ref[...])
    o_ref[...] = x_ref[...] + g_ref[...] * acc


def _out_proj(yf, ys, w_out, x2, gate, length, tm=1024, tn=1024):
    n, fw = yf.shape
    sw = ys.shape[-1]
    d = w_out.shape[-1]
    tm, tn = min(tm, length), min(tn, d)
    assert fw % sw == 0
    bsz = gate.shape[0]
    return pl.pallas_call(
        _out_proj_kernel,
        grid=(n // tm, d // tn),
        in_specs=[pl.BlockSpec((tm, fw), lambda i, j: (i, 0)),
                  pl.BlockSpec((tm, sw), lambda i, j: (i, 0)),
                  pl.BlockSpec((fw, tn), lambda i, j: (0, j)),
                  pl.BlockSpec((sw, tn), lambda i, j: (fw // sw, j)),
                  pl.BlockSpec((tm, tn), lambda i, j: (i, j)),
                  pl.BlockSpec((None, 1, tn), lambda i, j: ((i * tm) // length, 0, j))],
        out_specs=pl.BlockSpec((tm, tn), lambda i, j: (i, j)),
        out_shape=jax.ShapeDtypeStruct((n, d), F32),
        compiler_params=_cparams("parallel", "arbitrary"),
        name="out_proj_residual",
    )(yf, ys, w_out, w_out, x2, gate.reshape(bsz, 1, d))


def _ffn_hidden_start(j, th, hid):
    return pl.multiple_of(jnp.minimum(j * th, hid - th), math.gcd(th, hid - th))


def _ffn_kernel(hid, h_ref, wg_ref, wu_ref, wd_ref, x_hbm, gate_ref, fg_ref, o_ref, x_sem):
    i, j = pl.program_id(0), pl.program_id(1)
    tm = o_ref.shape[0]
    th = wg_ref.shape[-1]

    def residual_copy():
        return pltpu.make_async_copy(x_hbm.at[pl.ds(pl.multiple_of(i * tm, tm), tm), :], o_ref, x_sem)

    @pl.when(j == 0)
    def _():
        residual_copy().start()

    hh = h_ref[...]
    g = _dot(hh, wg_ref[...].astype(BF16))
    u = _dot(hh, wu_ref[...].astype(BF16))
    unit = _ffn_hidden_start(j, th, hid) + lax.broadcasted_iota(jnp.int32, (1, th), 1)
    a = jnp.where(unit >= j * th, g * jax.nn.sigmoid(g) * u, 0.0).astype(BF16)
    d = o_ref.shape[-1]
    nc = min(FFN_DOWN_CHUNK, d)
    wd = wd_ref[...].astype(BF16)

    @pl.when(j == 0)
    def _():
        residual_copy().wait()

    for c0 in range(0, d, nc):
        o_ref[:, c0:c0 + nc] += gate_ref[:, c0:c0 + nc] * _dot(a, wd[:, c0:c0 + nc])

    @pl.when(j == pl.num_programs(1) - 1)
    def _():
        fg = fg_ref[...]

        def norm_rows(r, carry):
            rows = pl.ds(pl.multiple_of(r * FFN_NORM_ROWS, FFN_NORM_ROWS), FFN_NORM_ROWS)
            x = o_ref[rows, :]
            ms = jnp.mean(x * x, axis=-1, keepdims=True)
            o_ref[rows, :] = x * lax.rsqrt(ms + EPS) * fg
            return carry

        lax.fori_loop(0, tm // FFN_NORM_ROWS, norm_rows, 0)


FFN_DOWN_CHUNK = 512
FFN_NORM_ROWS = 64


FFN_TH = 256
FFN_VMEM_LIMIT = 60 * 1024 * 1024


def _ffn(h, wg, wu, wd, x, gate, final_g, length, tm=1024):
    n, d = h.shape
    hid = wg.shape[-1]
    bsz = gate.shape[0]
    tm, th = min(tm, length), min(FFN_TH, hid)
    assert length % tm == 0 and tm % FFN_NORM_ROWS == 0
    col_map = lambda i, j: (0, _ffn_hidden_start(j, th, hid))
    return pl.pallas_call(
        functools.partial(_ffn_kernel, hid),
        grid=(n // tm, pl.cdiv(hid, th)),
        in_specs=[pl.BlockSpec((tm, d), lambda i, j: (i, 0), pipeline_mode=pl.Buffered(1)),
                  pl.BlockSpec((pl.Element(d), pl.Element(th)), col_map),
                  pl.BlockSpec((pl.Element(d), pl.Element(th)), col_map),
                  pl.BlockSpec((pl.Element(th), pl.Element(d)), lambda i, j: (_ffn_hidden_start(j, th, hid), 0)),
                  pl.BlockSpec(memory_space=pl.ANY),
                  pl.BlockSpec((None, 1, d), lambda i, j: ((i * tm) // length, 0, 0)),
                  pl.BlockSpec((1, d), lambda i, j: (0, 0))],
        out_specs=pl.BlockSpec((tm, d), lambda i, j: (i, 0), pipeline_mode=pl.Buffered(1)),
        out_shape=jax.ShapeDtypeStruct((n, d), F32),
        scratch_shapes=[pltpu.SemaphoreType.DMA(())],
        compiler_params=_cparams("parallel", "arbitrary", vmem=FFN_VMEM_LIMIT),
        name="swiglu_ffn",
    )(h, wg, wu, wd, x, gate.reshape(bsz, 1, d), final_g.reshape(1, d))


def _dft_split(length):
    n2 = 64
    while (length // n2) % PACK_ROWS:
        n2 //= 2
    assert n2 >= SUBLANES and length % n2 == 0
    return n2


def kernel(x, c, ctx, c_ctx, ada_w, ada_b, norm1_g, norm2_g, w_in, w_out, fourier_w, s5_lam_re, s5_lam_im, s5_log_dt, s5_b_re, s5_b_im, s5_c_re, s5_c_im, s5_d, glu_w_a, glu_b_a, glu_w_b, glu_b_b, ffn_w_gate, ffn_w_up, ffn_w_down, final_g):
    bsz, length, d = x.shape
    depth = ada_w.shape[0]
    assert depth == 1, "single-layer block"
    lyr = 0
    heads, hd, _ = fourier_w.shape[1:]
    fw = heads * hd
    _, g, p, hgrp = s5_b_re.shape[1:]
    sw = g * hgrp
    assert w_in.shape[-1] == fw + sw and length % (2 * CHUNK) == 0 and ctx.shape[1] % (2 * CHUNK) == 0
    n = bsz * length

    a8 = jnp.zeros((8, d), F32).at[:bsz].set(c.astype(F32)).at[bsz].set(c_ctx.astype(F32))
    mods = _ada(jnp.concatenate([a8, a8], axis=0), ada_w[lyr], ada_b[lyr]).reshape(8, N_MOD, d)
    sh1, sc1, g1, sh2, sc2, g2 = (mods[:bsz, i] for i in range(N_MOD))
    csh1, csc1 = mods[bsz:bsz + 1, 0], mods[bsz:bsz + 1, 1]

    w_in_b = w_in[lyr].astype(BF16)
    ang = (2.0 * np.pi / hd) * ((np.arange(hd)[:, None] * np.arange(hd)[None, :]) % hd).astype(np.float64)
    cd = jnp.asarray(np.cos(ang) / math.sqrt(hd), F32)
    sd = jnp.asarray(np.sin(ang) / math.sqrt(hd), F32)
    csd = jnp.broadcast_to(jnp.stack([cd, sd])[:, None], (2, heads, hd, hd)).reshape(2 * heads, hd, hd)
    wf2 = jnp.concatenate([fourier_w[lyr], fourier_w[lyr]], axis=0).astype(F32)
    folded = _fold(csd, wf2)
    wcs = jnp.concatenate([folded[:heads], folded[heads:]], axis=-1).astype(BF16)
    k_tabs, e_mat, d_mat, dec = _s5_tables(s5_lam_re[lyr], s5_lam_im[lyr], s5_log_dt[lyr], s5_b_re[lyr],
                                         s5_b_im[lyr], s5_c_re[lyr], s5_c_im[lyr])
    sel = _toeplitz_select(CHUNK)
    dsk = jnp.repeat(s5_d[lyr].astype(F32).reshape(g, 1, hgrp), CHUNK, axis=-1)

    hc = _norm_mod(ctx, norm1_g[lyr], csh1, csc1)
    nctx_tok = bsz * ctx.shape[1]
    zc3 = _proj_t(w_in_b, fw, sw, hc.reshape(nctx_tok, d))

    hm = _norm_mod(x, norm1_g[lyr], sh1, sc1).reshape(n, d)
    n2 = _dft_split(length)
    pc4, ps4 = _four_in(hm, w_in_b, wcs, bsz, length, n2)
    g_tab, cs_tab = _dft_tables(length, n2)
    y_four = _dft(pc4, ps4, g_tab, cs_tab).reshape(n, fw)

    z3 = _proj_t(w_in_b, fw, sw, hm)
    ctx_s, lat_s = _s5_states(zc3, z3, e_mat)
    h_in = _s5_scan(dec, ctx_s, lat_s, bsz)
    g3 = _s5_out(z3, k_tabs, sel, d_mat, _carry_expanders(hgrp, CHUNK), h_in, dsk)
    y_s = _glu(g3, glu_w_a[lyr], glu_w_b[lyr], glu_b_a[lyr], glu_b_b[lyr])

    x1 = _out_proj(y_four, y_s, w_out[lyr].astype(BF16), x.reshape(n, d), g1, length)

    hm2 = _norm_mod(x1.reshape(bsz, length, d), norm2_g[lyr], sh2, sc2).reshape(n, d)
    out = _ffn(hm2, ffn_w_gate[lyr], ffn_w_up[lyr], ffn_w_down[lyr], x1, g2, final_g, length)
    return out.reshape(bsz, length, d)
```

```python
import functools
import math

import numpy as np
import jax
import jax.numpy as jnp
from jax import lax
from jax.experimental import pallas as pl
from jax.experimental.pallas import tpu as pltpu

F32 = jnp.float32
BF16 = jnp.bfloat16
EPS = 1e-6
CHUNK = 64
N_MOD = 6
V7X_VMEM_LIMIT = 56 * 1024 * 1024
HI = lax.Precision.HIGHEST
LANES = 128
SUBLANES = 8
PACK_ROWS = 16
PITCH_PAD = 8


def _cparams(*sem, vmem=V7X_VMEM_LIMIT):
    return pltpu.CompilerParams(dimension_semantics=sem, vmem_limit_bytes=vmem)


def _dot(a, b):
    return jnp.dot(a, b, preferred_element_type=F32)


def _dot_split(a, b):
    a_hi, b_hi = a.astype(BF16), b.astype(BF16)
    a_lo = (a - a_hi.astype(F32)).astype(BF16)
    b_lo = (b - b_hi.astype(F32)).astype(BF16)
    return _dot(a_hi, b_hi) + _dot(a_hi, b_lo) + _dot(a_lo, b_hi)


def _ada_kernel(a_ref, w_ref, b_ref, o_ref):
    a = a_ref[...]
    s = a * jax.nn.sigmoid(a)
    s_hi = s.astype(BF16).astype(F32)
    row = lax.broadcasted_iota(jnp.int32, s.shape, 0)
    lhs = jnp.where(row < 8, s_hi, s - s_hi).astype(BF16)
    w = w_ref[...]
    w_hi = w.astype(BF16)
    w_lo = (w - w_hi.astype(F32)).astype(BF16)
    r = _dot(lhs, w_hi) + _dot(lhs, w_lo)
    o_ref[...] = r[0:8] + r[8:16] + b_ref[...]


def _ada(a16, w, b, tn=1024):
    d, n = w.shape
    tn = min(tn, n)
    return pl.pallas_call(
        _ada_kernel,
        grid=(n // tn,),
        in_specs=[pl.BlockSpec((16, d), lambda j: (0, 0)),
                  pl.BlockSpec((d, tn), lambda j: (0, j)),
                  pl.BlockSpec((1, tn), lambda j: (0, j))],
        out_specs=pl.BlockSpec((8, tn), lambda j: (0, j)),
        out_shape=jax.ShapeDtypeStruct((8, n), F32),
        compiler_params=_cparams("parallel"),
        name="ada_matvec",
    )(a16, w, b.reshape(1, n))


def _norm_mod_kernel(x_ref, g_ref, sh_ref, sc_ref, o_ref):
    x = x_ref[...]
    ms = jnp.mean(x * x, axis=-1, keepdims=True)
    y = x * lax.rsqrt(ms + EPS) * g_ref[...]
    o_ref[...] = (y * (1.0 + sc_ref[...]) + sh_ref[...]).astype(o_ref.dtype)


def _norm_mod(x, g, sh, sc, tm=512):
    bsz, length, d = x.shape
    tm = min(tm, length)
    bm = sh.shape[0]
    mod_map = (lambda b, i: (b, 0, 0)) if bm == bsz else (lambda b, i: (0, 0, 0))
    return pl.pallas_call(
        _norm_mod_kernel,
        grid=(bsz, length // tm),
        in_specs=[pl.BlockSpec((None, tm, d), lambda b, i: (b, i, 0)),
                  pl.BlockSpec((1, d), lambda b, i: (0, 0)),
                  pl.BlockSpec((None, 1, d), mod_map),
                  pl.BlockSpec((None, 1, d), mod_map)],
        out_specs=pl.BlockSpec((None, tm, d), lambda b, i: (b, i, 0)),
        out_shape=jax.ShapeDtypeStruct((bsz, length, d), BF16),
        compiler_params=_cparams("parallel", "parallel"),
        name="norm_mod",
    )(x, g.reshape(1, d), sh.reshape(bm, 1, d), sc.reshape(bm, 1, d))


def _fold_kernel(a_ref, b_ref, o_ref):
    o_ref[...] = jnp.dot(a_ref[...], b_ref[...], preferred_element_type=F32, precision=HI)


def _fold(a, b):
    hh, m, k = a.shape
    n = b.shape[-1]
    return pl.pallas_call(
        _fold_kernel,
        grid=(hh,),
        in_specs=[pl.BlockSpec((None, m, k), lambda h: (h, 0, 0)),
                  pl.BlockSpec((None, k, n), lambda h: (h, 0, 0))],
        out_specs=pl.BlockSpec((None, m, n), lambda h: (h, 0, 0)),
        out_shape=jax.ShapeDtypeStruct((hh, m, n), F32),
        compiler_params=_cparams("parallel"),
        name="weight_fold",
    )(a, b)


def _four_in_kernel(h_ref, w_ref, wcs_ref, pc_ref, ps_ref, z_scr, zp_scr):
    n2, tn1, hd = pc_ref.shape
    pitch = n2 + PITCH_PAD
    nq = hd // LANES
    z = _dot(h_ref[...], w_ref[...])
    for q in range(nq):
        for i1 in range(tn1):
            z_scr[q, i1 * pitch:i1 * pitch + n2, :] = z[i1 * n2:(i1 + 1) * n2, q * LANES:(q + 1) * LANES]
    for j2 in range(n2):
        for q in range(nq):
            zp_scr[j2 * tn1:(j2 + 1) * tn1, q * LANES:(q + 1) * LANES] = (
                z_scr[q, pl.ds(j2, tn1, stride=pitch), :].astype(BF16))
    p = _dot(zp_scr[...], wcs_ref[...])
    p3 = p.reshape(n2, tn1, 2 * hd)
    pc_ref[...] = p3[:, :, :hd].astype(BF16)
    ps_ref[...] = p3[:, :, hd:].astype(BF16)


def _four_in(h, w_f, wcs, bsz, length, n2):
    n, d = h.shape
    heads, hd, _ = wcs.shape
    n1 = length // n2
    tn1 = PACK_ROWS
    tm = tn1 * n2
    tpb = length // tm
    fw = heads * hd
    out = jax.ShapeDtypeStruct((bsz, n2, n1, fw), BF16)
    ospec = pl.BlockSpec((None, n2, tn1, hd), lambda i, j: (i // tpb, 0, i % tpb, j))
    return pl.pallas_call(
        _four_in_kernel,
        grid=(n // tm, heads),
        in_specs=[pl.BlockSpec((tm, d), lambda i, j: (i, 0)),
                  pl.BlockSpec((d, hd), lambda i, j: (0, j)),
                  pl.BlockSpec((None, hd, 2 * hd), lambda i, j: (j, 0, 0))],
        out_specs=[ospec, ospec],
        out_shape=[out, out],
        scratch_shapes=[pltpu.VMEM((hd // LANES, tn1 * (n2 + PITCH_PAD), LANES), F32), pltpu.VMEM((tm, hd), BF16)],
        compiler_params=_cparams("parallel", "arbitrary"),
        name="fourier_in_proj",
    )(h, w_f, wcs)


def _dft_tables(length, n2):
    n1 = length // n2
    k1 = np.arange(n1)[:, None, None]
    i1 = np.arange(n1)[None, :, None]
    i2 = np.arange(n2)[None, None, :]
    phase = (k1 * (n2 * i1 + i2)) % length
    phi = (2.0 * np.pi / length) * phase.astype(np.float64)
    c = np.cos(phi).transpose(2, 0, 1) / math.sqrt(n1)
    s = np.sin(phi).transpose(2, 0, 1) / math.sqrt(n1)
    g = np.concatenate([np.concatenate([c, -s], axis=2),
                        np.concatenate([-s, -c], axis=2)], axis=1)
    k2 = np.arange(n2)[:, None]
    j2 = np.arange(n2)[None, :]
    th = (2.0 * np.pi / n2) * ((k2 * j2) % n2).astype(np.float64)
    cs = np.concatenate([np.cos(th), np.sin(th)], axis=1) / math.sqrt(n2)
    return jnp.asarray(g, dtype=BF16), jnp.asarray(cs, dtype=BF16)


def _dft_kernel(g_ref, cs_ref, pc_ref, ps_ref, o_ref, t_scr, y_scr):
    n2, n1, _ = pc_ref.shape
    m = 2 * n1
    p1 = m + PITCH_PAD
    p2 = n1 + PITCH_PAD

    def stage1(j2, carry):
        rhs = jnp.concatenate([pc_ref[j2], ps_ref[j2]], axis=0)
        t_scr[pl.ds(pl.multiple_of(j2 * p1, SUBLANES), m), :] = _dot(g_ref[j2], rhs)
        return carry

    lax.fori_loop(0, n2, stage1, 0, unroll=True)

    cs = cs_ref[...]

    def stage2(i, carry):
        k1 = 2 * i
        cols = []
        for dk in range(2):
            re = t_scr[pl.ds(k1 + dk, n2, stride=p1), :]
            im = t_scr[pl.ds(n1 + k1 + dk, n2, stride=p1), :]
            cols.append(jnp.concatenate([re, im], axis=0).astype(BF16))
        res = _dot(cs, jnp.concatenate(cols, axis=1))
        for dk in range(2):
            y_scr[pl.ds(k1 + dk, n2, stride=p2), :] = res[:, dk * LANES:(dk + 1) * LANES]
        return carry

    lax.fori_loop(0, n1 // 2, stage2, 0, unroll=True)
    for k2 in range(n2):
        o_ref[k2 * n1:(k2 + 1) * n1, :] = y_scr[k2 * p2:k2 * p2 + n1, :].astype(o_ref.dtype)


def _dft(pc4, ps4, g, cs):
    bsz, n2, n1, w = pc4.shape
    m = 2 * n1
    ispec = pl.BlockSpec((None, n2, n1, LANES), lambda b, j: (b, 0, 0, j))
    return pl.pallas_call(
        _dft_kernel,
        grid=(bsz, w // LANES),
        in_specs=[pl.BlockSpec((n2, m, m), lambda b, j: (0, 0, 0), pipeline_mode=pl.Buffered(1)),
                  pl.BlockSpec((n2, 2 * n2), lambda b, j: (0, 0)),
                  ispec, ispec],
        out_specs=pl.BlockSpec((None, n2 * n1, LANES), lambda b, j: (b, 0, j)),
        out_shape=jax.ShapeDtypeStruct((bsz, n2 * n1, w), BF16),
        scratch_shapes=[pltpu.VMEM((n2 * (m + PITCH_PAD), LANES), F32),
                        pltpu.VMEM((n2 * (n1 + PITCH_PAD), LANES), F32)],
        compiler_params=_cparams("parallel", "parallel"),
        name="position_dft",
    )(g, cs, pc4, ps4)


def _nt_kernel(w_ref, h_ref, o_ref, stage_ref):
    acc = lax.dot_general(w_ref[...], h_ref[...], (((0,), (1,)), ((), ())), preferred_element_type=F32)
    rows, tq, _ = o_ref.shape
    if tq % SUBLANES == 0:
        for q in range(tq):
            stage_ref[pl.ds(q, rows, stride=tq), :] = acc[:, q * LANES:(q + 1) * LANES]
        o_ref[...] = stage_ref[...].reshape(rows, tq, LANES)
    else:
        for q in range(tq):
            o_ref[:, q, :] = acc[:, q * LANES:(q + 1) * LANES]


def _proj_t(w, col0, c, h, tmc=1024, tn=1024):
    d = w.shape[0]
    n = h.shape[0]
    tmc, tn = min(tmc, c), min(tn, n)
    assert col0 % tmc == 0 and c % tmc == 0
    return pl.pallas_call(
        _nt_kernel,
        grid=(n // tn, c // tmc),
        in_specs=[pl.BlockSpec((d, tmc), lambda i, j: (0, col0 // tmc + j)),
                  pl.BlockSpec((tn, d), lambda i, j: (i, 0))],
        out_specs=pl.BlockSpec((tmc, tn // LANES, LANES), lambda i, j: (j, i, 0)),
        out_shape=jax.ShapeDtypeStruct((c, n // LANES, LANES), F32),
        scratch_shapes=[pltpu.VMEM((tmc * (tn // LANES), LANES), F32)],
        compiler_params=_cparams("parallel", "arbitrary"),
        name="s5_in_proj_t",
    )(w, h)


def _s5_tables(lam_re, lam_im, log_dt, b_re, b_im, c_re, c_im):
    t = CHUNK
    _, g, p = lam_re.shape
    h = b_re.shape[-1]
    dt = jnp.exp(log_dt.astype(F32))[..., None]
    lr, li = jnp.minimum(lam_re.astype(F32), -1e-4), lam_im.astype(F32)
    ar, ai = lr * dt, li * dt

    lbm = jnp.exp(ar)
    lbr, lbi = lbm * jnp.cos(ai), lbm * jnp.sin(ai)

    ks = jnp.arange(2 * t, dtype=F32)
    tab_m = jnp.exp(ar[..., None] * ks)
    tab_r, tab_i = tab_m * jnp.cos(ai[..., None] * ks), tab_m * jnp.sin(ai[..., None] * ks)

    def powers(d, first, step, steps_major=False):
        first = int(first)
        if step > 0:
            out = tab_r[d, :, :, first:first + t], tab_i[d, :, :, first:first + t]
        else:
            out = (jnp.flip(tab_r[d, :, :, first - t + 1:first + 1], -1),
                   jnp.flip(tab_i[d, :, :, first - t + 1:first + 1], -1))
        return tuple(jnp.swapaxes(a, 1, 2) for a in out) if steps_major else out

    nr, ni = lbr - 1.0, lbi
    den = lr * lr + li * li
    qr, qi = (nr * lr + ni * li) / den, (ni * lr - nr * li) / den
    br, bi = (jnp.swapaxes(a.astype(F32), -1, -2) for a in (b_re, b_im))
    bbr = qr[:, :, None] * br - qi[:, :, None] * bi
    bbi = qr[:, :, None] * bi + qi[:, :, None] * br
    cr, ci = c_re.astype(F32), c_im.astype(F32)

    kb_a = jnp.concatenate([bbr[0], bbr[0], bbr[1], bbr[1]], axis=-1)
    kb_b = jnp.concatenate([-bbi[0], bbi[0], -bbi[1], bbi[1]], axis=-1)
    kc_a = jnp.concatenate([cr[0], ci[0], cr[1], ci[1]], axis=-1)
    kc_b = jnp.concatenate([ci[0], cr[0], ci[1], cr[1]], axis=-1)
    wfr, wfi = powers(0, 0.0, 1.0)
    wbr, wbi = powers(1, t - 1.0, -1.0)
    fwd = lambda a: jnp.pad(a, ((0, 0), (0, 0), (t - 1, 1)))
    bwd = lambda a: jnp.pad(a, ((0, 0), (0, 0), (0, t)))
    kw = jnp.concatenate([fwd(wfr), fwd(-wfi), bwd(wbr), bwd(-wbi)], axis=1)

    pfr, pfi = powers(0, t - 1.0, -1.0, steps_major=True)
    pbr, pbi = powers(1, 0.0, 1.0, steps_major=True)
    ew_a = jnp.concatenate([pfr, pfr, pbr, pbr], axis=-1)
    ew_b = jnp.concatenate([pfi, pfi, pbi, pbi], axis=-1)
    eb_a = jnp.concatenate([bbr[0], bbi[0], bbr[1], bbi[1]], axis=-1)
    eb_b = jnp.concatenate([-bbi[0], bbr[0], -bbi[1], bbr[1]], axis=-1)

    qfr, qfi = powers(0, 1.0, 1.0)
    qbr, qbi = powers(1, float(t), -1.0)
    crf, cif, crb, cib = (jnp.swapaxes(a, 1, 2) for a in (cr[0], ci[0], cr[1], ci[1]))
    dc = jnp.concatenate([jnp.concatenate([crf, -crf, crb, -crb], axis=1),
                          jnp.concatenate([-cif, -cif, -cib, -cib], axis=1)], axis=-1)
    dq = jnp.concatenate([jnp.concatenate([qfr, qfi, qbr, qbi], axis=1),
                          jnp.concatenate([qfi, qfr, qbi, qbr], axis=1)], axis=-1)

    dec = jnp.stack([tab_r[0, :, :, t], tab_i[0, :, :, t],
                     tab_r[1, :, :, t], tab_i[1, :, :, t]]).reshape(4, g * p)
    return (kb_a, kb_b, kc_a, kc_b, kw), (ew_a, ew_b, eb_a, eb_b), (dc.astype(BF16), dq.astype(BF16)), dec


def _carry_expanders(h, t):
    ht = h * t
    rep = np.zeros((2 * h, 2 * ht), np.float32)
    til = np.zeros((2 * t, 2 * ht), np.float32)
    for half in range(2):
        for ho in range(h):
            for tt in range(t):
                rep[half * h + ho, half * ht + ho * t + tt] = 1.0
                til[half * t + tt, half * ht + ho * t + tt] = 1.0
    return jnp.asarray(rep, dtype=BF16), jnp.asarray(til, dtype=BF16)


def _toeplitz_select(t):
    sub = SUBLANES
    sel = np.zeros((4 * t, (t // sub) * 2 * t), np.float32)
    for q in range(t // sub):
        for tt in range(t):
            j = tt - sub * q + t - sub
            sel[j, q * 2 * t + tt] = 1.0
            sel[2 * t + j, q * 2 * t + t + tt] = 1.0
    return jnp.asarray(sel, dtype=BF16)


GPS = 4


def _gather_chunks(z_ref, gi, h):
    t = z_ref.shape[-1] // 2
    lo = lax.broadcasted_iota(jnp.int32, z_ref.shape[1:], 1) < t
    ev, od = [], []
    for k in range(0, h, 2):
        za, zb = z_ref[gi * h + k], z_ref[gi * h + k + 1]
        ev.append(jnp.where(lo, za, pltpu.roll(zb, t, axis=1)))
        od.append(jnp.where(lo, pltpu.roll(za, t, axis=1), zb))
    return jnp.concatenate([jnp.concatenate(ev, axis=1), jnp.concatenate(od, axis=1)], axis=0)


def _s5_state_kernel(zc_ref, z_ref, ewa_ref, ewb_ref, eba_ref, ebb_ref, *refs):
    couts, louts, e_scr = refs[0:4], refs[4:8], refs[8]
    h = z_ref.shape[0] // GPS
    t = ewa_ref.shape[1]
    p = ewa_ref.shape[-1] // 4
    for gi in range(GPS):
        wa, wb = ewa_ref[gi], ewb_ref[gi]
        for k in range(h):
            e_scr[k * t:(k + 1) * t, :] = (wa * eba_ref[gi, k:k + 1, :] + wb * ebb_ref[gi, k:k + 1, :]).astype(BF16)
        for src, outs in ((zc_ref, couts), (z_ref, louts)):
            a2 = _gather_chunks(src, gi, h).astype(BF16)
            s = _dot(a2, e_scr[...])
            for k in range(4):
                outs[k][:, gi * p:(gi + 1) * p] = s[:, k * p:(k + 1) * p]


def _s5_states(zc3, z3, e_tabs):
    gh, npair, t2 = z3.shape
    cpair = zc3.shape[1]
    t = t2 // 2
    g, _, p4 = e_tabs[0].shape
    h, p = gh // g, p4 // 4
    wspec = pl.BlockSpec((GPS, t, p4), lambda i: (i, 0, 0))
    bspec = pl.BlockSpec((GPS, h, p4), lambda i: (i, 0, 0))
    res = pl.pallas_call(
        _s5_state_kernel,
        grid=(g // GPS,),
        in_specs=[pl.BlockSpec((GPS * h, cpair, t2), lambda i: (i, 0, 0)),
                  pl.BlockSpec((GPS * h, npair, t2), lambda i: (i, 0, 0)), wspec, wspec, bspec, bspec],
        out_specs=[pl.BlockSpec((2 * cpair, GPS * p), lambda i: (0, i))] * 4 +
                  [pl.BlockSpec((2 * npair, GPS * p), lambda i: (0, i))] * 4,
        out_shape=[jax.ShapeDtypeStruct((2 * cpair, g * p), F32)] * 4 +
                  [jax.ShapeDtypeStruct((2 * npair, g * p), F32)] * 4,
        scratch_shapes=[pltpu.VMEM((h * t, p4), BF16)],
        compiler_params=_cparams("parallel"),
        name="s5_chunk_states",
    )(zc3, z3, *e_tabs)
    return res[:4], res[4:]


def _s5_scan_kernel(bsz, dec_ref, cfre, cfim, cbre, cbim, sfre, sfim, sbre, sbim,
                    hfre, hfim, hbre, hbim):
    nctx = cfre.shape[0] // bsz
    nlat = sfre.shape[0] // bsz
    width = dec_ref.shape[-1]
    fr, fi = dec_ref[0:1, :], dec_ref[1:2, :]
    br, bi = dec_ref[2:3, :], dec_ref[3:4, :]
    zero = jnp.zeros((1, width), F32)

    def step(ar, ai, hr, hi, sr, si):
        return ar * hr - ai * hi + sr, ar * hi + ai * hr + si

    def row(nchunk, b, j):
        return (j % 2) * (bsz * nchunk // 2) + b * (nchunk // 2) + j // 2

    init = []
    for b in range(bsz):
        hr, hi = zero, zero
        for j in range(nctx):
            r = row(nctx, b, j)
            hr, hi = step(fr, fi, hr, hi, cfre[r:r + 1, :], cfim[r:r + 1, :])
        init += [hr, hi]
        hr, hi = zero, zero
        for j in range(nctx - 1, -1, -1):
            r = row(nctx, b, j)
            hr, hi = step(br, bi, hr, hi, cbre[r:r + 1, :], cbim[r:r + 1, :])
        init += [hr, hi]

    def body(k, carry):
        out = []
        for b in range(bsz):
            hr, hi, gr, gi = carry[4 * b:4 * b + 4]
            rf = row(nlat, b, k)
            hfre[pl.ds(rf, 1), :] = hr
            hfim[pl.ds(rf, 1), :] = hi
            out += list(step(fr, fi, hr, hi, sfre[pl.ds(rf, 1), :], sfim[pl.ds(rf, 1), :]))
            rb = row(nlat, b, nlat - 1 - k)
            hbre[pl.ds(rb, 1), :] = gr
            hbim[pl.ds(rb, 1), :] = gi
            out += list(step(br, bi, gr, gi, sbre[pl.ds(rb, 1), :], sbim[pl.ds(rb, 1), :]))
        return tuple(out)

    lax.fori_loop(0, nlat, body, tuple(init), unroll=2)


def _s5_scan(dec, ctx_s, lat_s, bsz, tw=1024):
    gp = dec.shape[-1]
    rc, rl = ctx_s[0].shape[0], lat_s[0].shape[0]
    assert (rc // bsz) % 2 == 0 and (rl // bsz) % 2 == 0
    tw = min(tw, gp)
    cspec = pl.BlockSpec((rc, tw), lambda i: (0, i))
    lspec = pl.BlockSpec((rl, tw), lambda i: (0, i))
    out = jax.ShapeDtypeStruct((rl, gp), F32)
    return pl.pallas_call(
        functools.partial(_s5_scan_kernel, bsz),
        grid=(gp // tw,),
        in_specs=[pl.BlockSpec((4, tw), lambda i: (0, i))] + [cspec] * 4 + [lspec] * 4,
        out_specs=[lspec] * 4,
        out_shape=[out] * 4,
        compiler_params=_cparams("parallel"),
        name="s5_state_scan",
    )(dec, *ctx_s, *lat_s)


def _gelu_tanh(x):
    return 0.5 * x * (1.0 + jnp.tanh(math.sqrt(2.0 / math.pi) * (x + 0.044715 * (x * x * x))))


def _expand_toeplitz(k_ref, sel_ref, lhs_ref, res_ref, w_ref, h, t):
    half = h // 2
    sub = SUBLANES
    rows_per_hi = half * sub

    def fill(hi, carry):
        for hp in range(half):
            for par in range(2):
                v = k_ref[pl.ds(hi * h + 2 * hp + par, 1), :]
                b = pltpu.roll(jnp.broadcast_to(v, (sub, 2 * t)), 2 * t - (sub - 1), axis=1, stride=1, stride_axis=0)
                rows = pl.ds(pl.multiple_of(hi * rows_per_hi + hp * sub, sub), sub)
                lhs_ref[rows, par * 2 * t:(par + 1) * 2 * t] = b
        return carry

    lax.fori_loop(0, h, fill, 0, unroll=True)
    res_ref[...] = _dot(lhs_ref[...].astype(BF16), sel_ref[...])

    def shuffle(hi, carry):
        base = pl.multiple_of(hi * rows_per_hi, rows_per_hi)
        r_hi = res_ref[pl.ds(base, rows_per_hi), :]
        rows = [jnp.concatenate([r_hi[hp * sub:(hp + 1) * sub, q * 2 * t:(q + 1) * 2 * t] for hp in range(half)],
                                axis=1)
                for q in range(t // sub)]
        w_ref[pl.ds(base, t), :] = jnp.concatenate(rows, axis=0).astype(BF16)
        return carry

    lax.fori_loop(0, h, shuffle, 0, unroll=True)


def _s5_out_kernel(z_ref, kba_ref, kbb_ref, kca_ref, kcb_ref, kw_ref, sel_ref, dc_ref, dq_ref, rep_ref, til_ref,
                   hfre, hfim, hbre, hbim, dsk_ref, o_ref, km_scr, k_scr, lhs_ref, res_ref, w_ref):
    h = z_ref.shape[0] // GPS
    npair = z_ref.shape[1]
    t = z_ref.shape[-1] // 2
    ht = h * t
    p = dc_ref.shape[1] // 4
    lo = lax.broadcasted_iota(jnp.int32, z_ref.shape[1:], 1) < t
    for gi in range(GPS):
        for hi in range(h):
            km_scr[hi * h:(hi + 1) * h, :] = (kba_ref[gi, hi:hi + 1, :] * kca_ref[gi]
                                              + kbb_ref[gi, hi:hi + 1, :] * kcb_ref[gi])
        k_scr[...] = _dot_split(km_scr[...], kw_ref[gi])
        _expand_toeplitz(k_scr, sel_ref, lhs_ref, res_ref, w_ref, h, t)
        ce = _dot(dc_ref[gi], rep_ref[...])
        qe = _dot(dq_ref[gi], til_ref[...])
        dmat = (ce[:, :ht] * qe[:, :ht] + ce[:, ht:] * qe[:, ht:]).astype(BF16)
        u = _gather_chunks(z_ref, gi, h)
        sl = slice(gi * p, (gi + 1) * p)
        hp = jnp.concatenate([hfre[:, sl], hfim[:, sl], hbre[:, sl], hbim[:, sl]], axis=-1).astype(BF16)
        y = _dot(u.astype(BF16), w_ref[...]) + _dot(hp, dmat)
        gl = _gelu_tanh(y + u * dsk_ref[gi])
        for k in range(0, h, 2):
            te = gl[0:npair, k * t:(k + 2) * t]
            to = gl[npair:2 * npair, k * t:(k + 2) * t]
            o_ref[gi * h + k] = jnp.where(lo, te, pltpu.roll(to, t, axis=1))
            o_ref[gi * h + k + 1] = jnp.where(lo, pltpu.roll(te, t, axis=1), to)


def _s5_out(z3, k_tabs, sel, d_tabs, expanders, hin, dsk):
    gh, npair, t2z = z3.shape
    nchunk, t = 2 * npair, t2z // 2
    kb_a, kb_b, kc_a, kc_b, kw = k_tabs
    g, kp, t2 = kw.shape
    h = gh // g
    hh, ht = h * h, h * t
    assert t2 == 2 * t and (h // 2) * SUBLANES == t and t2 == LANES
    dc, dq = d_tabs
    rep, til = expanders
    p4 = dc.shape[1]
    p = p4 // 4
    hspec = pl.BlockSpec((nchunk, GPS * p), lambda i: (0, i))
    const = lambda a: pl.BlockSpec(a.shape, lambda i: (0, 0))
    return pl.pallas_call(
        _s5_out_kernel,
        grid=(g // GPS,),
        in_specs=[pl.BlockSpec((GPS * h, npair, t2z), lambda i: (i, 0, 0)),
                  ] + [pl.BlockSpec((GPS, h, kp), lambda i: (i, 0, 0))] * 4 + [
                  pl.BlockSpec((GPS, kp, t2), lambda i: (i, 0, 0)),
                  const(sel),
                  pl.BlockSpec((GPS, p4, 2 * h), lambda i: (i, 0, 0)),
                  pl.BlockSpec((GPS, p4, 2 * t), lambda i: (i, 0, 0)),
                  const(rep), const(til)] + [hspec] * 4 +
                 [pl.BlockSpec((GPS, 1, ht), lambda i: (i, 0, 0))],
        out_specs=pl.BlockSpec((GPS * h, npair, t2z), lambda i: (i, 0, 0)),
        out_shape=jax.ShapeDtypeStruct((gh, npair, t2z), F32),
        scratch_shapes=[pltpu.VMEM((hh, kp), F32), pltpu.VMEM((hh, t2), F32), pltpu.VMEM((ht, 4 * t), F32),
                        pltpu.VMEM((ht, ht), F32), pltpu.VMEM((ht, ht), BF16)],
        compiler_params=_cparams("parallel"),
        name="s5_chunk_out",
    )(z3, kb_a, kb_b, kc_a, kc_b, kw, sel, dc, dq, rep, til, *hin, dsk)


def _glu_kernel(gt_ref, wa_ref, wb_ref, ba_ref, bb_ref, o_ref, g_scr, stage_ref):
    @pl.when(pl.program_id(1) == 0)
    def _():
        c, tq, _ = gt_ref.shape
        half = stage_ref.shape[0] // tq
        for c0 in range(0, c, half):
            stage_ref[...] = gt_ref[c0:c0 + half].reshape(half * tq, LANES)
            for q in range(tq):
                g_scr[c0:c0 + half, q * LANES:(q + 1) * LANES] = stage_ref[pl.ds(q, half, stride=tq), :].astype(BF16)

    dn = (((0,), (0,)), ((), ()))
    gt = g_scr[...]
    a = lax.dot_general(gt, wa_ref[...].astype(BF16), dn, preferred_element_type=F32) + ba_ref[...]
    b = lax.dot_general(gt, wb_ref[...].astype(BF16), dn, preferred_element_type=F32) + bb_ref[...]
    o_ref[...] = (a * jax.nn.sigmoid(b)).astype(o_ref.dtype)


def _glu(g3, wa, wb, ba, bb, tm=1024, tn=512):
    c, nq, _ = g3.shape
    n = nq * LANES
    co = wa.shape[-1]
    tm, tn = min(tm, n), min(tn, co)
    stage_rows = (c // 2) * (tm // LANES)
    return pl.pallas_call(
        _glu_kernel,
        grid=(n // tm, co // tn),
        in_specs=[pl.BlockSpec((c, tm // LANES, LANES), lambda i, j: (0, i, 0)),
                  pl.BlockSpec((c, tn), lambda i, j: (0, j)),
                  pl.BlockSpec((c, tn), lambda i, j: (0, j)),
                  pl.BlockSpec((1, tn), lambda i, j: (0, j)),
                  pl.BlockSpec((1, tn), lambda i, j: (0, j))],
        out_specs=pl.BlockSpec((tm, tn), lambda i, j: (i, j)),
        out_shape=jax.ShapeDtypeStruct((n, co), BF16),
        scratch_shapes=[pltpu.VMEM((c, tm), BF16), pltpu.VMEM((stage_rows, LANES), F32)],
        compiler_params=_cparams("parallel", "arbitrary"),
        name="s5_glu",
    )(g3, wa, wb, ba.reshape(1, co), bb.reshape(1, co))


def _out_proj_kernel(yf_ref, ys_ref, wf_ref, ws_ref, x_ref, g_ref, o_ref):
    acc = _dot(yf_ref[...], wf_ref[...]) + _dot(ys_ref[...], ws_ref[...])
    o_ref[...] = x_ref[...] + g_ref[...] * acc


def _out_proj(yf, ys, w_out, x2, gate, length, tm=1024, tn=1024):
    n, fw = yf.shape
    sw = ys.shape[-1]
    d = w_out.shape[-1]
    tm, tn = min(tm, length), min(tn, d)
    assert fw % sw == 0
    bsz = gate.shape[0]
    return pl.pallas_call(
        _out_proj_kernel,
        grid=(n // tm, d // tn),
        in_specs=[pl.BlockSpec((tm, fw), lambda i, j: (i, 0)),
                  pl.BlockSpec((tm, sw), lambda i, j: (i, 0)),
                  pl.BlockSpec((fw, tn), lambda i, j: (0, j)),
                  pl.BlockSpec((sw, tn), lambda i, j: (fw // sw, j)),
                  pl.BlockSpec((tm, tn), lambda i, j: (i, j)),
                  pl.BlockSpec((None, 1, tn), lambda i, j: ((i * tm) // length, 0, j))],
        out_specs=pl.BlockSpec((tm, tn), lambda i, j: (i, j)),
        out_shape=jax.ShapeDtypeStruct((n, d), F32),
        compiler_params=_cparams("parallel", "arbitrary"),
        name="out_proj_residual",
    )(yf, ys, w_out, w_out, x2, gate.reshape(bsz, 1, d))


def _ffn_hidden_start(j, th, hid):
    return pl.multiple_of(jnp.minimum(j * th, hid - th), math.gcd(th, hid - th))


def _ffn_kernel(hid, h_ref, wg_ref, wu_ref, wd_ref, x_hbm, gate_ref, fg_ref, o_ref, x_sem):
    i, j = pl.program_id(0), pl.program_id(1)
    tm = o_ref.shape[0]
    th = wg_ref.shape[-1]

    def residual_copy():
        return pltpu.make_async_copy(x_hbm.at[pl.ds(pl.multiple_of(i * tm, tm), tm), :], o_ref, x_sem)

    @pl.when(j == 0)
    def _():
        residual_copy().start()

    hh = h_ref[...]
    g = _dot(hh, wg_ref[...].astype(BF16))
    u = _dot(hh, wu_ref[...].astype(BF16))
    unit = _ffn_hidden_start(j, th, hid) + lax.broadcasted_iota(jnp.int32, (1, th), 1)
    a = jnp.where(unit >= j * th, g * jax.nn.sigmoid(g) * u, 0.0).astype(BF16)
    d = o_ref.shape[-1]
    nc = min(FFN_DOWN_CHUNK, d)
    wd = wd_ref[...].astype(BF16)

    @pl.when(j == 0)
    def _():
        residual_copy().wait()

    for c0 in range(0, d, nc):
        o_ref[:, c0:c0 + nc] += gate_ref[:, c0:c0 + nc] * _dot(a, wd[:, c0:c0 + nc])

    @pl.when(j == pl.num_programs(1) - 1)
    def _():
        fg = fg_ref[...]

        def norm_rows(r, carry):
            rows = pl.ds(pl.multiple_of(r * FFN_NORM_ROWS, FFN_NORM_ROWS), FFN_NORM_ROWS)
            x = o_ref[rows, :]
            ms = jnp.mean(x * x, axis=-1, keepdims=True)
            o_ref[rows, :] = x * lax.rsqrt(ms + EPS) * fg
            return carry

        lax.fori_loop(0, tm // FFN_NORM_ROWS, norm_rows, 0)


FFN_DOWN_CHUNK = 512
FFN_NORM_ROWS = 64


FFN_TH = 256
FFN_VMEM_LIMIT = 60 * 1024 * 1024


def _ffn(h, wg, wu, wd, x, gate, final_g, length, tm=1024):
    n, d = h.shape
    hid = wg.shape[-1]
    bsz = gate.shape[0]
    tm, th = min(tm, length), min(FFN_TH, hid)
    assert length % tm == 0 and tm % FFN_NORM_ROWS == 0
    col_map = lambda i, j: (0, _ffn_hidden_start(j, th, hid))
    return pl.pallas_call(
        functools.partial(_ffn_kernel, hid),
        grid=(n // tm, pl.cdiv(hid, th)),
        in_specs=[pl.BlockSpec((tm, d), lambda i, j: (i, 0), pipeline_mode=pl.Buffered(1)),
                  pl.BlockSpec((pl.Element(d), pl.Element(th)), col_map),
                  pl.BlockSpec((pl.Element(d), pl.Element(th)), col_map),
                  pl.BlockSpec((pl.Element(th), pl.Element(d)), lambda i, j: (_ffn_hidden_start(j, th, hid), 0)),
                  pl.BlockSpec(memory_space=pl.ANY),
                  pl.BlockSpec((None, 1, d), lambda i, j: ((i * tm) // length, 0, 0)),
                  pl.BlockSpec((1, d), lambda i, j: (0, 0))],
        out_specs=pl.BlockSpec((tm, d), lambda i, j: (i, 0), pipeline_mode=pl.Buffered(1)),
        out_shape=jax.ShapeDtypeStruct((n, d), F32),
        scratch_shapes=[pltpu.SemaphoreType.DMA(())],
        compiler_params=_cparams("parallel", "arbitrary", vmem=FFN_VMEM_LIMIT),
        name="swiglu_ffn",
    )(h, wg, wu, wd, x, gate.reshape(bsz, 1, d), final_g.reshape(1, d))


def _dft_split(length):
    n2 = 64
    while (length // n2) % PACK_ROWS:
        n2 //= 2
    assert n2 >= SUBLANES and length % n2 == 0
    return n2


def kernel(x, c, ctx, c_ctx, ada_w, ada_b, norm1_g, norm2_g, w_in, w_out, fourier_w, s5_lam_re, s5_lam_im, s5_log_dt, s5_b_re, s5_b_im, s5_c_re, s5_c_im, s5_d, glu_w_a, glu_b_a, glu_w_b, glu_b_b, ffn_w_gate, ffn_w_up, ffn_w_down, final_g):
    bsz, length, d = x.shape
    depth = ada_w.shape[0]
    assert depth == 1, "single-layer block"
    lyr = 0
    heads, hd, _ = fourier_w.shape[1:]
    fw = heads * hd
    _, g, p, hgrp = s5_b_re.shape[1:]
    sw = g * hgrp
    assert w_in.shape[-1] == fw + sw and length % (2 * CHUNK) == 0 and ctx.shape[1] % (2 * CHUNK) == 0
    n = bsz * length

    a8 = jnp.zeros((8, d), F32).at[:bsz].set(c.astype(F32)).at[bsz].set(c_ctx.astype(F32))
    mods = _ada(jnp.concatenate([a8, a8], axis=0), ada_w[lyr], ada_b[lyr]).reshape(8, N_MOD, d)
    sh1, sc1, g1, sh2, sc2, g2 = (mods[:bsz, i] for i in range(N_MOD))
    csh1, csc1 = mods[bsz:bsz + 1, 0], mods[bsz:bsz + 1, 1]

    w_in_b = w_in[lyr].astype(BF16)
    ang = (2.0 * np.pi / hd) * ((np.arange(hd)[:, None] * np.arange(hd)[None, :]) % hd).astype(np.float64)
    cd = jnp.asarray(np.cos(ang) / math.sqrt(hd), F32)
    sd = jnp.asarray(np.sin(ang) / math.sqrt(hd), F32)
    csd = jnp.broadcast_to(jnp.stack([cd, sd])[:, None], (2, heads, hd, hd)).reshape(2 * heads, hd, hd)
    wf2 = jnp.concatenate([fourier_w[lyr], fourier_w[lyr]], axis=0).astype(F32)
    folded = _fold(csd, wf2)
    wcs = jnp.concatenate([folded[:heads], folded[heads:]], axis=-1).astype(BF16)
    k_tabs, e_mat, d_mat, dec = _s5_tables(s5_lam_re[lyr], s5_lam_im[lyr], s5_log_dt[lyr], s5_b_re[lyr],
                                         s5_b_im[lyr], s5_c_re[lyr], s5_c_im[lyr])
    sel = _toeplitz_select(CHUNK)
    dsk = jnp.repeat(s5_d[lyr].astype(F32).reshape(g, 1, hgrp), CHUNK, axis=-1)

    hc = _norm_mod(ctx, norm1_g[lyr], csh1, csc1)
    nctx_tok = bsz * ctx.shape[1]
    zc3 = _proj_t(w_in_b, fw, sw, hc.reshape(nctx_tok, d))

    hm = _norm_mod(x, norm1_g[lyr], sh1, sc1).reshape(n, d)
    n2 = _dft_split(length)
    pc4, ps4 = _four_in(hm, w_in_b, wcs, bsz, length, n2)
    g_tab, cs_tab = _dft_tables(length, n2)
    y_four = _dft(pc4, ps4, g_tab, cs_tab).reshape(n, fw)

    z3 = _proj_t(w_in_b, fw, sw, hm)
    ctx_s, lat_s = _s5_states(zc3, z3, e_mat)
    h_in = _s5_scan(dec, ctx_s, lat_s, bsz)
    g3 = _s5_out(z3, k_tabs, sel, d_mat, _carry_expanders(hgrp, CHUNK), h_in, dsk)
    y_s = _glu(g3, glu_w_a[lyr], glu_w_b[lyr], glu_b_a[lyr], glu_b_b[lyr])

    x1 = _out_proj(y_four, y_s, w_out[lyr].astype(BF16), x.reshape(n, d), g1, length)

    hm2 = _norm_mod(x1.reshape(bsz, length, d), norm2_g[lyr], sh2, sc2).reshape(n, d)
    out = _ffn(hm2, ffn_w_gate[lyr], ffn_w_up[lyr], ffn_w_down[lyr], x1, g2, final_g, length)
    return out.reshape(bsz, length, d)
```

```python
import functools
import math

import numpy as np
import jax
import jax.numpy as jnp
from jax import lax
from jax.experimental import pallas as pl
from jax.experimental.pallas import tpu as pltpu

F32 = jnp.float32
BF16 = jnp.bfloat16
EPS = 1e-6
CHUNK = 64
N_MOD = 6
V7X_VMEM_LIMIT = 56 * 1024 * 1024
HI = lax.Precision.HIGHEST
LANES = 128
SUBLANES = 8
PACK_ROWS = 16
PITCH_PAD = 8


def _cparams(*sem, vmem=V7X_VMEM_LIMIT):
    return pltpu.CompilerParams(dimension_semantics=sem, vmem_limit_bytes=vmem)


def _dot(a, b):
    return jnp.dot(a, b, preferred_element_type=F32)


def _dot_split(a, b):
    a_hi, b_hi = a.astype(BF16), b.astype(BF16)
    a_lo = (a - a_hi.astype(F32)).astype(BF16)
    b_lo = (b - b_hi.astype(F32)).astype(BF16)
    return _dot(a_hi, b_hi) + _dot(a_hi, b_lo) + _dot(a_lo, b_hi)


def _ada_kernel(a_ref, w_ref, b_ref, o_ref):
    a = a_ref[...]
    s = a * jax.nn.sigmoid(a)
    s_hi = s.astype(BF16).astype(F32)
    row = lax.broadcasted_iota(jnp.int32, s.shape, 0)
    lhs = jnp.where(row < 8, s_hi, s - s_hi).astype(BF16)
    w = w_ref[...]
    w_hi = w.astype(BF16)
    w_lo = (w - w_hi.astype(F32)).astype(BF16)
    r = _dot(lhs, w_hi) + _dot(lhs, w_lo)
    o_ref[...] = r[0:8] + r[8:16] + b_ref[...]


def _ada(a16, w, b, tn=1024):
    d, n = w.shape
    tn = min(tn, n)
    return pl.pallas_call(
        _ada_kernel,
        grid=(n // tn,),
        in_specs=[pl.BlockSpec((16, d), lambda j: (0, 0)),
                  pl.BlockSpec((d, tn), lambda j: (0, j)),
                  pl.BlockSpec((1, tn), lambda j: (0, j))],
        out_specs=pl.BlockSpec((8, tn), lambda j: (0, j)),
        out_shape=jax.ShapeDtypeStruct((8, n), F32),
        compiler_params=_cparams("parallel"),
        name="ada_matvec",
    )(a16, w, b.reshape(1, n))


def _norm_mod_kernel(x_ref, g_ref, sh_ref, sc_ref, o_ref):
    x = x_ref[...]
    ms = jnp.mean(x * x, axis=-1, keepdims=True)
    y = x * lax.rsqrt(ms + EPS) * g_ref[...]
    o_ref[...] = (y * (1.0 + sc_ref[...]) + sh_ref[...]).astype(o_ref.dtype)


def _norm_mod(x, g, sh, sc, tm=512):
    bsz, length, d = x.shape
    tm = min(tm, length)
    bm = sh.shape[0]
    mod_map = (lambda b, i: (b, 0, 0)) if bm == bsz else (lambda b, i: (0, 0, 0))
    return pl.pallas_call(
        _norm_mod_kernel,
        grid=(bsz, length // tm),
        in_specs=[pl.BlockSpec((None, tm, d), lambda b, i: (b, i, 0)),
                  pl.BlockSpec((1, d), lambda b, i: (0, 0)),
                  pl.BlockSpec((None, 1, d), mod_map),
                  pl.BlockSpec((None, 1, d), mod_map)],
        out_specs=pl.BlockSpec((None, tm, d), lambda b, i: (b, i, 0)),
        out_shape=jax.ShapeDtypeStruct((bsz, length, d), BF16),
        compiler_params=_cparams("parallel", "parallel"),
        name="norm_mod",
    )(x, g.reshape(1, d), sh.reshape(bm, 1, d), sc.reshape(bm, 1, d))


def _fold_kernel(a_ref, b_ref, o_ref):
    o_ref[...] = jnp.dot(a_ref[...], b_ref[...], preferred_element_type=F32, precision=HI)


def _fold(a, b):
    hh, m, k = a.shape
    n = b.shape[-1]
    return pl.pallas_call(
        _fold_kernel,
        grid=(hh,),
        in_specs=[pl.BlockSpec((None, m, k), lambda h: (h, 0, 0)),
                  pl.BlockSpec((None, k, n), lambda h: (h, 0, 0))],
        out_specs=pl.BlockSpec((None, m, n), lambda h: (h, 0, 0)),
        out_shape=jax.ShapeDtypeStruct((hh, m, n), F32),
        compiler_params=_cparams("parallel"),
        name="weight_fold",
    )(a, b)


def _four_in_kernel(h_ref, w_ref, wcs_ref, pc_ref, ps_ref, z_scr, zp_scr):
    n2, tn1, hd = pc_ref.shape
    pitch = n2 + PITCH_PAD
    nq = hd // LANES
    z = _dot(h_ref[...], w_ref[...])
    for q in range(nq):
        for i1 in range(tn1):
            z_scr[q, i1 * pitch:i1 * pitch + n2, :] = z[i1 * n2:(i1 + 1) * n2, q * LANES:(q + 1) * LANES]
    for j2 in range(n2):
        for q in range(nq):
            zp_scr[j2 * tn1:(j2 + 1) * tn1, q * LANES:(q + 1) * LANES] = (
                z_scr[q, pl.ds(j2, tn1, stride=pitch), :].astype(BF16))
    p = _dot(zp_scr[...], wcs_ref[...])
    p3 = p.reshape(n2, tn1, 2 * hd)
    pc_ref[...] = p3[:, :, :hd].astype(BF16)
    ps_ref[...] = p3[:, :, hd:].astype(BF16)


def _four_in(h, w_f, wcs, bsz, length, n2):
    n, d = h.shape
    heads, hd, _ = wcs.shape
    n1 = length // n2
    tn1 = PACK_ROWS
    tm = tn1 * n2
    tpb = length // tm
    fw = heads * hd
    out = jax.ShapeDtypeStruct((bsz, n2, n1, fw), BF16)
    ospec = pl.BlockSpec((None, n2, tn1, hd), lambda i, j: (i // tpb, 0, i % tpb, j))
    return pl.pallas_call(
        _four_in_kernel,
        grid=(n // tm, heads),
        in_specs=[pl.BlockSpec((tm, d), lambda i, j: (i, 0)),
                  pl.BlockSpec((d, hd), lambda i, j: (0, j)),
                  pl.BlockSpec((None, hd, 2 * hd), lambda i, j: (j, 0, 0))],
        out_specs=[ospec, ospec],
        out_shape=[out, out],
        scratch_shapes=[pltpu.VMEM((hd // LANES, tn1 * (n2 + PITCH_PAD), LANES), F32), pltpu.VMEM((tm, hd), BF16)],
        compiler_params=_cparams("parallel", "arbitrary"),
        name="fourier_in_proj",
    )(h, w_f, wcs)


def _dft_tables(length, n2):
    n1 = length // n2
    k1 = np.arange(n1)[:, None, None]
    i1 = np.arange(n1)[None, :, None]
    i2 = np.arange(n2)[None, None, :]
    phase = (k1 * (n2 * i1 + i2)) % length
    phi = (2.0 * np.pi / length) * phase.astype(np.float64)
    c = np.cos(phi).transpose(2, 0, 1) / math.sqrt(n1)
    s = np.sin(phi).transpose(2, 0, 1) / math.sqrt(n1)
    g = np.concatenate([np.concatenate([c, -s], axis=2),
                        np.concatenate([-s, -c], axis=2)], axis=1)
    k2 = np.arange(n2)[:, None]
    j2 = np.arange(n2)[None, :]
    th = (2.0 * np.pi / n2) * ((k2 * j2) % n2).astype(np.float64)
    cs = np.concatenate([np.cos(th), np.sin(th)], axis=1) / math.sqrt(n2)
    return jnp.asarray(g, dtype=BF16), jnp.asarray(cs, dtype=BF16)


def _dft_kernel(g_ref, cs_ref, pc_ref, ps_ref, o_ref, t_scr, y_scr):
    n2, n1, _ = pc_ref.shape
    m = 2 * n1
    p1 = m + PITCH_PAD
    p2 = n1 + PITCH_PAD

    def stage1(j2, carry):
        rhs = jnp.concatenate([pc_ref[j2], ps_ref[j2]], axis=0)
        t_scr[pl.ds(pl.multiple_of(j2 * p1, SUBLANES), m), :] = _dot(g_ref[j2], rhs)
        return carry

    lax.fori_loop(0, n2, stage1, 0, unroll=True)

    cs = cs_ref[...]

    def stage2(i, carry):
        k1 = 2 * i
        cols = []
        for dk in range(2):
            re = t_scr[pl.ds(k1 + dk, n2, stride=p1), :]
            im = t_scr[pl.ds(n1 + k1 + dk, n2, stride=p1), :]
            cols.append(jnp.concatenate([re, im], axis=0).astype(BF16))
        res = _dot(cs, jnp.concatenate(cols, axis=1))
        for dk in range(2):
            y_scr[pl.ds(k1 + dk, n2, stride=p2), :] = res[:, dk * LANES:(dk + 1) * LANES]
        return carry

    lax.fori_loop(0, n1 // 2, stage2, 0, unroll=True)
    for k2 in range(n2):
        o_ref[k2 * n1:(k2 + 1) * n1, :] = y_scr[k2 * p2:k2 * p2 + n1, :].astype(o_ref.dtype)


def _dft(pc4, ps4, g, cs):
    bsz, n2, n1, w = pc4.shape
    m = 2 * n1
    ispec = pl.BlockSpec((None, n2, n1, LANES), lambda b, j: (b, 0, 0, j))
    return pl.pallas_call(
        _dft_kernel,
        grid=(bsz, w // LANES),
        in_specs=[pl.BlockSpec((n2, m, m), lambda b, j: (0, 0, 0), pipeline_mode=pl.Buffered(1)),
                  pl.BlockSpec((n2, 2 * n2), lambda b, j: (0, 0)),
                  ispec, ispec],
        out_specs=pl.BlockSpec((None, n2 * n1, LANES), lambda b, j: (b, 0, j)),
        out_shape=jax.ShapeDtypeStruct((bsz, n2 * n1, w), BF16),
        scratch_shapes=[pltpu.VMEM((n2 * (m + PITCH_PAD), LANES), F32),
                        pltpu.VMEM((n2 * (n1 + PITCH_PAD), LANES), F32)],
        compiler_params=_cparams("parallel", "parallel"),
        name="position_dft",
    )(g, cs, pc4, ps4)


def _nt_kernel(w_ref, h_ref, o_ref, stage_ref):
    acc = lax.dot_general(w_ref[...], h_ref[...], (((0,), (1,)), ((), ())), preferred_element_type=F32)
    rows, tq, _ = o_ref.shape
    if tq % SUBLANES == 0:
        for q in range(tq):
            stage_ref[pl.ds(q, rows, stride=tq), :] = acc[:, q * LANES:(q + 1) * LANES]
        o_ref[...] = stage_ref[...].reshape(rows, tq, LANES)
    else:
        for q in range(tq):
            o_ref[:, q, :] = acc[:, q * LANES:(q + 1) * LANES]


def _proj_t(w, col0, c, h, tmc=1024, tn=1024):
    d = w.shape[0]
    n = h.shape[0]
    tmc, tn = min(tmc, c), min(tn, n)
    assert col0 % tmc == 0 and c % tmc == 0
    return pl.pallas_call(
        _nt_kernel,
        grid=(n // tn, c // tmc),
        in_specs=[pl.BlockSpec((d, tmc), lambda i, j: (0, col0 // tmc + j)),
                  pl.BlockSpec((tn, d), lambda i, j: (i, 0))],
        out_specs=pl.BlockSpec((tmc, tn // LANES, LANES), lambda i, j: (j, i, 0)),
        out_shape=jax.ShapeDtypeStruct((c, n // LANES, LANES), F32),
        scratch_shapes=[pltpu.VMEM((tmc * (tn // LANES), LANES), F32)],
        compiler_params=_cparams("parallel", "arbitrary"),
        name="s5_in_proj_t",
    )(w, h)


def _s5_tables(lam_re, lam_im, log_dt, b_re, b_im, c_re, c_im):
    t = CHUNK
    _, g, p = lam_re.shape
    h = b_re.shape[-1]
    dt = jnp.exp(log_dt.astype(F32))[..., None]
    lr, li = jnp.minimum(lam_re.astype(F32), -1e-4), lam_im.astype(F32)
    ar, ai = lr * dt, li * dt

    lbm = jnp.exp(ar)
    lbr, lbi = lbm * jnp.cos(ai), lbm * jnp.sin(ai)

    ks = jnp.arange(2 * t, dtype=F32)
    tab_m = jnp.exp(ar[..., None] * ks)
    tab_r, tab_i = tab_m * jnp.cos(ai[..., None] * ks), tab_m * jnp.sin(ai[..., None] * ks)

    def powers(d, first, step, steps_major=False):
        first = int(first)
        if step > 0:
            out = tab_r[d, :, :, first:first + t], tab_i[d, :, :, first:first + t]
        else:
            out = (jnp.flip(tab_r[d, :, :, first - t + 1:first + 1], -1),
                   jnp.flip(tab_i[d, :, :, first - t + 1:first + 1], -1))
        return tuple(jnp.swapaxes(a, 1, 2) for a in out) if steps_major else out

    nr, ni = lbr - 1.0, lbi
    den = lr * lr + li * li
    qr, qi = (nr * lr + ni * li) / den, (ni * lr - nr * li) / den
    br, bi = (jnp.swapaxes(a.astype(F32), -1, -2) for a in (b_re, b_im))
    bbr = qr[:, :, None] * br - qi[:, :, None] * bi
    bbi = qr[:, :, None] * bi + qi[:, :, None] * br
    cr, ci = c_re.astype(F32), c_im.astype(F32)

    kb_a = jnp.concatenate([bbr[0], bbr[0], bbr[1], bbr[1]], axis=-1)
    kb_b = jnp.concatenate([-bbi[0], bbi[0], -bbi[1], bbi[1]], axis=-1)
    kc_a = jnp.concatenate([cr[0], ci[0], cr[1], ci[1]], axis=-1)
    kc_b = jnp.concatenate([ci[0], cr[0], ci[1], cr[1]], axis=-1)
    wfr, wfi = powers(0, 0.0, 1.0)
    wbr, wbi = powers(1, t - 1.0, -1.0)
    fwd = lambda a: jnp.pad(a, ((0, 0), (0, 0), (t - 1, 1)))
    bwd = lambda a: jnp.pad(a, ((0, 0), (0, 0), (0, t)))
    kw = jnp.concatenate([fwd(wfr), fwd(-wfi), bwd(wbr), bwd(-wbi)], axis=1)

    pfr, pfi = powers(0, t - 1.0, -1.0, steps_major=True)
    pbr, pbi = powers(1, 0.0, 1.0, steps_major=True)
    ew_a = jnp.concatenate([pfr, pfr, pbr, pbr], axis=-1)
    ew_b = jnp.concatenate([pfi, pfi, pbi, pbi], axis=-1)
    eb_a = jnp.concatenate([bbr[0], bbi[0], bbr[1], bbi[1]], axis=-1)
    eb_b = jnp.concatenate([-bbi[0], bbr[0], -bbi[1], bbr[1]], axis=-1)

    qfr, qfi = powers(0, 1.0, 1.0)
    qbr, qbi = powers(1, float(t), -1.0)
    crf, cif, crb, cib = (jnp.swapaxes(a, 1, 2) for a in (cr[0], ci[0], cr[1], ci[1]))
    dc = jnp.concatenate([jnp.concatenate([crf, -crf, crb, -crb], axis=1),
                          jnp.concatenate([-cif, -cif, -cib, -cib], axis=1)], axis=-1)
    dq = jnp.concatenate([jnp.concatenate([qfr, qfi, qbr, qbi], axis=1),
                          jnp.concatenate([qfi, qfr, qbi, qbr], axis=1)], axis=-1)

    dec = jnp.stack([tab_r[0, :, :, t], tab_i[0, :, :, t],
                     tab_r[1, :, :, t], tab_i[1, :, :, t]]).reshape(4, g * p)
    return (kb_a, kb_b, kc_a, kc_b, kw), (ew_a, ew_b, eb_a, eb_b), (dc.astype(BF16), dq.astype(BF16)), dec


def _carry_expanders(h, t):
    ht = h * t
    rep = np.zeros((2 * h, 2 * ht), np.float32)
    til = np.zeros((2 * t, 2 * ht), np.float32)
    for half in range(2):
        for ho in range(h):
            for tt in range(t):
                rep[half * h + ho, half * ht + ho * t + tt] = 1.0
                til[half * t + tt, half * ht + ho * t + tt] = 1.0
    return jnp.asarray(rep, dtype=BF16), jnp.asarray(til, dtype=BF16)


def _toeplitz_select(t):
    sub = SUBLANES
    sel = np.zeros((4 * t, (t // sub) * 2 * t), np.float32)
    for q in range(t // sub):
        for tt in range(t):
            j = tt - sub * q + t - sub
            sel[j, q * 2 * t + tt] = 1.0
            sel[2 * t + j, q * 2 * t + t + tt] = 1.0
    return jnp.asarray(sel, dtype=BF16)


GPS = 4


def _gather_chunks(z_ref, gi, h):
    t = z_ref.shape[-1] // 2
    lo = lax.broadcasted_iota(jnp.int32, z_ref.shape[1:], 1) < t
    ev, od = [], []
    for k in range(0, h, 2):
        za, zb = z_ref[gi * h + k], z_ref[gi * h + k + 1]
        ev.append(jnp.where(lo, za, pltpu.roll(zb, t, axis=1)))
        od.append(jnp.where(lo, pltpu.roll(za, t, axis=1), zb))
    return jnp.concatenate([jnp.concatenate(ev, axis=1), jnp.concatenate(od, axis=1)], axis=0)


def _s5_state_kernel(zc_ref, z_ref, ewa_ref, ewb_ref, eba_ref, ebb_ref, *refs):
    couts, louts, e_scr = refs[0:4], refs[4:8], refs[8]
    h = z_ref.shape[0] // GPS
    t = ewa_ref.shape[1]
    p = ewa_ref.shape[-1] // 4
    for gi in range(GPS):
        wa, wb = ewa_ref[gi], ewb_ref[gi]
        for k in range(h):
            e_scr[k * t:(k + 1) * t, :] = (wa * eba_ref[gi, k:k + 1, :] + wb * ebb_ref[gi, k:k + 1, :]).astype(BF16)
        for src, outs in ((zc_ref, couts), (z_ref, louts)):
            a2 = _gather_chunks(src, gi, h).astype(BF16)
            s = _dot(a2, e_scr[...])
            for k in range(4):
                outs[k][:, gi * p:(gi + 1) * p] = s[:, k * p:(k + 1) * p]


def _s5_states(zc3, z3, e_tabs):
    gh, npair, t2 = z3.shape
    cpair = zc3.shape[1]
    t = t2 // 2
    g, _, p4 = e_tabs[0].shape
    h, p = gh // g, p4 // 4
    wspec = pl.BlockSpec((GPS, t, p4), lambda i: (i, 0, 0))
    bspec = pl.BlockSpec((GPS, h, p4), lambda i: (i, 0, 0))
    res = pl.pallas_call(
        _s5_state_kernel,
        grid=(g // GPS,),
        in_specs=[pl.BlockSpec((GPS * h, cpair, t2), lambda i: (i, 0, 0)),
                  pl.BlockSpec((GPS * h, npair, t2), lambda i: (i, 0, 0)), wspec, wspec, bspec, bspec],
        out_specs=[pl.BlockSpec((2 * cpair, GPS * p), lambda i: (0, i))] * 4 +
                  [pl.BlockSpec((2 * npair, GPS * p), lambda i: (0, i))] * 4,
        out_shape=[jax.ShapeDtypeStruct((2 * cpair, g * p), F32)] * 4 +
                  [jax.ShapeDtypeStruct((2 * npair, g * p), F32)] * 4,
        scratch_shapes=[pltpu.VMEM((h * t, p4), BF16)],
        compiler_params=_cparams("parallel"),
        name="s5_chunk_states",
    )(zc3, z3, *e_tabs)
    return res[:4], res[4:]


def _s5_scan_kernel(bsz, dec_ref, cfre, cfim, cbre, cbim, sfre, sfim, sbre, sbim,
                    hfre, hfim, hbre, hbim):
    nctx = cfre.shape[0] // bsz
    nlat = sfre.shape[0] // bsz
    width = dec_ref.shape[-1]
    fr, fi = dec_ref[0:1, :], dec_ref[1:2, :]
    br, bi = dec_ref[2:3, :], dec_ref[3:4, :]
    zero = jnp.zeros((1, width), F32)

    def step(ar, ai, hr, hi, sr, si):
        return ar * hr - ai * hi + sr, ar * hi + ai * hr + si

    def row(nchunk, b, j):
        return (j % 2) * (bsz * nchunk // 2) + b * (nchunk // 2) + j // 2

    init = []
    for b in range(bsz):
        hr, hi = zero, zero
        for j in range(nctx):
            r = row(nctx, b, j)
            hr, hi = step(fr, fi, hr, hi, cfre[r:r + 1, :], cfim[r:r + 1, :])
        init += [hr, hi]
        hr, hi = zero, zero
        for j in range(nctx - 1, -1, -1):
            r = row(nctx, b, j)
            hr, hi = step(br, bi, hr, hi, cbre[r:r + 1, :], cbim[r:r + 1, :])
        init += [hr, hi]

    def body(k, carry):
        out = []
        for b in range(bsz):
            hr, hi, gr, gi = carry[4 * b:4 * b + 4]
            rf = row(nlat, b, k)
            hfre[pl.ds(rf, 1), :] = hr
            hfim[pl.ds(rf, 1), :] = hi
            out += list(step(fr, fi, hr, hi, sfre[pl.ds(rf, 1), :], sfim[pl.ds(rf, 1), :]))
            rb = row(nlat, b, nlat - 1 - k)
            hbre[pl.ds(rb, 1), :] = gr
            hbim[pl.ds(rb, 1), :] = gi
            out += list(step(br, bi, gr, gi, sbre[pl.ds(rb, 1), :], sbim[pl.ds(rb, 1), :]))
        return tuple(out)

    lax.fori_loop(0, nlat, body, tuple(init), unroll=2)


def _s5_scan(dec, ctx_s, lat_s, bsz, tw=1024):
    gp = dec.shape[-1]
    rc, rl = ctx_s[0].shape[0], lat_s[0].shape[0]
    assert (rc // bsz) % 2 == 0 and (rl // bsz) % 2 == 0
    tw = min(tw, gp)
    cspec = pl.BlockSpec((rc, tw), lambda i: (0, i))
    lspec = pl.BlockSpec((rl, tw), lambda i: (0, i))
    out = jax.ShapeDtypeStruct((rl, gp), F32)
    return pl.pallas_call(
        functools.partial(_s5_scan_kernel, bsz),
        grid=(gp // tw,),
        in_specs=[pl.BlockSpec((4, tw), lambda i: (0, i))] + [cspec] * 4 + [lspec] * 4,
        out_specs=[lspec] * 4,
        out_shape=[out] * 4,
        compiler_params=_cparams("parallel"),
        name="s5_state_scan",
    )(dec, *ctx_s, *lat_s)


def _gelu_tanh(x):
    return 0.5 * x * (1.0 + jnp.tanh(math.sqrt(2.0 / math.pi) * (x + 0.044715 * (x * x * x))))


def _expand_toeplitz(k_ref, sel_ref, lhs_ref, res_ref, w_ref, h, t):
    half = h // 2
    sub = SUBLANES
    rows_per_hi = half * sub

    def fill(hi, carry):
        for hp in range(half):
            for par in range(2):
                v = k_ref[pl.ds(hi * h + 2 * hp + par, 1), :]
                b = pltpu.roll(jnp.broadcast_to(v, (sub, 2 * t)), 2 * t - (sub - 1), axis=1, stride=1, stride_axis=0)
                rows = pl.ds(pl.multiple_of(hi * rows_per_hi + hp * sub, sub), sub)
                lhs_ref[rows, par * 2 * t:(par + 1) * 2 * t] = b
        return carry

    lax.fori_loop(0, h, fill, 0, unroll=True)
    res_ref[...] = _dot(lhs_ref[...].astype(BF16), sel_ref[...])

    def shuffle(hi, carry):
        base = pl.multiple_of(hi * rows_per_hi, rows_per_hi)
        r_hi = res_ref[pl.ds(base, rows_per_hi), :]
        rows = [jnp.concatenate([r_hi[hp * sub:(hp + 1) * sub, q * 2 * t:(q + 1) * 2 * t] for hp in range(half)],
                                axis=1)
                for q in range(t // sub)]
        w_ref[pl.ds(base, t), :] = jnp.concatenate(rows, axis=0).astype(BF16)
        return carry

    lax.fori_loop(0, h, shuffle, 0, unroll=True)


def _s5_out_kernel(z_ref, kba_ref, kbb_ref, kca_ref, kcb_ref, kw_ref, sel_ref, dc_ref, dq_ref, rep_ref, til_ref,
                   hfre, hfim, hbre, hbim, dsk_ref, o_ref, km_scr, k_scr, lhs_ref, res_ref, w_ref):
    h = z_ref.shape[0] // GPS
    npair = z_ref.shape[1]
    t = z_ref.shape[-1] // 2
    ht = h * t
    p = dc_ref.shape[1] // 4
    lo = lax.broadcasted_iota(jnp.int32, z_ref.shape[1:], 1) < t
    for gi in range(GPS):
        for hi in range(h):
            km_scr[hi * h:(hi + 1) * h, :] = (kba_ref[gi, hi:hi + 1, :] * kca_ref[gi]
                                              + kbb_ref[gi, hi:hi + 1, :] * kcb_ref[gi])
        k_scr[...] = _dot_split(km_scr[...], kw_ref[gi])
        _expand_toeplitz(k_scr, sel_ref, lhs_ref, res_ref, w_ref, h, t)
        ce = _dot(dc_ref[gi], rep_ref[...])
        qe = _dot(dq_ref[gi], til_ref[...])
        dmat = (ce[:, :ht] * qe[:, :ht] + ce[:, ht:] * qe[:, ht:]).astype(BF16)
        u = _gather_chunks(z_ref, gi, h)
        sl = slice(gi * p, (gi + 1) * p)
        hp = jnp.concatenate([hfre[:, sl], hfim[:, sl], hbre[:, sl], hbim[:, sl]], axis=-1).astype(BF16)
        y = _dot(u.astype(BF16), w_ref[...]) + _dot(hp, dmat)
        gl = _gelu_tanh(y + u * dsk_ref[gi])
        for k in range(0, h, 2):
            te = gl[0:npair, k * t:(k + 2) * t]
            to = gl[npair:2 * npair, k * t:(k + 2) * t]
            o_ref[gi * h + k] = jnp.where(lo, te, pltpu.roll(to, t, axis=1))
            o_ref[gi * h + k + 1] = jnp.where(lo, pltpu.roll(te, t, axis=1), to)


def _s5_out(z3, k_tabs, sel, d_tabs, expanders, hin, dsk):
    gh, npair, t2z = z3.shape
    nchunk, t = 2 * npair, t2z // 2
    kb_a, kb_b, kc_a, kc_b, kw = k_tabs
    g, kp, t2 = kw.shape
    h = gh // g
    hh, ht = h * h, h * t
    assert t2 == 2 * t and (h // 2) * SUBLANES == t and t2 == LANES
    dc, dq = d_tabs
    rep, til = expanders
    p4 = dc.shape[1]
    p = p4 // 4
    hspec = pl.BlockSpec((nchunk, GPS * p), lambda i: (0, i))
    const = lambda a: pl.BlockSpec(a.shape, lambda i: (0, 0))
    return pl.pallas_call(
        _s5_out_kernel,
        grid=(g // GPS,),
        in_specs=[pl.BlockSpec((GPS * h, npair, t2z), lambda i: (i, 0, 0)),
                  ] + [pl.BlockSpec((GPS, h, kp), lambda i: (i, 0, 0))] * 4 + [
                  pl.BlockSpec((GPS, kp, t2), lambda i: (i, 0, 0)),
                  const(sel),
                  pl.BlockSpec((GPS, p4, 2 * h), lambda i: (i, 0, 0)),
                  pl.BlockSpec((GPS, p4, 2 * t), lambda i: (i, 0, 0)),
                  const(rep), const(til)] + [hspec] * 4 +
                 [pl.BlockSpec((GPS, 1, ht), lambda i: (i, 0, 0))],
        out_specs=pl.BlockSpec((GPS * h, npair, t2z), lambda i: (i, 0, 0)),
        out_shape=jax.ShapeDtypeStruct((gh, npair, t2z), F32),
        scratch_shapes=[pltpu.VMEM((hh, kp), F32), pltpu.VMEM((hh, t2), F32), pltpu.VMEM((ht, 4 * t), F32),
                        pltpu.VMEM((ht, ht), F32), pltpu.VMEM((ht, ht), BF16)],
        compiler_params=_cparams("parallel"),
        name="s5_chunk_out",
    )(z3, kb_a, kb_b, kc_a, kc_b, kw, sel, dc, dq, rep, til, *hin, dsk)


def _glu_kernel(gt_ref, wa_ref, wb_ref, ba_ref, bb_ref, o_ref, g_scr, stage_ref):
    @pl.when(pl.program_id(1) == 0)
    def _():
        c, tq, _ = gt_ref.shape
        half = stage_ref.shape[0] // tq
        for c0 in range(0, c, half):
            stage_ref[...] = gt_ref[c0:c0 + half].reshape(half * tq, LANES)
            for q in range(tq):
                g_scr[c0:c0 + half, q * LANES:(q + 1) * LANES] = stage_ref[pl.ds(q, half, stride=tq), :].astype(BF16)

    dn = (((0,), (0,)), ((), ()))
    gt = g_scr[...]
    a = lax.dot_general(gt, wa_ref[...].astype(BF16), dn, preferred_element_type=F32) + ba_ref[...]
    b = lax.dot_general(gt, wb_ref[...].astype(BF16), dn, preferred_element_type=F32) + bb_ref[...]
    o_ref[...] = (a * jax.nn.sigmoid(b)).astype(o_ref.dtype)


def _glu(g3, wa, wb, ba, bb, tm=1024, tn=512):
    c, nq, _ = g3.shape
    n = nq * LANES
    co = wa.shape[-1]
    tm, tn = min(tm, n), min(tn, co)
    stage_rows = (c // 2) * (tm // LANES)
    return pl.pallas_call(
        _glu_kernel,
        grid=(n // tm, co // tn),
        in_specs=[pl.BlockSpec((c, tm // LANES, LANES), lambda i, j: (0, i, 0)),
                  pl.BlockSpec((c, tn), lambda i, j: (0, j)),
                  pl.BlockSpec((c, tn), lambda i, j: (0, j)),
                  pl.BlockSpec((1, tn), lambda i, j: (0, j)),
                  pl.BlockSpec((1, tn), lambda i, j: (0, j))],
        out_specs=pl.BlockSpec((tm, tn), lambda i, j: (i, j)),
        out_shape=jax.ShapeDtypeStruct((n, co), BF16),
        scratch_shapes=[pltpu.VMEM((c, tm), BF16), pltpu.VMEM((stage_rows, LANES), F32)],
        compiler_params=_cparams("parallel", "arbitrary"),
        name="s5_glu",
    )(g3, wa, wb, ba.reshape(1, co), bb.reshape(1, co))


def _out_proj_kernel(yf_ref, ys_ref, wf_ref, ws_ref, x_ref, g_ref, o_ref):
    acc = _dot(yf_ref[...], wf_ref[...]) + _dot(ys_ref[...], ws_ref[...])
    o_ref[...] = x_ref[...] + g_ref[...] * acc


def _out_proj(yf, ys, w_out, x2, gate, length, tm=1024, tn=1024):
    n, fw = yf.shape
    sw = ys.shape[-1]
    d = w_out.shape[-1]
    tm, tn = min(tm, length), min(tn, d)
    assert fw % sw == 0
    bsz = gate.shape[0]
    return pl.pallas_call(
        _out_proj_kernel,
        grid=(n // tm, d // tn),
        in_specs=[pl.BlockSpec((tm, fw), lambda i, j: (i, 0)),
                  pl.BlockSpec((tm, sw), lambda i, j: (i, 0)),
                  pl.BlockSpec((fw, tn), lambda i, j: (0, j)),
                  pl.BlockSpec((sw, tn), lambda i, j: (fw // sw, j)),
                  pl.BlockSpec((tm, tn), lambda i, j: (i, j)),
                  pl.BlockSpec((None, 1, tn), lambda i, j: ((i * tm) // length, 0, j))],
        out_specs=pl.BlockSpec((tm, tn), lambda i, j: (i, j)),
        out_shape=jax.ShapeDtypeStruct((n, d), F32),
        compiler_params=_cparams("parallel", "arbitrary"),
        name="out_proj_residual",
    )(yf, ys, w_out, w_out, x2, gate.reshape(bsz, 1, d))


def _ffn_hidden_start(j, th, hid):
    return pl.multiple_of(jnp.minimum(j * th, hid - th), math.gcd(th, hid - th))


def _ffn_kernel(hid, h_ref, wg_ref, wu_ref, wd_ref, x_hbm, gate_ref, fg_ref, o_ref, x_sem):
    i, j = pl.program_id(0), pl.program_id(1)
    tm = o_ref.shape[0]
    th = wg_ref.shape[-1]

    def residual_copy():
        return pltpu.make_async_copy(x_hbm.at[pl.ds(pl.multiple_of(i * tm, tm), tm), :], o_ref, x_sem)

    @pl.when(j == 0)
    def _():
        residual_copy().start()

    hh = h_ref[...]
    g = _dot(hh, wg_ref[...].astype(BF16))
    u = _dot(hh, wu_ref[...].astype(BF16))
    unit = _ffn_hidden_start(j, th, hid) + lax.broadcasted_iota(jnp.int32, (1, th), 1)
    a = jnp.where(unit >= j * th, g * jax.nn.sigmoid(g) * u, 0.0).astype(BF16)
    d = o_ref.shape[-1]
    nc = min(FFN_DOWN_CHUNK, d)

    @pl.when(j == 0)
    def _():
        residual_copy().wait()

    for c0 in range(0, d, nc):
        o_ref[:, c0:c0 + nc] += gate_ref[:, c0:c0 + nc] * _dot(a, wd_ref[:, c0:c0 + nc].astype(BF16))

    @pl.when(j == pl.num_programs(1) - 1)
    def _():
        fg = fg_ref[...]

        def norm_rows(r, carry):
            rows = pl.ds(pl.multiple_of(r * FFN_NORM_ROWS, FFN_NORM_ROWS), FFN_NORM_ROWS)
            x = o_ref[rows, :]
            ms = jnp.mean(x * x, axis=-1, keepdims=True)
            o_ref[rows, :] = x * lax.rsqrt(ms + EPS) * fg
            return carry

        lax.fori_loop(0, tm // FFN_NORM_ROWS, norm_rows, 0)


FFN_DOWN_CHUNK = 512
FFN_NORM_ROWS = 64


FFN_TH = 256
FFN_VMEM_LIMIT = 60 * 1024 * 1024


def _ffn(h, wg, wu, wd, x, gate, final_g, length, tm=1024):
    n, d = h.shape
    hid = wg.shape[-1]
    bsz = gate.shape[0]
    tm, th = min(tm, length), min(FFN_TH, hid)
    assert length % tm == 0 and tm % FFN_NORM_ROWS == 0
    col_map = lambda i, j: (0, _ffn_hidden_start(j, th, hid))
    return pl.pallas_call(
        functools.partial(_ffn_kernel, hid),
        grid=(n // tm, pl.cdiv(hid, th)),
        in_specs=[pl.BlockSpec((tm, d), lambda i, j: (i, 0), pipeline_mode=pl.Buffered(1)),
                  pl.BlockSpec((pl.Element(d), pl.Element(th)), col_map),
                  pl.BlockSpec((pl.Element(d), pl.Element(th)), col_map),
                  pl.BlockSpec((pl.Element(th), pl.Element(d)), lambda i, j: (_ffn_hidden_start(j, th, hid), 0)),
                  pl.BlockSpec(memory_space=pl.ANY),
                  pl.BlockSpec((None, 1, d), lambda i, j: ((i * tm) // length, 0, 0)),
                  pl.BlockSpec((1, d), lambda i, j: (0, 0))],
        out_specs=pl.BlockSpec((tm, d), lambda i, j: (i, 0), pipeline_mode=pl.Buffered(1)),
        out_shape=jax.ShapeDtypeStruct((n, d), F32),
        scratch_shapes=[pltpu.SemaphoreType.DMA(())],
        compiler_params=_cparams("parallel", "arbitrary", vmem=FFN_VMEM_LIMIT),
        name="swiglu_ffn",
    )(h, wg, wu, wd, x, gate.reshape(bsz, 1, d), final_g.reshape(1, d))


def _dft_split(length):
    n2 = 64
    while (length // n2) % PACK_ROWS:
        n2 //= 2
    assert n2 >= SUBLANES and length % n2 == 0
    return n2


def kernel(x, c, ctx, c_ctx, ada_w, ada_b, norm1_g, norm2_g, w_in, w_out, fourier_w, s5_lam_re, s5_lam_im, s5_log_dt, s5_b_re, s5_b_im, s5_c_re, s5_c_im, s5_d, glu_w_a, glu_b_a, glu_w_b, glu_b_b, ffn_w_gate, ffn_w_up, ffn_w_down, final_g):
    bsz, length, d = x.shape
    depth = ada_w.shape[0]
    assert depth == 1, "single-layer block"
    lyr = 0
    heads, hd, _ = fourier_w.shape[1:]
    fw = heads * hd
    _, g, p, hgrp = s5_b_re.shape[1:]
    sw = g * hgrp
    assert w_in.shape[-1] == fw + sw and length % (2 * CHUNK) == 0 and ctx.shape[1] % (2 * CHUNK) == 0
    n = bsz * length

    a8 = jnp.zeros((8, d), F32).at[:bsz].set(c.astype(F32)).at[bsz].set(c_ctx.astype(F32))
    mods = _ada(jnp.concatenate([a8, a8], axis=0), ada_w[lyr], ada_b[lyr]).reshape(8, N_MOD, d)
    sh1, sc1, g1, sh2, sc2, g2 = (mods[:bsz, i] for i in range(N_MOD))
    csh1, csc1 = mods[bsz:bsz + 1, 0], mods[bsz:bsz + 1, 1]

    w_in_b = w_in[lyr].astype(BF16)
    ang = (2.0 * np.pi / hd) * ((np.arange(hd)[:, None] * np.arange(hd)[None, :]) % hd).astype(np.float64)
    cd = jnp.asarray(np.cos(ang) / math.sqrt(hd), F32)
    sd = jnp.asarray(np.sin(ang) / math.sqrt(hd), F32)
    csd = jnp.broadcast_to(jnp.stack([cd, sd])[:, None], (2, heads, hd, hd)).reshape(2 * heads, hd, hd)
    wf2 = jnp.concatenate([fourier_w[lyr], fourier_w[lyr]], axis=0).astype(F32)
    folded = _fold(csd, wf2)
    wcs = jnp.concatenate([folded[:heads], folded[heads:]], axis=-1).astype(BF16)
    k_tabs, e_mat, d_mat, dec = _s5_tables(s5_lam_re[lyr], s5_lam_im[lyr], s5_log_dt[lyr], s5_b_re[lyr],
                                         s5_b_im[lyr], s5_c_re[lyr], s5_c_im[lyr])
    sel = _toeplitz_select(CHUNK)
    dsk = jnp.repeat(s5_d[lyr].astype(F32).reshape(g, 1, hgrp), CHUNK, axis=-1)

    hc = _norm_mod(ctx, norm1_g[lyr], csh1, csc1)
    nctx_tok = bsz * ctx.shape[1]
    zc3 = _proj_t(w_in_b, fw, sw, hc.reshape(nctx_tok, d))

    hm = _norm_mod(x, norm1_g[lyr], sh1, sc1).reshape(n, d)
    n2 = _dft_split(length)
    pc4, ps4 = _four_in(hm, w_in_b, wcs, bsz, length, n2)
    g_tab, cs_tab = _dft_tables(length, n2)
    y_four = _dft(pc4, ps4, g_tab, cs_tab).reshape(n, fw)

    z3 = _proj_t(w_in_b, fw, sw, hm)
    ctx_s, lat_s = _s5_states(zc3, z3, e_mat)
    h_in = _s5_scan(dec, ctx_s, lat_s, bsz)
    g3 = _s5_out(z3, k_tabs, sel, d_mat, _carry_expanders(hgrp, CHUNK), h_in, dsk)
    y_s = _glu(g3, glu_w_a[lyr], glu_w_b[lyr], glu_b_a[lyr], glu_b_b[lyr])

    x1 = _out_proj(y_four, y_s, w_out[lyr].astype(BF16), x.reshape(n, d), g1, length)

    hm2 = _norm_mod(x1.reshape(bsz, length, d), norm2_g[lyr], sh2, sc2).reshape(n, d)
    out = _ffn(hm2, ffn_w_gate[lyr], ffn_w_up[lyr], ffn_w_down[lyr], x1, g2, final_g, length)
    return out.reshape(bsz, length, d)
```

```python
import functools
import math

import numpy as np
import jax
import jax.numpy as jnp
from jax import lax
from jax.experimental import pallas as pl
from jax.experimental.pallas import tpu as pltpu

F32 = jnp.float32
BF16 = jnp.bfloat16
EPS = 1e-6
CHUNK = 64
N_MOD = 6
V7X_VMEM_LIMIT = 56 * 1024 * 1024
HI = lax.Precision.HIGHEST
LANES = 128
SUBLANES = 8
PACK_ROWS = 16
PITCH_PAD = 8


def _cparams(*sem, vmem=V7X_VMEM_LIMIT):
    return pltpu.CompilerParams(dimension_semantics=sem, vmem_limit_bytes=vmem)


def _dot(a, b):
    return jnp.dot(a, b, preferred_element_type=F32)


def _dot_split(a, b):
    a_hi, b_hi = a.astype(BF16), b.astype(BF16)
    a_lo = (a - a_hi.astype(F32)).astype(BF16)
    b_lo = (b - b_hi.astype(F32)).astype(BF16)
    return _dot(a_hi, b_hi) + _dot(a_hi, b_lo) + _dot(a_lo, b_hi)


def _ada_kernel(a_ref, w_ref, b_ref, o_ref):
    a = a_ref[...]
    s = a * jax.nn.sigmoid(a)
    s_hi = s.astype(BF16).astype(F32)
    row = lax.broadcasted_iota(jnp.int32, s.shape, 0)
    lhs = jnp.where(row < 8, s_hi, s - s_hi).astype(BF16)
    w = w_ref[...]
    w_hi = w.astype(BF16)
    w_lo = (w - w_hi.astype(F32)).astype(BF16)
    r = _dot(lhs, w_hi) + _dot(lhs, w_lo)
    o_ref[...] = r[0:8] + r[8:16] + b_ref[...]


def _ada(a16, w, b, tn=1024):
    d, n = w.shape
    tn = min(tn, n)
    return pl.pallas_call(
        _ada_kernel,
        grid=(n // tn,),
        in_specs=[pl.BlockSpec((16, d), lambda j: (0, 0)),
                  pl.BlockSpec((d, tn), lambda j: (0, j)),
                  pl.BlockSpec((1, tn), lambda j: (0, j))],
        out_specs=pl.BlockSpec((8, tn), lambda j: (0, j)),
        out_shape=jax.ShapeDtypeStruct((8, n), F32),
        compiler_params=_cparams("parallel"),
        name="ada_matvec",
    )(a16, w, b.reshape(1, n))


def _norm_mod_kernel(x_ref, g_ref, sh_ref, sc_ref, o_ref):
    x = x_ref[...]
    ms = jnp.mean(x * x, axis=-1, keepdims=True)
    y = x * lax.rsqrt(ms + EPS) * g_ref[...]
    o_ref[...] = (y * (1.0 + sc_ref[...]) + sh_ref[...]).astype(o_ref.dtype)


def _norm_mod(x, g, sh, sc, tm=512):
    bsz, length, d = x.shape
    tm = min(tm, length)
    bm = sh.shape[0]
    mod_map = (lambda b, i: (b, 0, 0)) if bm == bsz else (lambda b, i: (0, 0, 0))
    return pl.pallas_call(
        _norm_mod_kernel,
        grid=(bsz, length // tm),
        in_specs=[pl.BlockSpec((None, tm, d), lambda b, i: (b, i, 0)),
                  pl.BlockSpec((1, d), lambda b, i: (0, 0)),
                  pl.BlockSpec((None, 1, d), mod_map),
                  pl.BlockSpec((None, 1, d), mod_map)],
        out_specs=pl.BlockSpec((None, tm, d), lambda b, i: (b, i, 0)),
        out_shape=jax.ShapeDtypeStruct((bsz, length, d), BF16),
        compiler_params=_cparams("parallel", "parallel"),
        name="norm_mod",
    )(x, g.reshape(1, d), sh.reshape(bm, 1, d), sc.reshape(bm, 1, d))


def _fold_kernel(a_ref, b_ref, o_ref):
    o_ref[...] = jnp.dot(a_ref[...], b_ref[...], preferred_element_type=F32, precision=HI)


def _fold(a, b):
    hh, m, k = a.shape
    n = b.shape[-1]
    return pl.pallas_call(
        _fold_kernel,
        grid=(hh,),
        in_specs=[pl.BlockSpec((None, m, k), lambda h: (h, 0, 0)),
                  pl.BlockSpec((None, k, n), lambda h: (h, 0, 0))],
        out_specs=pl.BlockSpec((None, m, n), lambda h: (h, 0, 0)),
        out_shape=jax.ShapeDtypeStruct((hh, m, n), F32),
        compiler_params=_cparams("parallel"),
        name="weight_fold",
    )(a, b)


def _four_in_kernel(h_ref, w_ref, wcs_ref, pc_ref, ps_ref, z_scr, zp_scr):
    n2, tn1, hd = pc_ref.shape
    pitch = n2 + PITCH_PAD
    nq = hd // LANES
    z = _dot(h_ref[...], w_ref[...])
    for q in range(nq):
        for i1 in range(tn1):
            z_scr[q, i1 * pitch:i1 * pitch + n2, :] = z[i1 * n2:(i1 + 1) * n2, q * LANES:(q + 1) * LANES]
    for j2 in range(n2):
        for q in range(nq):
            zp_scr[j2 * tn1:(j2 + 1) * tn1, q * LANES:(q + 1) * LANES] = (
                z_scr[q, pl.ds(j2, tn1, stride=pitch), :].astype(BF16))
    p = _dot(zp_scr[...], wcs_ref[...])
    p3 = p.reshape(n2, tn1, 2 * hd)
    pc_ref[...] = p3[:, :, :hd].astype(BF16)
    ps_ref[...] = p3[:, :, hd:].astype(BF16)


def _four_in(h, w_f, wcs, bsz, length, n2):
    n, d = h.shape
    heads, hd, _ = wcs.shape
    n1 = length // n2
    tn1 = PACK_ROWS
    tm = tn1 * n2
    tpb = length // tm
    fw = heads * hd
    out = jax.ShapeDtypeStruct((bsz, n2, n1, fw), BF16)
    ospec = pl.BlockSpec((None, n2, tn1, hd), lambda i, j: (i // tpb, 0, i % tpb, j))
    return pl.pallas_call(
        _four_in_kernel,
        grid=(n // tm, heads),
        in_specs=[pl.BlockSpec((tm, d), lambda i, j: (i, 0)),
                  pl.BlockSpec((d, hd), lambda i, j: (0, j)),
                  pl.BlockSpec((None, hd, 2 * hd), lambda i, j: (j, 0, 0))],
        out_specs=[ospec, ospec],
        out_shape=[out, out],
        scratch_shapes=[pltpu.VMEM((hd // LANES, tn1 * (n2 + PITCH_PAD), LANES), F32), pltpu.VMEM((tm, hd), BF16)],
        compiler_params=_cparams("parallel", "arbitrary"),
        name="fourier_in_proj",
    )(h, w_f, wcs)


def _dft_tables(length, n2):
    n1 = length // n2
    k1 = np.arange(n1)[:, None, None]
    i1 = np.arange(n1)[None, :, None]
    i2 = np.arange(n2)[None, None, :]
    phase = (k1 * (n2 * i1 + i2)) % length
    phi = (2.0 * np.pi / length) * phase.astype(np.float64)
    c = np.cos(phi).transpose(2, 0, 1) / math.sqrt(n1)
    s = np.sin(phi).transpose(2, 0, 1) / math.sqrt(n1)
    g = np.concatenate([np.concatenate([c, -s], axis=2),
                        np.concatenate([-s, -c], axis=2)], axis=1)
    k2 = np.arange(n2)[:, None]
    j2 = np.arange(n2)[None, :]
    th = (2.0 * np.pi / n2) * ((k2 * j2) % n2).astype(np.float64)
    cs = np.concatenate([np.cos(th), np.sin(th)], axis=1) / math.sqrt(n2)
    return jnp.asarray(g, dtype=BF16), jnp.asarray(cs, dtype=BF16)


def _dft_kernel(g_ref, cs_ref, pc_ref, ps_ref, o_ref, t_scr, y_scr):
    n2, n1, _ = pc_ref.shape
    m = 2 * n1
    p1 = m + PITCH_PAD
    p2 = n1 + PITCH_PAD

    def stage1(j2, carry):
        rhs = jnp.concatenate([pc_ref[j2], ps_ref[j2]], axis=0)
        t_scr[pl.ds(pl.multiple_of(j2 * p1, SUBLANES), m), :] = _dot(g_ref[j2], rhs)
        return carry

    lax.fori_loop(0, n2, stage1, 0, unroll=True)

    cs = cs_ref[...]

    def stage2(i, carry):
        k1 = 2 * i
        cols = []
        for dk in range(2):
            re = t_scr[pl.ds(k1 + dk, n2, stride=p1), :]
            im = t_scr[pl.ds(n1 + k1 + dk, n2, stride=p1), :]
            cols.append(jnp.concatenate([re, im], axis=0).astype(BF16))
        res = _dot(cs, jnp.concatenate(cols, axis=1))
        for dk in range(2):
            y_scr[pl.ds(k1 + dk, n2, stride=p2), :] = res[:, dk * LANES:(dk + 1) * LANES]
        return carry

    lax.fori_loop(0, n1 // 2, stage2, 0, unroll=True)
    for k2 in range(n2):
        o_ref[k2 * n1:(k2 + 1) * n1, :] = y_scr[k2 * p2:k2 * p2 + n1, :].astype(o_ref.dtype)


def _dft(pc4, ps4, g, cs):
    bsz, n2, n1, w = pc4.shape
    m = 2 * n1
    ispec = pl.BlockSpec((None, n2, n1, LANES), lambda b, j: (b, 0, 0, j))
    return pl.pallas_call(
        _dft_kernel,
        grid=(bsz, w // LANES),
        in_specs=[pl.BlockSpec((n2, m, m), lambda b, j: (0, 0, 0), pipeline_mode=pl.Buffered(1)),
                  pl.BlockSpec((n2, 2 * n2), lambda b, j: (0, 0)),
                  ispec, ispec],
        out_specs=pl.BlockSpec((None, n2 * n1, LANES), lambda b, j: (b, 0, j)),
        out_shape=jax.ShapeDtypeStruct((bsz, n2 * n1, w), BF16),
        scratch_shapes=[pltpu.VMEM((n2 * (m + PITCH_PAD), LANES), F32),
                        pltpu.VMEM((n2 * (n1 + PITCH_PAD), LANES), F32)],
        compiler_params=_cparams("parallel", "parallel"),
        name="position_dft",
    )(g, cs, pc4, ps4)


def _nt_kernel(w_ref, h_ref, o_ref, stage_ref):
    acc = lax.dot_general(w_ref[...], h_ref[...], (((0,), (1,)), ((), ())), preferred_element_type=F32)
    rows, tq, _ = o_ref.shape
    if tq % SUBLANES == 0:
        for q in range(tq):
            stage_ref[pl.ds(q, rows, stride=tq), :] = acc[:, q * LANES:(q + 1) * LANES]
        o_ref[...] = stage_ref[...].reshape(rows, tq, LANES)
    else:
        for q in range(tq):
            o_ref[:, q, :] = acc[:, q * LANES:(q + 1) * LANES]


def _proj_t(w, col0, c, h, tmc=1024, tn=1024):
    d = w.shape[0]
    n = h.shape[0]
    tmc, tn = min(tmc, c), min(tn, n)
    assert col0 % tmc == 0 and c % tmc == 0
    return pl.pallas_call(
        _nt_kernel,
        grid=(n // tn, c // tmc),
        in_specs=[pl.BlockSpec((d, tmc), lambda i, j: (0, col0 // tmc + j)),
                  pl.BlockSpec((tn, d), lambda i, j: (i, 0))],
        out_specs=pl.BlockSpec((tmc, tn // LANES, LANES), lambda i, j: (j, i, 0)),
        out_shape=jax.ShapeDtypeStruct((c, n // LANES, LANES), F32),
        scratch_shapes=[pltpu.VMEM((tmc * (tn // LANES), LANES), F32)],
        compiler_params=_cparams("parallel", "arbitrary"),
        name="s5_in_proj_t",
    )(w, h)


def _s5_tables(lam_re, lam_im, log_dt, b_re, b_im, c_re, c_im):
    t = CHUNK
    _, g, p = lam_re.shape
    h = b_re.shape[-1]
    dt = jnp.exp(log_dt.astype(F32))[..., None]
    lr, li = jnp.minimum(lam_re.astype(F32), -1e-4), lam_im.astype(F32)
    ar, ai = lr * dt, li * dt

    lbm = jnp.exp(ar)
    lbr, lbi = lbm * jnp.cos(ai), lbm * jnp.sin(ai)

    ks = jnp.arange(2 * t, dtype=F32)
    tab_m = jnp.exp(ar[..., None] * ks)
    tab_r, tab_i = tab_m * jnp.cos(ai[..., None] * ks), tab_m * jnp.sin(ai[..., None] * ks)

    def powers(d, first, step, steps_major=False):
        first = int(first)
        if step > 0:
            out = tab_r[d, :, :, first:first + t], tab_i[d, :, :, first:first + t]
        else:
            out = (jnp.flip(tab_r[d, :, :, first - t + 1:first + 1], -1),
                   jnp.flip(tab_i[d, :, :, first - t + 1:first + 1], -1))
        return tuple(jnp.swapaxes(a, 1, 2) for a in out) if steps_major else out

    nr, ni = lbr - 1.0, lbi
    den = lr * lr + li * li
    qr, qi = (nr * lr + ni * li) / den, (ni * lr - nr * li) / den
    br, bi = (jnp.swapaxes(a.astype(F32), -1, -2) for a in (b_re, b_im))
    bbr = qr[:, :, None] * br - qi[:, :, None] * bi
    bbi = qr[:, :, None] * bi + qi[:, :, None] * br
    cr, ci = c_re.astype(F32), c_im.astype(F32)

    kb_a = jnp.concatenate([bbr[0], bbr[0], bbr[1], bbr[1]], axis=-1)
    kb_b = jnp.concatenate([-bbi[0], bbi[0], -bbi[1], bbi[1]], axis=-1)
    kc_a = jnp.concatenate([cr[0], ci[0], cr[1], ci[1]], axis=-1)
    kc_b = jnp.concatenate([ci[0], cr[0], ci[1], cr[1]], axis=-1)
    wfr, wfi = powers(0, 0.0, 1.0)
    wbr, wbi = powers(1, t - 1.0, -1.0)
    fwd = lambda a: jnp.pad(a, ((0, 0), (0, 0), (t - 1, 1)))
    bwd = lambda a: jnp.pad(a, ((0, 0), (0, 0), (0, t)))
    kw = jnp.concatenate([fwd(wfr), fwd(-wfi), bwd(wbr), bwd(-wbi)], axis=1)

    pfr, pfi = powers(0, t - 1.0, -1.0, steps_major=True)
    pbr, pbi = powers(1, 0.0, 1.0, steps_major=True)
    ew_a = jnp.concatenate([pfr, pfr, pbr, pbr], axis=-1)
    ew_b = jnp.concatenate([pfi, pfi, pbi, pbi], axis=-1)
    eb_a = jnp.concatenate([bbr[0], bbi[0], bbr[1], bbi[1]], axis=-1)
    eb_b = jnp.concatenate([-bbi[0], bbr[0], -bbi[1], bbr[1]], axis=-1)

    qfr, qfi = powers(0, 1.0, 1.0)
    qbr, qbi = powers(1, float(t), -1.0)
    crf, cif, crb, cib = (jnp.swapaxes(a, 1, 2) for a in (cr[0], ci[0], cr[1], ci[1]))
    dc = jnp.concatenate([jnp.concatenate([crf, -crf, crb, -crb], axis=1),
                          jnp.concatenate([-cif, -cif, -cib, -cib], axis=1)], axis=-1)
    dq = jnp.concatenate([jnp.concatenate([qfr, qfi, qbr, qbi], axis=1),
                          jnp.concatenate([qfi, qfr, qbi, qbr], axis=1)], axis=-1)

    dec = jnp.stack([tab_r[0, :, :, t], tab_i[0, :, :, t],
                     tab_r[1, :, :, t], tab_i[1, :, :, t]]).reshape(4, g * p)
    return (kb_a, kb_b, kc_a, kc_b, kw), (ew_a, ew_b, eb_a, eb_b), (dc.astype(BF16), dq.astype(BF16)), dec


def _carry_expanders(h, t):
    ht = h * t
    rep = np.zeros((2 * h, 2 * ht), np.float32)
    til = np.zeros((2 * t, 2 * ht), np.float32)
    for half in range(2):
        for ho in range(h):
            for tt in range(t):
                rep[half * h + ho, half * ht + ho * t + tt] = 1.0
                til[half * t + tt, half * ht + ho * t + tt] = 1.0
    return jnp.asarray(rep, dtype=BF16), jnp.asarray(til, dtype=BF16)


def _toeplitz_select(t):
    sub = SUBLANES
    sel = np.zeros((4 * t, (t // sub) * 2 * t), np.float32)
    for q in range(t // sub):
        for tt in range(t):
            j = tt - sub * q + t - sub
            sel[j, q * 2 * t + tt] = 1.0
            sel[2 * t + j, q * 2 * t + t + tt] = 1.0
    return jnp.asarray(sel, dtype=BF16)


GPS = 4


def _gather_chunks(z_ref, gi, h):
    t = z_ref.shape[-1] // 2
    lo = lax.broadcasted_iota(jnp.int32, z_ref.shape[1:], 1) < t
    ev, od = [], []
    for k in range(0, h, 2):
        za, zb = z_ref[gi * h + k], z_ref[gi * h + k + 1]
        ev.append(jnp.where(lo, za, pltpu.roll(zb, t, axis=1)))
        od.append(jnp.where(lo, pltpu.roll(za, t, axis=1), zb))
    return jnp.concatenate([jnp.concatenate(ev, axis=1), jnp.concatenate(od, axis=1)], axis=0)


def _s5_state_kernel(zc_ref, z_ref, ewa_ref, ewb_ref, eba_ref, ebb_ref, *refs):
    couts, louts, e_scr = refs[0:4], refs[4:8], refs[8]
    h = z_ref.shape[0] // GPS
    t = ewa_ref.shape[1]
    p = ewa_ref.shape[-1] // 4
    for gi in range(GPS):
        wa, wb = ewa_ref[gi], ewb_ref[gi]
        for k in range(h):
            e_scr[k * t:(k + 1) * t, :] = (wa * eba_ref[gi, k:k + 1, :] + wb * ebb_ref[gi, k:k + 1, :]).astype(BF16)
        for src, outs in ((zc_ref, couts), (z_ref, louts)):
            a2 = _gather_chunks(src, gi, h).astype(BF16)
            s = _dot(a2, e_scr[...])
            for k in range(4):
                outs[k][:, gi * p:(gi + 1) * p] = s[:, k * p:(k + 1) * p]


def _s5_states(zc3, z3, e_tabs):
    gh, npair, t2 = z3.shape
    cpair = zc3.shape[1]
    t = t2 // 2
    g, _, p4 = e_tabs[0].shape
    h, p = gh // g, p4 // 4
    wspec = pl.BlockSpec((GPS, t, p4), lambda i: (i, 0, 0))
    bspec = pl.BlockSpec((GPS, h, p4), lambda i: (i, 0, 0))
    res = pl.pallas_call(
        _s5_state_kernel,
        grid=(g // GPS,),
        in_specs=[pl.BlockSpec((GPS * h, cpair, t2), lambda i: (i, 0, 0)),
                  pl.BlockSpec((GPS * h, npair, t2), lambda i: (i, 0, 0)), wspec, wspec, bspec, bspec],
        out_specs=[pl.BlockSpec((2 * cpair, GPS * p), lambda i: (0, i))] * 4 +
                  [pl.BlockSpec((2 * npair, GPS * p), lambda i: (0, i))] * 4,
        out_shape=[jax.ShapeDtypeStruct((2 * cpair, g * p), F32)] * 4 +
                  [jax.ShapeDtypeStruct((2 * npair, g * p), F32)] * 4,
        scratch_shapes=[pltpu.VMEM((h * t, p4), BF16)],
        compiler_params=_cparams("parallel"),
        name="s5_chunk_states",
    )(zc3, z3, *e_tabs)
    return res[:4], res[4:]


def _s5_scan_kernel(bsz, dec_ref, cfre, cfim, cbre, cbim, sfre, sfim, sbre, sbim,
                    hfre, hfim, hbre, hbim):
    nctx = cfre.shape[0] // bsz
    nlat = sfre.shape[0] // bsz
    width = dec_ref.shape[-1]
    fr, fi = dec_ref[0:1, :], dec_ref[1:2, :]
    br, bi = dec_ref[2:3, :], dec_ref[3:4, :]
    zero = jnp.zeros((1, width), F32)

    def step(ar, ai, hr, hi, sr, si):
        return ar * hr - ai * hi + sr, ar * hi + ai * hr + si

    def row(nchunk, b, j):
        return (j % 2) * (bsz * nchunk // 2) + b * (nchunk // 2) + j // 2

    init = []
    for b in range(bsz):
        hr, hi = zero, zero
        for j in range(nctx):
            r = row(nctx, b, j)
            hr, hi = step(fr, fi, hr, hi, cfre[r:r + 1, :], cfim[r:r + 1, :])
        init += [hr, hi]
        hr, hi = zero, zero
        for j in range(nctx - 1, -1, -1):
            r = row(nctx, b, j)
            hr, hi = step(br, bi, hr, hi, cbre[r:r + 1, :], cbim[r:r + 1, :])
        init += [hr, hi]

    def body(k, carry):
        out = []
        for b in range(bsz):
            hr, hi, gr, gi = carry[4 * b:4 * b + 4]
            rf = row(nlat, b, k)
            hfre[pl.ds(rf, 1), :] = hr
            hfim[pl.ds(rf, 1), :] = hi
            out += list(step(fr, fi, hr, hi, sfre[pl.ds(rf, 1), :], sfim[pl.ds(rf, 1), :]))
            rb = row(nlat, b, nlat - 1 - k)
            hbre[pl.ds(rb, 1), :] = gr
            hbim[pl.ds(rb, 1), :] = gi
            out += list(step(br, bi, gr, gi, sbre[pl.ds(rb, 1), :], sbim[pl.ds(rb, 1), :]))
        return tuple(out)

    lax.fori_loop(0, nlat, body, tuple(init), unroll=2)


def _s5_scan(dec, ctx_s, lat_s, bsz, tw=1024):
    gp = dec.shape[-1]
    rc, rl = ctx_s[0].shape[0], lat_s[0].shape[0]
    assert (rc // bsz) % 2 == 0 and (rl // bsz) % 2 == 0
    tw = min(tw, gp)
    cspec = pl.BlockSpec((rc, tw), lambda i: (0, i))
    lspec = pl.BlockSpec((rl, tw), lambda i: (0, i))
    out = jax.ShapeDtypeStruct((rl, gp), F32)
    return pl.pallas_call(
        functools.partial(_s5_scan_kernel, bsz),
        grid=(gp // tw,),
        in_specs=[pl.BlockSpec((4, tw), lambda i: (0, i))] + [cspec] * 4 + [lspec] * 4,
        out_specs=[lspec] * 4,
        out_shape=[out] * 4,
        compiler_params=_cparams("parallel"),
        name="s5_state_scan",
    )(dec, *ctx_s, *lat_s)


def _gelu_tanh(x):
    return 0.5 * x * (1.0 + jnp.tanh(math.sqrt(2.0 / math.pi) * (x + 0.044715 * (x * x * x))))


def _expand_toeplitz(k_ref, sel_ref, lhs_ref, res_ref, w_ref, h, t):
    half = h // 2
    sub = SUBLANES
    rows_per_hi = half * sub

    def fill(hi, carry):
        for hp in range(half):
            for par in range(2):
                v = k_ref[pl.ds(hi * h + 2 * hp + par, 1), :]
                b = pltpu.roll(jnp.broadcast_to(v, (sub, 2 * t)), 2 * t - (sub - 1), axis=1, stride=1, stride_axis=0)
                rows = pl.ds(pl.multiple_of(hi * rows_per_hi + hp * sub, sub), sub)
                lhs_ref[rows, par * 2 * t:(par + 1) * 2 * t] = b
        return carry

    lax.fori_loop(0, h, fill, 0, unroll=True)
    res_ref[...] = _dot(lhs_ref[...].astype(BF16), sel_ref[...])

    def shuffle(hi, carry):
        base = pl.multiple_of(hi * rows_per_hi, rows_per_hi)
        r_hi = res_ref[pl.ds(base, rows_per_hi), :]
        rows = [jnp.concatenate([r_hi[hp * sub:(hp + 1) * sub, q * 2 * t:(q + 1) * 2 * t] for hp in range(half)],
                                axis=1)
                for q in range(t // sub)]
        w_ref[pl.ds(base, t), :] = jnp.concatenate(rows, axis=0).astype(BF16)
        return carry

    lax.fori_loop(0, h, shuffle, 0, unroll=True)


def _s5_out_kernel(z_ref, kba_ref, kbb_ref, kca_ref, kcb_ref, kw_ref, sel_ref, dc_ref, dq_ref, rep_ref, til_ref,
                   hfre, hfim, hbre, hbim, dsk_ref, o_ref, km_scr, k_scr, lhs_ref, res_ref, w_ref):
    h = z_ref.shape[0] // GPS
    npair = z_ref.shape[1]
    t = z_ref.shape[-1] // 2
    ht = h * t
    p = dc_ref.shape[1] // 4
    lo = lax.broadcasted_iota(jnp.int32, z_ref.shape[1:], 1) < t
    for gi in range(GPS):
        for hi in range(h):
            km_scr[hi * h:(hi + 1) * h, :] = (kba_ref[gi, hi:hi + 1, :] * kca_ref[gi]
                                              + kbb_ref[gi, hi:hi + 1, :] * kcb_ref[gi])
        k_scr[...] = _dot_split(km_scr[...], kw_ref[gi])
        _expand_toeplitz(k_scr, sel_ref, lhs_ref, res_ref, w_ref, h, t)
        ce = _dot(dc_ref[gi], rep_ref[...])
        qe = _dot(dq_ref[gi], til_ref[...])
        dmat = (ce[:, :ht] * qe[:, :ht] + ce[:, ht:] * qe[:, ht:]).astype(BF16)
        u = _gather_chunks(z_ref, gi, h)
        sl = slice(gi * p, (gi + 1) * p)
        hp = jnp.concatenate([hfre[:, sl], hfim[:, sl], hbre[:, sl], hbim[:, sl]], axis=-1).astype(BF16)
        y = _dot(u.astype(BF16), w_ref[...]) + _dot(hp, dmat)
        gl = _gelu_tanh(y + u * dsk_ref[gi])
        for k in range(0, h, 2):
            te = gl[0:npair, k * t:(k + 2) * t]
            to = gl[npair:2 * npair, k * t:(k + 2) * t]
            o_ref[gi * h + k] = jnp.where(lo, te, pltpu.roll(to, t, axis=1))
            o_ref[gi * h + k + 1] = jnp.where(lo, pltpu.roll(te, t, axis=1), to)


def _s5_out(z3, k_tabs, sel, d_tabs, expanders, hin, dsk):
    gh, npair, t2z = z3.shape
    nchunk, t = 2 * npair, t2z // 2
    kb_a, kb_b, kc_a, kc_b, kw = k_tabs
    g, kp, t2 = kw.shape
    h = gh // g
    hh, ht = h * h, h * t
    assert t2 == 2 * t and (h // 2) * SUBLANES == t and t2 == LANES
    dc, dq = d_tabs
    rep, til = expanders
    p4 = dc.shape[1]
    p = p4 // 4
    hspec = pl.BlockSpec((nchunk, GPS * p), lambda i: (0, i))
    const = lambda a: pl.BlockSpec(a.shape, lambda i: (0, 0))
    return pl.pallas_call(
        _s5_out_kernel,
        grid=(g // GPS,),
        in_specs=[pl.BlockSpec((GPS * h, npair, t2z), lambda i: (i, 0, 0)),
                  ] + [pl.BlockSpec((GPS, h, kp), lambda i: (i, 0, 0))] * 4 + [
                  pl.BlockSpec((GPS, kp, t2), lambda i: (i, 0, 0)),
                  const(sel),
                  pl.BlockSpec((GPS, p4, 2 * h), lambda i: (i, 0, 0)),
                  pl.BlockSpec((GPS, p4, 2 * t), lambda i: (i, 0, 0)),
                  const(rep), const(til)] + [hspec] * 4 +
                 [pl.BlockSpec((GPS, 1, ht), lambda i: (i, 0, 0))],
        out_specs=pl.BlockSpec((GPS * h, npair, t2z), lambda i: (i, 0, 0)),
        out_shape=jax.ShapeDtypeStruct((gh, npair, t2z), F32),
        scratch_shapes=[pltpu.VMEM((hh, kp), F32), pltpu.VMEM((hh, t2), F32), pltpu.VMEM((ht, 4 * t), F32),
                        pltpu.VMEM((ht, ht), F32), pltpu.VMEM((ht, ht), BF16)],
        compiler_params=_cparams("parallel"),
        name="s5_chunk_out",
    )(z3, kb_a, kb_b, kc_a, kc_b, kw, sel, dc, dq, rep, til, *hin, dsk)


def _glu_kernel(gt_ref, wa_ref, wb_ref, ba_ref, bb_ref, o_ref, g_scr, stage_ref):
    @pl.when(pl.program_id(1) == 0)
    def _():
        c, tq, _ = gt_ref.shape
        half = stage_ref.shape[0] // tq
        for c0 in range(0, c, half):
            stage_ref[...] = gt_ref[c0:c0 + half].reshape(half * tq, LANES)
            for q in range(tq):
                g_scr[c0:c0 + half, q * LANES:(q + 1) * LANES] = stage_ref[pl.ds(q, half, stride=tq), :].astype(BF16)

    dn = (((0,), (0,)), ((), ()))
    gt = g_scr[...]
    a = lax.dot_general(gt, wa_ref[...].astype(BF16), dn, preferred_element_type=F32) + ba_ref[...]
    b = lax.dot_general(gt, wb_ref[...].astype(BF16), dn, preferred_element_type=F32) + bb_ref[...]
    o_ref[...] = (a * jax.nn.sigmoid(b)).astype(o_ref.dtype)


def _glu(g3, wa, wb, ba, bb, tm=1024, tn=512):
    c, nq, _ = g3.shape
    n = nq * LANES
    co = wa.shape[-1]
    tm, tn = min(tm, n), min(tn, co)
    stage_rows = (c // 2) * (tm // LANES)
    return pl.pallas_call(
        _glu_kernel,
        grid=(n // tm, co // tn),
        in_specs=[pl.BlockSpec((c, tm // LANES, LANES), lambda i, j: (0, i, 0)),
                  pl.BlockSpec((c, tn), lambda i, j: (0, j)),
                  pl.BlockSpec((c, tn), lambda i, j: (0, j)),
                  pl.BlockSpec((1, tn), lambda i, j: (0, j)),
                  pl.BlockSpec((1, tn), lambda i, j: (0, j))],
        out_specs=pl.BlockSpec((tm, tn), lambda i, j: (i, j)),
        out_shape=jax.ShapeDtypeStruct((n, co), BF16),
        scratch_shapes=[pltpu.VMEM((c, tm), BF16), pltpu.VMEM((stage_rows, LANES), F32)],
        compiler_params=_cparams("parallel", "arbitrary"),
        name="s5_glu",
    )(g3, wa, wb, ba.reshape(1, co), bb.reshape(1, co))


def _out_proj_kernel(yf_ref, ys_ref, wf_ref, ws_ref, x_ref, g_ref, o_ref):
    acc = _dot(yf_ref[...], wf_ref[...]) + _dot(ys_ref[...], ws_ref[...])
    o_ref[...] = x_ref[...] + g_ref[...] * acc


def _out_proj(yf, ys, w_out, x2, gate, length, tm=1024, tn=1024):
    n, fw = yf.shape
    sw = ys.shape[-1]
    d = w_out.shape[-1]
    tm, tn = min(tm, length), min(tn, d)
    assert fw % sw == 0
    bsz = gate.shape[0]
    return pl.pallas_call(
        _out_proj_kernel,
        grid=(n // tm, d // tn),
        in_specs=[pl.BlockSpec((tm, fw), lambda i, j: (i, 0)),
                  pl.BlockSpec((tm, sw), lambda i, j: (i, 0)),
                  pl.BlockSpec((fw, tn), lambda i, j: (0, j)),
                  pl.BlockSpec((sw, tn), lambda i, j: (fw // sw, j)),
                  pl.BlockSpec((tm, tn), lambda i, j: (i, j)),
                  pl.BlockSpec((None, 1, tn), lambda i, j: ((i * tm) // length, 0, j))],
        out_specs=pl.BlockSpec((tm, tn), lambda i, j: (i, j)),
        out_shape=jax.ShapeDtypeStruct((n, d), F32),
        compiler_params=_cparams("parallel", "arbitrary"),
        name="out_proj_residual",
    )(yf, ys, w_out, w_out, x2, gate.reshape(bsz, 1, d))


def _ffn_hidden_start(j, th, hid):
    return pl.multiple_of(jnp.minimum(j * th, hid - th), math.gcd(th, hid - th))


def _ffn_kernel(hid, h_ref, wg_ref, wu_ref, wd_ref, x_hbm, gate_ref, fg_ref, o_hbm, o_ref, x_sem, o_sem):
    i, j = pl.program_id(0), pl.program_id(1)
    tm = o_ref.shape[0]
    th = wg_ref.shape[-1]
    row0 = pl.multiple_of(i * tm, tm)

    def residual_copy():
        return pltpu.make_async_copy(x_hbm.at[pl.ds(row0, tm), :], o_ref, x_sem)

    def output_copy(r):
        rows = pl.ds(pl.multiple_of(r * FFN_NORM_ROWS, FFN_NORM_ROWS), FFN_NORM_ROWS)
        dst = pl.ds(pl.multiple_of(row0 + r * FFN_NORM_ROWS, FFN_NORM_ROWS), FFN_NORM_ROWS)
        return pltpu.make_async_copy(o_ref.at[rows, :], o_hbm.at[dst, :], o_sem.at[r])

    @pl.when(j == 0)
    def _():
        residual_copy().start()

    hh = h_ref[...]
    g = _dot(hh, wg_ref[...].astype(BF16))
    u = _dot(hh, wu_ref[...].astype(BF16))
    unit = _ffn_hidden_start(j, th, hid) + lax.broadcasted_iota(jnp.int32, (1, th), 1)
    a = jnp.where(unit >= j * th, g * jax.nn.sigmoid(g) * u, 0.0).astype(BF16)
    d = o_ref.shape[-1]
    nc = min(FFN_DOWN_CHUNK, d)

    @pl.when(j == 0)
    def _():
        residual_copy().wait()

    for c0 in range(0, d, nc):
        o_ref[:, c0:c0 + nc] += gate_ref[:, c0:c0 + nc] * _dot(a, wd_ref[:, c0:c0 + nc].astype(BF16))

    @pl.when(j == pl.num_programs(1) - 1)
    def _():
        fg = fg_ref[...]

        def norm_rows(r, carry):
            rows = pl.ds(pl.multiple_of(r * FFN_NORM_ROWS, FFN_NORM_ROWS), FFN_NORM_ROWS)
            x = o_ref[rows, :]
            ms = jnp.mean(x * x, axis=-1, keepdims=True)
            o_ref[rows, :] = x * lax.rsqrt(ms + EPS) * fg
            output_copy(r).start()
            return carry

        lax.fori_loop(0, tm // FFN_NORM_ROWS, norm_rows, 0)

        def drain(r, carry):
            output_copy(r).wait()
            return carry

        lax.fori_loop(0, tm // FFN_NORM_ROWS, drain, 0)


FFN_DOWN_CHUNK = 512
FFN_NORM_ROWS = 64


FFN_TH = 256
FFN_VMEM_LIMIT = 60 * 1024 * 1024


def _ffn(h, wg, wu, wd, x, gate, final_g, length, tm=1024):
    n, d = h.shape
    hid = wg.shape[-1]
    bsz = gate.shape[0]
    tm, th = min(tm, length), min(FFN_TH, hid)
    assert length % tm == 0 and tm % FFN_NORM_ROWS == 0
    col_map = lambda i, j: (0, _ffn_hidden_start(j, th, hid))
    return pl.pallas_call(
        functools.partial(_ffn_kernel, hid),
        grid=(n // tm, pl.cdiv(hid, th)),
        in_specs=[pl.BlockSpec((tm, d), lambda i, j: (i, 0)),
                  pl.BlockSpec((pl.Element(d), pl.Element(th)), col_map),
                  pl.BlockSpec((pl.Element(d), pl.Element(th)), col_map),
                  pl.BlockSpec((pl.Element(th), pl.Element(d)), lambda i, j: (_ffn_hidden_start(j, th, hid), 0)),
                  pl.BlockSpec(memory_space=pl.ANY),
                  pl.BlockSpec((None, 1, d), lambda i, j: ((i * tm) // length, 0, 0)),
                  pl.BlockSpec((1, d), lambda i, j: (0, 0))],
        out_specs=pl.BlockSpec(memory_space=pl.ANY),
        out_shape=jax.ShapeDtypeStruct((n, d), F32),
        scratch_shapes=[pltpu.VMEM((tm, d), F32), pltpu.SemaphoreType.DMA(()),
                        pltpu.SemaphoreType.DMA((tm // FFN_NORM_ROWS,))],
        compiler_params=_cparams("parallel", "arbitrary", vmem=FFN_VMEM_LIMIT),
        name="swiglu_ffn",
    )(h, wg, wu, wd, x, gate.reshape(bsz, 1, d), final_g.reshape(1, d))


def _dft_split(length):
    n2 = 64
    while (length // n2) % PACK_ROWS:
        n2 //= 2
    assert n2 >= SUBLANES and length % n2 == 0
    return n2


def kernel(x, c, ctx, c_ctx, ada_w, ada_b, norm1_g, norm2_g, w_in, w_out, fourier_w, s5_lam_re, s5_lam_im, s5_log_dt, s5_b_re, s5_b_im, s5_c_re, s5_c_im, s5_d, glu_w_a, glu_b_a, glu_w_b, glu_b_b, ffn_w_gate, ffn_w_up, ffn_w_down, final_g):
    bsz, length, d = x.shape
    depth = ada_w.shape[0]
    assert depth == 1, "single-layer block"
    lyr = 0
    heads, hd, _ = fourier_w.shape[1:]
    fw = heads * hd
    _, g, p, hgrp = s5_b_re.shape[1:]
    sw = g * hgrp
    assert w_in.shape[-1] == fw + sw and length % (2 * CHUNK) == 0 and ctx.shape[1] % (2 * CHUNK) == 0
    n = bsz * length

    a8 = jnp.zeros((8, d), F32).at[:bsz].set(c.astype(F32)).at[bsz].set(c_ctx.astype(F32))
    mods = _ada(jnp.concatenate([a8, a8], axis=0), ada_w[lyr], ada_b[lyr]).reshape(8, N_MOD, d)
    sh1, sc1, g1, sh2, sc2, g2 = (mods[:bsz, i] for i in range(N_MOD))
    csh1, csc1 = mods[bsz:bsz + 1, 0], mods[bsz:bsz + 1, 1]

    w_in_b = w_in[lyr].astype(BF16)
    ang = (2.0 * np.pi / hd) * ((np.arange(hd)[:, None] * np.arange(hd)[None, :]) % hd).astype(np.float64)
    cd = jnp.asarray(np.cos(ang) / math.sqrt(hd), F32)
    sd = jnp.asarray(np.sin(ang) / math.sqrt(hd), F32)
    csd = jnp.broadcast_to(jnp.stack([cd, sd])[:, None], (2, heads, hd, hd)).reshape(2 * heads, hd, hd)
    wf2 = jnp.concatenate([fourier_w[lyr], fourier_w[lyr]], axis=0).astype(F32)
    folded = _fold(csd, wf2)
    wcs = jnp.concatenate([folded[:heads], folded[heads:]], axis=-1).astype(BF16)
    k_tabs, e_mat, d_mat, dec = _s5_tables(s5_lam_re[lyr], s5_lam_im[lyr], s5_log_dt[lyr], s5_b_re[lyr],
                                         s5_b_im[lyr], s5_c_re[lyr], s5_c_im[lyr])
    sel = _toeplitz_select(CHUNK)
    dsk = jnp.repeat(s5_d[lyr].astype(F32).reshape(g, 1, hgrp), CHUNK, axis=-1)

    hc = _norm_mod(ctx, norm1_g[lyr], csh1, csc1)
    nctx_tok = bsz * ctx.shape[1]
    zc3 = _proj_t(w_in_b, fw, sw, hc.reshape(nctx_tok, d))

    hm = _norm_mod(x, norm1_g[lyr], sh1, sc1).reshape(n, d)
    n2 = _dft_split(length)
    pc4, ps4 = _four_in(hm, w_in_b, wcs, bsz, length, n2)
    g_tab, cs_tab = _dft_tables(length, n2)
    y_four = _dft(pc4, ps4, g_tab, cs_tab).reshape(n, fw)

    z3 = _proj_t(w_in_b, fw, sw, hm)
    ctx_s, lat_s = _s5_states(zc3, z3, e_mat)
    h_in = _s5_scan(dec, ctx_s, lat_s, bsz)
    g3 = _s5_out(z3, k_tabs, sel, d_mat, _carry_expanders(hgrp, CHUNK), h_in, dsk)
    y_s = _glu(g3, glu_w_a[lyr], glu_w_b[lyr], glu_b_a[lyr], glu_b_b[lyr])

    x1 = _out_proj(y_four, y_s, w_out[lyr].astype(BF16), x.reshape(n, d), g1, length)

    hm2 = _norm_mod(x1.reshape(bsz, length, d), norm2_g[lyr], sh2, sc2).reshape(n, d)
    out = _ffn(hm2, ffn_w_gate[lyr], ffn_w_up[lyr], ffn_w_down[lyr], x1, g2, final_g, length)
    return out.reshape(bsz, length, d)
```

```python
import functools
import math

import numpy as np
import jax
import jax.numpy as jnp
from jax import lax
from jax.experimental import pallas as pl
from jax.experimental.pallas import tpu as pltpu

F32 = jnp.float32
BF16 = jnp.bfloat16
EPS = 1e-6
CHUNK = 64
N_MOD = 6
V7X_VMEM_LIMIT = 56 * 1024 * 1024
HI = lax.Precision.HIGHEST
LANES = 128
SUBLANES = 8
PACK_ROWS = 16
PITCH_PAD = 8


def _cparams(*sem, vmem=V7X_VMEM_LIMIT):
    return pltpu.CompilerParams(dimension_semantics=sem, vmem_limit_bytes=vmem)


def _dot(a, b):
    return jnp.dot(a, b, preferred_element_type=F32)


def _dot_split(a, b):
    a_hi, b_hi = a.astype(BF16), b.astype(BF16)
    a_lo = (a - a_hi.astype(F32)).astype(BF16)
    b_lo = (b - b_hi.astype(F32)).astype(BF16)
    return _dot(a_hi, b_hi) + _dot(a_hi, b_lo) + _dot(a_lo, b_hi)


def _ada_kernel(a_ref, w_ref, b_ref, o_ref):
    a = a_ref[...]
    s = a * jax.nn.sigmoid(a)
    s_hi = s.astype(BF16).astype(F32)
    row = lax.broadcasted_iota(jnp.int32, s.shape, 0)
    lhs = jnp.where(row < 8, s_hi, s - s_hi).astype(BF16)
    w = w_ref[...]
    w_hi = w.astype(BF16)
    w_lo = (w - w_hi.astype(F32)).astype(BF16)
    r = _dot(lhs, w_hi) + _dot(lhs, w_lo)
    o_ref[...] = r[0:8] + r[8:16] + b_ref[...]


def _ada(a16, w, b, tn=1024):
    d, n = w.shape
    tn = min(tn, n)
    return pl.pallas_call(
        _ada_kernel,
        grid=(n // tn,),
        in_specs=[pl.BlockSpec((16, d), lambda j: (0, 0)),
                  pl.BlockSpec((d, tn), lambda j: (0, j)),
                  pl.BlockSpec((1, tn), lambda j: (0, j))],
        out_specs=pl.BlockSpec((8, tn), lambda j: (0, j)),
        out_shape=jax.ShapeDtypeStruct((8, n), F32),
        compiler_params=_cparams("parallel"),
        name="ada_matvec",
    )(a16, w, b.reshape(1, n))


def _norm_mod_kernel(x_ref, g_ref, sh_ref, sc_ref, o_ref):
    x = x_ref[...]
    ms = jnp.mean(x * x, axis=-1, keepdims=True)
    y = x * lax.rsqrt(ms + EPS) * g_ref[...]
    o_ref[...] = (y * (1.0 + sc_ref[...]) + sh_ref[...]).astype(o_ref.dtype)


def _norm_mod(x, g, sh, sc, tm=512):
    bsz, length, d = x.shape
    tm = min(tm, length)
    bm = sh.shape[0]
    mod_map = (lambda b, i: (b, 0, 0)) if bm == bsz else (lambda b, i: (0, 0, 0))
    return pl.pallas_call(
        _norm_mod_kernel,
        grid=(bsz, length // tm),
        in_specs=[pl.BlockSpec((None, tm, d), lambda b, i: (b, i, 0)),
                  pl.BlockSpec((1, d), lambda b, i: (0, 0)),
                  pl.BlockSpec((None, 1, d), mod_map),
                  pl.BlockSpec((None, 1, d), mod_map)],
        out_specs=pl.BlockSpec((None, tm, d), lambda b, i: (b, i, 0)),
        out_shape=jax.ShapeDtypeStruct((bsz, length, d), BF16),
        compiler_params=_cparams("parallel", "parallel"),
        name="norm_mod",
    )(x, g.reshape(1, d), sh.reshape(bm, 1, d), sc.reshape(bm, 1, d))


def _fold_kernel(a_ref, b_ref, o_ref):
    o_ref[...] = jnp.dot(a_ref[...], b_ref[...], preferred_element_type=F32, precision=HI)


def _fold(a, b):
    hh, m, k = a.shape
    n = b.shape[-1]
    return pl.pallas_call(
        _fold_kernel,
        grid=(hh,),
        in_specs=[pl.BlockSpec((None, m, k), lambda h: (h, 0, 0)),
                  pl.BlockSpec((None, k, n), lambda h: (h, 0, 0))],
        out_specs=pl.BlockSpec((None, m, n), lambda h: (h, 0, 0)),
        out_shape=jax.ShapeDtypeStruct((hh, m, n), F32),
        compiler_params=_cparams("parallel"),
        name="weight_fold",
    )(a, b)


def _four_in_kernel(h_ref, w_ref, wcs_ref, pc_ref, ps_ref, z_scr, zp_scr):
    n2, tn1, hd = pc_ref.shape
    pitch = n2 + PITCH_PAD
    nq = hd // LANES
    z = _dot(h_ref[...], w_ref[...])
    for q in range(nq):
        for i1 in range(tn1):
            z_scr[q, i1 * pitch:i1 * pitch + n2, :] = z[i1 * n2:(i1 + 1) * n2, q * LANES:(q + 1) * LANES]
    for j2 in range(n2):
        for q in range(nq):
            zp_scr[j2 * tn1:(j2 + 1) * tn1, q * LANES:(q + 1) * LANES] = (
                z_scr[q, pl.ds(j2, tn1, stride=pitch), :].astype(BF16))
    p = _dot(zp_scr[...], wcs_ref[...])
    p3 = p.reshape(n2, tn1, 2 * hd)
    pc_ref[...] = p3[:, :, :hd].astype(BF16)
    ps_ref[...] = p3[:, :, hd:].astype(BF16)


def _four_in(h, w_f, wcs, bsz, length, n2):
    n, d = h.shape
    heads, hd, _ = wcs.shape
    n1 = length // n2
    tn1 = PACK_ROWS
    tm = tn1 * n2
    tpb = length // tm
    fw = heads * hd
    out = jax.ShapeDtypeStruct((bsz, n2, n1, fw), BF16)
    ospec = pl.BlockSpec((None, n2, tn1, hd), lambda i, j: (i // tpb, 0, i % tpb, j))
    return pl.pallas_call(
        _four_in_kernel,
        grid=(n // tm, heads),
        in_specs=[pl.BlockSpec((tm, d), lambda i, j: (i, 0)),
                  pl.BlockSpec((d, hd), lambda i, j: (0, j)),
                  pl.BlockSpec((None, hd, 2 * hd), lambda i, j: (j, 0, 0))],
        out_specs=[ospec, ospec],
        out_shape=[out, out],
        scratch_shapes=[pltpu.VMEM((hd // LANES, tn1 * (n2 + PITCH_PAD), LANES), F32), pltpu.VMEM((tm, hd), BF16)],
        compiler_params=_cparams("parallel", "arbitrary"),
        name="fourier_in_proj",
    )(h, w_f, wcs)


def _dft_tables(length, n2):
    n1 = length // n2
    k1 = np.arange(n1)[:, None, None]
    i1 = np.arange(n1)[None, :, None]
    i2 = np.arange(n2)[None, None, :]
    phase = (k1 * (n2 * i1 + i2)) % length
    phi = (2.0 * np.pi / length) * phase.astype(np.float64)
    c = np.cos(phi).transpose(2, 0, 1) / math.sqrt(n1)
    s = np.sin(phi).transpose(2, 0, 1) / math.sqrt(n1)
    g = np.concatenate([np.concatenate([c, -s], axis=2),
                        np.concatenate([-s, -c], axis=2)], axis=1)
    k2 = np.arange(n2)[:, None]
    j2 = np.arange(n2)[None, :]
    th = (2.0 * np.pi / n2) * ((k2 * j2) % n2).astype(np.float64)
    cs = np.concatenate([np.cos(th), np.sin(th)], axis=1) / math.sqrt(n2)
    return jnp.asarray(g, dtype=BF16), jnp.asarray(cs, dtype=BF16)


def _dft_kernel(g_ref, cs_ref, pc_ref, ps_ref, o_ref, t_scr, y_scr):
    n2, n1, _ = pc_ref.shape
    m = 2 * n1
    p1 = m + PITCH_PAD
    p2 = n1 + PITCH_PAD

    def stage1(j2, carry):
        rhs = jnp.concatenate([pc_ref[j2], ps_ref[j2]], axis=0)
        t_scr[pl.ds(pl.multiple_of(j2 * p1, SUBLANES), m), :] = _dot(g_ref[j2], rhs)
        return carry

    lax.fori_loop(0, n2, stage1, 0, unroll=True)

    cs = cs_ref[...]

    def stage2(i, carry):
        k1 = 2 * i
        cols = []
        for dk in range(2):
            re = t_scr[pl.ds(k1 + dk, n2, stride=p1), :]
            im = t_scr[pl.ds(n1 + k1 + dk, n2, stride=p1), :]
            cols.append(jnp.concatenate([re, im], axis=0).astype(BF16))
        res = _dot(cs, jnp.concatenate(cols, axis=1))
        for dk in range(2):
            y_scr[pl.ds(k1 + dk, n2, stride=p2), :] = res[:, dk * LANES:(dk + 1) * LANES]
        return carry

    lax.fori_loop(0, n1 // 2, stage2, 0, unroll=True)
    for k2 in range(n2):
        o_ref[k2 * n1:(k2 + 1) * n1, :] = y_scr[k2 * p2:k2 * p2 + n1, :].astype(o_ref.dtype)


def _dft(pc4, ps4, g, cs):
    bsz, n2, n1, w = pc4.shape
    m = 2 * n1
    ispec = pl.BlockSpec((None, n2, n1, LANES), lambda b, j: (b, 0, 0, j))
    return pl.pallas_call(
        _dft_kernel,
        grid=(bsz, w // LANES),
        in_specs=[pl.BlockSpec((n2, m, m), lambda b, j: (0, 0, 0), pipeline_mode=pl.Buffered(1)),
                  pl.BlockSpec((n2, 2 * n2), lambda b, j: (0, 0)),
                  ispec, ispec],
        out_specs=pl.BlockSpec((None, n2 * n1, LANES), lambda b, j: (b, 0, j)),
        out_shape=jax.ShapeDtypeStruct((bsz, n2 * n1, w), BF16),
        scratch_shapes=[pltpu.VMEM((n2 * (m + PITCH_PAD), LANES), F32),
                        pltpu.VMEM((n2 * (n1 + PITCH_PAD), LANES), F32)],
        compiler_params=_cparams("parallel", "parallel"),
        name="position_dft",
    )(g, cs, pc4, ps4)


def _nt_kernel(w_ref, h_ref, o_ref, stage_ref):
    acc = lax.dot_general(w_ref[...], h_ref[...], (((0,), (1,)), ((), ())), preferred_element_type=F32)
    rows, tq, _ = o_ref.shape
    if tq % SUBLANES == 0:
        for q in range(tq):
            stage_ref[pl.ds(q, rows, stride=tq), :] = acc[:, q * LANES:(q + 1) * LANES]
        o_ref[...] = stage_ref[...].reshape(rows, tq, LANES)
    else:
        for q in range(tq):
            o_ref[:, q, :] = acc[:, q * LANES:(q + 1) * LANES]


def _proj_t(w, col0, c, h, tmc=1024, tn=1024):
    d = w.shape[0]
    n = h.shape[0]
    tmc, tn = min(tmc, c), min(tn, n)
    assert col0 % tmc == 0 and c % tmc == 0
    return pl.pallas_call(
        _nt_kernel,
        grid=(n // tn, c // tmc),
        in_specs=[pl.BlockSpec((d, tmc), lambda i, j: (0, col0 // tmc + j)),
                  pl.BlockSpec((tn, d), lambda i, j: (i, 0))],
        out_specs=pl.BlockSpec((tmc, tn // LANES, LANES), lambda i, j: (j, i, 0)),
        out_shape=jax.ShapeDtypeStruct((c, n // LANES, LANES), F32),
        scratch_shapes=[pltpu.VMEM((tmc * (tn // LANES), LANES), F32)],
        compiler_params=_cparams("parallel", "arbitrary"),
        name="s5_in_proj_t",
    )(w, h)


def _s5_tables(lam_re, lam_im, log_dt, b_re, b_im, c_re, c_im):
    t = CHUNK
    _, g, p = lam_re.shape
    h = b_re.shape[-1]
    dt = jnp.exp(log_dt.astype(F32))[..., None]
    lr, li = jnp.minimum(lam_re.astype(F32), -1e-4), lam_im.astype(F32)
    ar, ai = lr * dt, li * dt

    lbm = jnp.exp(ar)
    lbr, lbi = lbm * jnp.cos(ai), lbm * jnp.sin(ai)

    ks = jnp.arange(2 * t, dtype=F32)
    tab_m = jnp.exp(ar[..., None] * ks)
    tab_r, tab_i = tab_m * jnp.cos(ai[..., None] * ks), tab_m * jnp.sin(ai[..., None] * ks)

    def powers(d, first, step, steps_major=False):
        first = int(first)
        if step > 0:
            out = tab_r[d, :, :, first:first + t], tab_i[d, :, :, first:first + t]
        else:
            out = (jnp.flip(tab_r[d, :, :, first - t + 1:first + 1], -1),
                   jnp.flip(tab_i[d, :, :, first - t + 1:first + 1], -1))
        return tuple(jnp.swapaxes(a, 1, 2) for a in out) if steps_major else out

    nr, ni = lbr - 1.0, lbi
    den = lr * lr + li * li
    qr, qi = (nr * lr + ni * li) / den, (ni * lr - nr * li) / den
    br, bi = (jnp.swapaxes(a.astype(F32), -1, -2) for a in (b_re, b_im))
    bbr = qr[:, :, None] * br - qi[:, :, None] * bi
    bbi = qr[:, :, None] * bi + qi[:, :, None] * br
    cr, ci = c_re.astype(F32), c_im.astype(F32)

    kb_a = jnp.concatenate([bbr[0], bbr[0], bbr[1], bbr[1]], axis=-1)
    kb_b = jnp.concatenate([-bbi[0], bbi[0], -bbi[1], bbi[1]], axis=-1)
    kc_a = jnp.concatenate([cr[0], ci[0], cr[1], ci[1]], axis=-1)
    kc_b = jnp.concatenate([ci[0], cr[0], ci[1], cr[1]], axis=-1)
    wfr, wfi = powers(0, 0.0, 1.0)
    wbr, wbi = powers(1, t - 1.0, -1.0)
    fwd = lambda a: jnp.pad(a, ((0, 0), (0, 0), (t - 1, 1)))
    bwd = lambda a: jnp.pad(a, ((0, 0), (0, 0), (0, t)))
    kw = jnp.concatenate([fwd(wfr), fwd(-wfi), bwd(wbr), bwd(-wbi)], axis=1)

    pfr, pfi = powers(0, t - 1.0, -1.0, steps_major=True)
    pbr, pbi = powers(1, 0.0, 1.0, steps_major=True)
    ew_a = jnp.concatenate([pfr, pfr, pbr, pbr], axis=-1)
    ew_b = jnp.concatenate([pfi, pfi, pbi, pbi], axis=-1)
    eb_a = jnp.concatenate([bbr[0], bbi[0], bbr[1], bbi[1]], axis=-1)
    eb_b = jnp.concatenate([-bbi[0], bbr[0], -bbi[1], bbr[1]], axis=-1)

    qfr, qfi = powers(0, 1.0, 1.0)
    qbr, qbi = powers(1, float(t), -1.0)
    crf, cif, crb, cib = (jnp.swapaxes(a, 1, 2) for a in (cr[0], ci[0], cr[1], ci[1]))
    dc = jnp.concatenate([jnp.concatenate([crf, -crf, crb, -crb], axis=1),
                          jnp.concatenate([-cif, -cif, -cib, -cib], axis=1)], axis=-1)
    dq = jnp.concatenate([jnp.concatenate([qfr, qfi, qbr, qbi], axis=1),
                          jnp.concatenate([qfi, qfr, qbi, qbr], axis=1)], axis=-1)

    dec = jnp.stack([tab_r[0, :, :, t], tab_i[0, :, :, t],
                     tab_r[1, :, :, t], tab_i[1, :, :, t]]).reshape(4, g * p)
    return (kb_a, kb_b, kc_a, kc_b, kw), (ew_a, ew_b, eb_a, eb_b), (dc.astype(BF16), dq.astype(BF16)), dec


def _carry_expanders(h, t):
    ht = h * t
    rep = np.zeros((2 * h, 2 * ht), np.float32)
    til = np.zeros((2 * t, 2 * ht), np.float32)
    for half in range(2):
        for ho in range(h):
            for tt in range(t):
                rep[half * h + ho, half * ht + ho * t + tt] = 1.0
                til[half * t + tt, half * ht + ho * t + tt] = 1.0
    return jnp.asarray(rep, dtype=BF16), jnp.asarray(til, dtype=BF16)


def _toeplitz_select(t):
    sub = SUBLANES
    sel = np.zeros((4 * t, (t // sub) * 2 * t), np.float32)
    for q in range(t // sub):
        for tt in range(t):
            j = tt - sub * q + t - sub
            sel[j, q * 2 * t + tt] = 1.0
            sel[2 * t + j, q * 2 * t + t + tt] = 1.0
    return jnp.asarray(sel, dtype=BF16)


GPS = 4


def _gather_chunks(z_ref, gi, h):
    t = z_ref.shape[-1] // 2
    lo = lax.broadcasted_iota(jnp.int32, z_ref.shape[1:], 1) < t
    ev, od = [], []
    for k in range(0, h, 2):
        za, zb = z_ref[gi * h + k], z_ref[gi * h + k + 1]
        ev.append(jnp.where(lo, za, pltpu.roll(zb, t, axis=1)))
        od.append(jnp.where(lo, pltpu.roll(za, t, axis=1), zb))
    return jnp.concatenate([jnp.concatenate(ev, axis=1), jnp.concatenate(od, axis=1)], axis=0)


def _s5_state_kernel(zc_ref, z_ref, ewa_ref, ewb_ref, eba_ref, ebb_ref, *refs):
    couts, louts, e_scr = refs[0:4], refs[4:8], refs[8]
    h = z_ref.shape[0] // GPS
    t = ewa_ref.shape[1]
    p = ewa_ref.shape[-1] // 4
    for gi in range(GPS):
        wa, wb = ewa_ref[gi], ewb_ref[gi]
        for k in range(h):
            e_scr[k * t:(k + 1) * t, :] = (wa * eba_ref[gi, k:k + 1, :] + wb * ebb_ref[gi, k:k + 1, :]).astype(BF16)
        for src, outs in ((zc_ref, couts), (z_ref, louts)):
            a2 = _gather_chunks(src, gi, h).astype(BF16)
            s = _dot(a2, e_scr[...])
            for k in range(4):
                outs[k][:, gi * p:(gi + 1) * p] = s[:, k * p:(k + 1) * p]


def _s5_states(zc3, z3, e_tabs):
    gh, npair, t2 = z3.shape
    cpair = zc3.shape[1]
    t = t2 // 2
    g, _, p4 = e_tabs[0].shape
    h, p = gh // g, p4 // 4
    wspec = pl.BlockSpec((GPS, t, p4), lambda i: (i, 0, 0))
    bspec = pl.BlockSpec((GPS, h, p4), lambda i: (i, 0, 0))
    res = pl.pallas_call(
        _s5_state_kernel,
        grid=(g // GPS,),
        in_specs=[pl.BlockSpec((GPS * h, cpair, t2), lambda i: (i, 0, 0)),
                  pl.BlockSpec((GPS * h, npair, t2), lambda i: (i, 0, 0)), wspec, wspec, bspec, bspec],
        out_specs=[pl.BlockSpec((2 * cpair, GPS * p), lambda i: (0, i))] * 4 +
                  [pl.BlockSpec((2 * npair, GPS * p), lambda i: (0, i))] * 4,
        out_shape=[jax.ShapeDtypeStruct((2 * cpair, g * p), F32)] * 4 +
                  [jax.ShapeDtypeStruct((2 * npair, g * p), F32)] * 4,
        scratch_shapes=[pltpu.VMEM((h * t, p4), BF16)],
        compiler_params=_cparams("parallel"),
        name="s5_chunk_states",
    )(zc3, z3, *e_tabs)
    return res[:4], res[4:]


def _s5_scan_kernel(bsz, dec_ref, cfre, cfim, cbre, cbim, sfre, sfim, sbre, sbim,
                    hfre, hfim, hbre, hbim):
    nctx = cfre.shape[0] // bsz
    nlat = sfre.shape[0] // bsz
    width = dec_ref.shape[-1]
    fr, fi = dec_ref[0:1, :], dec_ref[1:2, :]
    br, bi = dec_ref[2:3, :], dec_ref[3:4, :]
    zero = jnp.zeros((1, width), F32)

    def step(ar, ai, hr, hi, sr, si):
        return ar * hr - ai * hi + sr, ar * hi + ai * hr + si

    def row(nchunk, b, j):
        return (j % 2) * (bsz * nchunk // 2) + b * (nchunk // 2) + j // 2

    init = []
    for b in range(bsz):
        hr, hi = zero, zero
        for j in range(nctx):
            r = row(nctx, b, j)
            hr, hi = step(fr, fi, hr, hi, cfre[r:r + 1, :], cfim[r:r + 1, :])
        init += [hr, hi]
        hr, hi = zero, zero
        for j in range(nctx - 1, -1, -1):
            r = row(nctx, b, j)
            hr, hi = step(br, bi, hr, hi, cbre[r:r + 1, :], cbim[r:r + 1, :])
        init += [hr, hi]

    def body(k, carry):
        out = []
        for b in range(bsz):
            hr, hi, gr, gi = carry[4 * b:4 * b + 4]
            rf = row(nlat, b, k)
            hfre[pl.ds(rf, 1), :] = hr
            hfim[pl.ds(rf, 1), :] = hi
            out += list(step(fr, fi, hr, hi, sfre[pl.ds(rf, 1), :], sfim[pl.ds(rf, 1), :]))
            rb = row(nlat, b, nlat - 1 - k)
            hbre[pl.ds(rb, 1), :] = gr
            hbim[pl.ds(rb, 1), :] = gi
            out += list(step(br, bi, gr, gi, sbre[pl.ds(rb, 1), :], sbim[pl.ds(rb, 1), :]))
        return tuple(out)

    lax.fori_loop(0, nlat, body, tuple(init), unroll=2)


def _s5_scan(dec, ctx_s, lat_s, bsz, tw=1024):
    gp = dec.shape[-1]
    rc, rl = ctx_s[0].shape[0], lat_s[0].shape[0]
    assert (rc // bsz) % 2 == 0 and (rl // bsz) % 2 == 0
    tw = min(tw, gp)
    cspec = pl.BlockSpec((rc, tw), lambda i: (0, i))
    lspec = pl.BlockSpec((rl, tw), lambda i: (0, i))
    out = jax.ShapeDtypeStruct((rl, gp), F32)
    return pl.pallas_call(
        functools.partial(_s5_scan_kernel, bsz),
        grid=(gp // tw,),
        in_specs=[pl.BlockSpec((4, tw), lambda i: (0, i))] + [cspec] * 4 + [lspec] * 4,
        out_specs=[lspec] * 4,
        out_shape=[out] * 4,
        compiler_params=_cparams("parallel"),
        name="s5_state_scan",
    )(dec, *ctx_s, *lat_s)


def _gelu_tanh(x):
    return 0.5 * x * (1.0 + jnp.tanh(math.sqrt(2.0 / math.pi) * (x + 0.044715 * (x * x * x))))


def _expand_toeplitz(k_ref, sel_ref, lhs_ref, res_ref, w_ref, h, t):
    half = h // 2
    sub = SUBLANES
    rows_per_hi = half * sub

    def fill(hi, carry):
        for hp in range(half):
            for par in range(2):
                v = k_ref[pl.ds(hi * h + 2 * hp + par, 1), :]
                b = pltpu.roll(jnp.broadcast_to(v, (sub, 2 * t)), 2 * t - (sub - 1), axis=1, stride=1, stride_axis=0)
                rows = pl.ds(pl.multiple_of(hi * rows_per_hi + hp * sub, sub), sub)
                lhs_ref[rows, par * 2 * t:(par + 1) * 2 * t] = b
        return carry

    lax.fori_loop(0, h, fill, 0, unroll=True)
    res_ref[...] = _dot(lhs_ref[...].astype(BF16), sel_ref[...])

    def shuffle(hi, carry):
        base = pl.multiple_of(hi * rows_per_hi, rows_per_hi)
        r_hi = res_ref[pl.ds(base, rows_per_hi), :]
        rows = [jnp.concatenate([r_hi[hp * sub:(hp + 1) * sub, q * 2 * t:(q + 1) * 2 * t] for hp in range(half)],
                                axis=1)
                for q in range(t // sub)]
        w_ref[pl.ds(base, t), :] = jnp.concatenate(rows, axis=0).astype(BF16)
        return carry

    lax.fori_loop(0, h, shuffle, 0, unroll=True)


def _s5_out_kernel(z_ref, kba_ref, kbb_ref, kca_ref, kcb_ref, kw_ref, sel_ref, dc_ref, dq_ref, rep_ref, til_ref,
                   hfre, hfim, hbre, hbim, dsk_ref, o_ref, km_scr, k_scr, lhs_ref, res_ref, w_ref):
    h = z_ref.shape[0] // GPS
    npair = z_ref.shape[1]
    t = z_ref.shape[-1] // 2
    ht = h * t
    p = dc_ref.shape[1] // 4
    lo = lax.broadcasted_iota(jnp.int32, z_ref.shape[1:], 1) < t
    for gi in range(GPS):
        for hi in range(h):
            km_scr[hi * h:(hi + 1) * h, :] = (kba_ref[gi, hi:hi + 1, :] * kca_ref[gi]
                                              + kbb_ref[gi, hi:hi + 1, :] * kcb_ref[gi])
        k_scr[...] = _dot_split(km_scr[...], kw_ref[gi])
        _expand_toeplitz(k_scr, sel_ref, lhs_ref, res_ref, w_ref, h, t)
        ce = _dot(dc_ref[gi], rep_ref[...])
        qe = _dot(dq_ref[gi], til_ref[...])
        dmat = (ce[:, :ht] * qe[:, :ht] + ce[:, ht:] * qe[:, ht:]).astype(BF16)
        u = _gather_chunks(z_ref, gi, h)
        sl = slice(gi * p, (gi + 1) * p)
        hp = jnp.concatenate([hfre[:, sl], hfim[:, sl], hbre[:, sl], hbim[:, sl]], axis=-1).astype(BF16)
        y = _dot(u.astype(BF16), w_ref[...]) + _dot(hp, dmat)
        gl = _gelu_tanh(y + u * dsk_ref[gi])
        for k in range(0, h, 2):
            te = gl[0:npair, k * t:(k + 2) * t]
            to = gl[npair:2 * npair, k * t:(k + 2) * t]
            o_ref[gi * h + k] = jnp.where(lo, te, pltpu.roll(to, t, axis=1))
            o_ref[gi * h + k + 1] = jnp.where(lo, pltpu.roll(te, t, axis=1), to)


def _s5_out(z3, k_tabs, sel, d_tabs, expanders, hin, dsk):
    gh, npair, t2z = z3.shape
    nchunk, t = 2 * npair, t2z // 2
    kb_a, kb_b, kc_a, kc_b, kw = k_tabs
    g, kp, t2 = kw.shape
    h = gh // g
    hh, ht = h * h, h * t
    assert t2 == 2 * t and (h // 2) * SUBLANES == t and t2 == LANES
    dc, dq = d_tabs
    rep, til = expanders
    p4 = dc.shape[1]
    p = p4 // 4
    hspec = pl.BlockSpec((nchunk, GPS * p), lambda i: (0, i))
    const = lambda a: pl.BlockSpec(a.shape, lambda i: (0, 0))
    return pl.pallas_call(
        _s5_out_kernel,
        grid=(g // GPS,),
        in_specs=[pl.BlockSpec((GPS * h, npair, t2z), lambda i: (i, 0, 0)),
                  ] + [pl.BlockSpec((GPS, h, kp), lambda i: (i, 0, 0))] * 4 + [
                  pl.BlockSpec((GPS, kp, t2), lambda i: (i, 0, 0)),
                  const(sel),
                  pl.BlockSpec((GPS, p4, 2 * h), lambda i: (i, 0, 0)),
                  pl.BlockSpec((GPS, p4, 2 * t), lambda i: (i, 0, 0)),
                  const(rep), const(til)] + [hspec] * 4 +
                 [pl.BlockSpec((GPS, 1, ht), lambda i: (i, 0, 0))],
        out_specs=pl.BlockSpec((GPS * h, npair, t2z), lambda i: (i, 0, 0)),
        out_shape=jax.ShapeDtypeStruct((gh, npair, t2z), F32),
        scratch_shapes=[pltpu.VMEM((hh, kp), F32), pltpu.VMEM((hh, t2), F32), pltpu.VMEM((ht, 4 * t), F32),
                        pltpu.VMEM((ht, ht), F32), pltpu.VMEM((ht, ht), BF16)],
        compiler_params=_cparams("parallel"),
        name="s5_chunk_out",
    )(z3, kb_a, kb_b, kc_a, kc_b, kw, sel, dc, dq, rep, til, *hin, dsk)


def _glu_kernel(gt_ref, wa_ref, wb_ref, ba_ref, bb_ref, o_ref, g_scr, stage_ref):
    @pl.when(pl.program_id(1) == 0)
    def _():
        c, tq, _ = gt_ref.shape
        half = stage_ref.shape[0] // tq
        for c0 in range(0, c, half):
            stage_ref[...] = gt_ref[c0:c0 + half].reshape(half * tq, LANES)
            for q in range(tq):
                g_scr[c0:c0 + half, q * LANES:(q + 1) * LANES] = stage_ref[pl.ds(q, half, stride=tq), :].astype(BF16)

    dn = (((0,), (0,)), ((), ()))
    gt = g_scr[...]
    a = lax.dot_general(gt, wa_ref[...].astype(BF16), dn, preferred_element_type=F32) + ba_ref[...]
    b = lax.dot_general(gt, wb_ref[...].astype(BF16), dn, preferred_element_type=F32) + bb_ref[...]
    o_ref[...] = (a * jax.nn.sigmoid(b)).astype(o_ref.dtype)


def _glu(g3, wa, wb, ba, bb, tm=1024, tn=512):
    c, nq, _ = g3.shape
    n = nq * LANES
    co = wa.shape[-1]
    tm, tn = min(tm, n), min(tn, co)
    stage_rows = (c // 2) * (tm // LANES)
    return pl.pallas_call(
        _glu_kernel,
        grid=(n // tm, co // tn),
        in_specs=[pl.BlockSpec((c, tm // LANES, LANES), lambda i, j: (0, i, 0)),
                  pl.BlockSpec((c, tn), lambda i, j: (0, j)),
                  pl.BlockSpec((c, tn), lambda i, j: (0, j)),
                  pl.BlockSpec((1, tn), lambda i, j: (0, j)),
                  pl.BlockSpec((1, tn), lambda i, j: (0, j))],
        out_specs=pl.BlockSpec((tm, tn), lambda i, j: (i, j)),
        out_shape=jax.ShapeDtypeStruct((n, co), BF16),
        scratch_shapes=[pltpu.VMEM((c, tm), BF16), pltpu.VMEM((stage_rows, LANES), F32)],
        compiler_params=_cparams("parallel", "arbitrary"),
        name="s5_glu",
    )(g3, wa, wb, ba.reshape(1, co), bb.reshape(1, co))


def _out_proj_kernel(yf_ref, ys_ref, wf_ref, ws_ref, x_ref, g_ref, ng_ref, sh_ref, sc_ref, o_ref, hm_ref):
    j = pl.program_id(1)
    tm, tn = x_ref.shape
    acc = _dot(yf_ref[...], wf_ref[...]) + _dot(ys_ref[...], ws_ref[...])
    o_ref[:, pl.ds(pl.multiple_of(j * tn, tn), tn)] = x_ref[...] + g_ref[...] * acc

    @pl.when(j == pl.num_programs(1) - 1)
    def _():
        scale, sh = ng_ref[...] * (1.0 + sc_ref[...]), sh_ref[...]

        def norm_rows(r, carry):
            rows = pl.ds(pl.multiple_of(r * PACK_ROWS, PACK_ROWS), PACK_ROWS)
            x = o_ref[rows, :]
            ms = jnp.mean(x * x, axis=-1, keepdims=True)
            hm_ref[rows, :] = (x * lax.rsqrt(ms + EPS) * scale + sh).astype(hm_ref.dtype)
            return carry

        lax.fori_loop(0, tm // PACK_ROWS, norm_rows, 0, unroll=2)


def _out_proj(yf, ys, w_out, x2, gate, norm_g, sh, sc, length, tm=512, tn=1024):
    n, fw = yf.shape
    sw = ys.shape[-1]
    d = w_out.shape[-1]
    tm, tn = min(tm, length), min(tn, d)
    assert fw % sw == 0 and tm % (2 * PACK_ROWS) == 0
    bsz = gate.shape[0]
    batch_row = pl.BlockSpec((None, 1, d), lambda i, j: ((i * tm) // length, 0, 0))
    return pl.pallas_call(
        _out_proj_kernel,
        grid=(n // tm, d // tn),
        in_specs=[pl.BlockSpec((tm, fw), lambda i, j: (i, 0)),
                  pl.BlockSpec((tm, sw), lambda i, j: (i, 0)),
                  pl.BlockSpec((fw, tn), lambda i, j: (0, j)),
                  pl.BlockSpec((sw, tn), lambda i, j: (fw // sw, j)),
                  pl.BlockSpec((tm, tn), lambda i, j: (i, j)),
                  pl.BlockSpec((None, 1, tn), lambda i, j: ((i * tm) // length, 0, j)),
                  pl.BlockSpec((1, d), lambda i, j: (0, 0)),
                  batch_row, batch_row],
        out_specs=[pl.BlockSpec((tm, d), lambda i, j: (i, 0)),
                   pl.BlockSpec((tm, d), lambda i, j: (i, 0))],
        out_shape=[jax.ShapeDtypeStruct((n, d), F32), jax.ShapeDtypeStruct((n, d), BF16)],
        compiler_params=_cparams("parallel", "arbitrary"),
        name="out_proj_residual",
    )(yf, ys, w_out, w_out, x2, gate.reshape(bsz, 1, d), norm_g.reshape(1, d), sh.reshape(bsz, 1, d),
      sc.reshape(bsz, 1, d))


def _ffn_hidden_start(j, th, hid):
    return pl.multiple_of(jnp.minimum(j * th, hid - th), math.gcd(th, hid - th))


def _ffn_kernel(hid, h_ref, wg_ref, wu_ref, wd_ref, x_hbm, gate_ref, fg_ref, o_hbm, o_ref, x_sem, o_sem):
    i, j = pl.program_id(0), pl.program_id(1)
    tm = o_ref.shape[0]
    th = wg_ref.shape[-1]
    row0 = pl.multiple_of(i * tm, tm)

    def residual_copy():
        return pltpu.make_async_copy(x_hbm.at[pl.ds(row0, tm), :], o_ref, x_sem)

    def output_copy(r):
        rows = pl.ds(pl.multiple_of(r * FFN_NORM_ROWS, FFN_NORM_ROWS), FFN_NORM_ROWS)
        dst = pl.ds(pl.multiple_of(row0 + r * FFN_NORM_ROWS, FFN_NORM_ROWS), FFN_NORM_ROWS)
        return pltpu.make_async_copy(o_ref.at[rows, :], o_hbm.at[dst, :], o_sem.at[r])

    @pl.when(j == 0)
    def _():
        residual_copy().start()

    hh = h_ref[...]
    g = _dot(hh, wg_ref[...].astype(BF16))
    u = _dot(hh, wu_ref[...].astype(BF16))
    unit = _ffn_hidden_start(j, th, hid) + lax.broadcasted_iota(jnp.int32, (1, th), 1)
    a = jnp.where(unit >= j * th, g * jax.nn.sigmoid(g) * u, 0.0).astype(BF16)
    d = o_ref.shape[-1]
    nc = min(FFN_DOWN_CHUNK, d)

    @pl.when(j == 0)
    def _():
        residual_copy().wait()

    for c0 in range(0, d, nc):
        o_ref[:, c0:c0 + nc] += gate_ref[:, c0:c0 + nc] * _dot(a, wd_ref[:, c0:c0 + nc].astype(BF16))

    @pl.when(j == pl.num_programs(1) - 1)
    def _():
        fg = fg_ref[...]

        def norm_rows(r, carry):
            rows = pl.ds(pl.multiple_of(r * FFN_NORM_ROWS, FFN_NORM_ROWS), FFN_NORM_ROWS)
            x = o_ref[rows, :]
            ms = jnp.mean(x * x, axis=-1, keepdims=True)
            o_ref[rows, :] = x * lax.rsqrt(ms + EPS) * fg
            output_copy(r).start()
            return carry

        lax.fori_loop(0, tm // FFN_NORM_ROWS, norm_rows, 0)

        def drain(r, carry):
            output_copy(r).wait()
            return carry

        lax.fori_loop(0, tm // FFN_NORM_ROWS, drain, 0)


FFN_DOWN_CHUNK = 512
FFN_NORM_ROWS = 64


FFN_TH = 256
FFN_VMEM_LIMIT = 60 * 1024 * 1024


def _ffn(h, wg, wu, wd, x, gate, final_g, length, tm=1024):
    n, d = h.shape
    hid = wg.shape[-1]
    bsz = gate.shape[0]
    tm, th = min(tm, length), min(FFN_TH, hid)
    assert length % tm == 0 and tm % FFN_NORM_ROWS == 0
    col_map = lambda i, j: (0, _ffn_hidden_start(j, th, hid))
    return pl.pallas_call(
        functools.partial(_ffn_kernel, hid),
        grid=(n // tm, pl.cdiv(hid, th)),
        in_specs=[pl.BlockSpec((tm, d), lambda i, j: (i, 0)),
                  pl.BlockSpec((pl.Element(d), pl.Element(th)), col_map),
                  pl.BlockSpec((pl.Element(d), pl.Element(th)), col_map),
                  pl.BlockSpec((pl.Element(th), pl.Element(d)), lambda i, j: (_ffn_hidden_start(j, th, hid), 0)),
                  pl.BlockSpec(memory_space=pl.ANY),
                  pl.BlockSpec((None, 1, d), lambda i, j: ((i * tm) // length, 0, 0)),
                  pl.BlockSpec((1, d), lambda i, j: (0, 0))],
        out_specs=pl.BlockSpec(memory_space=pl.ANY),
        out_shape=jax.ShapeDtypeStruct((n, d), F32),
        scratch_shapes=[pltpu.VMEM((tm, d), F32), pltpu.SemaphoreType.DMA(()),
                        pltpu.SemaphoreType.DMA((tm // FFN_NORM_ROWS,))],
        compiler_params=_cparams("parallel", "arbitrary", vmem=FFN_VMEM_LIMIT),
        name="swiglu_ffn",
    )(h, wg, wu, wd, x, gate.reshape(bsz, 1, d), final_g.reshape(1, d))


def _dft_split(length):
    n2 = 64
    while (length // n2) % PACK_ROWS:
        n2 //= 2
    assert n2 >= SUBLANES and length % n2 == 0
    return n2


def kernel(x, c, ctx, c_ctx, ada_w, ada_b, norm1_g, norm2_g, w_in, w_out, fourier_w, s5_lam_re, s5_lam_im, s5_log_dt, s5_b_re, s5_b_im, s5_c_re, s5_c_im, s5_d, glu_w_a, glu_b_a, glu_w_b, glu_b_b, ffn_w_gate, ffn_w_up, ffn_w_down, final_g):
    bsz, length, d = x.shape
    depth = ada_w.shape[0]
    assert depth == 1, "single-layer block"
    lyr = 0
    heads, hd, _ = fourier_w.shape[1:]
    fw = heads * hd
    _, g, p, hgrp = s5_b_re.shape[1:]
    sw = g * hgrp
    assert w_in.shape[-1] == fw + sw and length % (2 * CHUNK) == 0 and ctx.shape[1] % (2 * CHUNK) == 0
    n = bsz * length

    a8 = jnp.zeros((8, d), F32).at[:bsz].set(c.astype(F32)).at[bsz].set(c_ctx.astype(F32))
    mods = _ada(jnp.concatenate([a8, a8], axis=0), ada_w[lyr], ada_b[lyr]).reshape(8, N_MOD, d)
    sh1, sc1, g1, sh2, sc2, g2 = (mods[:bsz, i] for i in range(N_MOD))
    csh1, csc1 = mods[bsz:bsz + 1, 0], mods[bsz:bsz + 1, 1]

    w_in_b = w_in[lyr].astype(BF16)
    ang = (2.0 * np.pi / hd) * ((np.arange(hd)[:, None] * np.arange(hd)[None, :]) % hd).astype(np.float64)
    cd = jnp.asarray(np.cos(ang) / math.sqrt(hd), F32)
    sd = jnp.asarray(np.sin(ang) / math.sqrt(hd), F32)
    csd = jnp.broadcast_to(jnp.stack([cd, sd])[:, None], (2, heads, hd, hd)).reshape(2 * heads, hd, hd)
    wf2 = jnp.concatenate([fourier_w[lyr], fourier_w[lyr]], axis=0).astype(F32)
    folded = _fold(csd, wf2)
    wcs = jnp.concatenate([folded[:heads], folded[heads:]], axis=-1).astype(BF16)
    k_tabs, e_mat, d_mat, dec = _s5_tables(s5_lam_re[lyr], s5_lam_im[lyr], s5_log_dt[lyr], s5_b_re[lyr],
                                         s5_b_im[lyr], s5_c_re[lyr], s5_c_im[lyr])
    sel = _toeplitz_select(CHUNK)
    dsk = jnp.repeat(s5_d[lyr].astype(F32).reshape(g, 1, hgrp), CHUNK, axis=-1)

    hc = _norm_mod(ctx, norm1_g[lyr], csh1, csc1)
    nctx_tok = bsz * ctx.shape[1]
    zc3 = _proj_t(w_in_b, fw, sw, hc.reshape(nctx_tok, d))

    hm = _norm_mod(x, norm1_g[lyr], sh1, sc1).reshape(n, d)
    n2 = _dft_split(length)
    pc4, ps4 = _four_in(hm, w_in_b, wcs, bsz, length, n2)
    g_tab, cs_tab = _dft_tables(length, n2)
    y_four = _dft(pc4, ps4, g_tab, cs_tab).reshape(n, fw)

    z3 = _proj_t(w_in_b, fw, sw, hm)
    ctx_s, lat_s = _s5_states(zc3, z3, e_mat)
    h_in = _s5_scan(dec, ctx_s, lat_s, bsz)
    g3 = _s5_out(z3, k_tabs, sel, d_mat, _carry_expanders(hgrp, CHUNK), h_in, dsk)
    y_s = _glu(g3, glu_w_a[lyr], glu_w_b[lyr], glu_b_a[lyr], glu_b_b[lyr])

    x1, hm2 = _out_proj(y_four, y_s, w_out[lyr].astype(BF16), x.reshape(n, d), g1, norm2_g[lyr], sh2, sc2, length)
    out = _ffn(hm2, ffn_w_gate[lyr], ffn_w_up[lyr], ffn_w_down[lyr], x1, g2, final_g, length)
    return out.reshape(bsz, length, d)
```
